```python
import jax, jax.numpy as jnp
from jax import lax
import numpy as np

D_MODEL = 2048
BATCH = 4
SEQ = 2048
DEPTH = 4

N_MIXERS = 4
N_META = 16
DEEPNORM_ALPHA = (2.0 * DEPTH) ** 0.25
DEEPNORM_BETA = (8.0 * DEPTH) ** -0.25
LN_EPS = 1e-5
RMS_EPS = 1e-6

CONV_WIDTH = 3

RWKV_HEAD = 64
RWKV_HEADS = D_MODEL // RWKV_HEAD
RWKV_DECAY_LORA = max(32, int(round(1.8 * D_MODEL ** 0.5 / 32)) * 32)
RWKV_AAA_LORA = max(32, int(round(1.8 * D_MODEL ** 0.5 / 32)) * 32)
RWKV_GATE_LORA = max(32, int(round(0.6 * D_MODEL ** 0.8 / 32)) * 32)
RWKV_GN_EPS = 1e-5 * RWKV_HEAD
RWKV_N_MIX = 6

HGRN_EXPAND = 128
HGRN_HEADS = D_MODEL // HGRN_EXPAND
HGRN_HEAD_V = D_MODEL // HGRN_HEADS
HGRN_CHUNK = 64

FOX_HEAD = 128
FOX_HEADS = D_MODEL // FOX_HEAD
FOX_BLOCK = 128

D_FF = 256 * ((8 * D_MODEL // 3 + 255) // 256)
N_EXPERTS = 8
TOP_K = 2

kernel_name = 'hybrid_conv_rwkv7_hgrn2_fox_moe_trunk'


def _layer_norm(x, g, b):
    xf = x.astype(jnp.float32)
    mu = jnp.mean(xf, axis=-1, keepdims=True)
    var = jnp.mean(jnp.square(xf - mu), axis=-1, keepdims=True)
    return ((xf - mu) * lax.rsqrt(var + LN_EPS) * g + b).astype(x.dtype)


def _rms_norm(x, g):
    xf = x.astype(jnp.float32)
    return (xf * lax.rsqrt(jnp.mean(jnp.square(xf), axis=-1, keepdims=True) + RMS_EPS) * g).astype(x.dtype)


def _heads(z, n_heads):
    b, t, _ = z.shape
    return z.reshape(b, t, n_heads, -1).transpose(0, 2, 1, 3)


def _merge_heads(z):
    b, h, t, d = z.shape
    return z.transpose(0, 2, 1, 3).reshape(b, t, h * d)


def short_conv_mixer(x, w_in, conv_w, conv_b, w_out):
    gate_b, gate_c, h = jnp.split(x @ w_in, 3, axis=-1)
    u = gate_c * h
    v = lax.conv_general_dilated(u, conv_w[:, None, :], window_strides=(1,),
                                 padding=[(CONV_WIDTH - 1, 0)],
                                 dimension_numbers=('NWC', 'WIO', 'NWC'),
                                 feature_group_count=u.shape[-1]) + conv_b
    return (gate_b * v) @ w_out


def rwkv7_mixer(x, mu, w_r, w_k, w_v, w0, w1, w2, a0, a1, a2, g1, g2,
                k_k, k_a, r_k, gn_g, gn_b, w_out):
    b, t, d = x.shape
    f32 = jnp.float32
    hs = (b, t, RWKV_HEADS, RWKV_HEAD)
    xx = jnp.pad(x, ((0, 0), (1, 0), (0, 0)))[:, :-1] - x
    xr, xw, xk, xv, xa, xg = (x + xx * mu[j] for j in range(RWKV_N_MIX))
    r = (xr @ w_r).astype(f32).reshape(hs)
    k = (xk @ w_k).astype(f32)
    v = (xv @ w_v).astype(f32).reshape(hs)
    w_log = -jax.nn.softplus(-(w0 + jnp.tanh(xw @ w1) @ w2).astype(f32)) - 0.5
    decay = jnp.exp(-jnp.exp(w_log)).reshape(hs)
    a = jax.nn.sigmoid((a0 + (xa @ a1) @ a2).astype(f32)).reshape(hs)
    g = jax.nn.sigmoid(xg @ g1) @ g2
    kk = (k * k_k).reshape(hs)
    kk = kk / jnp.maximum(jnp.sqrt(jnp.sum(jnp.square(kk), axis=-1, keepdims=True)), 1e-12)
    k = k.reshape(hs) * (1.0 + (a - 1.0) * k_a.reshape(RWKV_HEADS, RWKV_HEAD))

    def step(state, inp):
        r_t, w_t, k_t, v_t, kk_t, a_t = inp
        sa = jnp.einsum('bhij,bhj->bhi', state, -kk_t)
        state = (state * w_t[:, :, None, :]
                 + sa[..., None] * (kk_t * a_t)[:, :, None, :]
                 + v_t[..., None] * k_t[:, :, None, :])
        return state, jnp.einsum('bhij,bhj->bhi', state, r_t)

    time_major = tuple(z.transpose(1, 0, 2, 3) for z in (r, decay, k, v, kk, a))
    s0 = jnp.zeros((b, RWKV_HEADS, RWKV_HEAD, RWKV_HEAD), f32)
    _, y = lax.scan(step, s0, time_major)
    y = y.transpose(1, 0, 2, 3)
    mu_y = jnp.mean(y, axis=-1, keepdims=True)
    var_y = jnp.mean(jnp.square(y - mu_y), axis=-1, keepdims=True)
    y = ((y - mu_y) * lax.rsqrt(var_y + RWKV_GN_EPS)).reshape(b, t, d) * gn_g + gn_b
    bonus = jnp.sum(r * k * r_k, axis=-1, keepdims=True) * v
    y = y + bonus.reshape(b, t, d)
    return (y * g).astype(x.dtype) @ w_out


def _gla_chunk_scan(q, k, log_f, v, s0, chunk):
    b, h, t, dk = q.shape
    dv = v.shape[-1]
    n = t // chunk

    def to_chunks(z):
        return z.reshape(b, h, n, chunk, z.shape[-1]).transpose(2, 0, 1, 3, 4)

    causal = jnp.tril(jnp.ones((chunk, chunk), dtype=bool))

    def step(state, inp):
        qc, kc, gc, vc = inp
        bcum = jnp.cumsum(gc, axis=-2)
        rel = bcum[:, :, :, None, :] - bcum[:, :, None, :, :]
        dec = jnp.exp(jnp.where(causal[:, :, None], rel, -jnp.inf))
        scores = jnp.einsum('bhtd,bhtsd,bhsd->bhts', qc, dec, kc)
        o = (jnp.einsum('bhts,bhse->bhte', scores, vc)
             + jnp.einsum('bhtd,bhde->bhte', qc * jnp.exp(bcum), state))
        b_last = bcum[:, :, -1:, :]
        state = (jnp.exp(b_last[:, :, 0, :])[..., None] * state
                 + jnp.einsum('bhsd,bhse->bhde', kc * jnp.exp(b_last - bcum), vc))
        return state, o

    state, o = lax.scan(step, s0, tuple(to_chunks(z) for z in (q, k, log_f, v)))
    return state, o.transpose(1, 2, 0, 3, 4).reshape(b, h, t, dv)


def hgrn2_mixer(x, w_in, lb_table, layer_idx, norm_g, w_out):
    b, t, d = x.shape
    f32 = jnp.float32
    q, f_logit, i_in, g = jnp.split(x @ w_in, 4, axis=-1)
    lb = jnp.cumsum(jax.nn.softmax(lb_table.astype(f32), axis=0), axis=0)
    lb = lb[layer_idx] - lb[0]
    log_f = jnp.logaddexp(jnp.log(lb), jnp.log1p(-lb) + jax.nn.log_sigmoid(f_logit.astype(f32)))
    k = -jnp.expm1(log_f)
    qh = _heads(jax.nn.silu(q).astype(f32), HGRN_HEADS)
    kh = _heads(k, HGRN_HEADS)
    gh = _heads(log_f, HGRN_HEADS)
    vh = _heads(i_in.astype(f32), HGRN_HEADS)
    s0 = jnp.zeros((b, HGRN_HEADS, HGRN_EXPAND, HGRN_HEAD_V), f32)
    m = N_META
    s_meta, o_meta = _gla_chunk_scan(qh[:, :, :m], kh[:, :, :m], gh[:, :, :m], vh[:, :, :m], s0, m)
    _, o_rest = _gla_chunk_scan(qh[:, :, m:], kh[:, :, m:], gh[:, :, m:], vh[:, :, m:], s_meta, HGRN_CHUNK)
    o = jnp.concatenate([o_meta, o_rest], axis=2).transpose(0, 2, 1, 3)
    o = o * lax.rsqrt(jnp.mean(jnp.square(o), axis=-1, keepdims=True) + RMS_EPS)
    o = o.reshape(b, t, d) * norm_g * jax.nn.silu(g.astype(f32))
    return o.astype(x.dtype) @ w_out


def fox_mixer(x, w_in, b_f, q_norm_g, k_norm_g, w_out):
    b, t, d = x.shape
    f32 = jnp.float32
    q, k, v, g, f_logit = jnp.split(x @ w_in, [d, 2 * d, 3 * d, 4 * d], axis=-1)
    log_f = jax.nn.log_sigmoid((f_logit + b_f).astype(f32))
    c = jnp.cumsum(log_f, axis=1).transpose(0, 2, 1)
    qh = _rms_norm(_heads(q, FOX_HEADS), q_norm_g) * (FOX_HEAD ** -0.5)
    kh = _rms_norm(_heads(k, FOX_HEADS), k_norm_g)
    vh = _heads(v, FOX_HEADS)
    pos = jnp.arange(t)

    def attend(q_blk, c_blk, pos_blk):
        s = (jnp.einsum('bhqd,bhkd->bhqk', q_blk, kh).astype(f32)
             + c_blk[..., None] - c[:, :, None, :])
        s = jnp.where(pos[None, :] <= pos_blk[:, None], s, -jnp.inf)
        p = jax.nn.softmax(s, axis=-1)
        return jnp.einsum('bhqk,bhkd->bhqd', p.astype(vh.dtype), vh)

    o_meta = attend(qh[:, :, :N_META], c[:, :, :N_META], pos[:N_META])
    n_blocks = (t - N_META) // FOX_BLOCK

    def block(j):
        start = N_META + j * FOX_BLOCK
        return attend(lax.dynamic_slice_in_dim(qh, start, FOX_BLOCK, axis=2),
                      lax.dynamic_slice_in_dim(c, start, FOX_BLOCK, axis=2),
                      start + jnp.arange(FOX_BLOCK))

    o_rest = lax.map(block, jnp.arange(n_blocks))
    o_rest = o_rest.transpose(1, 2, 0, 3, 4).reshape(b, FOX_HEADS, n_blocks * FOX_BLOCK, FOX_HEAD)
    o = _merge_heads(jnp.concatenate([o_meta, o_rest], axis=2))
    return (o * jax.nn.sigmoid(g)) @ w_out


def swiglu(x, w1, w3, w2):
    return (jax.nn.silu(x @ w1) * (x @ w3)) @ w2


def moe_swiglu(x, router_w, router_b, w1, w3, w2):
    b, t, d = x.shape
    xt = x.reshape(b * t, d)
    logits = (xt @ router_w).astype(jnp.float32) + router_b
    top_v, top_i = lax.top_k(logits, TOP_K)
    gates = jax.nn.softmax(top_v, axis=-1)
    combine = jnp.sum(jax.nn.one_hot(top_i, N_EXPERTS, dtype=jnp.float32) * gates[..., None], axis=1)
    y = jnp.zeros_like(xt)
    for e in range(N_EXPERTS):
        y = y + combine[:, e:e + 1].astype(x.dtype) * swiglu(xt, w1[e], w3[e], w2[e])
    return y.reshape(b, t, d)


def setup_inputs(seed: int = 0) -> dict:
    key = jax.random.key(seed)
    ks = jax.random.split(key, 64)
    counter = iter(range(64))
    D, F, E = D_MODEL, D_FF, N_EXPERTS

    def nrm(shape, scale):
        return scale * jax.random.normal(ks[next(counter)], shape, jnp.float32)

    def gain(shape):
        return 1.0 + nrm(shape, 0.02)

    def unif(shape, lo, hi):
        return jax.random.uniform(ks[next(counter)], shape, jnp.float32, lo, hi)

    s_in = D ** -0.5
    s_out = DEEPNORM_BETA * D ** -0.5
    s_ff_out = DEEPNORM_BETA * F ** -0.5
    return {
        'x': nrm((BATCH, SEQ, D), 1.0),
        'meta': nrm((N_META, D), 1.0),
        'ln_mix_g': gain((DEPTH, D)),
        'ln_mix_b': nrm((DEPTH, D), 0.02),
        'ln_ffn_g': gain((DEPTH, D)),
        'ln_ffn_b': nrm((DEPTH, D), 0.02),
        'conv_w_in': nrm((D, 3 * D), s_in),
        'conv_w': nrm((CONV_WIDTH, D), CONV_WIDTH ** -0.5),
        'conv_b': nrm((D,), 0.02),
        'conv_w_out': nrm((D, D), s_out),
        'rwkv_mu': unif((RWKV_N_MIX, D), 0.0, 1.0),
        'rwkv_w_r': nrm((D, D), s_in),
        'rwkv_w_k': nrm((D, D), s_in),
        'rwkv_w_v': nrm((D, D), s_in),
        'rwkv_w0': unif((D,), -6.5, -1.5),
        'rwkv_w1': nrm((D, RWKV_DECAY_LORA), s_in),
        'rwkv_w2': nrm((RWKV_DECAY_LORA, D), 0.1 * RWKV_DECAY_LORA ** -0.5),
        'rwkv_a0': nrm((D,), 0.1),
        'rwkv_a1': nrm((D, RWKV_AAA_LORA), s_in),
        'rwkv_a2': nrm((RWKV_AAA_LORA, D), RWKV_AAA_LORA ** -0.5),
        'rwkv_g1': nrm((D, RWKV_GATE_LORA), s_in),
        'rwkv_g2': nrm((RWKV_GATE_LORA, D), RWKV_GATE_LORA ** -0.5),
        'rwkv_k_k': 0.85 + nrm((D,), 0.02),
        'rwkv_k_a': gain((D,)),
        'rwkv_r_k': nrm((RWKV_HEADS, RWKV_HEAD), 0.1),
        'rwkv_gn_g': gain((D,)),
        'rwkv_gn_b': nrm((D,), 0.02),
        'rwkv_w_out': nrm((D, D), s_out),
        'hgrn_w_in': nrm((D, 4 * D), s_in),
        'hgrn_lb': nrm((DEPTH, D), 0.1),
        'hgrn_norm_g': gain((D,)),
        'hgrn_w_out': nrm((D, D), s_out),
        'fox_w_in': nrm((D, 4 * D + FOX_HEADS), s_in),
        'fox_b_f': 2.0 + nrm((FOX_HEADS,), 0.1),
        'fox_q_norm_g': gain((FOX_HEAD,)),
        'fox_k_norm_g': gain((FOX_HEAD,)),
        'fox_w_out': nrm((D, D), s_out),
        'ffn0_w1': nrm((D, F), s_in),
        'ffn0_w3': nrm((D, F), s_in),
        'ffn0_w2': nrm((F, D), s_ff_out),
        'moe1_router': nrm((D, E), s_in),
        'moe1_router_b': nrm((E,), 0.01),
        'moe1_w1': nrm((E, D, F), s_in),
        'moe1_w3': nrm((E, D, F), s_in),
        'moe1_w2': nrm((E, F, D), s_ff_out),
        'ffn2_w1': nrm((D, F), s_in),
        'ffn2_w3': nrm((D, F), s_in),
        'ffn2_w2': nrm((F, D), s_ff_out),
        'moe3_router': nrm((D, E), s_in),
        'moe3_router_b': nrm((E,), 0.01),
        'moe3_w1': nrm((E, D, F), s_in),
        'moe3_w3': nrm((E, D, F), s_in),
        'moe3_w2': nrm((E, F, D), s_ff_out),
    }


def reference(x, meta, ln_mix_g, ln_mix_b, ln_ffn_g, ln_ffn_b,
              conv_w_in, conv_w, conv_b, conv_w_out,
              rwkv_mu, rwkv_w_r, rwkv_w_k, rwkv_w_v, rwkv_w0, rwkv_w1, rwkv_w2,
              rwkv_a0, rwkv_a1, rwkv_a2, rwkv_g1, rwkv_g2, rwkv_k_k, rwkv_k_a, rwkv_r_k,
              rwkv_gn_g, rwkv_gn_b, rwkv_w_out,
              hgrn_w_in, hgrn_lb, hgrn_norm_g, hgrn_w_out,
              fox_w_in, fox_b_f, fox_q_norm_g, fox_k_norm_g, fox_w_out,
              ffn0_w1, ffn0_w3, ffn0_w2,
              moe1_router, moe1_router_b, moe1_w1, moe1_w3, moe1_w2,
              ffn2_w1, ffn2_w3, ffn2_w2,
              moe3_router, moe3_router_b, moe3_w1, moe3_w3, moe3_w2):
    b = x.shape[0]
    h = jnp.concatenate([jnp.broadcast_to(meta[None].astype(x.dtype), (b, N_META, meta.shape[-1])), x], axis=1)

    mixers = [
        lambda z, i: short_conv_mixer(z, conv_w_in, conv_w, conv_b, conv_w_out),
        lambda z, i: rwkv7_mixer(z, rwkv_mu, rwkv_w_r, rwkv_w_k, rwkv_w_v, rwkv_w0, rwkv_w1, rwkv_w2,
                                 rwkv_a0, rwkv_a1, rwkv_a2, rwkv_g1, rwkv_g2, rwkv_k_k, rwkv_k_a,
                                 rwkv_r_k, rwkv_gn_g, rwkv_gn_b, rwkv_w_out),
        lambda z, i: hgrn2_mixer(z, hgrn_w_in, hgrn_lb, i, hgrn_norm_g, hgrn_w_out),
        lambda z, i: fox_mixer(z, fox_w_in, fox_b_f, fox_q_norm_g, fox_k_norm_g, fox_w_out),
    ]
    ffns = [
        lambda z: swiglu(z, ffn0_w1, ffn0_w3, ffn0_w2),
        lambda z: moe_swiglu(z, moe1_router, moe1_router_b, moe1_w1, moe1_w3, moe1_w2),
        lambda z: swiglu(z, ffn2_w1, ffn2_w3, ffn2_w2),
        lambda z: moe_swiglu(z, moe3_router, moe3_router_b, moe3_w1, moe3_w3, moe3_w2),
    ]
    for i in range(DEPTH):
        h = _layer_norm(DEEPNORM_ALPHA * h + mixers[i % N_MIXERS](h, i), ln_mix_g[i], ln_mix_b[i])
        h = _layer_norm(DEEPNORM_ALPHA * h + ffns[i](h), ln_ffn_g[i], ln_ffn_b[i])
    return h[:, N_META:]
```

```python
import functools

import jax
import jax.numpy as jnp
from jax import lax
from jax.experimental import pallas as pl
from jax.experimental.pallas import tpu as pltpu

F32 = jnp.float32
BF16 = jnp.bfloat16

N_META = 16
LN_EPS = 1e-5
RMS_EPS = 1e-6
RWKV_HEAD = 64
HGRN_HEAD = 128
FOX_HEAD = 128
TOP_K = 2
LANES = 128
VMEM_LIMIT_BYTES = 56 * 2**20


def _cparams(*sem):
    return pltpu.CompilerParams(dimension_semantics=sem, vmem_limit_bytes=VMEM_LIMIT_BYTES)


def _divisor_tile(n, cap, mult):
    best = None
    for d in range(mult, min(n, cap) + 1, mult):
        if n % d == 0:
            best = d
    assert best is not None, (n, cap, mult)
    return best


def _layer_norm(y, g, b):
    mu = jnp.mean(y, axis=-1, keepdims=True)
    yc = y - mu
    var = jnp.mean(yc * yc, axis=-1, keepdims=True)
    return yc * lax.rsqrt(var + LN_EPS) * g + b


def _sigmoid(x):
    return 1.0 / (1.0 + jnp.exp(-x))


def _silu(x):
    return x * _sigmoid(x)


def _mm_res_ln_kernel(z_ref, w_ref, h_ref, g_ref, b_ref, o_ref, ob_ref, acc_ref, *, nk, alpha):
    k = pl.program_id(1)
    part = jnp.dot(z_ref[...], w_ref[...].astype(BF16), preferred_element_type=F32)

    @pl.when(k == 0)
    def _():
        acc_ref[...] = part

    @pl.when(k > 0)
    def _():
        acc_ref[...] += part

    @pl.when(k == nk - 1)
    def _():
        y = _layer_norm(alpha * h_ref[...] + acc_ref[...], g_ref[...], b_ref[...])
        o_ref[...] = y
        ob_ref[...] = y.astype(BF16)


def _mm_res_ln(z, w, h, g, b, alpha, tm):
    n, kdim = z.shape
    d = w.shape[1]
    tk = _divisor_tile(kdim, 512, LANES)
    nk = kdim // tk
    return pl.pallas_call(
        functools.partial(_mm_res_ln_kernel, nk=nk, alpha=alpha),
        grid=(n // tm, nk),
        in_specs=[
            pl.BlockSpec((tm, tk), lambda m, k: (m, k)),
            pl.BlockSpec((tk, d), lambda m, k: (k, 0)),
            pl.BlockSpec((tm, d), lambda m, k: (m, 0)),
            pl.BlockSpec((1, d), lambda m, k: (0, 0)),
            pl.BlockSpec((1, d), lambda m, k: (0, 0)),
        ],
        out_specs=[
            pl.BlockSpec((tm, d), lambda m, k: (m, 0)),
            pl.BlockSpec((tm, d), lambda m, k: (m, 0)),
        ],
        out_shape=[jax.ShapeDtypeStruct((n, d), F32), jax.ShapeDtypeStruct((n, d), BF16)],
        scratch_shapes=[pltpu.VMEM((tm, d), F32)],
        compiler_params=_cparams("parallel", "arbitrary"),
        name="mm_res_ln",
    )(z, w, h, g.reshape(1, d), b.reshape(1, d))


def _ffn_up_kernel(exp_ref, nact_ref, x_ref, w1_ref, w3_ref, o_ref, w1b_ref, w3b_ref):
    c = pl.program_id(1)
    prev = exp_ref[jnp.maximum(c - 1, 0)]
    new_weights = jnp.logical_or(c == 0, exp_ref[c] != prev)

    @pl.when(new_weights)
    def _():
        w1b_ref[...] = w1_ref[...].astype(BF16)
        w3b_ref[...] = w3_ref[...].astype(BF16)

    @pl.when(c < nact_ref[0])
    def _():
        x = x_ref[...]
        a = jnp.dot(x, w1b_ref[...], preferred_element_type=F32)
        bb = jnp.dot(x, w3b_ref[...], preferred_element_type=F32)
        o_ref[...] = (_silu(a) * bb).astype(BF16)

    @pl.when(c >= nact_ref[0])
    def _():
        o_ref[...] = jnp.zeros_like(o_ref)


def _ffn_up(x, w1, w3, chunk_expert, n_active, rows):
    p, d = x.shape
    f = w1.shape[2]
    tf = _divisor_tile(f, 512, LANES)
    grid_spec = pltpu.PrefetchScalarGridSpec(
        num_scalar_prefetch=2,
        grid=(f // tf, p // rows),
        in_specs=[
            pl.BlockSpec((rows, d), lambda j, c, e, na: (c, 0)),
            pl.BlockSpec((None, d, tf), lambda j, c, e, na: (e[c], 0, j)),
            pl.BlockSpec((None, d, tf), lambda j, c, e, na: (e[c], 0, j)),
        ],
        out_specs=pl.BlockSpec((rows, tf), lambda j, c, e, na: (c, j)),
        scratch_shapes=[pltpu.VMEM((d, tf), BF16), pltpu.VMEM((d, tf), BF16)],
    )
    return pl.pallas_call(
        _ffn_up_kernel,
        grid_spec=grid_spec,
        out_shape=jax.ShapeDtypeStruct((p, f), BF16),
        compiler_params=_cparams("arbitrary", "arbitrary"),
        name="ffn_up",
    )(chunk_expert, n_active, x, w1, w3)


def _dense_ffn(hb, h, w1, w3, w2, g, b, alpha, tm):
    n = hb.shape[0]
    nchunks = n // tm
    hmid = _ffn_up(hb, w1[None], w3[None], jnp.zeros((nchunks,), jnp.int32),
                   jnp.full((1,), nchunks, jnp.int32), tm)
    return _mm_res_ln(hmid, w2, h, g, b, alpha, tm)


def _conv_proj_kernel(x_ref, wb_ref, wc_ref, wh_ref, cw_ref, cb_ref, o_ref,
                      wbb_ref, wcb_ref, whb_ref, carry_ref):
    bi = pl.program_id(1)
    ti = pl.program_id(2)

    @pl.when(jnp.logical_and(bi == 0, ti == 0))
    def _():
        wbb_ref[...] = wb_ref[...].astype(BF16)
        wcb_ref[...] = wc_ref[...].astype(BF16)
        whb_ref[...] = wh_ref[...].astype(BF16)

    @pl.when(ti == 0)
    def _():
        carry_ref[...] = jnp.zeros_like(carry_ref)

    x = x_ref[...]
    gate_b = jnp.dot(x, wbb_ref[...], preferred_element_type=F32)
    gate_c = jnp.dot(x, wcb_ref[...], preferred_element_type=F32)
    hh = jnp.dot(x, whb_ref[...], preferred_element_type=F32)
    u = gate_c * hh
    tm = u.shape[0]
    prev1 = carry_ref[7:8, :]
    prev2 = carry_ref[6:7, :]
    row = lax.broadcasted_iota(jnp.int32, (tm, 1), 0)
    r1 = jnp.where(row == 0, prev1, pltpu.roll(u, 1, axis=0))
    r2 = jnp.where(row == 0, prev2, jnp.where(row == 1, prev1, pltpu.roll(u, 2, axis=0)))
    v = cw_ref[0:1, :] * r2 + cw_ref[1:2, :] * r1 + cw_ref[2:3, :] * u + cb_ref[...]
    carry_ref[...] = u[tm - 8:, :]
    o_ref[...] = (gate_b * v).astype(BF16)


def _conv_proj(hb, w_in, conv_w, conv_b, batch, tm):
    n, d = hb.shape
    tn = _divisor_tile(d, 512, LANES)
    nd = d // tn
    tpb = n // batch // tm
    return pl.pallas_call(
        _conv_proj_kernel,
        grid=(nd, batch, tpb),
        in_specs=[
            pl.BlockSpec((tm, d), lambda j, bi, ti: (bi * tpb + ti, 0)),
            pl.BlockSpec((d, tn), lambda j, bi, ti: (0, j)),
            pl.BlockSpec((d, tn), lambda j, bi, ti: (0, nd + j)),
            pl.BlockSpec((d, tn), lambda j, bi, ti: (0, 2 * nd + j)),
            pl.BlockSpec((3, tn), lambda j, bi, ti: (0, j)),
            pl.BlockSpec((1, tn), lambda j, bi, ti: (0, j)),
        ],
        out_specs=pl.BlockSpec((tm, tn), lambda j, bi, ti: (bi * tpb + ti, j)),
        out_shape=jax.ShapeDtypeStruct((n, d), BF16),
        scratch_shapes=[pltpu.VMEM((d, tn), BF16)] * 3 + [pltpu.VMEM((8, tn), F32)],
        compiler_params=_cparams("arbitrary", "arbitrary", "arbitrary"),
        name="conv_proj",
    )(hb, w_in, w_in, w_in, conv_w, conv_b.reshape(1, d))


def _conv_mixer_layer(h, hb, p, g, b, alpha, batch, tm):
    z = _conv_proj(hb, p["conv_w_in"], p["conv_w"], p["conv_b"], batch, tm)
    return _mm_res_ln(z, p["conv_w_out"], h, g, b, alpha, tm)


MOE_ROWS = 256


def _split_bf16(a):
    hi = a.astype(BF16)
    lo = (a - hi.astype(F32)).astype(BF16)
    return hi, lo


def _dot_f32(a, b):
    ah, al = _split_bf16(a)
    bh, bl = _split_bf16(b)
    return (jnp.dot(ah, bh, preferred_element_type=F32)
            + (jnp.dot(ah, bl, preferred_element_type=F32)
               + jnp.dot(al, bh, preferred_element_type=F32)))


def _router_kernel(h_ref, w_ref, b_ref, info_ref, cnt_ref, carry_ref):
    i = pl.program_id(0)

    @pl.when(i == 0)
    def _():
        carry_ref[...] = jnp.zeros_like(carry_ref)

    logits = _dot_f32(h_ref[...], w_ref[...]) + b_ref[...]
    tm, ne = logits.shape
    lane = lax.broadcasted_iota(jnp.int32, (tm, ne), 1)
    m1 = jnp.max(logits, axis=-1, keepdims=True)
    i1 = jnp.min(jnp.where(logits == m1, lane, ne), axis=-1, keepdims=True)
    mask1 = lane == i1
    rest = jnp.where(mask1, -jnp.inf, logits)
    m2 = jnp.max(rest, axis=-1, keepdims=True)
    i2 = jnp.min(jnp.where(rest == m2, lane, ne), axis=-1, keepdims=True)
    mask2 = lane == i2
    dd = jnp.exp(m2 - m1)
    g1 = 1.0 / (1.0 + dd)
    g2 = dd / (1.0 + dd)
    sel = jnp.where(jnp.logical_or(mask1, mask2), 1.0, 0.0)
    r_i = lax.broadcasted_iota(jnp.int32, (tm, tm), 0)
    c_i = lax.broadcasted_iota(jnp.int32, (tm, tm), 1)
    tril = jnp.where(c_i < r_i, 1.0, 0.0).astype(BF16)
    rank = jnp.dot(tril, sel.astype(BF16), preferred_element_type=F32) + carry_ref[...]
    r1 = jnp.sum(jnp.where(mask1, rank, 0.0), axis=-1, keepdims=True)
    r2 = jnp.sum(jnp.where(mask2, rank, 0.0), axis=-1, keepdims=True)
    info = jnp.where(lane == 0, i1.astype(F32),
           jnp.where(lane == 1, i2.astype(F32),
           jnp.where(lane == 2, g1,
           jnp.where(lane == 3, g2,
           jnp.where(lane == 4, r1,
           jnp.where(lane == 5, r2, 0.0))))))
    info_ref[...] = info
    total = carry_ref[...] + jnp.sum(sel, axis=0, keepdims=True)
    carry_ref[...] = total
    cnt_ref[...] = total


def _router(h, w, b, tm):
    n, d = h.shape
    ne = w.shape[1]
    assert ne >= 6
    return pl.pallas_call(
        _router_kernel,
        grid=(n // tm,),
        in_specs=[
            pl.BlockSpec((tm, d), lambda i: (i, 0)),
            pl.BlockSpec((d, ne), lambda i: (0, 0)),
            pl.BlockSpec((1, ne), lambda i: (0, 0)),
        ],
        out_specs=[
            pl.BlockSpec((tm, ne), lambda i: (i, 0)),
            pl.BlockSpec((1, ne), lambda i: (0, 0)),
        ],
        out_shape=[jax.ShapeDtypeStruct((n, ne), F32), jax.ShapeDtypeStruct((1, ne), F32)],
        scratch_shapes=[pltpu.VMEM((1, ne), F32)],
        compiler_params=_cparams("arbitrary"),
        name="moe_router",
    )(h, w, b.reshape(1, ne))


def _row_copy(src_hbm, row, dst_vmem, r, sem):
    return pltpu.make_async_copy(src_hbm.at[pl.ds(row, 1), :], dst_vmem.at[pl.ds(r, 1), :], sem)


def _moe_gather_kernel(src_ref, nact_ref, h_ref, o_ref, buf_ref, sem):
    c = pl.program_id(0)
    rows = buf_ref.shape[0]

    @pl.when(c < nact_ref[0])
    def _():
        def start(r, carry):
            _row_copy(h_ref, src_ref[c * rows + r], buf_ref, r, sem).start()
            return carry

        lax.fori_loop(0, rows, start, 0)

        def wait(r, carry):
            _row_copy(h_ref, 0, buf_ref, r, sem).wait()
            return carry

        lax.fori_loop(0, rows, wait, 0)
        o_ref[...] = buf_ref[...].astype(BF16)

    @pl.when(c >= nact_ref[0])
    def _():
        o_ref[...] = jnp.zeros_like(o_ref)


def _moe_gather(h, src, n_active, p, rows):
    n, d = h.shape
    grid_spec = pltpu.PrefetchScalarGridSpec(
        num_scalar_prefetch=2,
        grid=(p // rows,),
        in_specs=[pl.BlockSpec(memory_space=pl.ANY)],
        out_specs=pl.BlockSpec((rows, d), lambda c, s, na: (c, 0)),
        scratch_shapes=[pltpu.VMEM((rows, d), F32), pltpu.SemaphoreType.DMA],
    )
    return pl.pallas_call(
        _moe_gather_kernel,
        grid_spec=grid_spec,
        out_shape=jax.ShapeDtypeStruct((p, d), BF16),
        compiler_params=_cparams("arbitrary"),
        name="moe_gather",
    )(src, n_active, h)


def _ffn_down_kernel(exp_ref, nact_ref, x_ref, w_ref, o_ref, wb_ref):
    c = pl.program_id(1)
    prev = exp_ref[jnp.maximum(c - 1, 0)]
    new_weights = jnp.logical_or(c == 0, exp_ref[c] != prev)

    @pl.when(new_weights)
    def _():
        wb_ref[...] = w_ref[...].astype(BF16)

    @pl.when(c < nact_ref[0])
    def _():
        o_ref[...] = jnp.dot(x_ref[...], wb_ref[...], preferred_element_type=F32)

    @pl.when(c >= nact_ref[0])
    def _():
        o_ref[...] = jnp.zeros_like(o_ref)


def _ffn_down(x, w2, chunk_expert, n_active, rows):
    p, f = x.shape
    d = w2.shape[2]
    tn = _divisor_tile(d, 512, LANES)
    grid_spec = pltpu.PrefetchScalarGridSpec(
        num_scalar_prefetch=2,
        grid=(d // tn, p // rows),
        in_specs=[
            pl.BlockSpec((rows, f), lambda j, c, e, na: (c, 0)),
            pl.BlockSpec((None, f, tn), lambda j, c, e, na: (e[c], 0, j)),
        ],
        out_specs=pl.BlockSpec((rows, tn), lambda j, c, e, na: (c, j)),
        scratch_shapes=[pltpu.VMEM((f, tn), BF16)],
    )
    return pl.pallas_call(
        _ffn_down_kernel,
        grid_spec=grid_spec,
        out_shape=jax.ShapeDtypeStruct((p, d), F32),
        compiler_params=_cparams("arbitrary", "arbitrary"),
        name="ffn_down",
    )(chunk_expert, n_active, x, w2)


def _moe_combine_kernel(p1_ref, p2_ref, y_ref, h_ref, info_ref, g_ref, b_ref, o_ref, ob_ref,
                        buf1_ref, buf2_ref, sem, *, alpha):
    i = pl.program_id(0)
    tm = buf1_ref.shape[0]

    def start(r, carry):
        _row_copy(y_ref, p1_ref[i * tm + r], buf1_ref, r, sem).start()
        _row_copy(y_ref, p2_ref[i * tm + r], buf2_ref, r, sem).start()
        return carry

    lax.fori_loop(0, tm, start, 0)

    def wait(r, carry):
        _row_copy(y_ref, 0, buf1_ref, r, sem).wait()
        _row_copy(y_ref, 0, buf2_ref, r, sem).wait()
        return carry

    lax.fori_loop(0, tm, wait, 0)
    info = info_ref[...]
    y = alpha * h_ref[...] + (info[:, 2:3] * buf1_ref[...] + info[:, 3:4] * buf2_ref[...])
    y = _layer_norm(y, g_ref[...], b_ref[...])
    o_ref[...] = y
    ob_ref[...] = y.astype(BF16)


def _moe_combine(y, h, info, p1, p2, g, b, alpha, tm):
    n, d = h.shape
    ne = info.shape[1]
    grid_spec = pltpu.PrefetchScalarGridSpec(
        num_scalar_prefetch=2,
        grid=(n // tm,),
        in_specs=[
            pl.BlockSpec(memory_space=pl.ANY),
            pl.BlockSpec((tm, d), lambda i, a, c: (i, 0)),
            pl.BlockSpec((tm, ne), lambda i, a, c: (i, 0)),
            pl.BlockSpec((1, d), lambda i, a, c: (0, 0)),
            pl.BlockSpec((1, d), lambda i, a, c: (0, 0)),
        ],
        out_specs=[
            pl.BlockSpec((tm, d), lambda i, a, c: (i, 0)),
            pl.BlockSpec((tm, d), lambda i, a, c: (i, 0)),
        ],
        scratch_shapes=[pltpu.VMEM((tm, d), F32), pltpu.VMEM((tm, d), F32), pltpu.SemaphoreType.DMA],
    )
    return pl.pallas_call(
        functools.partial(_moe_combine_kernel, alpha=alpha),
        grid_spec=grid_spec,
        out_shape=[jax.ShapeDtypeStruct((n, d), F32), jax.ShapeDtypeStruct((n, d), BF16)],
        compiler_params=_cparams("arbitrary"),
        name="moe_combine",
    )(p1, p2, y, h, info, g.reshape(1, d), b.reshape(1, d))


def _moe_ffn(h, hb, router_w, router_b, w1, w3, w2, g, b, alpha, tm):
    n, d = h.shape
    ne = router_w.shape[1]
    rows = MOE_ROWS
    info, counts = _router(h, router_w, router_b, tm)
    counts = counts[0].astype(jnp.int32)
    nchunk_e = (counts + rows - 1) // rows
    chunk_end = jnp.cumsum(nchunk_e)
    starts = (chunk_end - nchunk_e) * rows
    n_chunks = (n * TOP_K + ne * (rows - 1)) // rows
    p = n_chunks * rows
    n_active = chunk_end[-1:].astype(jnp.int32)
    chunk_expert = jnp.minimum(
        jnp.sum(jnp.arange(n_chunks, dtype=jnp.int32)[:, None] >= chunk_end[None, :], axis=1), ne - 1
    ).astype(jnp.int32)
    i1 = info[:, 0].astype(jnp.int32)
    i2 = info[:, 1].astype(jnp.int32)
    p1 = starts[i1] + info[:, 4].astype(jnp.int32)
    p2 = starts[i2] + info[:, 5].astype(jnp.int32)
    tok = jnp.arange(n, dtype=jnp.int32)
    src = jnp.zeros((p,), jnp.int32).at[p1].set(tok).at[p2].set(tok)
    xs = _moe_gather(h, src, n_active, p, rows)
    hmid = _ffn_up(xs, w1, w3, chunk_expert, n_active, rows)
    y = _ffn_down(hmid, w2, chunk_expert, n_active, rows)
    tmc = _divisor_tile(n, 384, 16)
    return _moe_combine(y, h, info, p1, p2, g, b, alpha, tmc)


def _proj_kernel(*refs, epilogue, n_extra, n_out):
    x_ref, w_ref = refs[0], refs[1]
    extra = refs[2:2 + n_extra]
    outs = refs[2 + n_extra:2 + n_extra + n_out]
    wb_ref = refs[2 + n_extra + n_out]

    @pl.when(pl.program_id(1) == 0)
    def _():
        wb_ref[...] = w_ref[...].astype(BF16)

    y = jnp.dot(x_ref[...], wb_ref[...], preferred_element_type=F32)
    res = epilogue(y, *[e[...] for e in extra])
    for o_ref, r in zip(outs, res):
        o_ref[...] = r.astype(o_ref.dtype)


def _proj(xb, w, col0, ncols, epilogue, extras, out_dtypes, tm, name, tn_cap=512):
    n, kdim = xb.shape
    tn = _divisor_tile(ncols, tn_cap, LANES)
    assert col0 % tn == 0
    off = col0 // tn
    outs = pl.pallas_call(
        functools.partial(_proj_kernel, epilogue=epilogue, n_extra=len(extras), n_out=len(out_dtypes)),
        grid=(ncols // tn, n // tm),
        in_specs=[
            pl.BlockSpec((tm, kdim), lambda j, m: (m, 0)),
            pl.BlockSpec((kdim, tn), lambda j, m: (0, off + j)),
        ] + [pl.BlockSpec((1, tn), lambda j, m: (0, j))] * len(extras),
        out_specs=[pl.BlockSpec((tm, tn), lambda j, m: (m, j))] * len(out_dtypes),
        out_shape=[jax.ShapeDtypeStruct((n, ncols), dt) for dt in out_dtypes],
        scratch_shapes=[pltpu.VMEM((kdim, tn), BF16)],
        compiler_params=_cparams("arbitrary", "arbitrary"),
        name=name,
    )(xb, w, *[e.reshape(1, ncols) for e in extras])
    return outs


def _tril_bf16(c, inclusive):
    r_i = lax.broadcasted_iota(jnp.int32, (c, c), 0)
    c_i = lax.broadcasted_iota(jnp.int32, (c, c), 1)
    keep = (c_i <= r_i) if inclusive else (c_i < r_i)
    return jnp.where(keep, 1.0, 0.0).astype(BF16)


def _cumsum_rows(x, tril):
    hi = x.astype(BF16)
    r1 = x - hi.astype(F32)
    mid = r1.astype(BF16)
    lo = (r1 - mid.astype(F32)).astype(BF16)
    return (jnp.dot(tril, hi, preferred_element_type=F32)
            + (jnp.dot(tril, mid, preferred_element_type=F32)
               + jnp.dot(tril, lo, preferred_element_type=F32)))


def _dot_nt(a, b):
    return lax.dot_general(a.astype(BF16), b.astype(BF16), (((1,), (1,)), ((), ())),
                           preferred_element_type=F32)


def _dot_tn(a, b):
    return lax.dot_general(a.astype(BF16), b.astype(BF16), (((0,), (0,)), ((), ())),
                           preferred_element_type=F32)


def _dot_nn(a, b):
    return jnp.dot(a.astype(BF16), b.astype(BF16), preferred_element_type=F32)


def _chunk_len(t):
    return _divisor_tile(t, 64, 16)


HGRN_SUB = 16


def _hgrn_scan_kernel(q_ref, lf_ref, k_ref, v_ref, gs_ref, ng_ref, o_ref, st_ref, *, chunk):
    t = q_ref.shape[0]
    dk = q_ref.shape[1]
    nsub = chunk // HGRN_SUB
    st_ref[...] = jnp.zeros_like(st_ref)
    tril = _tril_bf16(chunk, True)
    row16 = lax.broadcasted_iota(jnp.int32, (HGRN_SUB, 1), 0)

    def body(c, carry):
        start = pl.multiple_of(c * chunk, 16)
        rows = pl.ds(start, chunk)
        q = q_ref[rows, :]
        k = k_ref[rows, :]
        v = v_ref[rows, :]
        cum = _cumsum_rows(lf_ref[rows, :], tril)
        st = st_ref[...]
        o_inter = _dot_nt(q * jnp.exp(cum), st)
        vb = v.astype(BF16)
        outs = []
        for i in range(nsub):
            lo, hi = i * HGRN_SUB, (i + 1) * HGRN_SUB
            qi, ki, vi, cumi = q[lo:hi], k[lo:hi], v[lo:hi], cum[lo:hi]
            oi = o_inter[lo:hi]
            if i > 0:
                ci = cum[lo:lo + 1]
                qt = qi * jnp.exp(cumi - ci)
                kt = k[0:lo] * jnp.exp(ci - cum[0:lo])
                oi = oi + jnp.dot(_dot_nt(qt, kt).astype(BF16), vb[0:lo], preferred_element_type=F32)
            for s in range(HGRN_SUB):
                dec = jnp.exp(jnp.minimum(cumi - cumi[s:s + 1], 0.0))
                col = jnp.sum(qi * dec * ki[s:s + 1], axis=-1, keepdims=True)
                col = jnp.where(row16 >= s, col, 0.0)
                oi = oi + col * vi[s:s + 1]
            outs.append(oi)
        o = jnp.concatenate(outs, axis=0)
        cl = cum[chunk - 1:chunk]
        kd = k * jnp.exp(cl - cum)
        st_ref[...] = st * jnp.exp(cl) + _dot_tn(v, kd)
        o = o * lax.rsqrt(jnp.mean(o * o, axis=-1, keepdims=True) + RMS_EPS)
        o_ref[rows, :] = (o * ng_ref[...] * gs_ref[rows, :]).astype(BF16)
        return carry

    lax.fori_loop(0, t // chunk, body, 0)


def _hgrn_scan(q, lf, k, v, gs, norm_g, batch):
    n, d = q.shape
    t = n // batch
    chunk = _chunk_len(t)
    nh = d // HGRN_HEAD
    blk = pl.BlockSpec((t, HGRN_HEAD), lambda b, h: (b, h))
    return pl.pallas_call(
        functools.partial(_hgrn_scan_kernel, chunk=chunk),
        grid=(batch, nh),
        in_specs=[blk, blk, blk, blk, blk, pl.BlockSpec((1, HGRN_HEAD), lambda b, h: (0, h))],
        out_specs=blk,
        out_shape=jax.ShapeDtypeStruct((n, d), BF16),
        scratch_shapes=[pltpu.VMEM((HGRN_HEAD, HGRN_HEAD), F32)],
        compiler_params=_cparams("parallel", "parallel"),
        name="hgrn_scan",
    )(q, lf, k, v, gs, norm_g.reshape(1, d))


def _hgrn_gate_epilogue(y, lb):
    sig = _sigmoid(y)
    log_f = jnp.log(lb + (1.0 - lb) * sig)
    k = (1.0 - lb) * _sigmoid(-y)
    return log_f, k


def _hgrn_mixer_layer(h, hb, p, layer_idx, g, b, alpha, batch, tm):
    d = h.shape[1]
    w_in = p["hgrn_w_in"]
    lb = jnp.cumsum(jax.nn.softmax(p["hgrn_lb"].astype(F32), axis=0), axis=0)
    lb = lb[layer_idx] - lb[0]
    (q,) = _proj(hb, w_in, 0, d, lambda y: (_silu(y),), [], [F32], tm, "hgrn_proj_q")
    lf, k = _proj(hb, w_in, d, d, _hgrn_gate_epilogue, [lb], [F32, F32], tm, "hgrn_proj_f")
    (v,) = _proj(hb, w_in, 2 * d, d, lambda y: (y,), [], [F32], tm, "hgrn_proj_i")
    (gs,) = _proj(hb, w_in, 3 * d, d, lambda y: (_silu(y),), [], [F32], tm, "hgrn_proj_g")
    z = _hgrn_scan(q, lf, k, v, gs, p["hgrn_norm_g"], batch)
    return _mm_res_ln(z, p["hgrn_w_out"], h, g, b, alpha, tm)


def _fox_gate_kernel(h_ref, w_ref, bf_ref, c_ref, carry_ref):
    @pl.when(pl.program_id(1) == 0)
    def _():
        carry_ref[...] = jnp.zeros_like(carry_ref)

    x = _dot_f32(h_ref[...], w_ref[...]) + bf_ref[...]
    log_f = jnp.minimum(x, 0.0) - jnp.log(1.0 + jnp.exp(-jnp.abs(x)))
    tm = x.shape[0]
    c = _cumsum_rows(log_f, _tril_bf16(tm, True)) + carry_ref[...]
    c_ref[...] = c
    carry_ref[...] = c[tm - 1:tm, :]


def _fox_gate(h, w_f, b_f, batch, tm):
    n, d = h.shape
    nh = w_f.shape[1]
    tpb = n // batch // tm
    return pl.pallas_call(
        _fox_gate_kernel,
        grid=(batch, tpb),
        in_specs=[
            pl.BlockSpec((tm, d), lambda b, t: (b * tpb + t, 0)),
            pl.BlockSpec((d, nh), lambda b, t: (0, 0)),
            pl.BlockSpec((1, nh), lambda b, t: (0, 0)),
        ],
        out_specs=pl.BlockSpec((tm, nh), lambda b, t: (b * tpb + t, 0)),
        out_shape=jax.ShapeDtypeStruct((n, nh), F32),
        scratch_shapes=[pltpu.VMEM((1, nh), F32)],
        compiler_params=_cparams("arbitrary", "arbitrary"),
        name="fox_gate",
    )(h, w_f, b_f.reshape(1, nh))


def _fox_attn_kernel(q_ref, k_ref, v_ref, sg_ref, c_ref, ct_ref, o_ref, *, tq):
    hd = pl.program_id(1)
    t = q_ref.shape[0]
    nh = c_ref.shape[1]
    lane = lax.broadcasted_iota(jnp.int32, (t, nh), 1)
    c_col = jnp.sum(jnp.where(lane == hd, c_ref[...], 0.0), axis=-1, keepdims=True)
    c_row = ct_ref[pl.ds(hd, 1), :]
    for i in range(t // tq):
        lo, hi = i * tq, (i + 1) * tq
        s = lax.dot_general(q_ref[lo:hi, :], k_ref[0:hi, :], (((1,), (1,)), ((), ())),
                            preferred_element_type=F32)
        s = s + c_col[lo:hi] - c_row[:, 0:hi]
        r_i = lax.broadcasted_iota(jnp.int32, (tq, hi), 0) + lo
        c_i = lax.broadcasted_iota(jnp.int32, (tq, hi), 1)
        s = jnp.where(c_i <= r_i, s, -jnp.inf)
        m = jnp.max(s, axis=-1, keepdims=True)
        p = jnp.exp(s - m)
        l = jnp.sum(p, axis=-1, keepdims=True)
        o = jnp.dot(p.astype(BF16), v_ref[0:hi, :], preferred_element_type=F32) / l
        o_ref[lo:hi, :] = (o * sg_ref[lo:hi, :]).astype(BF16)


def _fox_attn(q, k, v, sg, c, ct, batch):
    n, d = q.shape
    t = n // batch
    nh = d // FOX_HEAD
    tq = _divisor_tile(t, 768, 16)
    blk = pl.BlockSpec((t, FOX_HEAD), lambda b, h: (b, h))
    return pl.pallas_call(
        functools.partial(_fox_attn_kernel, tq=tq),
        grid=(batch, nh),
        in_specs=[blk, blk, blk, blk,
                  pl.BlockSpec((t, nh), lambda b, h: (b, 0)),
                  pl.BlockSpec((None, nh, t), lambda b, h: (b, 0, 0))],
        out_specs=blk,
        out_shape=jax.ShapeDtypeStruct((n, d), BF16),
        compiler_params=_cparams("parallel", "parallel"),
        name="fox_attn",
    )(q, k, v, sg, c, ct)


def _head_rms_epilogue(scale):
    def epi(y, gain):
        outs = []
        for j in range(y.shape[1] // FOX_HEAD):
            yj = y[:, j * FOX_HEAD:(j + 1) * FOX_HEAD]
            yj = yj * lax.rsqrt(jnp.mean(yj * yj, axis=-1, keepdims=True) + RMS_EPS)
            outs.append(yj * gain[:, j * FOX_HEAD:(j + 1) * FOX_HEAD] * scale)
        return (jnp.concatenate(outs, axis=1),)
    return epi


def _fox_mixer_layer(h, hb, p, g, b, alpha, batch, tm):
    n, d = h.shape
    nh = d // FOX_HEAD
    w_in = p["fox_w_in"]
    qg = jnp.tile(p["fox_q_norm_g"], nh)
    kg = jnp.tile(p["fox_k_norm_g"], nh)
    (q,) = _proj(hb, w_in, 0, d, _head_rms_epilogue(FOX_HEAD ** -0.5), [qg], [BF16], tm, "fox_proj_q")
    (k,) = _proj(hb, w_in, d, d, _head_rms_epilogue(1.0), [kg], [BF16], tm, "fox_proj_k")
    (v,) = _proj(hb, w_in, 2 * d, d, lambda y: (y,), [], [BF16], tm, "fox_proj_v")
    (sg,) = _proj(hb, w_in, 3 * d, d, lambda y: (_sigmoid(y),), [], [F32], tm, "fox_proj_g")
    c = _fox_gate(h, w_in[:, 4 * d:], p["fox_b_f"], batch, tm)
    ct = c.reshape(batch, n // batch, nh).transpose(0, 2, 1)
    z = _fox_attn(q, k, v, sg, c, ct, batch)
    return _mm_res_ln(z, p["fox_w_out"], h, g, b, alpha, tm)


def _rwkv_mix_kernel(h_ref, mu_ref, *refs):
    outs, carry_ref = refs[:-1], refs[-1]

    @pl.when(pl.program_id(1) == 0)
    def _():
        carry_ref[...] = jnp.zeros_like(carry_ref)

    x = h_ref[...]
    tm = x.shape[0]
    row = lax.broadcasted_iota(jnp.int32, (tm, 1), 0)
    prev = jnp.where(row == 0, carry_ref[7:8, :], pltpu.roll(x, 1, axis=0))
    xx = prev - x
    carry_ref[...] = x[tm - 8:, :]
    for j, o_ref in enumerate(outs):
        o_ref[...] = (x + xx * mu_ref[j:j + 1, :]).astype(BF16)


def _rwkv_mix(h, mu, batch, tm):
    n, d = h.shape
    nmix = mu.shape[0]
    tpb = n // batch // tm
    blk = pl.BlockSpec((tm, d), lambda b, t: (b * tpb + t, 0))
    return pl.pallas_call(
        _rwkv_mix_kernel,
        grid=(batch, tpb),
        in_specs=[blk, pl.BlockSpec((nmix, d), lambda b, t: (0, 0))],
        out_specs=[blk] * nmix,
        out_shape=[jax.ShapeDtypeStruct((n, d), BF16)] * nmix,
        scratch_shapes=[pltpu.VMEM((8, d), F32)],
        compiler_params=_cparams("arbitrary", "arbitrary"),
        name="rwkv_mix",
    )(h, mu)


def _lora_kernel(x_ref, wa_ref, wb_ref, bias_ref, o_ref, wab_ref, wbb_ref, *, mid_act, out_act):
    @pl.when(pl.program_id(0) == 0)
    def _():
        wab_ref[...] = wa_ref[...].astype(BF16)
        wbb_ref[...] = wb_ref[...].astype(BF16)

    mid = mid_act(jnp.dot(x_ref[...], wab_ref[...], preferred_element_type=F32))
    y = jnp.dot(mid.astype(BF16), wbb_ref[...], preferred_element_type=F32)
    o_ref[...] = out_act(bias_ref[...] + y)


def _lora(xb, wa, wb, bias, mid_act, out_act, tm, name):
    n, d = xb.shape
    r = wa.shape[1]
    dout = wb.shape[1]
    return pl.pallas_call(
        functools.partial(_lora_kernel, mid_act=mid_act, out_act=out_act),
        grid=(n // tm,),
        in_specs=[
            pl.BlockSpec((tm, d), lambda i: (i, 0)),
            pl.BlockSpec((d, r), lambda i: (0, 0)),
            pl.BlockSpec((r, dout), lambda i: (0, 0)),
            pl.BlockSpec((1, dout), lambda i: (0, 0)),
        ],
        out_specs=pl.BlockSpec((tm, dout), lambda i: (i, 0)),
        out_shape=jax.ShapeDtypeStruct((n, dout), F32),
        scratch_shapes=[pltpu.VMEM((d, r), BF16), pltpu.VMEM((r, dout), BF16)],
        compiler_params=_cparams("arbitrary"),
        name=name,
    )(xb, wa, wb, bias.reshape(1, dout))


def _rwkv_log_decay(z):
    w_log = -(jnp.maximum(-z, 0.0) + jnp.log(1.0 + jnp.exp(-jnp.abs(z)))) - 0.5
    return -jnp.exp(w_log)


RWKV_GROUP = 4


def _seg_sum(x, bd):
    hi = x.astype(BF16)
    lo = (x - hi.astype(F32)).astype(BF16)
    return jnp.dot(hi, bd, preferred_element_type=F32) + jnp.dot(lo, bd, preferred_element_type=F32)


def _rwkv_scan_kernel(r_ref, kr_ref, v_ref, lw_ref, a_ref, g_ref, kk_p, ka_p, rk_p, gg_p, gb_p,
                      o_ref, kk_s, k_s, bonus_s, y_s, st_ref, *, chunk, ptile):
    t, w = r_ref.shape
    nhead = w // RWKV_HEAD
    sc = nhead * chunk
    lane_r = lax.broadcasted_iota(jnp.int32, (w, w), 0) // RWKV_HEAD
    lane_c = lax.broadcasted_iota(jnp.int32, (w, w), 1) // RWKV_HEAD
    bd = jnp.where(lane_r == lane_c, 1.0, 0.0).astype(BF16)

    def prologue(i, carry):
        rows = pl.ds(pl.multiple_of(i * ptile, 8), ptile)
        kr = kr_ref[rows, :]
        a = a_ref[rows, :]
        kkr = kr * kk_p[...]
        nrm = jnp.maximum(jnp.sqrt(_seg_sum(kkr * kkr, bd)), 1e-12)
        kk_s[rows, :] = kkr / nrm
        k = kr * (1.0 + (a - 1.0) * ka_p[...])
        k_s[rows, :] = k
        bonus_s[rows, :] = _seg_sum(r_ref[rows, :] * k * rk_p[...], bd) * v_ref[rows, :]
        return carry

    lax.fori_loop(0, t // ptile, prologue, 0)

    st_ref[...] = jnp.zeros_like(st_ref)
    tril = _tril_bf16(chunk, True)
    head_of_lane = lax.broadcasted_iota(jnp.int32, (chunk, w), 1) // RWKV_HEAD
    ri = lax.broadcasted_iota(jnp.int32, (2 * sc, sc), 0)
    ci = lax.broadcasted_iota(jnp.int32, (2 * sc, sc), 1)
    low_mask = ci < jnp.where(ri < sc, ri, ri - sc + 1)
    nsteps = max(1, (chunk - 1).bit_length())

    def stack(x):
        return jnp.concatenate([jnp.where(head_of_lane == hh, x, 0.0) for hh in range(nhead)], axis=0)

    def body(c, carry):
        rows = pl.ds(pl.multiple_of(c * chunk, 16), chunk)
        r = r_ref[rows, :]
        v = v_ref[rows, :]
        lw = lw_ref[rows, :]
        a = a_ref[rows, :]
        kk = kk_s[rows, :]
        k = k_s[rows, :]
        cum = _cumsum_rows(lw, tril)
        e_neg = jnp.exp(-cum)
        at2 = stack(-kk * jnp.exp(cum - lw))
        rt2 = stack(r * jnp.exp(cum))
        bvec = kk * a
        bb2 = stack(bvec * e_neg)
        kb2 = stack(k * e_neg)
        v2 = stack(v)
        st = st_ref[...]
        ar2 = jnp.concatenate([at2, rt2], axis=0)
        pb = jnp.where(low_mask, _dot_nt(ar2, bb2), 0.0)
        pk = jnp.where(low_mask, _dot_nt(ar2, kb2), 0.0)
        ps = _dot_nt(ar2, st)
        m_ab, m_rb = pb[:sc], pb[sc:]
        m_ak, m_rk = pk[:sc], pk[sc:]
        u2 = ps[:sc] + _dot_nn(m_ak, v2)
        lpow = m_ab
        for step in range(nsteps):
            u2 = u2 + _dot_nn(lpow, u2)
            if step + 1 < nsteps:
                lpow = _dot_nn(lpow, lpow)
        y2 = ps[sc:] + _dot_nn(m_rk, v2) + _dot_nn(m_rb, u2)
        y = y2[0:chunk]
        for hh in range(1, nhead):
            y = y + y2[hh * chunk:(hh + 1) * chunk]
        y_s[rows, :] = y
        cl = cum[chunk - 1:chunk]
        e_end = jnp.exp(cl - cum)
        kh2 = stack(k * e_end)
        bh2 = stack(bvec * e_end)
        st_ref[...] = (st * jnp.exp(cl)
                       + _dot_tn(jnp.concatenate([v2, u2], axis=0), jnp.concatenate([kh2, bh2], axis=0)))
        return carry

    lax.fori_loop(0, t // chunk, body, 0)

    inv = 1.0 / RWKV_HEAD

    def epilogue(i, carry):
        rows = pl.ds(pl.multiple_of(i * ptile, 8), ptile)
        y = y_s[rows, :]
        mu = _seg_sum(y, bd) * inv
        yc = y - mu
        var = _seg_sum(yc * yc, bd) * inv
        yn = yc * lax.rsqrt(var + 1e-5 * RWKV_HEAD) * gg_p[...] + gb_p[...]
        o_ref[rows, :] = ((yn + bonus_s[rows, :]) * g_ref[rows, :]).astype(BF16)
        return carry

    lax.fori_loop(0, t // ptile, epilogue, 0)


def _rwkv_scan(r, kr, v, lw, a, g, p, batch):
    n, d = r.shape
    t = n // batch
    chunk = _chunk_len(t)
    w = min(d, RWKV_GROUP * RWKV_HEAD)
    ptile = _divisor_tile(t, 768, 16)
    blk = pl.BlockSpec((t, w), lambda b, j: (b, j))
    prm = pl.BlockSpec((1, w), lambda b, j: (0, j))
    params = [p["rwkv_k_k"], p["rwkv_k_a"], p["rwkv_r_k"], p["rwkv_gn_g"], p["rwkv_gn_b"]]
    return pl.pallas_call(
        functools.partial(_rwkv_scan_kernel, chunk=chunk, ptile=ptile),
        grid=(batch, d // w),
        in_specs=[blk] * 6 + [prm] * 5,
        out_specs=blk,
        out_shape=jax.ShapeDtypeStruct((n, d), BF16),
        scratch_shapes=[pltpu.VMEM((t, w), F32)] * 4 + [pltpu.VMEM((w, w), F32)],
        compiler_params=_cparams("parallel", "parallel"),
        name="rwkv_scan",
    )(r, kr, v, lw, a, g, *[x.reshape(1, d) for x in params])


def _rwkv_mixer_layer(h, hb, p, g, b, alpha, batch, tm):
    n, d = h.shape
    ident = lambda y: y
    xr, xw, xk, xv, xa, xg = _rwkv_mix(h, p["rwkv_mu"], batch, tm)
    (r,) = _proj(xr, p["rwkv_w_r"], 0, d, lambda y: (y,), [], [F32], tm, "rwkv_proj_r")
    (kr,) = _proj(xk, p["rwkv_w_k"], 0, d, lambda y: (y,), [], [F32], tm, "rwkv_proj_k")
    (v,) = _proj(xv, p["rwkv_w_v"], 0, d, lambda y: (y,), [], [F32], tm, "rwkv_proj_v")
    lw = _lora(xw, p["rwkv_w1"], p["rwkv_w2"], p["rwkv_w0"], jnp.tanh, _rwkv_log_decay, tm, "rwkv_lora_w")
    a = _lora(xa, p["rwkv_a1"], p["rwkv_a2"], p["rwkv_a0"], ident, _sigmoid, tm, "rwkv_lora_a")
    gate = _lora(xg, p["rwkv_g1"], p["rwkv_g2"], jnp.zeros((d,), F32), _sigmoid, ident, tm, "rwkv_lora_g")
    z = _rwkv_scan(r, kr, v, lw, a, gate, p, batch)
    return _mm_res_ln(z, p["rwkv_w_out"], h, g, b, alpha, tm)


def kernel(x, meta, ln_mix_g, ln_mix_b, ln_ffn_g, ln_ffn_b, conv_w_in, conv_w, conv_b, conv_w_out, rwkv_mu, rwkv_w_r, rwkv_w_k, rwkv_w_v, rwkv_w0, rwkv_w1, rwkv_w2, rwkv_a0, rwkv_a1, rwkv_a2, rwkv_g1, rwkv_g2, rwkv_k_k, rwkv_k_a, rwkv_r_k, rwkv_gn_g, rwkv_gn_b, rwkv_w_out, hgrn_w_in, hgrn_lb, hgrn_norm_g, hgrn_w_out, fox_w_in, fox_b_f, fox_q_norm_g, fox_k_norm_g, fox_w_out, ffn0_w1, ffn0_w3, ffn0_w2, moe1_router, moe1_router_b, moe1_w1, moe1_w3, moe1_w2, ffn2_w1, ffn2_w3, ffn2_w2, moe3_router, moe3_router_b, moe3_w1, moe3_w3, moe3_w2):
    batch, seq, d = x.shape
    depth = ln_mix_g.shape[0]
    assert depth == 4
    alpha = (2.0 * depth) ** 0.25
    t = N_META + seq
    n = batch * t
    tm = _divisor_tile(t, 768, 16)
    p = dict(
        conv_w_in=conv_w_in, conv_w=conv_w, conv_b=conv_b, conv_w_out=conv_w_out,
        rwkv_mu=rwkv_mu, rwkv_w_r=rwkv_w_r, rwkv_w_k=rwkv_w_k, rwkv_w_v=rwkv_w_v, rwkv_w0=rwkv_w0,
        rwkv_w1=rwkv_w1, rwkv_w2=rwkv_w2, rwkv_a0=rwkv_a0, rwkv_a1=rwkv_a1, rwkv_a2=rwkv_a2,
        rwkv_g1=rwkv_g1, rwkv_g2=rwkv_g2, rwkv_k_k=rwkv_k_k, rwkv_k_a=rwkv_k_a, rwkv_r_k=rwkv_r_k,
        rwkv_gn_g=rwkv_gn_g, rwkv_gn_b=rwkv_gn_b, rwkv_w_out=rwkv_w_out,
        hgrn_w_in=hgrn_w_in, hgrn_lb=hgrn_lb, hgrn_norm_g=hgrn_norm_g, hgrn_w_out=hgrn_w_out,
        fox_w_in=fox_w_in, fox_b_f=fox_b_f, fox_q_norm_g=fox_q_norm_g, fox_k_norm_g=fox_k_norm_g,
        fox_w_out=fox_w_out,
    )
    h = jnp.concatenate(
        [jnp.broadcast_to(meta[None].astype(x.dtype), (batch, N_META, d)), x], axis=1).reshape(n, d)
    hb = h.astype(BF16)

    h, hb = _conv_mixer_layer(h, hb, p, ln_mix_g[0], ln_mix_b[0], alpha, batch, tm)
    h, hb = _dense_ffn(hb, h, ffn0_w1, ffn0_w3, ffn0_w2, ln_ffn_g[0], ln_ffn_b[0], alpha, tm)
    h, hb = _rwkv_mixer_layer(h, hb, p, ln_mix_g[1], ln_mix_b[1], alpha, batch, tm)
    h, hb = _moe_ffn(h, hb, moe1_router, moe1_router_b, moe1_w1, moe1_w3, moe1_w2,
                     ln_ffn_g[1], ln_ffn_b[1], alpha, tm)
    h, hb = _hgrn_mixer_layer(h, hb, p, 2, ln_mix_g[2], ln_mix_b[2], alpha, batch, tm)
    h, hb = _dense_ffn(hb, h, ffn2_w1, ffn2_w3, ffn2_w2, ln_ffn_g[2], ln_ffn_b[2], alpha, tm)
    h, hb = _fox_mixer_layer(h, hb, p, ln_mix_g[3], ln_mix_b[3], alpha, batch, tm)
    h, hb = _moe_ffn(h, hb, moe3_router, moe3_router_b, moe3_w1, moe3_w3, moe3_w2,
                     ln_ffn_g[3], ln_ffn_b[3], alpha, tm)
    return h.reshape(batch, t, d)[:, N_META:]
```

```python
import functools

import jax
import jax.numpy as jnp
from jax import lax
from jax.experimental import pallas as pl
from jax.experimental.pallas import tpu as pltpu

F32 = jnp.float32
BF16 = jnp.bfloat16

N_META = 16
LN_EPS = 1e-5
RMS_EPS = 1e-6
RWKV_HEAD = 64
HGRN_HEAD = 128
FOX_HEAD = 128
TOP_K = 2
LANES = 128
VMEM_LIMIT_BYTES = 56 * 2**20


def _cparams(*sem):
    return pltpu.CompilerParams(dimension_semantics=sem, vmem_limit_bytes=VMEM_LIMIT_BYTES)


def _divisor_tile(n, cap, mult):
    best = None
    for d in range(mult, min(n, cap) + 1, mult):
        if n % d == 0:
            best = d
    assert best is not None, (n, cap, mult)
    return best


def _layer_norm(y, g, b):
    mu = jnp.mean(y, axis=-1, keepdims=True)
    yc = y - mu
    var = jnp.mean(yc * yc, axis=-1, keepdims=True)
    return yc * lax.rsqrt(var + LN_EPS) * g + b


def _sigmoid(x):
    return 1.0 / (1.0 + jnp.exp(-x))


def _silu(x):
    return x * _sigmoid(x)


def _mm_res_ln_kernel(z_ref, w_ref, h_ref, g_ref, b_ref, o_ref, ob_ref, acc_ref, *, nk, alpha):
    k = pl.program_id(1)
    part = jnp.dot(z_ref[...], w_ref[...].astype(BF16), preferred_element_type=F32)

    @pl.when(k == 0)
    def _():
        acc_ref[...] = part

    @pl.when(k > 0)
    def _():
        acc_ref[...] += part

    @pl.when(k == nk - 1)
    def _():
        y = _layer_norm(alpha * h_ref[...] + acc_ref[...], g_ref[...], b_ref[...])
        o_ref[...] = y
        ob_ref[...] = y.astype(BF16)


WEIGHT_SLICE_ROWS = 256


def _load_weight_bf16(w_hbm, wb_ref, stage_ref, sem):
    rows = stage_ref.shape[1]
    nslice = w_hbm.shape[0] // rows

    def copy(s):
        return pltpu.make_async_copy(w_hbm.at[pl.ds(s * rows, rows), :], stage_ref.at[s % 2], sem.at[s % 2])

    copy(0).start()
    for s in range(nslice):
        if s + 1 < nslice:
            copy(s + 1).start()
        copy(s).wait()
        wb_ref[s * rows:(s + 1) * rows, :] = stage_ref[s % 2].astype(BF16)


def _mm_res_ln_resident_kernel(z_ref, w_ref, h_ref, g_ref, b_ref, o_ref, ob_ref, wb_ref, stage_ref, sem,
                               *, alpha):
    @pl.when(pl.program_id(0) == 0)
    def _():
        _load_weight_bf16(w_ref, wb_ref, stage_ref, sem)

    y = alpha * h_ref[...] + jnp.dot(z_ref[...], wb_ref[...], preferred_element_type=F32)
    y = _layer_norm(y, g_ref[...], b_ref[...])
    o_ref[...] = y
    ob_ref[...] = y.astype(BF16)


RESIDENT_WEIGHT_BYTES = 16 * 2**20


def _mm_res_ln_resident(z, w, h, g, b, alpha, tm):
    n, kdim = z.shape
    d = w.shape[1]
    ws = _divisor_tile(kdim, WEIGHT_SLICE_ROWS, 8)
    const = lambda m: (0, 0)
    return pl.pallas_call(
        functools.partial(_mm_res_ln_resident_kernel, alpha=alpha),
        grid=(n // tm,),
        in_specs=[
            pl.BlockSpec((tm, kdim), lambda m: (m, 0)),
            pl.BlockSpec(memory_space=pl.ANY),
            pl.BlockSpec((tm, d), lambda m: (m, 0)),
            pl.BlockSpec((1, d), const),
            pl.BlockSpec((1, d), const),
        ],
        out_specs=[
            pl.BlockSpec((tm, d), lambda m: (m, 0)),
            pl.BlockSpec((tm, d), lambda m: (m, 0)),
        ],
        out_shape=[jax.ShapeDtypeStruct((n, d), F32), jax.ShapeDtypeStruct((n, d), BF16)],
        scratch_shapes=[pltpu.VMEM((kdim, d), BF16), pltpu.VMEM((2, ws, d), F32),
                        pltpu.SemaphoreType.DMA((2,))],
        compiler_params=_cparams("arbitrary"),
        name="mm_res_ln_resident",
    )(z, w, h, g.reshape(1, d), b.reshape(1, d))


def _mm_res_ln(z, w, h, g, b, alpha, tm):
    n, kdim = z.shape
    d = w.shape[1]
    if kdim * d * 4 <= RESIDENT_WEIGHT_BYTES:
        return _mm_res_ln_resident(z, w, h, g, b, alpha, tm)
    tk = _divisor_tile(kdim, 512, LANES)
    nk = kdim // tk
    return pl.pallas_call(
        functools.partial(_mm_res_ln_kernel, nk=nk, alpha=alpha),
        grid=(n // tm, nk),
        in_specs=[
            pl.BlockSpec((tm, tk), lambda m, k: (m, k)),
            pl.BlockSpec((tk, d), lambda m, k: (k, 0)),
            pl.BlockSpec((tm, d), lambda m, k: (m, 0)),
            pl.BlockSpec((1, d), lambda m, k: (0, 0)),
            pl.BlockSpec((1, d), lambda m, k: (0, 0)),
        ],
        out_specs=[
            pl.BlockSpec((tm, d), lambda m, k: (m, 0)),
            pl.BlockSpec((tm, d), lambda m, k: (m, 0)),
        ],
        out_shape=[jax.ShapeDtypeStruct((n, d), F32), jax.ShapeDtypeStruct((n, d), BF16)],
        scratch_shapes=[pltpu.VMEM((tm, d), F32)],
        compiler_params=_cparams("parallel", "arbitrary"),
        name="mm_res_ln",
    )(z, w, h, g.reshape(1, d), b.reshape(1, d))


def _ffn_up_kernel(exp_ref, nact_ref, x_ref, w1_ref, w3_ref, o_ref, w1b_ref, w3b_ref):
    c = pl.program_id(1)
    prev = exp_ref[jnp.maximum(c - 1, 0)]
    new_weights = jnp.logical_or(c == 0, exp_ref[c] != prev)

    @pl.when(new_weights)
    def _():
        w1b_ref[...] = w1_ref[...].astype(BF16)
        w3b_ref[...] = w3_ref[...].astype(BF16)

    @pl.when(c < nact_ref[0])
    def _():
        x = x_ref[...]
        a = jnp.dot(x, w1b_ref[...], preferred_element_type=F32)
        bb = jnp.dot(x, w3b_ref[...], preferred_element_type=F32)
        o_ref[...] = (_silu(a) * bb).astype(BF16)

    @pl.when(c >= nact_ref[0])
    def _():
        o_ref[...] = jnp.zeros_like(o_ref)


def _ffn_up(x, w1, w3, chunk_expert, n_active, rows):
    p, d = x.shape
    f = w1.shape[2]
    tf = _divisor_tile(f, 512, LANES)
    grid_spec = pltpu.PrefetchScalarGridSpec(
        num_scalar_prefetch=2,
        grid=(f // tf, p // rows),
        in_specs=[
            pl.BlockSpec((rows, d), lambda j, c, e, na: (c, 0)),
            pl.BlockSpec((None, d, tf), lambda j, c, e, na: (e[c], 0, j)),
            pl.BlockSpec((None, d, tf), lambda j, c, e, na: (e[c], 0, j)),
        ],
        out_specs=pl.BlockSpec((rows, tf), lambda j, c, e, na: (c, j)),
        scratch_shapes=[pltpu.VMEM((d, tf), BF16), pltpu.VMEM((d, tf), BF16)],
    )
    return pl.pallas_call(
        _ffn_up_kernel,
        grid_spec=grid_spec,
        out_shape=jax.ShapeDtypeStruct((p, f), BF16),
        compiler_params=_cparams("arbitrary", "arbitrary"),
        name="ffn_up",
    )(chunk_expert, n_active, x, w1, w3)


def _dense_ffn(hb, h, w1, w3, w2, g, b, alpha, tm):
    n = hb.shape[0]
    nchunks = n // tm
    hmid = _ffn_up(hb, w1[None], w3[None], jnp.zeros((nchunks,), jnp.int32),
                   jnp.full((1,), nchunks, jnp.int32), tm)
    return _mm_res_ln(hmid, w2, h, g, b, alpha, tm)


def _conv_proj_kernel(x_ref, wb_ref, wc_ref, wh_ref, cw_ref, cb_ref, o_ref,
                      wbb_ref, wcb_ref, whb_ref, carry_ref):
    bi = pl.program_id(1)
    ti = pl.program_id(2)

    @pl.when(jnp.logical_and(bi == 0, ti == 0))
    def _():
        wbb_ref[...] = wb_ref[...].astype(BF16)
        wcb_ref[...] = wc_ref[...].astype(BF16)
        whb_ref[...] = wh_ref[...].astype(BF16)

    @pl.when(ti == 0)
    def _():
        carry_ref[...] = jnp.zeros_like(carry_ref)

    x = x_ref[...]
    gate_b = jnp.dot(x, wbb_ref[...], preferred_element_type=F32)
    gate_c = jnp.dot(x, wcb_ref[...], preferred_element_type=F32)
    hh = jnp.dot(x, whb_ref[...], preferred_element_type=F32)
    u = gate_c * hh
    tm = u.shape[0]
    prev1 = carry_ref[7:8, :]
    prev2 = carry_ref[6:7, :]
    row = lax.broadcasted_iota(jnp.int32, (tm, 1), 0)
    r1 = jnp.where(row == 0, prev1, pltpu.roll(u, 1, axis=0))
    r2 = jnp.where(row == 0, prev2, jnp.where(row == 1, prev1, pltpu.roll(u, 2, axis=0)))
    v = cw_ref[0:1, :] * r2 + cw_ref[1:2, :] * r1 + cw_ref[2:3, :] * u + cb_ref[...]
    carry_ref[...] = u[tm - 8:, :]
    o_ref[...] = (gate_b * v).astype(BF16)


def _conv_proj(hb, w_in, conv_w, conv_b, batch, tm):
    n, d = hb.shape
    tn = _divisor_tile(d, 512, LANES)
    nd = d // tn
    tpb = n // batch // tm
    return pl.pallas_call(
        _conv_proj_kernel,
        grid=(nd, batch, tpb),
        in_specs=[
            pl.BlockSpec((tm, d), lambda j, bi, ti: (bi * tpb + ti, 0)),
            pl.BlockSpec((d, tn), lambda j, bi, ti: (0, j)),
            pl.BlockSpec((d, tn), lambda j, bi, ti: (0, nd + j)),
            pl.BlockSpec((d, tn), lambda j, bi, ti: (0, 2 * nd + j)),
            pl.BlockSpec((3, tn), lambda j, bi, ti: (0, j)),
            pl.BlockSpec((1, tn), lambda j, bi, ti: (0, j)),
        ],
        out_specs=pl.BlockSpec((tm, tn), lambda j, bi, ti: (bi * tpb + ti, j)),
        out_shape=jax.ShapeDtypeStruct((n, d), BF16),
        scratch_shapes=[pltpu.VMEM((d, tn), BF16)] * 3 + [pltpu.VMEM((8, tn), F32)],
        compiler_params=_cparams("arbitrary", "arbitrary", "arbitrary"),
        name="conv_proj",
    )(hb, w_in, w_in, w_in, conv_w, conv_b.reshape(1, d))


def _conv_mixer_layer(h, hb, p, g, b, alpha, batch, tm):
    z = _conv_proj(hb, p["conv_w_in"], p["conv_w"], p["conv_b"], batch, tm)
    return _mm_res_ln(z, p["conv_w_out"], h, g, b, alpha, tm)


MOE_ROWS = 256


def _split_bf16(a):
    hi = a.astype(BF16)
    lo = (a - hi.astype(F32)).astype(BF16)
    return hi, lo


def _dot_f32(a, b):
    ah, al = _split_bf16(a)
    bh, bl = _split_bf16(b)
    return (jnp.dot(ah, bh, preferred_element_type=F32)
            + (jnp.dot(ah, bl, preferred_element_type=F32)
               + jnp.dot(al, bh, preferred_element_type=F32)))


def _router_kernel(h_ref, w_ref, b_ref, info_ref, cnt_ref, carry_ref):
    i = pl.program_id(0)

    @pl.when(i == 0)
    def _():
        carry_ref[...] = jnp.zeros_like(carry_ref)

    logits = _dot_f32(h_ref[...], w_ref[...]) + b_ref[...]
    tm, ne = logits.shape
    lane = lax.broadcasted_iota(jnp.int32, (tm, ne), 1)
    m1 = jnp.max(logits, axis=-1, keepdims=True)
    i1 = jnp.min(jnp.where(logits == m1, lane, ne), axis=-1, keepdims=True)
    mask1 = lane == i1
    rest = jnp.where(mask1, -jnp.inf, logits)
    m2 = jnp.max(rest, axis=-1, keepdims=True)
    i2 = jnp.min(jnp.where(rest == m2, lane, ne), axis=-1, keepdims=True)
    mask2 = lane == i2
    dd = jnp.exp(m2 - m1)
    g1 = 1.0 / (1.0 + dd)
    g2 = dd / (1.0 + dd)
    sel = jnp.where(jnp.logical_or(mask1, mask2), 1.0, 0.0)
    r_i = lax.broadcasted_iota(jnp.int32, (tm, tm), 0)
    c_i = lax.broadcasted_iota(jnp.int32, (tm, tm), 1)
    tril = jnp.where(c_i < r_i, 1.0, 0.0).astype(BF16)
    rank = jnp.dot(tril, sel.astype(BF16), preferred_element_type=F32) + carry_ref[...]
    r1 = jnp.sum(jnp.where(mask1, rank, 0.0), axis=-1, keepdims=True)
    r2 = jnp.sum(jnp.where(mask2, rank, 0.0), axis=-1, keepdims=True)
    info = jnp.where(lane == 0, i1.astype(F32),
           jnp.where(lane == 1, i2.astype(F32),
           jnp.where(lane == 2, g1,
           jnp.where(lane == 3, g2,
           jnp.where(lane == 4, r1,
           jnp.where(lane == 5, r2, 0.0))))))
    info_ref[...] = info
    total = carry_ref[...] + jnp.sum(sel, axis=0, keepdims=True)
    carry_ref[...] = total
    cnt_ref[...] = total


def _router(h, w, b, tm):
    n, d = h.shape
    ne = w.shape[1]
    assert ne >= 6
    return pl.pallas_call(
        _router_kernel,
        grid=(n // tm,),
        in_specs=[
            pl.BlockSpec((tm, d), lambda i: (i, 0)),
            pl.BlockSpec((d, ne), lambda i: (0, 0)),
            pl.BlockSpec((1, ne), lambda i: (0, 0)),
        ],
        out_specs=[
            pl.BlockSpec((tm, ne), lambda i: (i, 0)),
            pl.BlockSpec((1, ne), lambda i: (0, 0)),
        ],
        out_shape=[jax.ShapeDtypeStruct((n, ne), F32), jax.ShapeDtypeStruct((1, ne), F32)],
        scratch_shapes=[pltpu.VMEM((1, ne), F32)],
        compiler_params=_cparams("arbitrary"),
        name="moe_router",
    )(h, w, b.reshape(1, ne))


def _row_copy(src_hbm, row, dst_vmem, r, sem):
    return pltpu.make_async_copy(src_hbm.at[pl.ds(row, 1), :], dst_vmem.at[pl.ds(r, 1), :], sem)


def _rows_wait(src_hbm, dst_vmem, sem):
    pltpu.make_async_copy(src_hbm.at[pl.ds(0, dst_vmem.shape[0]), :], dst_vmem, sem).wait()


def _moe_gather_kernel(src_ref, nact_ref, h_ref, o_ref, buf_ref, sem):
    c = pl.program_id(0)
    rows = buf_ref.shape[0]

    @pl.when(c < nact_ref[0])
    def _():
        def start(r, carry):
            _row_copy(h_ref, src_ref[c * rows + r], buf_ref, r, sem).start()
            return carry

        lax.fori_loop(0, rows, start, 0, unroll=8)
        _rows_wait(h_ref, buf_ref, sem)
        o_ref[...] = buf_ref[...].astype(BF16)

    @pl.when(c >= nact_ref[0])
    def _():
        o_ref[...] = jnp.zeros_like(o_ref)


def _moe_gather(h, src, n_active, p, rows):
    n, d = h.shape
    grid_spec = pltpu.PrefetchScalarGridSpec(
        num_scalar_prefetch=2,
        grid=(p // rows,),
        in_specs=[pl.BlockSpec(memory_space=pl.ANY)],
        out_specs=pl.BlockSpec((rows, d), lambda c, s, na: (c, 0)),
        scratch_shapes=[pltpu.VMEM((rows, d), F32), pltpu.SemaphoreType.DMA],
    )
    return pl.pallas_call(
        _moe_gather_kernel,
        grid_spec=grid_spec,
        out_shape=jax.ShapeDtypeStruct((p, d), BF16),
        compiler_params=_cparams("arbitrary"),
        name="moe_gather",
    )(src, n_active, h)


def _ffn_down_kernel(exp_ref, nact_ref, x_ref, w_ref, o_ref, wb_ref):
    c = pl.program_id(1)
    prev = exp_ref[jnp.maximum(c - 1, 0)]
    new_weights = jnp.logical_or(c == 0, exp_ref[c] != prev)

    @pl.when(new_weights)
    def _():
        wb_ref[...] = w_ref[...].astype(BF16)

    @pl.when(c < nact_ref[0])
    def _():
        o_ref[...] = jnp.dot(x_ref[...], wb_ref[...], preferred_element_type=F32)

    @pl.when(c >= nact_ref[0])
    def _():
        o_ref[...] = jnp.zeros_like(o_ref)


def _ffn_down(x, w2, chunk_expert, n_active, rows):
    p, f = x.shape
    d = w2.shape[2]
    tn = _divisor_tile(d, 512, LANES)
    grid_spec = pltpu.PrefetchScalarGridSpec(
        num_scalar_prefetch=2,
        grid=(d // tn, p // rows),
        in_specs=[
            pl.BlockSpec((rows, f), lambda j, c, e, na: (c, 0)),
            pl.BlockSpec((None, f, tn), lambda j, c, e, na: (e[c], 0, j)),
        ],
        out_specs=pl.BlockSpec((rows, tn), lambda j, c, e, na: (c, j)),
        scratch_shapes=[pltpu.VMEM((f, tn), BF16)],
    )
    return pl.pallas_call(
        _ffn_down_kernel,
        grid_spec=grid_spec,
        out_shape=jax.ShapeDtypeStruct((p, d), F32),
        compiler_params=_cparams("arbitrary", "arbitrary"),
        name="ffn_down",
    )(chunk_expert, n_active, x, w2)


def _moe_combine_kernel(p1_ref, p2_ref, y_ref, h_ref, info_ref, g_ref, b_ref, o_ref, ob_ref,
                        buf1_ref, buf2_ref, sem, *, alpha):
    i = pl.program_id(0)
    tm = buf1_ref.shape[0]

    def start(r, carry):
        _row_copy(y_ref, p1_ref[i * tm + r], buf1_ref, r, sem).start()
        _row_copy(y_ref, p2_ref[i * tm + r], buf2_ref, r, sem).start()
        return carry

    lax.fori_loop(0, tm, start, 0, unroll=8)
    _rows_wait(y_ref, buf1_ref, sem)
    _rows_wait(y_ref, buf2_ref, sem)
    info = info_ref[...]
    y = alpha * h_ref[...] + (info[:, 2:3] * buf1_ref[...] + info[:, 3:4] * buf2_ref[...])
    y = _layer_norm(y, g_ref[...], b_ref[...])
    o_ref[...] = y
    ob_ref[...] = y.astype(BF16)


def _moe_combine(y, h, info, p1, p2, g, b, alpha, tm):
    n, d = h.shape
    ne = info.shape[1]
    grid_spec = pltpu.PrefetchScalarGridSpec(
        num_scalar_prefetch=2,
        grid=(n // tm,),
        in_specs=[
            pl.BlockSpec(memory_space=pl.ANY),
            pl.BlockSpec((tm, d), lambda i, a, c: (i, 0)),
            pl.BlockSpec((tm, ne), lambda i, a, c: (i, 0)),
            pl.BlockSpec((1, d), lambda i, a, c: (0, 0)),
            pl.BlockSpec((1, d), lambda i, a, c: (0, 0)),
        ],
        out_specs=[
            pl.BlockSpec((tm, d), lambda i, a, c: (i, 0)),
            pl.BlockSpec((tm, d), lambda i, a, c: (i, 0)),
        ],
        scratch_shapes=[pltpu.VMEM((tm, d), F32), pltpu.VMEM((tm, d), F32), pltpu.SemaphoreType.DMA],
    )
    return pl.pallas_call(
        functools.partial(_moe_combine_kernel, alpha=alpha),
        grid_spec=grid_spec,
        out_shape=[jax.ShapeDtypeStruct((n, d), F32), jax.ShapeDtypeStruct((n, d), BF16)],
        compiler_params=_cparams("arbitrary"),
        name="moe_combine",
    )(p1, p2, y, h, info, g.reshape(1, d), b.reshape(1, d))


def _moe_ffn(h, hb, router_w, router_b, w1, w3, w2, g, b, alpha, tm):
    n, d = h.shape
    ne = router_w.shape[1]
    rows = MOE_ROWS
    info, counts = _router(h, router_w, router_b, tm)
    counts = counts[0].astype(jnp.int32)
    nchunk_e = (counts + rows - 1) // rows
    chunk_end = jnp.cumsum(nchunk_e)
    starts = (chunk_end - nchunk_e) * rows
    n_chunks = (n * TOP_K + ne * (rows - 1)) // rows
    p = n_chunks * rows
    n_active = chunk_end[-1:].astype(jnp.int32)
    chunk_expert = jnp.minimum(
        jnp.sum(jnp.arange(n_chunks, dtype=jnp.int32)[:, None] >= chunk_end[None, :], axis=1), ne - 1
    ).astype(jnp.int32)
    i1 = info[:, 0].astype(jnp.int32)
    i2 = info[:, 1].astype(jnp.int32)
    p1 = starts[i1] + info[:, 4].astype(jnp.int32)
    p2 = starts[i2] + info[:, 5].astype(jnp.int32)
    tok = jnp.arange(n, dtype=jnp.int32)
    src = jnp.zeros((p,), jnp.int32).at[p1].set(tok).at[p2].set(tok)
    xs = _moe_gather(h, src, n_active, p, rows)
    hmid = _ffn_up(xs, w1, w3, chunk_expert, n_active, rows)
    y = _ffn_down(hmid, w2, chunk_expert, n_active, rows)
    tmc = _divisor_tile(n, 384, 16)
    return _moe_combine(y, h, info, p1, p2, g, b, alpha, tmc)


def _proj_kernel(*refs, epilogue, n_extra, n_out):
    x_ref, w_ref = refs[0], refs[1]
    extra = refs[2:2 + n_extra]
    outs = refs[2 + n_extra:2 + n_extra + n_out]
    wb_ref = refs[2 + n_extra + n_out]

    @pl.when(pl.program_id(1) == 0)
    def _():
        wb_ref[...] = w_ref[...].astype(BF16)

    y = jnp.dot(x_ref[...], wb_ref[...], preferred_element_type=F32)
    res = epilogue(y, *[e[...] for e in extra])
    for o_ref, r in zip(outs, res):
        o_ref[...] = r.astype(o_ref.dtype)


def _proj(xb, w, col0, ncols, epilogue, extras, out_dtypes, tm, name, tn_cap=1024):
    n, kdim = xb.shape
    tn = _divisor_tile(ncols, tn_cap, LANES)
    assert col0 % tn == 0
    off = col0 // tn
    outs = pl.pallas_call(
        functools.partial(_proj_kernel, epilogue=epilogue, n_extra=len(extras), n_out=len(out_dtypes)),
        grid=(ncols // tn, n // tm),
        in_specs=[
            pl.BlockSpec((tm, kdim), lambda j, m: (m, 0)),
            pl.BlockSpec((kdim, tn), lambda j, m: (0, off + j)),
        ] + [pl.BlockSpec((1, tn), lambda j, m: (0, j))] * len(extras),
        out_specs=[pl.BlockSpec((tm, tn), lambda j, m: (m, j))] * len(out_dtypes),
        out_shape=[jax.ShapeDtypeStruct((n, ncols), dt) for dt in out_dtypes],
        scratch_shapes=[pltpu.VMEM((kdim, tn), BF16)],
        compiler_params=_cparams("arbitrary", "arbitrary"),
        name=name,
    )(xb, w, *[e.reshape(1, ncols) for e in extras])
    return outs


def _tril_bf16(c, inclusive):
    r_i = lax.broadcasted_iota(jnp.int32, (c, c), 0)
    c_i = lax.broadcasted_iota(jnp.int32, (c, c), 1)
    keep = (c_i <= r_i) if inclusive else (c_i < r_i)
    return jnp.where(keep, 1.0, 0.0).astype(BF16)


def _cumsum_rows(x, tril):
    hi = x.astype(BF16)
    r1 = x - hi.astype(F32)
    mid = r1.astype(BF16)
    lo = (r1 - mid.astype(F32)).astype(BF16)
    return (jnp.dot(tril, hi, preferred_element_type=F32)
            + (jnp.dot(tril, mid, preferred_element_type=F32)
               + jnp.dot(tril, lo, preferred_element_type=F32)))


def _dot_nt(a, b):
    return lax.dot_general(a.astype(BF16), b.astype(BF16), (((1,), (1,)), ((), ())),
                           preferred_element_type=F32)


def _dot_tn(a, b):
    return lax.dot_general(a.astype(BF16), b.astype(BF16), (((0,), (0,)), ((), ())),
                           preferred_element_type=F32)


def _dot_nn(a, b):
    return jnp.dot(a.astype(BF16), b.astype(BF16), preferred_element_type=F32)


def _chunk_len(t):
    return _divisor_tile(t, 64, 16)


HGRN_SUB = 16


def _hgrn_scan_kernel(q_ref, lf_ref, v_ref, gs_ref, ng_ref, o_ref, st_ref, *, chunk):
    t, w = q_ref.shape
    nhead = w // HGRN_HEAD
    nsub = chunk // HGRN_SUB
    st_ref[...] = jnp.zeros_like(st_ref)
    tril = _tril_bf16(chunk, True)
    row16 = lax.broadcasted_iota(jnp.int32, (HGRN_SUB, 1), 0)

    def head_chunk(q, lf, v, st):
        k = 1.0 - jnp.exp(lf)
        cum = _cumsum_rows(lf, tril)
        o_inter = _dot_nt(q * jnp.exp(cum), st)
        vb = v.astype(BF16)
        outs = []
        for i in range(nsub):
            lo, hi = i * HGRN_SUB, (i + 1) * HGRN_SUB
            qi, ki, vi, cumi = q[lo:hi], k[lo:hi], v[lo:hi], cum[lo:hi]
            oi = o_inter[lo:hi]
            if i > 0:
                ci = cum[lo:lo + 1]
                qt = qi * jnp.exp(cumi - ci)
                kt = k[0:lo] * jnp.exp(ci - cum[0:lo])
                oi = oi + jnp.dot(_dot_nt(qt, kt).astype(BF16), vb[0:lo], preferred_element_type=F32)
            for s in range(HGRN_SUB):
                dec = jnp.exp(jnp.minimum(cumi - cumi[s:s + 1], 0.0))
                col = jnp.sum(qi * dec * ki[s:s + 1], axis=-1, keepdims=True)
                col = jnp.where(row16 >= s, col, 0.0)
                oi = oi + col * vi[s:s + 1]
            outs.append(oi)
        o = jnp.concatenate(outs, axis=0)
        cl = cum[chunk - 1:chunk]
        kd = k * jnp.exp(cl - cum)
        st_new = st * jnp.exp(cl) + _dot_tn(v, kd)
        o = o * lax.rsqrt(jnp.mean(o * o, axis=-1, keepdims=True) + RMS_EPS)
        return o, st_new

    def body(c, carry):
        rows = pl.ds(pl.multiple_of(c * chunk, 16), chunk)
        q = q_ref[rows, :].astype(F32)
        lf = lf_ref[rows, :]
        v = v_ref[rows, :].astype(F32)
        outs = []
        for hh in range(nhead):
            cols = slice(hh * HGRN_HEAD, (hh + 1) * HGRN_HEAD)
            o, st_new = head_chunk(q[:, cols], lf[:, cols], v[:, cols], st_ref[hh])
            st_ref[hh] = st_new
            outs.append(o)
        o = jnp.concatenate(outs, axis=1)
        o_ref[rows, :] = (o * ng_ref[...] * gs_ref[rows, :].astype(F32)).astype(BF16)
        return carry

    lax.fori_loop(0, t // chunk, body, 0)


HGRN_GROUP = 4


def _hgrn_scan(q, lf, v, gs, norm_g, batch):
    n, d = q.shape
    t = n // batch
    chunk = _chunk_len(t)
    w = min(d, HGRN_GROUP * HGRN_HEAD)
    blk = pl.BlockSpec((t, w), lambda b, j: (b, j))
    return pl.pallas_call(
        functools.partial(_hgrn_scan_kernel, chunk=chunk),
        grid=(batch, d // w),
        in_specs=[blk, blk, blk, blk, pl.BlockSpec((1, w), lambda b, j: (0, j))],
        out_specs=blk,
        out_shape=jax.ShapeDtypeStruct((n, d), BF16),
        scratch_shapes=[pltpu.VMEM((w // HGRN_HEAD, HGRN_HEAD, HGRN_HEAD), F32)],
        compiler_params=_cparams("parallel", "parallel"),
        name="hgrn_scan",
    )(q, lf, v, gs, norm_g.reshape(1, d))


def _hgrn_mixer_layer(h, hb, p, layer_idx, g, b, alpha, batch, tm):
    d = h.shape[1]
    w_in = p["hgrn_w_in"]
    lb = jnp.cumsum(jax.nn.softmax(p["hgrn_lb"].astype(F32), axis=0), axis=0)
    lb = lb[layer_idx] - lb[0]
    (q,) = _proj(hb, w_in, 0, d, lambda y: (_silu(y),), [], [BF16], tm, "hgrn_proj_q")
    (lf,) = _proj(hb, w_in, d, d, lambda y, lbv: (jnp.log(lbv + (1.0 - lbv) * _sigmoid(y)),),
                  [lb], [F32], tm, "hgrn_proj_f")
    (v,) = _proj(hb, w_in, 2 * d, d, lambda y: (y,), [], [BF16], tm, "hgrn_proj_i")
    (gs,) = _proj(hb, w_in, 3 * d, d, lambda y: (_silu(y),), [], [BF16], tm, "hgrn_proj_g")
    z = _hgrn_scan(q, lf, v, gs, p["hgrn_norm_g"], batch)
    return _mm_res_ln(z, p["hgrn_w_out"], h, g, b, alpha, tm)


def _fox_gate_kernel(h_ref, w_ref, bf_ref, c_ref, carry_ref):
    @pl.when(pl.program_id(1) == 0)
    def _():
        carry_ref[...] = jnp.zeros_like(carry_ref)

    x = _dot_f32(h_ref[...], w_ref[...]) + bf_ref[...]
    log_f = jnp.minimum(x, 0.0) - jnp.log(1.0 + jnp.exp(-jnp.abs(x)))
    tm = x.shape[0]
    c = _cumsum_rows(log_f, _tril_bf16(tm, True)) + carry_ref[...]
    c_ref[...] = c
    carry_ref[...] = c[tm - 1:tm, :]


def _fox_gate(h, w_f, b_f, batch, tm):
    n, d = h.shape
    nh = w_f.shape[1]
    tpb = n // batch // tm
    return pl.pallas_call(
        _fox_gate_kernel,
        grid=(batch, tpb),
        in_specs=[
            pl.BlockSpec((tm, d), lambda b, t: (b * tpb + t, 0)),
            pl.BlockSpec((d, nh), lambda b, t: (0, 0)),
            pl.BlockSpec((1, nh), lambda b, t: (0, 0)),
        ],
        out_specs=pl.BlockSpec((tm, nh), lambda b, t: (b * tpb + t, 0)),
        out_shape=jax.ShapeDtypeStruct((n, nh), F32),
        scratch_shapes=[pltpu.VMEM((1, nh), F32)],
        compiler_params=_cparams("arbitrary", "arbitrary"),
        name="fox_gate",
    )(h, w_f, b_f.reshape(1, nh))


def _fox_attn_kernel(q_ref, k_ref, v_ref, sg_ref, c_ref, ct_ref, o_ref, *, tq):
    hd = pl.program_id(1)
    t = q_ref.shape[0]
    nh = c_ref.shape[1]
    lane = lax.broadcasted_iota(jnp.int32, (t, nh), 1)
    c_col = jnp.sum(jnp.where(lane == hd, c_ref[...], 0.0), axis=-1, keepdims=True)
    c_row = ct_ref[pl.ds(hd, 1), :]
    for i in range(t // tq):
        lo, hi = i * tq, (i + 1) * tq
        s = lax.dot_general(q_ref[lo:hi, :], k_ref[0:hi, :], (((1,), (1,)), ((), ())),
                            preferred_element_type=F32)
        s = s + c_col[lo:hi] - c_row[:, 0:hi]
        r_i = lax.broadcasted_iota(jnp.int32, (tq, hi), 0) + lo
        c_i = lax.broadcasted_iota(jnp.int32, (tq, hi), 1)
        s = jnp.where(c_i <= r_i, s, -jnp.inf)
        m = jnp.max(s, axis=-1, keepdims=True)
        p = jnp.exp(s - m)
        l = jnp.sum(p, axis=-1, keepdims=True)
        o = jnp.dot(p.astype(BF16), v_ref[0:hi, :], preferred_element_type=F32) / l
        o_ref[lo:hi, :] = (o * sg_ref[lo:hi, :]).astype(BF16)


def _fox_attn(q, k, v, sg, c, ct, batch):
    n, d = q.shape
    t = n // batch
    nh = d // FOX_HEAD
    tq = _divisor_tile(t, 768, 16)
    blk = pl.BlockSpec((t, FOX_HEAD), lambda b, h: (b, h))
    return pl.pallas_call(
        functools.partial(_fox_attn_kernel, tq=tq),
        grid=(batch, nh),
        in_specs=[blk, blk, blk, blk,
                  pl.BlockSpec((t, nh), lambda b, h: (b, 0)),
                  pl.BlockSpec((None, nh, t), lambda b, h: (b, 0, 0))],
        out_specs=blk,
        out_shape=jax.ShapeDtypeStruct((n, d), BF16),
        compiler_params=_cparams("parallel", "parallel"),
        name="fox_attn",
    )(q, k, v, sg, c, ct)


def _head_rms_epilogue(scale):
    def epi(y, gain):
        outs = []
        for j in range(y.shape[1] // FOX_HEAD):
            yj = y[:, j * FOX_HEAD:(j + 1) * FOX_HEAD]
            yj = yj * lax.rsqrt(jnp.mean(yj * yj, axis=-1, keepdims=True) + RMS_EPS)
            outs.append(yj * gain[:, j * FOX_HEAD:(j + 1) * FOX_HEAD] * scale)
        return (jnp.concatenate(outs, axis=1),)
    return epi


def _fox_mixer_layer(h, hb, p, g, b, alpha, batch, tm):
    n, d = h.shape
    nh = d // FOX_HEAD
    w_in = p["fox_w_in"]
    qg = jnp.tile(p["fox_q_norm_g"], nh)
    kg = jnp.tile(p["fox_k_norm_g"], nh)
    (q,) = _proj(hb, w_in, 0, d, _head_rms_epilogue(FOX_HEAD ** -0.5), [qg], [BF16], tm, "fox_proj_q")
    (k,) = _proj(hb, w_in, d, d, _head_rms_epilogue(1.0), [kg], [BF16], tm, "fox_proj_k")
    (v,) = _proj(hb, w_in, 2 * d, d, lambda y: (y,), [], [BF16], tm, "fox_proj_v")
    (sg,) = _proj(hb, w_in, 3 * d, d, lambda y: (_sigmoid(y),), [], [F32], tm, "fox_proj_g")
    c = _fox_gate(h, w_in[:, 4 * d:], p["fox_b_f"], batch, tm)
    ct = c.reshape(batch, n // batch, nh).transpose(0, 2, 1)
    z = _fox_attn(q, k, v, sg, c, ct, batch)
    return _mm_res_ln(z, p["fox_w_out"], h, g, b, alpha, tm)


def _rwkv_mix_kernel(h_ref, mu_ref, *refs):
    outs, carry_ref = refs[:-1], refs[-1]

    @pl.when(pl.program_id(1) == 0)
    def _():
        carry_ref[...] = jnp.zeros_like(carry_ref)

    x = h_ref[...]
    tm = x.shape[0]
    row = lax.broadcasted_iota(jnp.int32, (tm, 1), 0)
    prev = jnp.where(row == 0, carry_ref[7:8, :], pltpu.roll(x, 1, axis=0))
    xx = prev - x
    carry_ref[...] = x[tm - 8:, :]
    for j, o_ref in enumerate(outs):
        o_ref[...] = (x + xx * mu_ref[j:j + 1, :]).astype(BF16)


def _rwkv_mix(h, mu, batch, tm):
    n, d = h.shape
    nmix = mu.shape[0]
    tpb = n // batch // tm
    blk = pl.BlockSpec((tm, d), lambda b, t: (b * tpb + t, 0))
    return pl.pallas_call(
        _rwkv_mix_kernel,
        grid=(batch, tpb),
        in_specs=[blk, pl.BlockSpec((nmix, d), lambda b, t: (0, 0))],
        out_specs=[blk] * nmix,
        out_shape=[jax.ShapeDtypeStruct((n, d), BF16)] * nmix,
        scratch_shapes=[pltpu.VMEM((8, d), F32)],
        compiler_params=_cparams("arbitrary", "arbitrary"),
        name="rwkv_mix",
    )(h, mu)


def _lora_kernel(x_ref, wa_ref, wb_ref, bias_ref, o_ref, wab_ref, wbb_ref, *, mid_act, out_act):
    @pl.when(pl.program_id(0) == 0)
    def _():
        wab_ref[...] = wa_ref[...].astype(BF16)
        wbb_ref[...] = wb_ref[...].astype(BF16)

    mid = mid_act(jnp.dot(x_ref[...], wab_ref[...], preferred_element_type=F32))
    y = jnp.dot(mid.astype(BF16), wbb_ref[...], preferred_element_type=F32)
    o_ref[...] = out_act(bias_ref[...] + y)


def _lora(xb, wa, wb, bias, mid_act, out_act, tm, name):
    n, d = xb.shape
    r = wa.shape[1]
    dout = wb.shape[1]
    return pl.pallas_call(
        functools.partial(_lora_kernel, mid_act=mid_act, out_act=out_act),
        grid=(n // tm,),
        in_specs=[
            pl.BlockSpec((tm, d), lambda i: (i, 0)),
            pl.BlockSpec((d, r), lambda i: (0, 0)),
            pl.BlockSpec((r, dout), lambda i: (0, 0)),
            pl.BlockSpec((1, dout), lambda i: (0, 0)),
        ],
        out_specs=pl.BlockSpec((tm, dout), lambda i: (i, 0)),
        out_shape=jax.ShapeDtypeStruct((n, dout), F32),
        scratch_shapes=[pltpu.VMEM((d, r), BF16), pltpu.VMEM((r, dout), BF16)],
        compiler_params=_cparams("arbitrary"),
        name=name,
    )(xb, wa, wb, bias.reshape(1, dout))


def _rwkv_log_decay(z):
    w_log = -(jnp.maximum(-z, 0.0) + jnp.log(1.0 + jnp.exp(-jnp.abs(z)))) - 0.5
    return -jnp.exp(w_log)


RWKV_GROUP = 4
RWKV_UNROLL = 4


def _seg_sum(x, bd):
    hi = x.astype(BF16)
    lo = (x - hi.astype(F32)).astype(BF16)
    return jnp.dot(hi, bd, preferred_element_type=F32) + jnp.dot(lo, bd, preferred_element_type=F32)


def _rwkv_scan_kernel(r_ref, kr_ref, v_ref, lw_ref, a_ref, g_ref, kk_p, ka_p, rk_p, gg_p, gb_p,
                      o_ref, kk_s, k_s, bonus_s, y_s, st_ref, *, chunk, ptile):
    t, w = r_ref.shape
    nhead = w // RWKV_HEAD
    sc = nhead * chunk
    lane_r = lax.broadcasted_iota(jnp.int32, (w, w), 0) // RWKV_HEAD
    lane_c = lax.broadcasted_iota(jnp.int32, (w, w), 1) // RWKV_HEAD
    bd = jnp.where(lane_r == lane_c, 1.0, 0.0).astype(BF16)

    def prologue(i, carry):
        rows = pl.ds(pl.multiple_of(i * ptile, 8), ptile)
        kr = kr_ref[rows, :]
        a = a_ref[rows, :]
        kkr = kr * kk_p[...]
        nrm = jnp.maximum(jnp.sqrt(_seg_sum(kkr * kkr, bd)), 1e-12)
        kk_s[rows, :] = kkr / nrm
        k = kr * (1.0 + (a - 1.0) * ka_p[...])
        k_s[rows, :] = k
        bonus_s[rows, :] = _seg_sum(r_ref[rows, :] * k * rk_p[...], bd) * v_ref[rows, :]
        return carry

    lax.fori_loop(0, t // ptile, prologue, 0)

    st_ref[...] = jnp.zeros_like(st_ref)
    tril = _tril_bf16(chunk, True)
    head_of_lane = lax.broadcasted_iota(jnp.int32, (chunk, w), 1) // RWKV_HEAD
    ri = lax.broadcasted_iota(jnp.int32, (2 * sc, sc), 0)
    ci = lax.broadcasted_iota(jnp.int32, (2 * sc, sc), 1)
    low_mask = ci < jnp.where(ri < sc, ri, ri - sc + 1)
    nsteps = max(1, (chunk - 1).bit_length())

    def stack(x):
        return jnp.concatenate([jnp.where(head_of_lane == hh, x, 0.0) for hh in range(nhead)], axis=0)

    eye = jnp.where(lax.broadcasted_iota(jnp.int32, (sc, sc), 0)
                    == lax.broadcasted_iota(jnp.int32, (sc, sc), 1), 1.0, 0.0)

    def prepare(c):
        start = c * chunk
        rows = pl.ds(start if isinstance(start, int) else pl.multiple_of(start, 16), chunk)
        r = r_ref[rows, :]
        v = v_ref[rows, :]
        lw = lw_ref[rows, :]
        a = a_ref[rows, :]
        kk = kk_s[rows, :]
        k = k_s[rows, :]
        cum = _cumsum_rows(lw, tril)
        e_neg = jnp.exp(-cum)
        at2 = stack(-kk * jnp.exp(cum - lw))
        rt2 = stack(r * jnp.exp(cum))
        bvec = kk * a
        bb2 = stack(bvec * e_neg)
        kb2 = stack(k * e_neg)
        v2 = stack(v)
        ar2 = jnp.concatenate([at2, rt2], axis=0).astype(BF16)
        cl = cum[chunk - 1:chunk]
        e_end = jnp.exp(cl - cum)
        khbh = jnp.concatenate([stack(k * e_end), stack(bvec * e_end)], axis=0).astype(BF16)
        yield None
        pb = jnp.where(low_mask, _dot_nt(ar2, bb2), 0.0)
        pk = jnp.where(low_mask, _dot_nt(ar2, kb2), 0.0)
        m_ab, m_rb = pb[:sc], pb[sc:]
        m_ak, m_rk = pk[:sc], pk[sc:]
        yield None
        u0 = _dot_nn(m_ak, v2)
        y0 = _dot_nn(m_rk, v2)
        tinv = eye + m_ab
        lpow = m_ab
        for _ in range(nsteps - 1):
            yield None
            lpow = _dot_nn(lpow, lpow)
            tinv = tinv + _dot_nn(tinv, lpow)
        yield dict(rows=rows, ar2=ar2, tinv=tinv.astype(BF16), m_rb=m_rb.astype(BF16),
                   u0=u0, y0=y0, v2=v2.astype(BF16), khbh=khbh, decay=jnp.exp(cl))

    def prepare_interleaved(chunk_ids):
        gens = [prepare(c) for c in chunk_ids]
        done = [None] * len(gens)
        while any(d is None for d in done):
            for j, gen in enumerate(gens):
                if done[j] is None:
                    out = next(gen)
                    if out is not None:
                        done[j] = out
        return done

    def advance(pc, st):
        ps = _dot_nt(pc["ar2"], st)
        u2 = jnp.dot(pc["tinv"], (ps[:sc] + pc["u0"]).astype(BF16), preferred_element_type=F32)
        u2b = u2.astype(BF16)
        y2 = ps[sc:] + pc["y0"] + jnp.dot(pc["m_rb"], u2b, preferred_element_type=F32)
        y = y2[0:chunk]
        for hh in range(1, nhead):
            y = y + y2[hh * chunk:(hh + 1) * chunk]
        y_s[pc["rows"], :] = y
        return st * pc["decay"] + _dot_tn(jnp.concatenate([pc["v2"], u2b], axis=0), pc["khbh"])

    nchunks = t // chunk

    def body(i, carry):
        prepared = prepare_interleaved([i * RWKV_UNROLL + j for j in range(RWKV_UNROLL)])
        st = st_ref[...]
        for pc in prepared:
            st = advance(pc, st)
        st_ref[...] = st
        return carry

    lax.fori_loop(0, nchunks // RWKV_UNROLL, body, 0)
    tail = list(range(nchunks - nchunks % RWKV_UNROLL, nchunks))
    if tail:
        st = st_ref[...]
        for pc in prepare_interleaved(tail):
            st = advance(pc, st)
        st_ref[...] = st

    inv = 1.0 / RWKV_HEAD

    def epilogue(i, carry):
        rows = pl.ds(pl.multiple_of(i * ptile, 8), ptile)
        y = y_s[rows, :]
        mu = _seg_sum(y, bd) * inv
        yc = y - mu
        var = _seg_sum(yc * yc, bd) * inv
        yn = yc * lax.rsqrt(var + 1e-5 * RWKV_HEAD) * gg_p[...] + gb_p[...]
        o_ref[rows, :] = ((yn + bonus_s[rows, :]) * g_ref[rows, :]).astype(BF16)
        return carry

    lax.fori_loop(0, t // ptile, epilogue, 0)


def _rwkv_scan(r, kr, v, lw, a, g, p, batch):
    n, d = r.shape
    t = n // batch
    chunk = _chunk_len(t)
    w = min(d, RWKV_GROUP * RWKV_HEAD)
    ptile = _divisor_tile(t, 768, 16)
    blk = pl.BlockSpec((t, w), lambda b, j: (b, j))
    prm = pl.BlockSpec((1, w), lambda b, j: (0, j))
    params = [p["rwkv_k_k"], p["rwkv_k_a"], p["rwkv_r_k"], p["rwkv_gn_g"], p["rwkv_gn_b"]]
    return pl.pallas_call(
        functools.partial(_rwkv_scan_kernel, chunk=chunk, ptile=ptile),
        grid=(batch, d // w),
        in_specs=[blk] * 6 + [prm] * 5,
        out_specs=blk,
        out_shape=jax.ShapeDtypeStruct((n, d), BF16),
        scratch_shapes=[pltpu.VMEM((t, w), F32)] * 4 + [pltpu.VMEM((w, w), F32)],
        compiler_params=_cparams("parallel", "parallel"),
        name="rwkv_scan",
    )(r, kr, v, lw, a, g, *[x.reshape(1, d) for x in params])


def _rwkv_mixer_layer(h, hb, p, g, b, alpha, batch, tm):
    n, d = h.shape
    ident = lambda y: y
    xr, xw, xk, xv, xa, xg = _rwkv_mix(h, p["rwkv_mu"], batch, tm)
    (r,) = _proj(xr, p["rwkv_w_r"], 0, d, lambda y: (y,), [], [F32], tm, "rwkv_proj_r")
    (kr,) = _proj(xk, p["rwkv_w_k"], 0, d, lambda y: (y,), [], [F32], tm, "rwkv_proj_k")
    (v,) = _proj(xv, p["rwkv_w_v"], 0, d, lambda y: (y,), [], [F32], tm, "rwkv_proj_v")
    lw = _lora(xw, p["rwkv_w1"], p["rwkv_w2"], p["rwkv_w0"], jnp.tanh, _rwkv_log_decay, tm, "rwkv_lora_w")
    a = _lora(xa, p["rwkv_a1"], p["rwkv_a2"], p["rwkv_a0"], ident, _sigmoid, tm, "rwkv_lora_a")
    gate = _lora(xg, p["rwkv_g1"], p["rwkv_g2"], jnp.zeros((d,), F32), _sigmoid, ident, tm, "rwkv_lora_g")
    z = _rwkv_scan(r, kr, v, lw, a, gate, p, batch)
    return _mm_res_ln(z, p["rwkv_w_out"], h, g, b, alpha, tm)


def kernel(x, meta, ln_mix_g, ln_mix_b, ln_ffn_g, ln_ffn_b, conv_w_in, conv_w, conv_b, conv_w_out, rwkv_mu, rwkv_w_r, rwkv_w_k, rwkv_w_v, rwkv_w0, rwkv_w1, rwkv_w2, rwkv_a0, rwkv_a1, rwkv_a2, rwkv_g1, rwkv_g2, rwkv_k_k, rwkv_k_a, rwkv_r_k, rwkv_gn_g, rwkv_gn_b, rwkv_w_out, hgrn_w_in, hgrn_lb, hgrn_norm_g, hgrn_w_out, fox_w_in, fox_b_f, fox_q_norm_g, fox_k_norm_g, fox_w_out, ffn0_w1, ffn0_w3, ffn0_w2, moe1_router, moe1_router_b, moe1_w1, moe1_w3, moe1_w2, ffn2_w1, ffn2_w3, ffn2_w2, moe3_router, moe3_router_b, moe3_w1, moe3_w3, moe3_w2):
    batch, seq, d = x.shape
    depth = ln_mix_g.shape[0]
    assert depth == 4
    alpha = (2.0 * depth) ** 0.25
    t = N_META + seq
    n = batch * t
    tm = _divisor_tile(t, 768, 16)
    p = dict(
        conv_w_in=conv_w_in, conv_w=conv_w, conv_b=conv_b, conv_w_out=conv_w_out,
        rwkv_mu=rwkv_mu, rwkv_w_r=rwkv_w_r, rwkv_w_k=rwkv_w_k, rwkv_w_v=rwkv_w_v, rwkv_w0=rwkv_w0,
        rwkv_w1=rwkv_w1, rwkv_w2=rwkv_w2, rwkv_a0=rwkv_a0, rwkv_a1=rwkv_a1, rwkv_a2=rwkv_a2,
        rwkv_g1=rwkv_g1, rwkv_g2=rwkv_g2, rwkv_k_k=rwkv_k_k, rwkv_k_a=rwkv_k_a, rwkv_r_k=rwkv_r_k,
        rwkv_gn_g=rwkv_gn_g, rwkv_gn_b=rwkv_gn_b, rwkv_w_out=rwkv_w_out,
        hgrn_w_in=hgrn_w_in, hgrn_lb=hgrn_lb, hgrn_norm_g=hgrn_norm_g, hgrn_w_out=hgrn_w_out,
        fox_w_in=fox_w_in, fox_b_f=fox_b_f, fox_q_norm_g=fox_q_norm_g, fox_k_norm_g=fox_k_norm_g,
        fox_w_out=fox_w_out,
    )
    h = jnp.concatenate(
        [jnp.broadcast_to(meta[None].astype(x.dtype), (batch, N_META, d)), x], axis=1).reshape(n, d)
    hb = h.astype(BF16)

    h, hb = _conv_mixer_layer(h, hb, p, ln_mix_g[0], ln_mix_b[0], alpha, batch, tm)
    h, hb = _dense_ffn(hb, h, ffn0_w1, ffn0_w3, ffn0_w2, ln_ffn_g[0], ln_ffn_b[0], alpha, tm)
    h, hb = _rwkv_mixer_layer(h, hb, p, ln_mix_g[1], ln_mix_b[1], alpha, batch, tm)
    h, hb = _moe_ffn(h, hb, moe1_router, moe1_router_b, moe1_w1, moe1_w3, moe1_w2,
                     ln_ffn_g[1], ln_ffn_b[1], alpha, tm)
    h, hb = _hgrn_mixer_layer(h, hb, p, 2, ln_mix_g[2], ln_mix_b[2], alpha, batch, tm)
    h, hb = _dense_ffn(hb, h, ffn2_w1, ffn2_w3, ffn2_w2, ln_ffn_g[2], ln_ffn_b[2], alpha, tm)
    h, hb = _fox_mixer_layer(h, hb, p, ln_mix_g[3], ln_mix_b[3], alpha, batch, tm)
    h, hb = _moe_ffn(h, hb, moe3_router, moe3_router_b, moe3_w1, moe3_w3, moe3_w2,
                     ln_ffn_g[3], ln_ffn_b[3], alpha, tm)
    return h.reshape(batch, t, d)[:, N_META:]
```

```python
import functools

import jax
import jax.numpy as jnp
from jax import lax
from jax.experimental import pallas as pl
from jax.experimental.pallas import tpu as pltpu

F32 = jnp.float32
BF16 = jnp.bfloat16

N_META = 16
LN_EPS = 1e-5
RMS_EPS = 1e-6
RWKV_HEAD = 64
HGRN_HEAD = 128
FOX_HEAD = 128
TOP_K = 2
LANES = 128
VMEM_LIMIT_BYTES = 56 * 2**20


def _cparams(*sem):
    return pltpu.CompilerParams(dimension_semantics=sem, vmem_limit_bytes=VMEM_LIMIT_BYTES)


def _divisor_tile(n, cap, mult):
    best = None
    for d in range(mult, min(n, cap) + 1, mult):
        if n % d == 0:
            best = d
    assert best is not None, (n, cap, mult)
    return best


def _layer_norm(y, g, b):
    mu = jnp.mean(y, axis=-1, keepdims=True)
    yc = y - mu
    var = jnp.mean(yc * yc, axis=-1, keepdims=True)
    return yc * lax.rsqrt(var + LN_EPS) * g + b


def _sigmoid(x):
    return 1.0 / (1.0 + jnp.exp(-x))


def _silu(x):
    return x * _sigmoid(x)


def _mm_res_ln_kernel(z_ref, w_ref, h_ref, g_ref, b_ref, o_ref, ob_ref, acc_ref, *, nk, alpha):
    k = pl.program_id(1)
    part = jnp.dot(z_ref[...], w_ref[...].astype(BF16), preferred_element_type=F32)

    @pl.when(k == 0)
    def _():
        acc_ref[...] = part

    @pl.when(k > 0)
    def _():
        acc_ref[...] += part

    @pl.when(k == nk - 1)
    def _():
        y = _layer_norm(alpha * h_ref[...] + acc_ref[...], g_ref[...], b_ref[...])
        o_ref[...] = y
        ob_ref[...] = y.astype(BF16)


WEIGHT_SLICE_ROWS = 256


def _load_weight_bf16(w_hbm, wb_ref, stage_ref, sem):
    rows = stage_ref.shape[1]
    nslice = w_hbm.shape[0] // rows

    def copy(s):
        return pltpu.make_async_copy(w_hbm.at[pl.ds(s * rows, rows), :], stage_ref.at[s % 2], sem.at[s % 2])

    copy(0).start()
    for s in range(nslice):
        if s + 1 < nslice:
            copy(s + 1).start()
        copy(s).wait()
        wb_ref[s * rows:(s + 1) * rows, :] = stage_ref[s % 2].astype(BF16)


def _mm_res_ln_resident_kernel(z_ref, w_ref, h_ref, g_ref, b_ref, o_ref, ob_ref, wb_ref, stage_ref, sem,
                               *, alpha):
    @pl.when(pl.program_id(0) == 0)
    def _():
        _load_weight_bf16(w_ref, wb_ref, stage_ref, sem)

    y = alpha * h_ref[...] + jnp.dot(z_ref[...], wb_ref[...], preferred_element_type=F32)
    y = _layer_norm(y, g_ref[...], b_ref[...])
    o_ref[...] = y
    ob_ref[...] = y.astype(BF16)


RESIDENT_WEIGHT_BYTES = 16 * 2**20


def _mm_res_ln_resident(z, w, h, g, b, alpha, tm):
    n, kdim = z.shape
    d = w.shape[1]
    ws = _divisor_tile(kdim, WEIGHT_SLICE_ROWS, 8)
    const = lambda m: (0, 0)
    return pl.pallas_call(
        functools.partial(_mm_res_ln_resident_kernel, alpha=alpha),
        grid=(n // tm,),
        in_specs=[
            pl.BlockSpec((tm, kdim), lambda m: (m, 0)),
            pl.BlockSpec(memory_space=pl.ANY),
            pl.BlockSpec((tm, d), lambda m: (m, 0)),
            pl.BlockSpec((1, d), const),
            pl.BlockSpec((1, d), const),
        ],
        out_specs=[
            pl.BlockSpec((tm, d), lambda m: (m, 0)),
            pl.BlockSpec((tm, d), lambda m: (m, 0)),
        ],
        out_shape=[jax.ShapeDtypeStruct((n, d), F32), jax.ShapeDtypeStruct((n, d), BF16)],
        scratch_shapes=[pltpu.VMEM((kdim, d), BF16), pltpu.VMEM((2, ws, d), F32),
                        pltpu.SemaphoreType.DMA((2,))],
        compiler_params=_cparams("arbitrary"),
        name="mm_res_ln_resident",
    )(z, w, h, g.reshape(1, d), b.reshape(1, d))


def _mm_res_ln(z, w, h, g, b, alpha, tm):
    n, kdim = z.shape
    d = w.shape[1]
    if kdim * d * 4 <= RESIDENT_WEIGHT_BYTES:
        return _mm_res_ln_resident(z, w, h, g, b, alpha, tm)
    tk = _divisor_tile(kdim, 512, LANES)
    nk = kdim // tk
    return pl.pallas_call(
        functools.partial(_mm_res_ln_kernel, nk=nk, alpha=alpha),
        grid=(n // tm, nk),
        in_specs=[
            pl.BlockSpec((tm, tk), lambda m, k: (m, k)),
            pl.BlockSpec((tk, d), lambda m, k: (k, 0)),
            pl.BlockSpec((tm, d), lambda m, k: (m, 0)),
            pl.BlockSpec((1, d), lambda m, k: (0, 0)),
            pl.BlockSpec((1, d), lambda m, k: (0, 0)),
        ],
        out_specs=[
            pl.BlockSpec((tm, d), lambda m, k: (m, 0)),
            pl.BlockSpec((tm, d), lambda m, k: (m, 0)),
        ],
        out_shape=[jax.ShapeDtypeStruct((n, d), F32), jax.ShapeDtypeStruct((n, d), BF16)],
        scratch_shapes=[pltpu.VMEM((tm, d), F32)],
        compiler_params=_cparams("parallel", "arbitrary"),
        name="mm_res_ln",
    )(z, w, h, g.reshape(1, d), b.reshape(1, d))


def _ffn_up_kernel(exp_ref, nact_ref, x_ref, w1_ref, w3_ref, o_ref, w1b_ref, w3b_ref):
    c = pl.program_id(1)
    prev = exp_ref[jnp.maximum(c - 1, 0)]
    new_weights = jnp.logical_or(c == 0, exp_ref[c] != prev)

    @pl.when(new_weights)
    def _():
        w1b_ref[...] = w1_ref[...].astype(BF16)
        w3b_ref[...] = w3_ref[...].astype(BF16)

    @pl.when(c < nact_ref[0])
    def _():
        x = x_ref[...]
        a = jnp.dot(x, w1b_ref[...], preferred_element_type=F32)
        bb = jnp.dot(x, w3b_ref[...], preferred_element_type=F32)
        o_ref[...] = (_silu(a) * bb).astype(BF16)

    @pl.when(c >= nact_ref[0])
    def _():
        o_ref[...] = jnp.zeros_like(o_ref)


def _ffn_up(x, w1, w3, chunk_expert, n_active, rows):
    p, d = x.shape
    f = w1.shape[2]
    tf = _divisor_tile(f, 512, LANES)
    grid_spec = pltpu.PrefetchScalarGridSpec(
        num_scalar_prefetch=2,
        grid=(f // tf, p // rows),
        in_specs=[
            pl.BlockSpec((rows, d), lambda j, c, e, na: (c, 0)),
            pl.BlockSpec((None, d, tf), lambda j, c, e, na: (e[c], 0, j)),
            pl.BlockSpec((None, d, tf), lambda j, c, e, na: (e[c], 0, j)),
        ],
        out_specs=pl.BlockSpec((rows, tf), lambda j, c, e, na: (c, j)),
        scratch_shapes=[pltpu.VMEM((d, tf), BF16), pltpu.VMEM((d, tf), BF16)],
    )
    return pl.pallas_call(
        _ffn_up_kernel,
        grid_spec=grid_spec,
        out_shape=jax.ShapeDtypeStruct((p, f), BF16),
        compiler_params=_cparams("arbitrary", "arbitrary"),
        name="ffn_up",
    )(chunk_expert, n_active, x, w1, w3)


def _dense_ffn(hb, h, w1, w3, w2, g, b, alpha, tm):
    n = hb.shape[0]
    nchunks = n // tm
    hmid = _ffn_up(hb, w1[None], w3[None], jnp.zeros((nchunks,), jnp.int32),
                   jnp.full((1,), nchunks, jnp.int32), tm)
    return _mm_res_ln(hmid, w2, h, g, b, alpha, tm)


def _conv_proj_kernel(x_ref, wb_ref, wc_ref, wh_ref, cw_ref, cb_ref, o_ref,
                      wbb_ref, wcb_ref, whb_ref, carry_ref):
    bi = pl.program_id(1)
    ti = pl.program_id(2)

    @pl.when(jnp.logical_and(bi == 0, ti == 0))
    def _():
        wbb_ref[...] = wb_ref[...].astype(BF16)
        wcb_ref[...] = wc_ref[...].astype(BF16)
        whb_ref[...] = wh_ref[...].astype(BF16)

    @pl.when(ti == 0)
    def _():
        carry_ref[...] = jnp.zeros_like(carry_ref)

    x = x_ref[...]
    gate_b = jnp.dot(x, wbb_ref[...], preferred_element_type=F32)
    gate_c = jnp.dot(x, wcb_ref[...], preferred_element_type=F32)
    hh = jnp.dot(x, whb_ref[...], preferred_element_type=F32)
    u = gate_c * hh
    tm = u.shape[0]
    prev1 = carry_ref[7:8, :]
    prev2 = carry_ref[6:7, :]
    row = lax.broadcasted_iota(jnp.int32, (tm, 1), 0)
    r1 = jnp.where(row == 0, prev1, pltpu.roll(u, 1, axis=0))
    r2 = jnp.where(row == 0, prev2, jnp.where(row == 1, prev1, pltpu.roll(u, 2, axis=0)))
    v = cw_ref[0:1, :] * r2 + cw_ref[1:2, :] * r1 + cw_ref[2:3, :] * u + cb_ref[...]
    carry_ref[...] = u[tm - 8:, :]
    o_ref[...] = (gate_b * v).astype(BF16)


def _conv_proj(hb, w_in, conv_w, conv_b, batch, tm):
    n, d = hb.shape
    tn = _divisor_tile(d, 512, LANES)
    nd = d // tn
    tpb = n // batch // tm
    return pl.pallas_call(
        _conv_proj_kernel,
        grid=(nd, batch, tpb),
        in_specs=[
            pl.BlockSpec((tm, d), lambda j, bi, ti: (bi * tpb + ti, 0)),
            pl.BlockSpec((d, tn), lambda j, bi, ti: (0, j)),
            pl.BlockSpec((d, tn), lambda j, bi, ti: (0, nd + j)),
            pl.BlockSpec((d, tn), lambda j, bi, ti: (0, 2 * nd + j)),
            pl.BlockSpec((3, tn), lambda j, bi, ti: (0, j)),
            pl.BlockSpec((1, tn), lambda j, bi, ti: (0, j)),
        ],
        out_specs=pl.BlockSpec((tm, tn), lambda j, bi, ti: (bi * tpb + ti, j)),
        out_shape=jax.ShapeDtypeStruct((n, d), BF16),
        scratch_shapes=[pltpu.VMEM((d, tn), BF16)] * 3 + [pltpu.VMEM((8, tn), F32)],
        compiler_params=_cparams("arbitrary", "arbitrary", "arbitrary"),
        name="conv_proj",
    )(hb, w_in, w_in, w_in, conv_w, conv_b.reshape(1, d))


def _conv_mixer_layer(h, hb, p, g, b, alpha, batch, tm):
    z = _conv_proj(hb, p["conv_w_in"], p["conv_w"], p["conv_b"], batch, tm)
    return _mm_res_ln(z, p["conv_w_out"], h, g, b, alpha, tm)


MOE_ROWS = 256


def _split_bf16(a):
    hi = a.astype(BF16)
    lo = (a - hi.astype(F32)).astype(BF16)
    return hi, lo


def _dot_f32(a, b):
    ah, al = _split_bf16(a)
    bh, bl = _split_bf16(b)
    return (jnp.dot(ah, bh, preferred_element_type=F32)
            + (jnp.dot(ah, bl, preferred_element_type=F32)
               + jnp.dot(al, bh, preferred_element_type=F32)))


def _router_kernel(h_ref, w_ref, b_ref, info_ref, cnt_ref, carry_ref):
    i = pl.program_id(0)

    @pl.when(i == 0)
    def _():
        carry_ref[...] = jnp.zeros_like(carry_ref)

    logits = _dot_f32(h_ref[...], w_ref[...]) + b_ref[...]
    tm, ne = logits.shape
    lane = lax.broadcasted_iota(jnp.int32, (tm, ne), 1)
    m1 = jnp.max(logits, axis=-1, keepdims=True)
    i1 = jnp.min(jnp.where(logits == m1, lane, ne), axis=-1, keepdims=True)
    mask1 = lane == i1
    rest = jnp.where(mask1, -jnp.inf, logits)
    m2 = jnp.max(rest, axis=-1, keepdims=True)
    i2 = jnp.min(jnp.where(rest == m2, lane, ne), axis=-1, keepdims=True)
    mask2 = lane == i2
    dd = jnp.exp(m2 - m1)
    g1 = 1.0 / (1.0 + dd)
    g2 = dd / (1.0 + dd)
    sel = jnp.where(jnp.logical_or(mask1, mask2), 1.0, 0.0)
    r_i = lax.broadcasted_iota(jnp.int32, (tm, tm), 0)
    c_i = lax.broadcasted_iota(jnp.int32, (tm, tm), 1)
    tril = jnp.where(c_i < r_i, 1.0, 0.0).astype(BF16)
    rank = jnp.dot(tril, sel.astype(BF16), preferred_element_type=F32) + carry_ref[...]
    r1 = jnp.sum(jnp.where(mask1, rank, 0.0), axis=-1, keepdims=True)
    r2 = jnp.sum(jnp.where(mask2, rank, 0.0), axis=-1, keepdims=True)
    info = jnp.where(lane == 0, i1.astype(F32),
           jnp.where(lane == 1, i2.astype(F32),
           jnp.where(lane == 2, g1,
           jnp.where(lane == 3, g2,
           jnp.where(lane == 4, r1,
           jnp.where(lane == 5, r2, 0.0))))))
    info_ref[...] = info
    total = carry_ref[...] + jnp.sum(sel, axis=0, keepdims=True)
    carry_ref[...] = total
    cnt_ref[...] = total


def _router(h, w, b, tm):
    n, d = h.shape
    ne = w.shape[1]
    assert ne >= 6
    return pl.pallas_call(
        _router_kernel,
        grid=(n // tm,),
        in_specs=[
            pl.BlockSpec((tm, d), lambda i: (i, 0)),
            pl.BlockSpec((d, ne), lambda i: (0, 0)),
            pl.BlockSpec((1, ne), lambda i: (0, 0)),
        ],
        out_specs=[
            pl.BlockSpec((tm, ne), lambda i: (i, 0)),
            pl.BlockSpec((1, ne), lambda i: (0, 0)),
        ],
        out_shape=[jax.ShapeDtypeStruct((n, ne), F32), jax.ShapeDtypeStruct((1, ne), F32)],
        scratch_shapes=[pltpu.VMEM((1, ne), F32)],
        compiler_params=_cparams("arbitrary"),
        name="moe_router",
    )(h, w, b.reshape(1, ne))


def _row_copy(src_hbm, row, dst_vmem, r, sem):
    return pltpu.make_async_copy(src_hbm.at[pl.ds(row, 1), :], dst_vmem.at[pl.ds(r, 1), :], sem)


def _rows_wait(src_hbm, dst_vmem, sem):
    pltpu.make_async_copy(src_hbm.at[pl.ds(0, dst_vmem.shape[0]), :], dst_vmem, sem).wait()


def _moe_gather_kernel(src_ref, nact_ref, h_ref, o_ref, buf_ref, sem):
    c = pl.program_id(0)
    rows = buf_ref.shape[1]

    def issue(chunk):
        slot = chunk % 2

        def start(r, carry):
            _row_copy(h_ref, src_ref[chunk * rows + r], buf_ref.at[slot], r, sem.at[slot]).start()
            return carry

        lax.fori_loop(0, rows, start, 0, unroll=8)

    @pl.when(jnp.logical_and(c == 0, nact_ref[0] > 0))
    def _():
        issue(c)

    @pl.when(c + 1 < nact_ref[0])
    def _():
        issue(c + 1)

    @pl.when(c < nact_ref[0])
    def _():
        slot = c % 2
        _rows_wait(h_ref, buf_ref.at[slot], sem.at[slot])
        o_ref[...] = buf_ref[slot].astype(BF16)

    @pl.when(c >= nact_ref[0])
    def _():
        o_ref[...] = jnp.zeros_like(o_ref)


def _moe_gather(h, src, n_active, p, rows):
    n, d = h.shape
    grid_spec = pltpu.PrefetchScalarGridSpec(
        num_scalar_prefetch=2,
        grid=(p // rows,),
        in_specs=[pl.BlockSpec(memory_space=pl.ANY)],
        out_specs=pl.BlockSpec((rows, d), lambda c, s, na: (c, 0)),
        scratch_shapes=[pltpu.VMEM((2, rows, d), F32), pltpu.SemaphoreType.DMA((2,))],
    )
    return pl.pallas_call(
        _moe_gather_kernel,
        grid_spec=grid_spec,
        out_shape=jax.ShapeDtypeStruct((p, d), BF16),
        compiler_params=_cparams("arbitrary"),
        name="moe_gather",
    )(src, n_active, h)


def _moe_combine_kernel(p1_ref, p2_ref, y_ref, h_ref, info_ref, g_ref, b_ref, o_ref, ob_ref,
                        buf1_ref, buf2_ref, sem, *, alpha):
    i = pl.program_id(0)
    tm = buf1_ref.shape[1]

    def issue(tile):
        slot = tile % 2

        def start(r, carry):
            _row_copy(y_ref, p1_ref[tile * tm + r], buf1_ref.at[slot], r, sem.at[slot]).start()
            _row_copy(y_ref, p2_ref[tile * tm + r], buf2_ref.at[slot], r, sem.at[slot]).start()
            return carry

        lax.fori_loop(0, tm, start, 0, unroll=8)

    @pl.when(i == 0)
    def _():
        issue(i)

    @pl.when(i + 1 < pl.num_programs(0))
    def _():
        issue(i + 1)

    slot = i % 2
    _rows_wait(y_ref, buf1_ref.at[slot], sem.at[slot])
    _rows_wait(y_ref, buf2_ref.at[slot], sem.at[slot])
    info = info_ref[...]
    y = alpha * h_ref[...] + (info[:, 2:3] * buf1_ref[slot] + info[:, 3:4] * buf2_ref[slot])
    y = _layer_norm(y, g_ref[...], b_ref[...])
    o_ref[...] = y
    ob_ref[...] = y.astype(BF16)


def _moe_combine(y, h, info, p1, p2, g, b, alpha, tm):
    n, d = h.shape
    ne = info.shape[1]
    grid_spec = pltpu.PrefetchScalarGridSpec(
        num_scalar_prefetch=2,
        grid=(n // tm,),
        in_specs=[
            pl.BlockSpec(memory_space=pl.ANY),
            pl.BlockSpec((tm, d), lambda i, a, c: (i, 0)),
            pl.BlockSpec((tm, ne), lambda i, a, c: (i, 0)),
            pl.BlockSpec((1, d), lambda i, a, c: (0, 0)),
            pl.BlockSpec((1, d), lambda i, a, c: (0, 0)),
        ],
        out_specs=[
            pl.BlockSpec((tm, d), lambda i, a, c: (i, 0)),
            pl.BlockSpec((tm, d), lambda i, a, c: (i, 0)),
        ],
        scratch_shapes=[pltpu.VMEM((2, tm, d), F32), pltpu.VMEM((2, tm, d), F32),
                        pltpu.SemaphoreType.DMA((2,))],
    )
    return pl.pallas_call(
        functools.partial(_moe_combine_kernel, alpha=alpha),
        grid_spec=grid_spec,
        out_shape=[jax.ShapeDtypeStruct((n, d), F32), jax.ShapeDtypeStruct((n, d), BF16)],
        compiler_params=_cparams("arbitrary"),
        name="moe_combine",
    )(p1, p2, y, h, info, g.reshape(1, d), b.reshape(1, d))


MOE_PASS_BLOCKS = 9
MOE_F_TILE = 256


def _expert_ffn_kernel(pe_ref, ps_ref, pn_ref, nu_ref, xs_ref, w1_ref, w3_ref, w2_ref, y_ref,
                       x_buf, acc_ref, w1b_ref, w3b_ref, w2b_ref, sem):
    p = pl.program_id(0)
    f = pl.program_id(1)
    nf = pl.num_programs(1)
    nb = pn_ref[p]
    sb = MOE_ROWS
    start = pl.multiple_of(ps_ref[p], MOE_ROWS)

    @pl.when(nb > 0)
    def _():
        @pl.when(f == 0)
        def _():
            cp = pltpu.make_async_copy(xs_ref.at[pl.ds(start, x_buf.shape[0]), :], x_buf, sem.at[0])
            cp.start()
            cp.wait()

        w1b_ref[...] = w1_ref[...].astype(BF16)
        w3b_ref[...] = w3_ref[...].astype(BF16)
        w2b_ref[...] = w2_ref[...].astype(BF16)

        def up(i):
            x = x_buf[pl.ds(pl.multiple_of(i * sb, sb), sb), :]
            a = jnp.dot(x, w1b_ref[...], preferred_element_type=F32)
            bb = jnp.dot(x, w3b_ref[...], preferred_element_type=F32)
            return (_silu(a) * bb).astype(BF16)

        def down(i, hmid, first):
            rows = pl.ds(pl.multiple_of(i * sb, sb), sb)
            part = jnp.dot(hmid, w2b_ref[...], preferred_element_type=F32)
            if first:
                acc_ref[rows, :] = part
            else:
                acc_ref[rows, :] += part

        def sweep(first):
            def body(i, hprev):
                down(i - 1, hprev, first)
                return up(i)

            down(nb - 1, lax.fori_loop(1, nb, body, up(0)), first)

        @pl.when(f == 0)
        def _():
            sweep(True)

        @pl.when(f > 0)
        def _():
            sweep(False)

        @pl.when(f == nf - 1)
        def _():
            def out_copy(i):
                rows = pl.ds(pl.multiple_of(i * sb, sb), sb)
                dst = pl.ds(pl.multiple_of(start + i * sb, sb), sb)
                return pltpu.make_async_copy(acc_ref.at[rows, :], y_ref.at[dst, :], sem.at[1])

            lax.fori_loop(0, nb, lambda i, c: (out_copy(i).start(), c)[1], 0)
            lax.fori_loop(0, nb, lambda i, c: (out_copy(i).wait(), c)[1], 0)

    @pl.when(jnp.logical_and(p == pl.num_programs(0) - 1, f == nf - 1))
    def _():
        acc_ref[0:sb, :] = jnp.zeros((sb, acc_ref.shape[1]), F32)

        def zero_copy(i):
            dst = pl.ds(pl.multiple_of(i * sb, sb), sb)
            return pltpu.make_async_copy(acc_ref.at[0:sb, :], y_ref.at[dst, :], sem.at[1])

        n_blocks = y_ref.shape[0] // sb
        lax.fori_loop(nu_ref[1], n_blocks, lambda i, c: (zero_copy(i).start(), c)[1], 0)
        lax.fori_loop(nu_ref[1], n_blocks, lambda i, c: (zero_copy(i).wait(), c)[1], 0)


def _expert_ffn(xs, w1, w3, w2, pass_expert, pass_start, pass_nb, n_used, p_rows):
    d = xs.shape[1]
    f = w1.shape[2]
    tf = _divisor_tile(f, MOE_F_TILE, LANES)
    nf = f // tf
    npass = pass_expert.shape[0]
    r = MOE_PASS_BLOCKS * MOE_ROWS

    def fidx(p, j, nu):
        return jnp.where(p < nu[0], j, nf - 1)

    grid_spec = pltpu.PrefetchScalarGridSpec(
        num_scalar_prefetch=4,
        grid=(npass, nf),
        in_specs=[
            pl.BlockSpec(memory_space=pl.ANY),
            pl.BlockSpec((None, d, tf), lambda p, j, pe, ps, pn, nu: (pe[p], 0, fidx(p, j, nu))),
            pl.BlockSpec((None, d, tf), lambda p, j, pe, ps, pn, nu: (pe[p], 0, fidx(p, j, nu))),
            pl.BlockSpec((None, tf, d), lambda p, j, pe, ps, pn, nu: (pe[p], fidx(p, j, nu), 0)),
        ],
        out_specs=pl.BlockSpec(memory_space=pl.ANY),
        scratch_shapes=[
            pltpu.VMEM((r, d), BF16), pltpu.VMEM((r, d), F32),
            pltpu.VMEM((d, tf), BF16), pltpu.VMEM((d, tf), BF16), pltpu.VMEM((tf, d), BF16),
            pltpu.SemaphoreType.DMA((2,)),
        ],
    )
    return pl.pallas_call(
        _expert_ffn_kernel,
        grid_spec=grid_spec,
        out_shape=jax.ShapeDtypeStruct((p_rows, d), F32),
        compiler_params=_cparams("arbitrary", "arbitrary"),
        name="expert_ffn",
    )(pass_expert, pass_start, pass_nb, n_used, xs, w1, w3, w2)


def _moe_ffn(h, hb, router_w, router_b, w1, w3, w2, g, b, alpha, tm):
    n, d = h.shape
    ne = router_w.shape[1]
    rows = MOE_ROWS
    info, counts = _router(h, router_w, router_b, tm)
    counts = counts[0].astype(jnp.int32)
    nblk_e = (counts + rows - 1) // rows
    blk_end = jnp.cumsum(nblk_e)
    starts = (blk_end - nblk_e) * rows
    n_blocks = (n * TOP_K + ne * (rows - 1)) // rows
    p = n_blocks * rows
    n_active = blk_end[-1:].astype(jnp.int32)
    npass_e = (nblk_e + MOE_PASS_BLOCKS - 1) // MOE_PASS_BLOCKS
    pass_end = jnp.cumsum(npass_e)
    max_pass = n_blocks // MOE_PASS_BLOCKS + ne
    pidx = jnp.arange(max_pass, dtype=jnp.int32)
    n_used = jnp.stack([pass_end[-1], blk_end[-1]]).astype(jnp.int32)
    last_expert = jnp.sum(pass_end < pass_end[-1]).astype(jnp.int32)
    pass_expert = jnp.minimum(jnp.sum(pidx[:, None] >= pass_end[None, :], axis=1), last_expert).astype(jnp.int32)
    local = pidx - (pass_end - npass_e)[pass_expert]
    pass_start = (starts[pass_expert] + local * (MOE_PASS_BLOCKS * rows)).astype(jnp.int32)
    pass_nb = jnp.where(pidx < n_used[0],
                        jnp.clip(nblk_e[pass_expert] - local * MOE_PASS_BLOCKS, 0, MOE_PASS_BLOCKS),
                        0).astype(jnp.int32)
    pass_start = jnp.where(pass_nb > 0, pass_start, 0).astype(jnp.int32)
    i1 = info[:, 0].astype(jnp.int32)
    i2 = info[:, 1].astype(jnp.int32)
    p1 = starts[i1] + info[:, 4].astype(jnp.int32)
    p2 = starts[i2] + info[:, 5].astype(jnp.int32)
    tok = jnp.arange(n, dtype=jnp.int32)
    p_in = p + MOE_PASS_BLOCKS * rows
    src = jnp.zeros((p_in,), jnp.int32).at[p1].set(tok).at[p2].set(tok)
    xs = _moe_gather(h, src, n_active, p_in, rows)
    y = _expert_ffn(xs, w1, w3, w2, pass_expert, pass_start, pass_nb, n_used, p)
    tmc = _divisor_tile(n, 384, 16)
    return _moe_combine(y, h, info, p1, p2, g, b, alpha, tmc)


def _proj_kernel(*refs, epilogue, n_extra, n_out):
    x_ref, w_ref = refs[0], refs[1]
    extra = refs[2:2 + n_extra]
    outs = refs[2 + n_extra:2 + n_extra + n_out]
    wb_ref = refs[2 + n_extra + n_out]

    @pl.when(pl.program_id(1) == 0)
    def _():
        wb_ref[...] = w_ref[...].astype(BF16)

    y = jnp.dot(x_ref[...], wb_ref[...], preferred_element_type=F32)
    res = epilogue(y, *[e[...] for e in extra])
    for o_ref, r in zip(outs, res):
        o_ref[...] = r.astype(o_ref.dtype)


def _proj(xb, w, col0, ncols, epilogue, extras, out_dtypes, tm, name, tn_cap=1024):
    n, kdim = xb.shape
    tn = _divisor_tile(ncols, tn_cap, LANES)
    assert col0 % tn == 0
    off = col0 // tn
    outs = pl.pallas_call(
        functools.partial(_proj_kernel, epilogue=epilogue, n_extra=len(extras), n_out=len(out_dtypes)),
        grid=(ncols // tn, n // tm),
        in_specs=[
            pl.BlockSpec((tm, kdim), lambda j, m: (m, 0)),
            pl.BlockSpec((kdim, tn), lambda j, m: (0, off + j)),
        ] + [pl.BlockSpec((1, tn), lambda j, m: (0, j))] * len(extras),
        out_specs=[pl.BlockSpec((tm, tn), lambda j, m: (m, j))] * len(out_dtypes),
        out_shape=[jax.ShapeDtypeStruct((n, ncols), dt) for dt in out_dtypes],
        scratch_shapes=[pltpu.VMEM((kdim, tn), BF16)],
        compiler_params=_cparams("arbitrary", "arbitrary"),
        name=name,
    )(xb, w, *[e.reshape(1, ncols) for e in extras])
    return outs


def _tril_bf16(c, inclusive):
    r_i = lax.broadcasted_iota(jnp.int32, (c, c), 0)
    c_i = lax.broadcasted_iota(jnp.int32, (c, c), 1)
    keep = (c_i <= r_i) if inclusive else (c_i < r_i)
    return jnp.where(keep, 1.0, 0.0).astype(BF16)


def _cumsum_rows(x, tril):
    hi = x.astype(BF16)
    r1 = x - hi.astype(F32)
    mid = r1.astype(BF16)
    lo = (r1 - mid.astype(F32)).astype(BF16)
    return (jnp.dot(tril, hi, preferred_element_type=F32)
            + (jnp.dot(tril, mid, preferred_element_type=F32)
               + jnp.dot(tril, lo, preferred_element_type=F32)))


def _dot_nt(a, b):
    return lax.dot_general(a.astype(BF16), b.astype(BF16), (((1,), (1,)), ((), ())),
                           preferred_element_type=F32)


def _dot_tn(a, b):
    return lax.dot_general(a.astype(BF16), b.astype(BF16), (((0,), (0,)), ((), ())),
                           preferred_element_type=F32)


def _dot_nn(a, b):
    return jnp.dot(a.astype(BF16), b.astype(BF16), preferred_element_type=F32)


def _chunk_len(t):
    return _divisor_tile(t, 64, 16)


HGRN_SUB = 16


def _hgrn_scan_kernel(q_ref, lf_ref, v_ref, gs_ref, ng_ref, o_ref, st_ref, *, chunk):
    t, w = q_ref.shape
    nhead = w // HGRN_HEAD
    nsub = chunk // HGRN_SUB
    st_ref[...] = jnp.zeros_like(st_ref)
    tril = _tril_bf16(chunk, True)
    row16 = lax.broadcasted_iota(jnp.int32, (HGRN_SUB, 1), 0)

    def head_chunk(q, lf, v, st):
        k = 1.0 - jnp.exp(lf)
        cum = _cumsum_rows(lf, tril)
        o_inter = _dot_nt(q * jnp.exp(cum), st)
        vb = v.astype(BF16)
        outs = []
        for i in range(nsub):
            lo, hi = i * HGRN_SUB, (i + 1) * HGRN_SUB
            qi, ki, vi, cumi = q[lo:hi], k[lo:hi], v[lo:hi], cum[lo:hi]
            oi = o_inter[lo:hi]
            if i > 0:
                ci = cum[lo:lo + 1]
                qt = qi * jnp.exp(cumi - ci)
                kt = k[0:lo] * jnp.exp(ci - cum[0:lo])
                oi = oi + jnp.dot(_dot_nt(qt, kt).astype(BF16), vb[0:lo], preferred_element_type=F32)
            for s in range(HGRN_SUB):
                dec = jnp.exp(jnp.minimum(cumi - cumi[s:s + 1], 0.0))
                col = jnp.sum(qi * dec * ki[s:s + 1], axis=-1, keepdims=True)
                col = jnp.where(row16 >= s, col, 0.0)
                oi = oi + col * vi[s:s + 1]
            outs.append(oi)
        o = jnp.concatenate(outs, axis=0)
        cl = cum[chunk - 1:chunk]
        kd = k * jnp.exp(cl - cum)
        st_new = st * jnp.exp(cl) + _dot_tn(v, kd)
        o = o * lax.rsqrt(jnp.mean(o * o, axis=-1, keepdims=True) + RMS_EPS)
        return o, st_new

    def body(c, carry):
        rows = pl.ds(pl.multiple_of(c * chunk, 16), chunk)
        q = q_ref[rows, :].astype(F32)
        lf = lf_ref[rows, :]
        v = v_ref[rows, :].astype(F32)
        outs = []
        for hh in range(nhead):
            cols = slice(hh * HGRN_HEAD, (hh + 1) * HGRN_HEAD)
            o, st_new = head_chunk(q[:, cols], lf[:, cols], v[:, cols], st_ref[hh])
            st_ref[hh] = st_new
            outs.append(o)
        o = jnp.concatenate(outs, axis=1)
        o_ref[rows, :] = (o * ng_ref[...] * gs_ref[rows, :].astype(F32)).astype(BF16)
        return carry

    lax.fori_loop(0, t // chunk, body, 0)


HGRN_GROUP = 4


def _hgrn_scan(q, lf, v, gs, norm_g, batch):
    n, d = q.shape
    t = n // batch
    chunk = _chunk_len(t)
    w = min(d, HGRN_GROUP * HGRN_HEAD)
    blk = pl.BlockSpec((t, w), lambda b, j: (b, j))
    return pl.pallas_call(
        functools.partial(_hgrn_scan_kernel, chunk=chunk),
        grid=(batch, d // w),
        in_specs=[blk, blk, blk, blk, pl.BlockSpec((1, w), lambda b, j: (0, j))],
        out_specs=blk,
        out_shape=jax.ShapeDtypeStruct((n, d), BF16),
        scratch_shapes=[pltpu.VMEM((w // HGRN_HEAD, HGRN_HEAD, HGRN_HEAD), F32)],
        compiler_params=_cparams("parallel", "parallel"),
        name="hgrn_scan",
    )(q, lf, v, gs, norm_g.reshape(1, d))


def _hgrn_mixer_layer(h, hb, p, layer_idx, g, b, alpha, batch, tm):
    d = h.shape[1]
    w_in = p["hgrn_w_in"]
    lb = jnp.cumsum(jax.nn.softmax(p["hgrn_lb"].astype(F32), axis=0), axis=0)
    lb = lb[layer_idx] - lb[0]
    (q,) = _proj(hb, w_in, 0, d, lambda y: (_silu(y),), [], [BF16], tm, "hgrn_proj_q")
    (lf,) = _proj(hb, w_in, d, d, lambda y, lbv: (jnp.log(lbv + (1.0 - lbv) * _sigmoid(y)),),
                  [lb], [F32], tm, "hgrn_proj_f")
    (v,) = _proj(hb, w_in, 2 * d, d, lambda y: (y,), [], [BF16], tm, "hgrn_proj_i")
    (gs,) = _proj(hb, w_in, 3 * d, d, lambda y: (_silu(y),), [], [BF16], tm, "hgrn_proj_g")
    z = _hgrn_scan(q, lf, v, gs, p["hgrn_norm_g"], batch)
    return _mm_res_ln(z, p["hgrn_w_out"], h, g, b, alpha, tm)


def _fox_gate_kernel(h_ref, w_ref, bf_ref, c_ref, carry_ref):
    @pl.when(pl.program_id(1) == 0)
    def _():
        carry_ref[...] = jnp.zeros_like(carry_ref)

    x = _dot_f32(h_ref[...], w_ref[...]) + bf_ref[...]
    log_f = jnp.minimum(x, 0.0) - jnp.log(1.0 + jnp.exp(-jnp.abs(x)))
    tm = x.shape[0]
    c = _cumsum_rows(log_f, _tril_bf16(tm, True)) + carry_ref[...]
    c_ref[...] = c
    carry_ref[...] = c[tm - 1:tm, :]


def _fox_gate(h, w_f, b_f, batch, tm):
    n, d = h.shape
    nh = w_f.shape[1]
    tpb = n // batch // tm
    return pl.pallas_call(
        _fox_gate_kernel,
        grid=(batch, tpb),
        in_specs=[
            pl.BlockSpec((tm, d), lambda b, t: (b * tpb + t, 0)),
            pl.BlockSpec((d, nh), lambda b, t: (0, 0)),
            pl.BlockSpec((1, nh), lambda b, t: (0, 0)),
        ],
        out_specs=pl.BlockSpec((tm, nh), lambda b, t: (b * tpb + t, 0)),
        out_shape=jax.ShapeDtypeStruct((n, nh), F32),
        scratch_shapes=[pltpu.VMEM((1, nh), F32)],
        compiler_params=_cparams("arbitrary", "arbitrary"),
        name="fox_gate",
    )(h, w_f, b_f.reshape(1, nh))


def _fox_attn_kernel(q_ref, k_ref, v_ref, sg_ref, c_ref, ct_ref, o_ref, *, tq):
    hd = pl.program_id(1)
    t = q_ref.shape[0]
    nh = c_ref.shape[1]
    lane = lax.broadcasted_iota(jnp.int32, (t, nh), 1)
    c_col = jnp.sum(jnp.where(lane == hd, c_ref[...], 0.0), axis=-1, keepdims=True)
    c_row = ct_ref[pl.ds(hd, 1), :]
    for i in range(t // tq):
        lo, hi = i * tq, (i + 1) * tq
        s = lax.dot_general(q_ref[lo:hi, :], k_ref[0:hi, :], (((1,), (1,)), ((), ())),
                            preferred_element_type=F32)
        s = s + c_col[lo:hi] - c_row[:, 0:hi]
        r_i = lax.broadcasted_iota(jnp.int32, (tq, hi), 0) + lo
        c_i = lax.broadcasted_iota(jnp.int32, (tq, hi), 1)
        s = jnp.where(c_i <= r_i, s, -jnp.inf)
        m = jnp.max(s, axis=-1, keepdims=True)
        p = jnp.exp(s - m)
        l = jnp.sum(p, axis=-1, keepdims=True)
        o = jnp.dot(p.astype(BF16), v_ref[0:hi, :], preferred_element_type=F32) / l
        o_ref[lo:hi, :] = (o * sg_ref[lo:hi, :]).astype(BF16)


def _fox_attn(q, k, v, sg, c, ct, batch):
    n, d = q.shape
    t = n // batch
    nh = d // FOX_HEAD
    tq = _divisor_tile(t, 768, 16)
    blk = pl.BlockSpec((t, FOX_HEAD), lambda b, h: (b, h))
    return pl.pallas_call(
        functools.partial(_fox_attn_kernel, tq=tq),
        grid=(batch, nh),
        in_specs=[blk, blk, blk, blk,
                  pl.BlockSpec((t, nh), lambda b, h: (b, 0)),
                  pl.BlockSpec((None, nh, t), lambda b, h: (b, 0, 0))],
        out_specs=blk,
        out_shape=jax.ShapeDtypeStruct((n, d), BF16),
        compiler_params=_cparams("parallel", "parallel"),
        name="fox_attn",
    )(q, k, v, sg, c, ct)


def _head_rms_epilogue(scale):
    def epi(y, gain):
        outs = []
        for j in range(y.shape[1] // FOX_HEAD):
            yj = y[:, j * FOX_HEAD:(j + 1) * FOX_HEAD]
            yj = yj * lax.rsqrt(jnp.mean(yj * yj, axis=-1, keepdims=True) + RMS_EPS)
            outs.append(yj * gain[:, j * FOX_HEAD:(j + 1) * FOX_HEAD] * scale)
        return (jnp.concatenate(outs, axis=1),)
    return epi


def _fox_mixer_layer(h, hb, p, g, b, alpha, batch, tm):
    n, d = h.shape
    nh = d // FOX_HEAD
    w_in = p["fox_w_in"]
    qg = jnp.tile(p["fox_q_norm_g"], nh)
    kg = jnp.tile(p["fox_k_norm_g"], nh)
    (q,) = _proj(hb, w_in, 0, d, _head_rms_epilogue(FOX_HEAD ** -0.5), [qg], [BF16], tm, "fox_proj_q")
    (k,) = _proj(hb, w_in, d, d, _head_rms_epilogue(1.0), [kg], [BF16], tm, "fox_proj_k")
    (v,) = _proj(hb, w_in, 2 * d, d, lambda y: (y,), [], [BF16], tm, "fox_proj_v")
    (sg,) = _proj(hb, w_in, 3 * d, d, lambda y: (_sigmoid(y),), [], [F32], tm, "fox_proj_g")
    c = _fox_gate(h, w_in[:, 4 * d:], p["fox_b_f"], batch, tm)
    ct = c.reshape(batch, n // batch, nh).transpose(0, 2, 1)
    z = _fox_attn(q, k, v, sg, c, ct, batch)
    return _mm_res_ln(z, p["fox_w_out"], h, g, b, alpha, tm)


def _rwkv_mix_kernel(h_ref, mu_ref, *refs):
    outs, carry_ref = refs[:-1], refs[-1]

    @pl.when(pl.program_id(1) == 0)
    def _():
        carry_ref[...] = jnp.zeros_like(carry_ref)

    x = h_ref[...]
    tm = x.shape[0]
    row = lax.broadcasted_iota(jnp.int32, (tm, 1), 0)
    prev = jnp.where(row == 0, carry_ref[7:8, :], pltpu.roll(x, 1, axis=0))
    xx = prev - x
    carry_ref[...] = x[tm - 8:, :]
    for j, o_ref in enumerate(outs):
        o_ref[...] = (x + xx * mu_ref[j:j + 1, :]).astype(BF16)


def _rwkv_mix(h, mu, batch, tm):
    n, d = h.shape
    nmix = mu.shape[0]
    tpb = n // batch // tm
    blk = pl.BlockSpec((tm, d), lambda b, t: (b * tpb + t, 0))
    return pl.pallas_call(
        _rwkv_mix_kernel,
        grid=(batch, tpb),
        in_specs=[blk, pl.BlockSpec((nmix, d), lambda b, t: (0, 0))],
        out_specs=[blk] * nmix,
        out_shape=[jax.ShapeDtypeStruct((n, d), BF16)] * nmix,
        scratch_shapes=[pltpu.VMEM((8, d), F32)],
        compiler_params=_cparams("arbitrary", "arbitrary"),
        name="rwkv_mix",
    )(h, mu)


def _lora_kernel(x_ref, wa_ref, wb_ref, bias_ref, o_ref, wab_ref, wbb_ref, *, mid_act, out_act):
    @pl.when(pl.program_id(0) == 0)
    def _():
        wab_ref[...] = wa_ref[...].astype(BF16)
        wbb_ref[...] = wb_ref[...].astype(BF16)

    mid = mid_act(jnp.dot(x_ref[...], wab_ref[...], preferred_element_type=F32))
    y = jnp.dot(mid.astype(BF16), wbb_ref[...], preferred_element_type=F32)
    o_ref[...] = out_act(bias_ref[...] + y)


def _lora(xb, wa, wb, bias, mid_act, out_act, tm, name):
    n, d = xb.shape
    r = wa.shape[1]
    dout = wb.shape[1]
    return pl.pallas_call(
        functools.partial(_lora_kernel, mid_act=mid_act, out_act=out_act),
        grid=(n // tm,),
        in_specs=[
            pl.BlockSpec((tm, d), lambda i: (i, 0)),
            pl.BlockSpec((d, r), lambda i: (0, 0)),
            pl.BlockSpec((r, dout), lambda i: (0, 0)),
            pl.BlockSpec((1, dout), lambda i: (0, 0)),
        ],
        out_specs=pl.BlockSpec((tm, dout), lambda i: (i, 0)),
        out_shape=jax.ShapeDtypeStruct((n, dout), F32),
        scratch_shapes=[pltpu.VMEM((d, r), BF16), pltpu.VMEM((r, dout), BF16)],
        compiler_params=_cparams("arbitrary"),
        name=name,
    )(xb, wa, wb, bias.reshape(1, dout))


def _rwkv_log_decay(z):
    w_log = -(jnp.maximum(-z, 0.0) + jnp.log(1.0 + jnp.exp(-jnp.abs(z)))) - 0.5
    return -jnp.exp(w_log)


RWKV_GROUP = 4
RWKV_UNROLL = 4


def _seg_sum(x, bd):
    hi = x.astype(BF16)
    lo = (x - hi.astype(F32)).astype(BF16)
    return jnp.dot(hi, bd, preferred_element_type=F32) + jnp.dot(lo, bd, preferred_element_type=F32)


def _rwkv_scan_kernel(r_ref, kr_ref, v_ref, lw_ref, a_ref, g_ref, kk_p, ka_p, rk_p, gg_p, gb_p,
                      o_ref, kk_s, k_s, bonus_s, y_s, st_ref, *, chunk, ptile):
    t, w = r_ref.shape
    nhead = w // RWKV_HEAD
    sc = nhead * chunk
    lane_r = lax.broadcasted_iota(jnp.int32, (w, w), 0) // RWKV_HEAD
    lane_c = lax.broadcasted_iota(jnp.int32, (w, w), 1) // RWKV_HEAD
    bd = jnp.where(lane_r == lane_c, 1.0, 0.0).astype(BF16)

    def prologue(i, carry):
        rows = pl.ds(pl.multiple_of(i * ptile, 8), ptile)
        kr = kr_ref[rows, :]
        a = a_ref[rows, :]
        kkr = kr * kk_p[...]
        nrm = jnp.maximum(jnp.sqrt(_seg_sum(kkr * kkr, bd)), 1e-12)
        kk_s[rows, :] = kkr / nrm
        k = kr * (1.0 + (a - 1.0) * ka_p[...])
        k_s[rows, :] = k
        bonus_s[rows, :] = _seg_sum(r_ref[rows, :] * k * rk_p[...], bd) * v_ref[rows, :]
        return carry

    lax.fori_loop(0, t // ptile, prologue, 0)

    st_ref[...] = jnp.zeros_like(st_ref)
    tril = _tril_bf16(chunk, True)
    head_of_lane = lax.broadcasted_iota(jnp.int32, (chunk, w), 1) // RWKV_HEAD
    ri = lax.broadcasted_iota(jnp.int32, (2 * sc, sc), 0)
    ci = lax.broadcasted_iota(jnp.int32, (2 * sc, sc), 1)
    low_mask = ci < jnp.where(ri < sc, ri, ri - sc + 1)
    nsteps = max(1, (chunk - 1).bit_length())

    def stack(x):
        return jnp.concatenate([jnp.where(head_of_lane == hh, x, 0.0) for hh in range(nhead)], axis=0)

    eye = jnp.where(lax.broadcasted_iota(jnp.int32, (sc, sc), 0)
                    == lax.broadcasted_iota(jnp.int32, (sc, sc), 1), 1.0, 0.0)

    def prepare(c):
        start = c * chunk
        rows = pl.ds(start if isinstance(start, int) else pl.multiple_of(start, 16), chunk)
        r = r_ref[rows, :]
        v = v_ref[rows, :]
        lw = lw_ref[rows, :]
        a = a_ref[rows, :]
        kk = kk_s[rows, :]
        k = k_s[rows, :]
        cum = _cumsum_rows(lw, tril)
        e_neg = jnp.exp(-cum)
        at2 = stack(-kk * jnp.exp(cum - lw))
        rt2 = stack(r * jnp.exp(cum))
        bvec = kk * a
        bb2 = stack(bvec * e_neg)
        kb2 = stack(k * e_neg)
        v2 = stack(v)
        ar2 = jnp.concatenate([at2, rt2], axis=0).astype(BF16)
        cl = cum[chunk - 1:chunk]
        e_end = jnp.exp(cl - cum)
        khbh = jnp.concatenate([stack(k * e_end), stack(bvec * e_end)], axis=0).astype(BF16)
        yield None
        pb = jnp.where(low_mask, _dot_nt(ar2, bb2), 0.0)
        pk = jnp.where(low_mask, _dot_nt(ar2, kb2), 0.0)
        m_ab, m_rb = pb[:sc], pb[sc:]
        m_ak, m_rk = pk[:sc], pk[sc:]
        yield None
        u0 = _dot_nn(m_ak, v2)
        y0 = _dot_nn(m_rk, v2)
        tinv = eye + m_ab
        lpow = m_ab
        for _ in range(nsteps - 1):
            yield None
            lpow = _dot_nn(lpow, lpow)
            tinv = tinv + _dot_nn(tinv, lpow)
        yield dict(rows=rows, ar2=ar2, tinv=tinv.astype(BF16), m_rb=m_rb.astype(BF16),
                   u0=u0, y0=y0, v2=v2.astype(BF16), khbh=khbh, decay=jnp.exp(cl))

    def prepare_interleaved(chunk_ids):
        gens = [prepare(c) for c in chunk_ids]
        done = [None] * len(gens)
        while any(d is None for d in done):
            for j, gen in enumerate(gens):
                if done[j] is None:
                    out = next(gen)
                    if out is not None:
                        done[j] = out
        return done

    def advance(pc, st):
        ps = _dot_nt(pc["ar2"], st)
        u2 = jnp.dot(pc["tinv"], (ps[:sc] + pc["u0"]).astype(BF16), preferred_element_type=F32)
        u2b = u2.astype(BF16)
        y2 = ps[sc:] + pc["y0"] + jnp.dot(pc["m_rb"], u2b, preferred_element_type=F32)
        y = y2[0:chunk]
        for hh in range(1, nhead):
            y = y + y2[hh * chunk:(hh + 1) * chunk]
        y_s[pc["rows"], :] = y
        return st * pc["decay"] + _dot_tn(jnp.concatenate([pc["v2"], u2b], axis=0), pc["khbh"])

    nchunks = t // chunk

    def body(i, carry):
        prepared = prepare_interleaved([i * RWKV_UNROLL + j for j in range(RWKV_UNROLL)])
        st = st_ref[...]
        for pc in prepared:
            st = advance(pc, st)
        st_ref[...] = st
        return carry

    lax.fori_loop(0, nchunks // RWKV_UNROLL, body, 0)
    tail = list(range(nchunks - nchunks % RWKV_UNROLL, nchunks))
    if tail:
        st = st_ref[...]
        for pc in prepare_interleaved(tail):
            st = advance(pc, st)
        st_ref[...] = st

    inv = 1.0 / RWKV_HEAD

    def epilogue(i, carry):
        rows = pl.ds(pl.multiple_of(i * ptile, 8), ptile)
        y = y_s[rows, :]
        mu = _seg_sum(y, bd) * inv
        yc = y - mu
        var = _seg_sum(yc * yc, bd) * inv
        yn = yc * lax.rsqrt(var + 1e-5 * RWKV_HEAD) * gg_p[...] + gb_p[...]
        o_ref[rows, :] = ((yn + bonus_s[rows, :]) * g_ref[rows, :]).astype(BF16)
        return carry

    lax.fori_loop(0, t // ptile, epilogue, 0)


def _rwkv_scan(r, kr, v, lw, a, g, p, batch):
    n, d = r.shape
    t = n // batch
    chunk = _chunk_len(t)
    w = min(d, RWKV_GROUP * RWKV_HEAD)
    ptile = _divisor_tile(t, 768, 16)
    blk = pl.BlockSpec((t, w), lambda b, j: (b, j))
    prm = pl.BlockSpec((1, w), lambda b, j: (0, j))
    params = [p["rwkv_k_k"], p["rwkv_k_a"], p["rwkv_r_k"], p["rwkv_gn_g"], p["rwkv_gn_b"]]
    return pl.pallas_call(
        functools.partial(_rwkv_scan_kernel, chunk=chunk, ptile=ptile),
        grid=(batch, d // w),
        in_specs=[blk] * 6 + [prm] * 5,
        out_specs=blk,
        out_shape=jax.ShapeDtypeStruct((n, d), BF16),
        scratch_shapes=[pltpu.VMEM((t, w), F32)] * 4 + [pltpu.VMEM((w, w), F32)],
        compiler_params=_cparams("parallel", "parallel"),
        name="rwkv_scan",
    )(r, kr, v, lw, a, g, *[x.reshape(1, d) for x in params])


def _rwkv_mixer_layer(h, hb, p, g, b, alpha, batch, tm):
    n, d = h.shape
    ident = lambda y: y
    xr, xw, xk, xv, xa, xg = _rwkv_mix(h, p["rwkv_mu"], batch, tm)
    (r,) = _proj(xr, p["rwkv_w_r"], 0, d, lambda y: (y,), [], [F32], tm, "rwkv_proj_r")
    (kr,) = _proj(xk, p["rwkv_w_k"], 0, d, lambda y: (y,), [], [F32], tm, "rwkv_proj_k")
    (v,) = _proj(xv, p["rwkv_w_v"], 0, d, lambda y: (y,), [], [F32], tm, "rwkv_proj_v")
    lw = _lora(xw, p["rwkv_w1"], p["rwkv_w2"], p["rwkv_w0"], jnp.tanh, _rwkv_log_decay, tm, "rwkv_lora_w")
    a = _lora(xa, p["rwkv_a1"], p["rwkv_a2"], p["rwkv_a0"], ident, _sigmoid, tm, "rwkv_lora_a")
    gate = _lora(xg, p["rwkv_g1"], p["rwkv_g2"], jnp.zeros((d,), F32), _sigmoid, ident, tm, "rwkv_lora_g")
    z = _rwkv_scan(r, kr, v, lw, a, gate, p, batch)
    return _mm_res_ln(z, p["rwkv_w_out"], h, g, b, alpha, tm)


def kernel(x, meta, ln_mix_g, ln_mix_b, ln_ffn_g, ln_ffn_b, conv_w_in, conv_w, conv_b, conv_w_out, rwkv_mu, rwkv_w_r, rwkv_w_k, rwkv_w_v, rwkv_w0, rwkv_w1, rwkv_w2, rwkv_a0, rwkv_a1, rwkv_a2, rwkv_g1, rwkv_g2, rwkv_k_k, rwkv_k_a, rwkv_r_k, rwkv_gn_g, rwkv_gn_b, rwkv_w_out, hgrn_w_in, hgrn_lb, hgrn_norm_g, hgrn_w_out, fox_w_in, fox_b_f, fox_q_norm_g, fox_k_norm_g, fox_w_out, ffn0_w1, ffn0_w3, ffn0_w2, moe1_router, moe1_router_b, moe1_w1, moe1_w3, moe1_w2, ffn2_w1, ffn2_w3, ffn2_w2, moe3_router, moe3_router_b, moe3_w1, moe3_w3, moe3_w2):
    batch, seq, d = x.shape
    depth = ln_mix_g.shape[0]
    assert depth == 4
    alpha = (2.0 * depth) ** 0.25
    t = N_META + seq
    n = batch * t
    tm = _divisor_tile(t, 768, 16)
    p = dict(
        conv_w_in=conv_w_in, conv_w=conv_w, conv_b=conv_b, conv_w_out=conv_w_out,
        rwkv_mu=rwkv_mu, rwkv_w_r=rwkv_w_r, rwkv_w_k=rwkv_w_k, rwkv_w_v=rwkv_w_v, rwkv_w0=rwkv_w0,
        rwkv_w1=rwkv_w1, rwkv_w2=rwkv_w2, rwkv_a0=rwkv_a0, rwkv_a1=rwkv_a1, rwkv_a2=rwkv_a2,
        rwkv_g1=rwkv_g1, rwkv_g2=rwkv_g2, rwkv_k_k=rwkv_k_k, rwkv_k_a=rwkv_k_a, rwkv_r_k=rwkv_r_k,
        rwkv_gn_g=rwkv_gn_g, rwkv_gn_b=rwkv_gn_b, rwkv_w_out=rwkv_w_out,
        hgrn_w_in=hgrn_w_in, hgrn_lb=hgrn_lb, hgrn_norm_g=hgrn_norm_g, hgrn_w_out=hgrn_w_out,
        fox_w_in=fox_w_in, fox_b_f=fox_b_f, fox_q_norm_g=fox_q_norm_g, fox_k_norm_g=fox_k_norm_g,
        fox_w_out=fox_w_out,
    )
    h = jnp.concatenate(
        [jnp.broadcast_to(meta[None].astype(x.dtype), (batch, N_META, d)), x], axis=1).reshape(n, d)
    hb = h.astype(BF16)

    h, hb = _conv_mixer_layer(h, hb, p, ln_mix_g[0], ln_mix_b[0], alpha, batch, tm)
    h, hb = _dense_ffn(hb, h, ffn0_w1, ffn0_w3, ffn0_w2, ln_ffn_g[0], ln_ffn_b[0], alpha, tm)
    h, hb = _rwkv_mixer_layer(h, hb, p, ln_mix_g[1], ln_mix_b[1], alpha, batch, tm)
    h, hb = _moe_ffn(h, hb, moe1_router, moe1_router_b, moe1_w1, moe1_w3, moe1_w2,
                     ln_ffn_g[1], ln_ffn_b[1], alpha, tm)
    h, hb = _hgrn_mixer_layer(h, hb, p, 2, ln_mix_g[2], ln_mix_b[2], alpha, batch, tm)
    h, hb = _dense_ffn(hb, h, ffn2_w1, ffn2_w3, ffn2_w2, ln_ffn_g[2], ln_ffn_b[2], alpha, tm)
    h, hb = _fox_mixer_layer(h, hb, p, ln_mix_g[3], ln_mix_b[3], alpha, batch, tm)
    h, hb = _moe_ffn(h, hb, moe3_router, moe3_router_b, moe3_w1, moe3_w3, moe3_w2,
                     ln_ffn_g[3], ln_ffn_b[3], alpha, tm)
    return h.reshape(batch, t, d)[:, N_META:]
```

```python
import functools

import jax
import jax.numpy as jnp
from jax import lax
from jax.experimental import pallas as pl
from jax.experimental.pallas import tpu as pltpu

F32 = jnp.float32
BF16 = jnp.bfloat16

N_META = 16
LN_EPS = 1e-5
RMS_EPS = 1e-6
RWKV_HEAD = 64
HGRN_HEAD = 128
FOX_HEAD = 128
TOP_K = 2
LANES = 128
VMEM_LIMIT_BYTES = 56 * 2**20


def _cparams(*sem):
    return pltpu.CompilerParams(dimension_semantics=sem, vmem_limit_bytes=VMEM_LIMIT_BYTES)


def _divisor_tile(n, cap, mult):
    best = None
    for d in range(mult, min(n, cap) + 1, mult):
        if n % d == 0:
            best = d
    assert best is not None, (n, cap, mult)
    return best


def _layer_norm(y, g, b):
    mu = jnp.mean(y, axis=-1, keepdims=True)
    yc = y - mu
    var = jnp.mean(yc * yc, axis=-1, keepdims=True)
    return yc * lax.rsqrt(var + LN_EPS) * g + b


def _sigmoid(x):
    return 1.0 / (1.0 + jnp.exp(-x))


def _silu(x):
    return x * _sigmoid(x)


def _mm_res_ln_kernel(z_ref, w_ref, h_ref, g_ref, b_ref, o_ref, ob_ref, *, nk, alpha):
    k = pl.program_id(1)
    part = jnp.dot(z_ref[...], w_ref[...], preferred_element_type=F32)

    @pl.when(k == 0)
    def _():
        o_ref[...] = part

    @pl.when(k > 0)
    def _():
        o_ref[...] += part

    @pl.when(k == nk - 1)
    def _():
        y = _layer_norm(alpha * h_ref[...] + o_ref[...], g_ref[...], b_ref[...])
        o_ref[...] = y
        ob_ref[...] = y.astype(BF16)


WEIGHT_SLICE_ROWS = 256


def _load_weight_bf16(w_hbm, wb_ref, stage_ref, sem):
    rows = stage_ref.shape[1]
    nslice = w_hbm.shape[0] // rows

    def copy(s):
        return pltpu.make_async_copy(w_hbm.at[pl.ds(s * rows, rows), :], stage_ref.at[s % 2], sem.at[s % 2])

    copy(0).start()
    for s in range(nslice):
        if s + 1 < nslice:
            copy(s + 1).start()
        copy(s).wait()
        wb_ref[s * rows:(s + 1) * rows, :] = stage_ref[s % 2].astype(BF16)


def _mm_res_ln_resident_kernel(z_ref, w_ref, h_ref, g_ref, b_ref, o_ref, ob_ref, wb_ref, stage_ref, sem,
                               *, alpha):
    @pl.when(pl.program_id(0) == 0)
    def _():
        _load_weight_bf16(w_ref, wb_ref, stage_ref, sem)

    y = alpha * h_ref[...] + jnp.dot(z_ref[...], wb_ref[...], preferred_element_type=F32)
    y = _layer_norm(y, g_ref[...], b_ref[...])
    o_ref[...] = y
    ob_ref[...] = y.astype(BF16)


RESIDENT_WEIGHT_BYTES = 16 * 2**20


def _mm_res_ln_resident(z, w, h, g, b, alpha, tm):
    n, kdim = z.shape
    d = w.shape[1]
    ws = _divisor_tile(kdim, WEIGHT_SLICE_ROWS, 8)
    const = lambda m: (0, 0)
    return pl.pallas_call(
        functools.partial(_mm_res_ln_resident_kernel, alpha=alpha),
        grid=(n // tm,),
        in_specs=[
            pl.BlockSpec((tm, kdim), lambda m: (m, 0)),
            pl.BlockSpec(memory_space=pl.ANY),
            pl.BlockSpec((tm, d), lambda m: (m, 0)),
            pl.BlockSpec((1, d), const),
            pl.BlockSpec((1, d), const),
        ],
        out_specs=[
            pl.BlockSpec((tm, d), lambda m: (m, 0)),
            pl.BlockSpec((tm, d), lambda m: (m, 0)),
        ],
        out_shape=[jax.ShapeDtypeStruct((n, d), F32), jax.ShapeDtypeStruct((n, d), BF16)],
        scratch_shapes=[pltpu.VMEM((kdim, d), BF16), pltpu.VMEM((2, ws, d), F32),
                        pltpu.SemaphoreType.DMA((2,))],
        compiler_params=_cparams("arbitrary"),
        name="mm_res_ln_resident",
    )(z, w, h, g.reshape(1, d), b.reshape(1, d))


def _mm_res_ln(z, w, h, g, b, alpha, tm):
    n, kdim = z.shape
    d = w.shape[1]
    if kdim * d * 4 <= RESIDENT_WEIGHT_BYTES:
        return _mm_res_ln_resident(z, w, h, g, b, alpha, tm)
    w = w.astype(BF16)
    tk = _divisor_tile(kdim, 1408, LANES)
    nk = kdim // tk
    return pl.pallas_call(
        functools.partial(_mm_res_ln_kernel, nk=nk, alpha=alpha),
        grid=(n // tm, nk),
        in_specs=[
            pl.BlockSpec((tm, tk), lambda m, k: (m, k)),
            pl.BlockSpec((tk, d), lambda m, k: (k, 0)),
            pl.BlockSpec((tm, d), lambda m, k: (m, 0)),
            pl.BlockSpec((1, d), lambda m, k: (0, 0)),
            pl.BlockSpec((1, d), lambda m, k: (0, 0)),
        ],
        out_specs=[
            pl.BlockSpec((tm, d), lambda m, k: (m, 0)),
            pl.BlockSpec((tm, d), lambda m, k: (m, 0)),
        ],
        out_shape=[jax.ShapeDtypeStruct((n, d), F32), jax.ShapeDtypeStruct((n, d), BF16)],
        compiler_params=_cparams("parallel", "arbitrary"),
        name="mm_res_ln",
    )(z, w, h, g.reshape(1, d), b.reshape(1, d))


def _ffn_up_kernel(exp_ref, nact_ref, x_ref, w1_ref, w3_ref, o_ref, w1b_ref, w3b_ref):
    c = pl.program_id(1)
    prev = exp_ref[jnp.maximum(c - 1, 0)]
    new_weights = jnp.logical_or(c == 0, exp_ref[c] != prev)

    @pl.when(new_weights)
    def _():
        w1b_ref[...] = w1_ref[...].astype(BF16)
        w3b_ref[...] = w3_ref[...].astype(BF16)

    @pl.when(c < nact_ref[0])
    def _():
        x = x_ref[...]
        a = jnp.dot(x, w1b_ref[...], preferred_element_type=F32)
        bb = jnp.dot(x, w3b_ref[...], preferred_element_type=F32)
        o_ref[...] = (_silu(a) * bb).astype(BF16)

    @pl.when(c >= nact_ref[0])
    def _():
        o_ref[...] = jnp.zeros_like(o_ref)


def _ffn_up(x, w1, w3, chunk_expert, n_active, rows):
    p, d = x.shape
    f = w1.shape[2]
    tf = _divisor_tile(f, 512, LANES)
    grid_spec = pltpu.PrefetchScalarGridSpec(
        num_scalar_prefetch=2,
        grid=(f // tf, p // rows),
        in_specs=[
            pl.BlockSpec((rows, d), lambda j, c, e, na: (c, 0)),
            pl.BlockSpec((None, d, tf), lambda j, c, e, na: (e[c], 0, j)),
            pl.BlockSpec((None, d, tf), lambda j, c, e, na: (e[c], 0, j)),
        ],
        out_specs=pl.BlockSpec((rows, tf), lambda j, c, e, na: (c, j)),
        scratch_shapes=[pltpu.VMEM((d, tf), BF16), pltpu.VMEM((d, tf), BF16)],
    )
    return pl.pallas_call(
        _ffn_up_kernel,
        grid_spec=grid_spec,
        out_shape=jax.ShapeDtypeStruct((p, f), BF16),
        compiler_params=_cparams("arbitrary", "arbitrary"),
        name="ffn_up",
    )(chunk_expert, n_active, x, w1, w3)


def _dense_ffn(hb, h, w1, w3, w2, g, b, alpha, tm):
    n = hb.shape[0]
    nchunks = n // tm
    hmid = _ffn_up(hb, w1[None], w3[None], jnp.zeros((nchunks,), jnp.int32),
                   jnp.full((1,), nchunks, jnp.int32), tm)
    return _mm_res_ln(hmid, w2, h, g, b, alpha, tm)


def _conv_proj_kernel(x_ref, wb_ref, wc_ref, wh_ref, cw_ref, cb_ref, o_ref,
                      wbb_ref, wcb_ref, whb_ref, carry_ref):
    bi = pl.program_id(1)
    ti = pl.program_id(2)

    @pl.when(jnp.logical_and(bi == 0, ti == 0))
    def _():
        wbb_ref[...] = wb_ref[...].astype(BF16)
        wcb_ref[...] = wc_ref[...].astype(BF16)
        whb_ref[...] = wh_ref[...].astype(BF16)

    @pl.when(ti == 0)
    def _():
        carry_ref[...] = jnp.zeros_like(carry_ref)

    x = x_ref[...]
    gate_b = jnp.dot(x, wbb_ref[...], preferred_element_type=F32)
    gate_c = jnp.dot(x, wcb_ref[...], preferred_element_type=F32)
    hh = jnp.dot(x, whb_ref[...], preferred_element_type=F32)
    u = gate_c * hh
    tm = u.shape[0]
    prev1 = carry_ref[7:8, :]
    prev2 = carry_ref[6:7, :]
    row = lax.broadcasted_iota(jnp.int32, (tm, 1), 0)
    r1 = jnp.where(row == 0, prev1, pltpu.roll(u, 1, axis=0))
    r2 = jnp.where(row == 0, prev2, jnp.where(row == 1, prev1, pltpu.roll(u, 2, axis=0)))
    v = cw_ref[0:1, :] * r2 + cw_ref[1:2, :] * r1 + cw_ref[2:3, :] * u + cb_ref[...]
    carry_ref[...] = u[tm - 8:, :]
    o_ref[...] = (gate_b * v).astype(BF16)


def _conv_proj(hb, w_in, conv_w, conv_b, batch, tm):
    n, d = hb.shape
    tn = _divisor_tile(d, 512, LANES)
    nd = d // tn
    tpb = n // batch // tm
    return pl.pallas_call(
        _conv_proj_kernel,
        grid=(nd, batch, tpb),
        in_specs=[
            pl.BlockSpec((tm, d), lambda j, bi, ti: (bi * tpb + ti, 0)),
            pl.BlockSpec((d, tn), lambda j, bi, ti: (0, j)),
            pl.BlockSpec((d, tn), lambda j, bi, ti: (0, nd + j)),
            pl.BlockSpec((d, tn), lambda j, bi, ti: (0, 2 * nd + j)),
            pl.BlockSpec((3, tn), lambda j, bi, ti: (0, j)),
            pl.BlockSpec((1, tn), lambda j, bi, ti: (0, j)),
        ],
        out_specs=pl.BlockSpec((tm, tn), lambda j, bi, ti: (bi * tpb + ti, j)),
        out_shape=jax.ShapeDtypeStruct((n, d), BF16),
        scratch_shapes=[pltpu.VMEM((d, tn), BF16)] * 3 + [pltpu.VMEM((8, tn), F32)],
        compiler_params=_cparams("arbitrary", "arbitrary", "arbitrary"),
        name="conv_proj",
    )(hb, w_in, w_in, w_in, conv_w, conv_b.reshape(1, d))


def _conv_mixer_layer(h, hb, p, g, b, alpha, batch, tm):
    z = _conv_proj(hb, p["conv_w_in"], p["conv_w"], p["conv_b"], batch, tm)
    return _mm_res_ln(z, p["conv_w_out"], h, g, b, alpha, tm)


MOE_ROWS = 128


def _split_bf16(a):
    hi = a.astype(BF16)
    lo = (a - hi.astype(F32)).astype(BF16)
    return hi, lo


def _dot_f32(a, b):
    ah, al = _split_bf16(a)
    bh, bl = _split_bf16(b)
    return (jnp.dot(ah, bh, preferred_element_type=F32)
            + (jnp.dot(ah, bl, preferred_element_type=F32)
               + jnp.dot(al, bh, preferred_element_type=F32)))


def _router_kernel(h_ref, w_ref, b_ref, info_ref, cnt_ref, carry_ref):
    i = pl.program_id(0)

    @pl.when(i == 0)
    def _():
        carry_ref[...] = jnp.zeros_like(carry_ref)

    logits = _dot_f32(h_ref[...], w_ref[...]) + b_ref[...]
    tm, ne = logits.shape
    lane = lax.broadcasted_iota(jnp.int32, (tm, ne), 1)
    m1 = jnp.max(logits, axis=-1, keepdims=True)
    i1 = jnp.min(jnp.where(logits == m1, lane, ne), axis=-1, keepdims=True)
    mask1 = lane == i1
    rest = jnp.where(mask1, -jnp.inf, logits)
    m2 = jnp.max(rest, axis=-1, keepdims=True)
    i2 = jnp.min(jnp.where(rest == m2, lane, ne), axis=-1, keepdims=True)
    mask2 = lane == i2
    dd = jnp.exp(m2 - m1)
    g1 = 1.0 / (1.0 + dd)
    g2 = dd / (1.0 + dd)
    sel = jnp.where(jnp.logical_or(mask1, mask2), 1.0, 0.0)
    r_i = lax.broadcasted_iota(jnp.int32, (tm, tm), 0)
    c_i = lax.broadcasted_iota(jnp.int32, (tm, tm), 1)
    tril = jnp.where(c_i < r_i, 1.0, 0.0).astype(BF16)
    rank = jnp.dot(tril, sel.astype(BF16), preferred_element_type=F32) + carry_ref[...]
    r1 = jnp.sum(jnp.where(mask1, rank, 0.0), axis=-1, keepdims=True)
    r2 = jnp.sum(jnp.where(mask2, rank, 0.0), axis=-1, keepdims=True)
    info = jnp.where(lane == 0, i1.astype(F32),
           jnp.where(lane == 1, i2.astype(F32),
           jnp.where(lane == 2, g1,
           jnp.where(lane == 3, g2,
           jnp.where(lane == 4, r1,
           jnp.where(lane == 5, r2, 0.0))))))
    info_ref[...] = info
    total = carry_ref[...] + jnp.sum(sel, axis=0, keepdims=True)
    carry_ref[...] = total
    cnt_ref[...] = total


def _router(h, w, b, tm):
    n, d = h.shape
    ne = w.shape[1]
    assert ne >= 6
    return pl.pallas_call(
        _router_kernel,
        grid=(n // tm,),
        in_specs=[
            pl.BlockSpec((tm, d), lambda i: (i, 0)),
            pl.BlockSpec((d, ne), lambda i: (0, 0)),
            pl.BlockSpec((1, ne), lambda i: (0, 0)),
        ],
        out_specs=[
            pl.BlockSpec((tm, ne), lambda i: (i, 0)),
            pl.BlockSpec((1, ne), lambda i: (0, 0)),
        ],
        out_shape=[jax.ShapeDtypeStruct((n, ne), F32), jax.ShapeDtypeStruct((1, ne), F32)],
        scratch_shapes=[pltpu.VMEM((1, ne), F32)],
        compiler_params=_cparams("arbitrary"),
        name="moe_router",
    )(h, w, b.reshape(1, ne))


def _row_copy(src_hbm, row, dst_vmem, r, sem):
    return pltpu.make_async_copy(src_hbm.at[pl.ds(row, 1), :], dst_vmem.at[pl.ds(r, 1), :], sem)


def _rows_wait(src_hbm, dst_vmem, sem):
    pltpu.make_async_copy(src_hbm.at[pl.ds(0, dst_vmem.shape[0]), :], dst_vmem, sem).wait()


def _moe_gather_kernel(src_ref, nact_ref, h_ref, o_ref, buf_ref, sem):
    c = pl.program_id(0)
    rows = buf_ref.shape[1]

    def issue(chunk):
        slot = chunk % 2

        def start(r, carry):
            _row_copy(h_ref, src_ref[chunk * rows + r], buf_ref.at[slot], r, sem.at[slot]).start()
            return carry

        lax.fori_loop(0, rows, start, 0, unroll=8)

    @pl.when(jnp.logical_and(c == 0, nact_ref[0] > 0))
    def _():
        issue(c)

    @pl.when(c + 1 < nact_ref[0])
    def _():
        issue(c + 1)

    @pl.when(c < nact_ref[0])
    def _():
        slot = c % 2
        _rows_wait(h_ref, buf_ref.at[slot], sem.at[slot])
        o_ref[...] = buf_ref[slot].astype(BF16)

    @pl.when(c >= nact_ref[0])
    def _():
        o_ref[...] = jnp.zeros_like(o_ref)


def _moe_gather(h, src, n_active, p, rows):
    n, d = h.shape
    grid_spec = pltpu.PrefetchScalarGridSpec(
        num_scalar_prefetch=2,
        grid=(p // rows,),
        in_specs=[pl.BlockSpec(memory_space=pl.ANY)],
        out_specs=pl.BlockSpec((rows, d), lambda c, s, na: (c, 0)),
        scratch_shapes=[pltpu.VMEM((2, rows, d), F32), pltpu.SemaphoreType.DMA((2,))],
    )
    return pl.pallas_call(
        _moe_gather_kernel,
        grid_spec=grid_spec,
        out_shape=jax.ShapeDtypeStruct((p, d), BF16),
        compiler_params=_cparams("arbitrary"),
        name="moe_gather",
    )(src, n_active, h)


def _moe_combine_kernel(p1_ref, p2_ref, y_ref, h_ref, info_ref, g_ref, b_ref, o_ref, ob_ref,
                        buf1_ref, buf2_ref, sem, *, alpha):
    i = pl.program_id(0)
    tm = buf1_ref.shape[1]

    def issue(tile):
        slot = tile % 2

        def start(r, carry):
            _row_copy(y_ref, p1_ref[tile * tm + r], buf1_ref.at[slot], r, sem.at[slot]).start()
            _row_copy(y_ref, p2_ref[tile * tm + r], buf2_ref.at[slot], r, sem.at[slot]).start()
            return carry

        lax.fori_loop(0, tm, start, 0, unroll=8)

    @pl.when(i == 0)
    def _():
        issue(i)

    @pl.when(i + 1 < pl.num_programs(0))
    def _():
        issue(i + 1)

    slot = i % 2
    _rows_wait(y_ref, buf1_ref.at[slot], sem.at[slot])
    _rows_wait(y_ref, buf2_ref.at[slot], sem.at[slot])
    info = info_ref[...]
    y = alpha * h_ref[...] + (info[:, 2:3] * buf1_ref[slot] + info[:, 3:4] * buf2_ref[slot])
    y = _layer_norm(y, g_ref[...], b_ref[...])
    o_ref[...] = y
    ob_ref[...] = y.astype(BF16)


def _moe_combine(y, h, info, p1, p2, g, b, alpha, tm):
    n, d = h.shape
    ne = info.shape[1]
    grid_spec = pltpu.PrefetchScalarGridSpec(
        num_scalar_prefetch=2,
        grid=(n // tm,),
        in_specs=[
            pl.BlockSpec(memory_space=pl.ANY),
            pl.BlockSpec((tm, d), lambda i, a, c: (i, 0)),
            pl.BlockSpec((tm, ne), lambda i, a, c: (i, 0)),
            pl.BlockSpec((1, d), lambda i, a, c: (0, 0)),
            pl.BlockSpec((1, d), lambda i, a, c: (0, 0)),
        ],
        out_specs=[
            pl.BlockSpec((tm, d), lambda i, a, c: (i, 0)),
            pl.BlockSpec((tm, d), lambda i, a, c: (i, 0)),
        ],
        scratch_shapes=[pltpu.VMEM((2, tm, d), F32), pltpu.VMEM((2, tm, d), F32),
                        pltpu.SemaphoreType.DMA((2,))],
    )
    return pl.pallas_call(
        functools.partial(_moe_combine_kernel, alpha=alpha),
        grid_spec=grid_spec,
        out_shape=[jax.ShapeDtypeStruct((n, d), F32), jax.ShapeDtypeStruct((n, d), BF16)],
        compiler_params=_cparams("arbitrary"),
        name="moe_combine",
    )(p1, p2, y, h, info, g.reshape(1, d), b.reshape(1, d))


MOE_PASS_BLOCKS = 18
MOE_F_TILE = 256


def _expert_ffn_kernel(pe_ref, ps_ref, pn_ref, nu_ref, xs_ref, w1_ref, w3_ref, w2_ref, y_ref,
                       x_buf, acc_ref, w1b_ref, w3b_ref, w2b_ref, sem):
    p = pl.program_id(0)
    f = pl.program_id(1)
    nf = pl.num_programs(1)
    nb = pn_ref[p]
    sb = MOE_ROWS
    start = pl.multiple_of(ps_ref[p], MOE_ROWS)
    npair = nb // 2

    @pl.when(nb > 0)
    def _():
        @pl.when(f == 0)
        def _():
            cp = pltpu.make_async_copy(xs_ref.at[pl.ds(start, x_buf.shape[0]), :], x_buf, sem.at[0])
            cp.start()
            cp.wait()

        w1b_ref[...] = w1_ref[...].astype(BF16)
        w3b_ref[...] = w3_ref[...].astype(BF16)
        w2b_ref[...] = w2_ref[...].astype(BF16)

        def up(unit, nunit):
            x = x_buf[pl.ds(pl.multiple_of(unit * sb, sb), nunit * sb), :]
            a = jnp.dot(x, w1b_ref[...], preferred_element_type=F32)
            bb = jnp.dot(x, w3b_ref[...], preferred_element_type=F32)
            return (_silu(a) * bb).astype(BF16)

        def down(unit, nunit, hmid, first):
            rows = pl.ds(pl.multiple_of(unit * sb, sb), nunit * sb)
            part = jnp.dot(hmid, w2b_ref[...], preferred_element_type=F32)
            if first:
                acc_ref[rows, :] = part
            else:
                acc_ref[rows, :] += part

        def sweep(first):
            @pl.when(npair > 0)
            def _():
                def body(i, hprev):
                    down(2 * (i - 1), 2, hprev, first)
                    return up(2 * i, 2)

                down(2 * (npair - 1), 2, lax.fori_loop(1, npair, body, up(0, 2)), first)

            @pl.when(nb % 2 == 1)
            def _():
                down(nb - 1, 1, up(nb - 1, 1), first)

        @pl.when(f == 0)
        def _():
            sweep(True)

        @pl.when(f > 0)
        def _():
            sweep(False)

        @pl.when(f == nf - 1)
        def _():
            def out_copy(i):
                rows = pl.ds(pl.multiple_of(i * sb, sb), sb)
                dst = pl.ds(pl.multiple_of(start + i * sb, sb), sb)
                return pltpu.make_async_copy(acc_ref.at[rows, :], y_ref.at[dst, :], sem.at[1])

            lax.fori_loop(0, nb, lambda i, c: (out_copy(i).start(), c)[1], 0)
            lax.fori_loop(0, nb, lambda i, c: (out_copy(i).wait(), c)[1], 0)

    @pl.when(jnp.logical_and(p == pl.num_programs(0) - 1, f == nf - 1))
    def _():
        acc_ref[0:sb, :] = jnp.zeros((sb, acc_ref.shape[1]), F32)

        def zero_copy(i):
            dst = pl.ds(pl.multiple_of(i * sb, sb), sb)
            return pltpu.make_async_copy(acc_ref.at[0:sb, :], y_ref.at[dst, :], sem.at[1])

        n_blocks = y_ref.shape[0] // sb
        lax.fori_loop(nu_ref[1], n_blocks, lambda i, c: (zero_copy(i).start(), c)[1], 0)
        lax.fori_loop(nu_ref[1], n_blocks, lambda i, c: (zero_copy(i).wait(), c)[1], 0)


def _expert_ffn(xs, w1, w3, w2, pass_expert, pass_start, pass_nb, n_used, p_rows):
    d = xs.shape[1]
    f = w1.shape[2]
    tf = _divisor_tile(f, MOE_F_TILE, LANES)
    nf = f // tf
    npass = pass_expert.shape[0]
    r = MOE_PASS_BLOCKS * MOE_ROWS

    def fidx(p, j, nu):
        return jnp.where(p < nu[0], j, nf - 1)

    grid_spec = pltpu.PrefetchScalarGridSpec(
        num_scalar_prefetch=4,
        grid=(npass, nf),
        in_specs=[
            pl.BlockSpec(memory_space=pl.ANY),
            pl.BlockSpec((None, d, tf), lambda p, j, pe, ps, pn, nu: (pe[p], 0, fidx(p, j, nu))),
            pl.BlockSpec((None, d, tf), lambda p, j, pe, ps, pn, nu: (pe[p], 0, fidx(p, j, nu))),
            pl.BlockSpec((None, tf, d), lambda p, j, pe, ps, pn, nu: (pe[p], fidx(p, j, nu), 0)),
        ],
        out_specs=pl.BlockSpec(memory_space=pl.ANY),
        scratch_shapes=[
            pltpu.VMEM((r, d), BF16), pltpu.VMEM((r, d), F32),
            pltpu.VMEM((d, tf), BF16), pltpu.VMEM((d, tf), BF16), pltpu.VMEM((tf, d), BF16),
            pltpu.SemaphoreType.DMA((2,)),
        ],
    )
    return pl.pallas_call(
        _expert_ffn_kernel,
        grid_spec=grid_spec,
        out_shape=jax.ShapeDtypeStruct((p_rows, d), F32),
        compiler_params=_cparams("arbitrary", "arbitrary"),
        name="expert_ffn",
    )(pass_expert, pass_start, pass_nb, n_used, xs, w1, w3, w2)


def _moe_ffn(h, hb, router_w, router_b, w1, w3, w2, g, b, alpha, tm):
    n, d = h.shape
    ne = router_w.shape[1]
    rows = MOE_ROWS
    info, counts = _router(h, router_w, router_b, tm)
    counts = counts[0].astype(jnp.int32)
    nblk_e = (counts + rows - 1) // rows
    blk_end = jnp.cumsum(nblk_e)
    starts = (blk_end - nblk_e) * rows
    n_blocks = (n * TOP_K + ne * (rows - 1)) // rows
    p = n_blocks * rows
    n_active = blk_end[-1:].astype(jnp.int32)
    npass_e = (nblk_e + MOE_PASS_BLOCKS - 1) // MOE_PASS_BLOCKS
    pass_end = jnp.cumsum(npass_e)
    max_pass = n_blocks // MOE_PASS_BLOCKS + ne
    pidx = jnp.arange(max_pass, dtype=jnp.int32)
    n_used = jnp.stack([pass_end[-1], blk_end[-1]]).astype(jnp.int32)
    last_expert = jnp.sum(pass_end < pass_end[-1]).astype(jnp.int32)
    pass_expert = jnp.minimum(jnp.sum(pidx[:, None] >= pass_end[None, :], axis=1), last_expert).astype(jnp.int32)
    local = pidx - (pass_end - npass_e)[pass_expert]
    pass_start = (starts[pass_expert] + local * (MOE_PASS_BLOCKS * rows)).astype(jnp.int32)
    pass_nb = jnp.where(pidx < n_used[0],
                        jnp.clip(nblk_e[pass_expert] - local * MOE_PASS_BLOCKS, 0, MOE_PASS_BLOCKS),
                        0).astype(jnp.int32)
    pass_start = jnp.where(pass_nb > 0, pass_start, 0).astype(jnp.int32)
    i1 = info[:, 0].astype(jnp.int32)
    i2 = info[:, 1].astype(jnp.int32)
    p1 = starts[i1] + info[:, 4].astype(jnp.int32)
    p2 = starts[i2] + info[:, 5].astype(jnp.int32)
    tok = jnp.arange(n, dtype=jnp.int32)
    p_in = p + MOE_PASS_BLOCKS * rows
    src = jnp.zeros((p_in,), jnp.int32).at[jnp.concatenate([p1, p2])].set(jnp.concatenate([tok, tok]))
    xs = _moe_gather(h, src, n_active, p_in, rows)
    y = _expert_ffn(xs, w1, w3, w2, pass_expert, pass_start, pass_nb, n_used, p)
    tmc = _divisor_tile(n, 384, 16)
    return _moe_combine(y, h, info, p1, p2, g, b, alpha, tmc)


def _proj_kernel(*refs, epilogue, n_extra, n_out):
    x_ref, w_ref = refs[0], refs[1]
    extra = refs[2:2 + n_extra]
    outs = refs[2 + n_extra:2 + n_extra + n_out]
    wb_ref = refs[2 + n_extra + n_out]

    @pl.when(pl.program_id(1) == 0)
    def _():
        wb_ref[...] = w_ref[...].astype(BF16)

    y = jnp.dot(x_ref[...], wb_ref[...], preferred_element_type=F32)
    res = epilogue(y, *[e[...] for e in extra])
    for o_ref, r in zip(outs, res):
        o_ref[...] = r.astype(o_ref.dtype)


def _proj(xb, w, col0, ncols, epilogue, extras, out_dtypes, tm, name, tn_cap=1024):
    n, kdim = xb.shape
    tn = _divisor_tile(ncols, tn_cap, LANES)
    assert col0 % tn == 0
    off = col0 // tn
    outs = pl.pallas_call(
        functools.partial(_proj_kernel, epilogue=epilogue, n_extra=len(extras), n_out=len(out_dtypes)),
        grid=(ncols // tn, n // tm),
        in_specs=[
            pl.BlockSpec((tm, kdim), lambda j, m: (m, 0)),
            pl.BlockSpec((kdim, tn), lambda j, m: (0, off + j)),
        ] + [pl.BlockSpec((1, tn), lambda j, m: (0, j))] * len(extras),
        out_specs=[pl.BlockSpec((tm, tn), lambda j, m: (m, j))] * len(out_dtypes),
        out_shape=[jax.ShapeDtypeStruct((n, ncols), dt) for dt in out_dtypes],
        scratch_shapes=[pltpu.VMEM((kdim, tn), BF16)],
        compiler_params=_cparams("arbitrary", "arbitrary"),
        name=name,
    )(xb, w, *[e.reshape(1, ncols) for e in extras])
    return outs


def _tril_bf16(c, inclusive):
    r_i = lax.broadcasted_iota(jnp.int32, (c, c), 0)
    c_i = lax.broadcasted_iota(jnp.int32, (c, c), 1)
    keep = (c_i <= r_i) if inclusive else (c_i < r_i)
    return jnp.where(keep, 1.0, 0.0).astype(BF16)


def _cumsum_rows(x, tril):
    hi = x.astype(BF16)
    r1 = x - hi.astype(F32)
    mid = r1.astype(BF16)
    lo = (r1 - mid.astype(F32)).astype(BF16)
    return (jnp.dot(tril, hi, preferred_element_type=F32)
            + (jnp.dot(tril, mid, preferred_element_type=F32)
               + jnp.dot(tril, lo, preferred_element_type=F32)))


def _dot_nt(a, b):
    return lax.dot_general(a.astype(BF16), b.astype(BF16), (((1,), (1,)), ((), ())),
                           preferred_element_type=F32)


def _dot_tn(a, b):
    return lax.dot_general(a.astype(BF16), b.astype(BF16), (((0,), (0,)), ((), ())),
                           preferred_element_type=F32)


def _dot_nn(a, b):
    return jnp.dot(a.astype(BF16), b.astype(BF16), preferred_element_type=F32)


def _chunk_len(t):
    return _divisor_tile(t, 64, 16)


def _round_robin(gens):
    done = [None] * len(gens)
    while any(d is None for d in done):
        for j, gen in enumerate(gens):
            if done[j] is None:
                done[j] = next(gen)
    return done


HGRN_SUB = 16


def _hgrn_scan_kernel(q_ref, lf_ref, v_ref, gs_ref, ng_ref, o_ref, st_ref, *, chunk):
    t, w = q_ref.shape
    nhead = w // HGRN_HEAD
    nsub = chunk // HGRN_SUB
    st_ref[...] = jnp.zeros_like(st_ref)
    tril = _tril_bf16(chunk, True)
    row16 = lax.broadcasted_iota(jnp.int32, (HGRN_SUB, 1), 0)

    def head_chunk(q, lf, v, st):
        k = 1.0 - jnp.exp(lf)
        cum = _cumsum_rows(lf, tril)
        yield None
        o_inter = _dot_nt(q * jnp.exp(cum), st)
        vb = v.astype(BF16)
        cl = cum[chunk - 1:chunk]
        kd = k * jnp.exp(cl - cum)
        st_new = st * jnp.exp(cl) + _dot_tn(v, kd)
        yield None
        outs = []
        for i in range(nsub):
            lo, hi = i * HGRN_SUB, (i + 1) * HGRN_SUB
            qi, ki, vi, cumi = q[lo:hi], k[lo:hi], v[lo:hi], cum[lo:hi]
            oi = o_inter[lo:hi]
            if i > 0:
                ci = cum[lo:lo + 1]
                qt = qi * jnp.exp(cumi - ci)
                kt = k[0:lo] * jnp.exp(ci - cum[0:lo])
                oi = oi + jnp.dot(_dot_nt(qt, kt).astype(BF16), vb[0:lo], preferred_element_type=F32)
            for s in range(HGRN_SUB):
                dec = jnp.exp(jnp.minimum(cumi - cumi[s:s + 1], 0.0))
                col = jnp.sum(qi * dec * ki[s:s + 1], axis=-1, keepdims=True)
                col = jnp.where(row16 >= s, col, 0.0)
                oi = oi + col * vi[s:s + 1]
                if s % 4 == 3:
                    yield None
            outs.append(oi)
        o = jnp.concatenate(outs, axis=0)
        o = o * lax.rsqrt(jnp.mean(o * o, axis=-1, keepdims=True) + RMS_EPS)
        yield o, st_new

    def body(c, carry):
        rows = pl.ds(pl.multiple_of(c * chunk, 16), chunk)
        q = q_ref[rows, :].astype(F32)
        lf = lf_ref[rows, :]
        v = v_ref[rows, :].astype(F32)
        heads = []
        for hh in range(nhead):
            cols = slice(hh * HGRN_HEAD, (hh + 1) * HGRN_HEAD)
            heads.append(head_chunk(q[:, cols], lf[:, cols], v[:, cols], st_ref[hh]))
        outs = []
        for hh, (o, st_new) in enumerate(_round_robin(heads)):
            st_ref[hh] = st_new
            outs.append(o)
        o = jnp.concatenate(outs, axis=1)
        o_ref[rows, :] = (o * ng_ref[...] * gs_ref[rows, :].astype(F32)).astype(BF16)
        return carry

    lax.fori_loop(0, t // chunk, body, 0)


HGRN_GROUP = 4


def _hgrn_scan(q, lf, v, gs, norm_g, batch):
    n, d = q.shape
    t = n // batch
    chunk = _chunk_len(t)
    w = min(d, HGRN_GROUP * HGRN_HEAD)
    blk = pl.BlockSpec((t, w), lambda b, j: (b, j))
    return pl.pallas_call(
        functools.partial(_hgrn_scan_kernel, chunk=chunk),
        grid=(batch, d // w),
        in_specs=[blk, blk, blk, blk, pl.BlockSpec((1, w), lambda b, j: (0, j))],
        out_specs=blk,
        out_shape=jax.ShapeDtypeStruct((n, d), BF16),
        scratch_shapes=[pltpu.VMEM((w // HGRN_HEAD, HGRN_HEAD, HGRN_HEAD), F32)],
        compiler_params=_cparams("parallel", "parallel"),
        name="hgrn_scan",
    )(q, lf, v, gs, norm_g.reshape(1, d))


def _hgrn_mixer_layer(h, hb, p, layer_idx, g, b, alpha, batch, tm):
    d = h.shape[1]
    w_in = p["hgrn_w_in"]
    lb = jnp.cumsum(jax.nn.softmax(p["hgrn_lb"].astype(F32), axis=0), axis=0)
    lb = lb[layer_idx] - lb[0]
    (q,) = _proj(hb, w_in, 0, d, lambda y: (_silu(y),), [], [BF16], tm, "hgrn_proj_q")
    (lf,) = _proj(hb, w_in, d, d, lambda y, lbv: (jnp.log(lbv + (1.0 - lbv) * _sigmoid(y)),),
                  [lb], [F32], tm, "hgrn_proj_f")
    (v,) = _proj(hb, w_in, 2 * d, d, lambda y: (y,), [], [BF16], tm, "hgrn_proj_i")
    (gs,) = _proj(hb, w_in, 3 * d, d, lambda y: (_silu(y),), [], [BF16], tm, "hgrn_proj_g")
    z = _hgrn_scan(q, lf, v, gs, p["hgrn_norm_g"], batch)
    return _mm_res_ln(z, p["hgrn_w_out"], h, g, b, alpha, tm)


def _fox_gate_kernel(h_ref, w_ref, bf_ref, c_ref, carry_ref):
    @pl.when(pl.program_id(1) == 0)
    def _():
        carry_ref[...] = jnp.zeros_like(carry_ref)

    x = _dot_f32(h_ref[...], w_ref[...]) + bf_ref[...]
    log_f = jnp.minimum(x, 0.0) - jnp.log(1.0 + jnp.exp(-jnp.abs(x)))
    tm = x.shape[0]
    c = _cumsum_rows(log_f, _tril_bf16(tm, True)) + carry_ref[...]
    c_ref[...] = c
    carry_ref[...] = c[tm - 1:tm, :]


def _fox_gate(h, w_f, b_f, batch, tm):
    n, d = h.shape
    nh = w_f.shape[1]
    tpb = n // batch // tm
    return pl.pallas_call(
        _fox_gate_kernel,
        grid=(batch, tpb),
        in_specs=[
            pl.BlockSpec((tm, d), lambda b, t: (b * tpb + t, 0)),
            pl.BlockSpec((d, nh), lambda b, t: (0, 0)),
            pl.BlockSpec((1, nh), lambda b, t: (0, 0)),
        ],
        out_specs=pl.BlockSpec((tm, nh), lambda b, t: (b * tpb + t, 0)),
        out_shape=jax.ShapeDtypeStruct((n, nh), F32),
        scratch_shapes=[pltpu.VMEM((1, nh), F32)],
        compiler_params=_cparams("arbitrary", "arbitrary"),
        name="fox_gate",
    )(h, w_f, b_f.reshape(1, nh))


def _fox_attn_kernel(q_ref, k_ref, v_ref, sg_ref, c_ref, ct_ref, o_ref, *, tq):
    hd = pl.program_id(1)
    t = q_ref.shape[0]
    nh = c_ref.shape[1]
    lane = lax.broadcasted_iota(jnp.int32, (t, nh), 1)
    c_col = jnp.sum(jnp.where(lane == hd, c_ref[...], 0.0), axis=-1, keepdims=True)
    c_row = ct_ref[pl.ds(hd, 1), :]
    for i in range(t // tq):
        lo, hi = i * tq, (i + 1) * tq
        s = lax.dot_general(q_ref[lo:hi, :], k_ref[0:hi, :], (((1,), (1,)), ((), ())),
                            preferred_element_type=F32)
        s = s + c_col[lo:hi] - c_row[:, 0:hi]
        r_i = lax.broadcasted_iota(jnp.int32, (tq, hi), 0) + lo
        c_i = lax.broadcasted_iota(jnp.int32, (tq, hi), 1)
        s = jnp.where(c_i <= r_i, s, -jnp.inf)
        m = jnp.max(s, axis=-1, keepdims=True)
        p = jnp.exp(s - m)
        l = jnp.sum(p, axis=-1, keepdims=True)
        o = jnp.dot(p.astype(BF16), v_ref[0:hi, :], preferred_element_type=F32) / l
        o_ref[lo:hi, :] = (o * sg_ref[lo:hi, :]).astype(BF16)


def _fox_attn(q, k, v, sg, c, ct, batch):
    n, d = q.shape
    t = n // batch
    nh = d // FOX_HEAD
    tq = _divisor_tile(t, 768, 16)
    blk = pl.BlockSpec((t, FOX_HEAD), lambda b, h: (b, h))
    return pl.pallas_call(
        functools.partial(_fox_attn_kernel, tq=tq),
        grid=(batch, nh),
        in_specs=[blk, blk, blk, blk,
                  pl.BlockSpec((t, nh), lambda b, h: (b, 0)),
                  pl.BlockSpec((None, nh, t), lambda b, h: (b, 0, 0))],
        out_specs=blk,
        out_shape=jax.ShapeDtypeStruct((n, d), BF16),
        compiler_params=_cparams("parallel", "parallel"),
        name="fox_attn",
    )(q, k, v, sg, c, ct)


def _head_rms_epilogue(scale):
    def epi(y, gain):
        outs = []
        for j in range(y.shape[1] // FOX_HEAD):
            yj = y[:, j * FOX_HEAD:(j + 1) * FOX_HEAD]
            yj = yj * lax.rsqrt(jnp.mean(yj * yj, axis=-1, keepdims=True) + RMS_EPS)
            outs.append(yj * gain[:, j * FOX_HEAD:(j + 1) * FOX_HEAD] * scale)
        return (jnp.concatenate(outs, axis=1),)
    return epi


def _fox_mixer_layer(h, hb, p, g, b, alpha, batch, tm):
    n, d = h.shape
    nh = d // FOX_HEAD
    w_in = p["fox_w_in"]
    qg = jnp.tile(p["fox_q_norm_g"], nh)
    kg = jnp.tile(p["fox_k_norm_g"], nh)
    (q,) = _proj(hb, w_in, 0, d, _head_rms_epilogue(FOX_HEAD ** -0.5), [qg], [BF16], tm, "fox_proj_q")
    (k,) = _proj(hb, w_in, d, d, _head_rms_epilogue(1.0), [kg], [BF16], tm, "fox_proj_k")
    (v,) = _proj(hb, w_in, 2 * d, d, lambda y: (y,), [], [BF16], tm, "fox_proj_v")
    (sg,) = _proj(hb, w_in, 3 * d, d, lambda y: (_sigmoid(y),), [], [F32], tm, "fox_proj_g")
    c = _fox_gate(h, w_in[:, 4 * d:], p["fox_b_f"], batch, tm)
    ct = c.reshape(batch, n // batch, nh).transpose(0, 2, 1)
    z = _fox_attn(q, k, v, sg, c, ct, batch)
    return _mm_res_ln(z, p["fox_w_out"], h, g, b, alpha, tm)


def _rwkv_mix_kernel(h_ref, mu_ref, *refs):
    outs, carry_ref = refs[:-1], refs[-1]

    @pl.when(pl.program_id(1) == 0)
    def _():
        carry_ref[...] = jnp.zeros_like(carry_ref)

    x = h_ref[...]
    tm = x.shape[0]
    row = lax.broadcasted_iota(jnp.int32, (tm, 1), 0)
    prev = jnp.where(row == 0, carry_ref[7:8, :], pltpu.roll(x, 1, axis=0))
    xx = prev - x
    carry_ref[...] = x[tm - 8:, :]
    for j, o_ref in enumerate(outs):
        o_ref[...] = (x + xx * mu_ref[j:j + 1, :]).astype(BF16)


def _rwkv_mix(h, mu, batch, tm):
    n, d = h.shape
    nmix = mu.shape[0]
    tpb = n // batch // tm
    blk = pl.BlockSpec((tm, d), lambda b, t: (b * tpb + t, 0))
    return pl.pallas_call(
        _rwkv_mix_kernel,
        grid=(batch, tpb),
        in_specs=[blk, pl.BlockSpec((nmix, d), lambda b, t: (0, 0))],
        out_specs=[blk] * nmix,
        out_shape=[jax.ShapeDtypeStruct((n, d), BF16)] * nmix,
        scratch_shapes=[pltpu.VMEM((8, d), F32)],
        compiler_params=_cparams("arbitrary", "arbitrary"),
        name="rwkv_mix",
    )(h, mu)


def _lora_kernel(x_ref, wa_ref, wb_ref, bias_ref, o_ref, wab_ref, wbb_ref, *, mid_act, out_act):
    @pl.when(pl.program_id(0) == 0)
    def _():
        wab_ref[...] = wa_ref[...].astype(BF16)
        wbb_ref[...] = wb_ref[...].astype(BF16)

    mid = mid_act(jnp.dot(x_ref[...], wab_ref[...], preferred_element_type=F32))
    y = jnp.dot(mid.astype(BF16), wbb_ref[...], preferred_element_type=F32)
    o_ref[...] = out_act(bias_ref[...] + y)


def _lora(xb, wa, wb, bias, mid_act, out_act, tm, name):
    n, d = xb.shape
    r = wa.shape[1]
    dout = wb.shape[1]
    return pl.pallas_call(
        functools.partial(_lora_kernel, mid_act=mid_act, out_act=out_act),
        grid=(n // tm,),
        in_specs=[
            pl.BlockSpec((tm, d), lambda i: (i, 0)),
            pl.BlockSpec((d, r), lambda i: (0, 0)),
            pl.BlockSpec((r, dout), lambda i: (0, 0)),
            pl.BlockSpec((1, dout), lambda i: (0, 0)),
        ],
        out_specs=pl.BlockSpec((tm, dout), lambda i: (i, 0)),
        out_shape=jax.ShapeDtypeStruct((n, dout), F32),
        scratch_shapes=[pltpu.VMEM((d, r), BF16), pltpu.VMEM((r, dout), BF16)],
        compiler_params=_cparams("arbitrary"),
        name=name,
    )(xb, wa, wb, bias.reshape(1, dout))


def _rwkv_log_decay(z):
    w_log = -(jnp.maximum(-z, 0.0) + jnp.log(1.0 + jnp.exp(-jnp.abs(z)))) - 0.5
    return -jnp.exp(w_log)


RWKV_GROUP = 4
RWKV_UNROLL = 4


def _seg_sum(x, bd):
    hi = x.astype(BF16)
    lo = (x - hi.astype(F32)).astype(BF16)
    return jnp.dot(hi, bd, preferred_element_type=F32) + jnp.dot(lo, bd, preferred_element_type=F32)


def _rwkv_scan_kernel(r_ref, kr_ref, v_ref, lw_ref, a_ref, g_ref, kk_p, ka_p, rk_p, gg_p, gb_p,
                      o_ref, kk_s, k_s, bonus_s, y_s, st_ref, *, chunk, ptile):
    t, w = r_ref.shape
    nhead = w // RWKV_HEAD
    sc = nhead * chunk
    lane_r = lax.broadcasted_iota(jnp.int32, (w, w), 0) // RWKV_HEAD
    lane_c = lax.broadcasted_iota(jnp.int32, (w, w), 1) // RWKV_HEAD
    bd = jnp.where(lane_r == lane_c, 1.0, 0.0).astype(BF16)

    def prologue(i, carry):
        rows = pl.ds(pl.multiple_of(i * ptile, 8), ptile)
        kr = kr_ref[rows, :]
        a = a_ref[rows, :]
        kkr = kr * kk_p[...]
        nrm = jnp.maximum(jnp.sqrt(_seg_sum(kkr * kkr, bd)), 1e-12)
        kk_s[rows, :] = kkr / nrm
        k = kr * (1.0 + (a - 1.0) * ka_p[...])
        k_s[rows, :] = k
        bonus_s[rows, :] = _seg_sum(r_ref[rows, :] * k * rk_p[...], bd) * v_ref[rows, :]
        return carry

    lax.fori_loop(0, t // ptile, prologue, 0)

    st_ref[...] = jnp.zeros_like(st_ref)
    tril = _tril_bf16(chunk, True)
    head_of_lane = lax.broadcasted_iota(jnp.int32, (chunk, w), 1) // RWKV_HEAD
    ri = lax.broadcasted_iota(jnp.int32, (2 * sc, sc), 0)
    ci = lax.broadcasted_iota(jnp.int32, (2 * sc, sc), 1)
    low_mask = ci < jnp.where(ri < sc, ri, ri - sc + 1)
    nsteps = max(1, (chunk - 1).bit_length())

    def stack(x):
        return jnp.concatenate([jnp.where(head_of_lane == hh, x, 0.0) for hh in range(nhead)], axis=0)

    eye = jnp.where(lax.broadcasted_iota(jnp.int32, (sc, sc), 0)
                    == lax.broadcasted_iota(jnp.int32, (sc, sc), 1), 1.0, 0.0)

    def prepare(c):
        start = c * chunk
        rows = pl.ds(start if isinstance(start, int) else pl.multiple_of(start, 16), chunk)
        r = r_ref[rows, :]
        v = v_ref[rows, :]
        lw = lw_ref[rows, :]
        a = a_ref[rows, :]
        kk = kk_s[rows, :]
        k = k_s[rows, :]
        cum = _cumsum_rows(lw, tril)
        e_neg = jnp.exp(-cum)
        at2 = stack(-kk * jnp.exp(cum - lw))
        rt2 = stack(r * jnp.exp(cum))
        bvec = kk * a
        bb2 = stack(bvec * e_neg)
        kb2 = stack(k * e_neg)
        v2 = stack(v)
        ar2 = jnp.concatenate([at2, rt2], axis=0).astype(BF16)
        cl = cum[chunk - 1:chunk]
        e_end = jnp.exp(cl - cum)
        khbh = jnp.concatenate([stack(k * e_end), stack(bvec * e_end)], axis=0).astype(BF16)
        yield None
        pb = jnp.where(low_mask, _dot_nt(ar2, bb2), 0.0)
        pk = jnp.where(low_mask, _dot_nt(ar2, kb2), 0.0)
        m_ab, m_rb = pb[:sc], pb[sc:]
        m_ak, m_rk = pk[:sc], pk[sc:]
        yield None
        u0 = _dot_nn(m_ak, v2)
        y0 = _dot_nn(m_rk, v2)
        tinv = eye + m_ab
        lpow = m_ab
        for _ in range(nsteps - 1):
            yield None
            lpow = _dot_nn(lpow, lpow)
            tinv = tinv + _dot_nn(tinv, lpow)
        yield dict(rows=rows, ar2=ar2, tinv=tinv.astype(BF16), m_rb=m_rb.astype(BF16),
                   u0=u0, y0=y0, v2=v2.astype(BF16), khbh=khbh, decay=jnp.exp(cl))

    def prepare_interleaved(chunk_ids):
        return _round_robin([prepare(c) for c in chunk_ids])

    def advance(pc, st):
        ps = _dot_nt(pc["ar2"], st)
        u2 = jnp.dot(pc["tinv"], (ps[:sc] + pc["u0"]).astype(BF16), preferred_element_type=F32)
        u2b = u2.astype(BF16)
        y2 = ps[sc:] + pc["y0"] + jnp.dot(pc["m_rb"], u2b, preferred_element_type=F32)
        y = y2[0:chunk]
        for hh in range(1, nhead):
            y = y + y2[hh * chunk:(hh + 1) * chunk]
        y_s[pc["rows"], :] = y
        return st * pc["decay"] + _dot_tn(jnp.concatenate([pc["v2"], u2b], axis=0), pc["khbh"])

    nchunks = t // chunk

    def body(i, carry):
        prepared = prepare_interleaved([i * RWKV_UNROLL + j for j in range(RWKV_UNROLL)])
        st = st_ref[...]
        for pc in prepared:
            st = advance(pc, st)
        st_ref[...] = st
        return carry

    lax.fori_loop(0, nchunks // RWKV_UNROLL, body, 0)
    tail = list(range(nchunks - nchunks % RWKV_UNROLL, nchunks))
    if tail:
        st = st_ref[...]
        for pc in prepare_interleaved(tail):
            st = advance(pc, st)
        st_ref[...] = st

    inv = 1.0 / RWKV_HEAD

    def epilogue(i, carry):
        rows = pl.ds(pl.multiple_of(i * ptile, 8), ptile)
        y = y_s[rows, :]
        mu = _seg_sum(y, bd) * inv
        yc = y - mu
        var = _seg_sum(yc * yc, bd) * inv
        yn = yc * lax.rsqrt(var + 1e-5 * RWKV_HEAD) * gg_p[...] + gb_p[...]
        o_ref[rows, :] = ((yn + bonus_s[rows, :]) * g_ref[rows, :]).astype(BF16)
        return carry

    lax.fori_loop(0, t // ptile, epilogue, 0)


def _rwkv_scan(r, kr, v, lw, a, g, p, batch):
    n, d = r.shape
    t = n // batch
    chunk = _chunk_len(t)
    w = min(d, RWKV_GROUP * RWKV_HEAD)
    ptile = _divisor_tile(t, 768, 16)
    blk = pl.BlockSpec((t, w), lambda b, j: (b, j))
    prm = pl.BlockSpec((1, w), lambda b, j: (0, j))
    params = [p["rwkv_k_k"], p["rwkv_k_a"], p["rwkv_r_k"], p["rwkv_gn_g"], p["rwkv_gn_b"]]
    return pl.pallas_call(
        functools.partial(_rwkv_scan_kernel, chunk=chunk, ptile=ptile),
        grid=(batch, d // w),
        in_specs=[blk] * 6 + [prm] * 5,
        out_specs=blk,
        out_shape=jax.ShapeDtypeStruct((n, d), BF16),
        scratch_shapes=[pltpu.VMEM((t, w), F32)] * 4 + [pltpu.VMEM((w, w), F32)],
        compiler_params=_cparams("parallel", "parallel"),
        name="rwkv_scan",
    )(r, kr, v, lw, a, g, *[x.reshape(1, d) for x in params])


def _rwkv_mixer_layer(h, hb, p, g, b, alpha, batch, tm):
    n, d = h.shape
    ident = lambda y: y
    xr, xw, xk, xv, xa, xg = _rwkv_mix(h, p["rwkv_mu"], batch, tm)
    (r,) = _proj(xr, p["rwkv_w_r"], 0, d, lambda y: (y,), [], [F32], tm, "rwkv_proj_r")
    (kr,) = _proj(xk, p["rwkv_w_k"], 0, d, lambda y: (y,), [], [F32], tm, "rwkv_proj_k")
    (v,) = _proj(xv, p["rwkv_w_v"], 0, d, lambda y: (y,), [], [F32], tm, "rwkv_proj_v")
    lw = _lora(xw, p["rwkv_w1"], p["rwkv_w2"], p["rwkv_w0"], jnp.tanh, _rwkv_log_decay, tm, "rwkv_lora_w")
    a = _lora(xa, p["rwkv_a1"], p["rwkv_a2"], p["rwkv_a0"], ident, _sigmoid, tm, "rwkv_lora_a")
    gate = _lora(xg, p["rwkv_g1"], p["rwkv_g2"], jnp.zeros((d,), F32), _sigmoid, ident, tm, "rwkv_lora_g")
    z = _rwkv_scan(r, kr, v, lw, a, gate, p, batch)
    return _mm_res_ln(z, p["rwkv_w_out"], h, g, b, alpha, tm)


def kernel(x, meta, ln_mix_g, ln_mix_b, ln_ffn_g, ln_ffn_b, conv_w_in, conv_w, conv_b, conv_w_out, rwkv_mu, rwkv_w_r, rwkv_w_k, rwkv_w_v, rwkv_w0, rwkv_w1, rwkv_w2, rwkv_a0, rwkv_a1, rwkv_a2, rwkv_g1, rwkv_g2, rwkv_k_k, rwkv_k_a, rwkv_r_k, rwkv_gn_g, rwkv_gn_b, rwkv_w_out, hgrn_w_in, hgrn_lb, hgrn_norm_g, hgrn_w_out, fox_w_in, fox_b_f, fox_q_norm_g, fox_k_norm_g, fox_w_out, ffn0_w1, ffn0_w3, ffn0_w2, moe1_router, moe1_router_b, moe1_w1, moe1_w3, moe1_w2, ffn2_w1, ffn2_w3, ffn2_w2, moe3_router, moe3_router_b, moe3_w1, moe3_w3, moe3_w2):
    batch, seq, d = x.shape
    depth = ln_mix_g.shape[0]
    assert depth == 4
    alpha = (2.0 * depth) ** 0.25
    t = N_META + seq
    n = batch * t
    tm = _divisor_tile(t, 768, 16)
    p = dict(
        conv_w_in=conv_w_in, conv_w=conv_w, conv_b=conv_b, conv_w_out=conv_w_out,
        rwkv_mu=rwkv_mu, rwkv_w_r=rwkv_w_r, rwkv_w_k=rwkv_w_k, rwkv_w_v=rwkv_w_v, rwkv_w0=rwkv_w0,
        rwkv_w1=rwkv_w1, rwkv_w2=rwkv_w2, rwkv_a0=rwkv_a0, rwkv_a1=rwkv_a1, rwkv_a2=rwkv_a2,
        rwkv_g1=rwkv_g1, rwkv_g2=rwkv_g2, rwkv_k_k=rwkv_k_k, rwkv_k_a=rwkv_k_a, rwkv_r_k=rwkv_r_k,
        rwkv_gn_g=rwkv_gn_g, rwkv_gn_b=rwkv_gn_b, rwkv_w_out=rwkv_w_out,
        hgrn_w_in=hgrn_w_in, hgrn_lb=hgrn_lb, hgrn_norm_g=hgrn_norm_g, hgrn_w_out=hgrn_w_out,
        fox_w_in=fox_w_in, fox_b_f=fox_b_f, fox_q_norm_g=fox_q_norm_g, fox_k_norm_g=fox_k_norm_g,
        fox_w_out=fox_w_out,
    )
    h = jnp.concatenate(
        [jnp.broadcast_to(meta[None].astype(x.dtype), (batch, N_META, d)), x], axis=1).reshape(n, d)
    hb = h.astype(BF16)

    h, hb = _conv_mixer_layer(h, hb, p, ln_mix_g[0], ln_mix_b[0], alpha, batch, tm)
    h, hb = _dense_ffn(hb, h, ffn0_w1, ffn0_w3, ffn0_w2, ln_ffn_g[0], ln_ffn_b[0], alpha, tm)
    h, hb = _rwkv_mixer_layer(h, hb, p, ln_mix_g[1], ln_mix_b[1], alpha, batch, tm)
    h, hb = _moe_ffn(h, hb, moe1_router, moe1_router_b, moe1_w1, moe1_w3, moe1_w2,
                     ln_ffn_g[1], ln_ffn_b[1], alpha, tm)
    h, hb = _hgrn_mixer_layer(h, hb, p, 2, ln_mix_g[2], ln_mix_b[2], alpha, batch, tm)
    h, hb = _dense_ffn(hb, h, ffn2_w1, ffn2_w3, ffn2_w2, ln_ffn_g[2], ln_ffn_b[2], alpha, tm)
    h, hb = _fox_mixer_layer(h, hb, p, ln_mix_g[3], ln_mix_b[3], alpha, batch, tm)
    h, hb = _moe_ffn(h, hb, moe3_router, moe3_router_b, moe3_w1, moe3_w3, moe3_w2,
                     ln_ffn_g[3], ln_ffn_b[3], alpha, tm)
    return h.reshape(batch, t, d)[:, N_META:]
```

```python
import functools

import jax
import jax.numpy as jnp
from jax import lax
from jax.experimental import pallas as pl
from jax.experimental.pallas import tpu as pltpu

F32 = jnp.float32
BF16 = jnp.bfloat16

N_META = 16
LN_EPS = 1e-5
RMS_EPS = 1e-6
RWKV_HEAD = 64
HGRN_HEAD = 128
FOX_HEAD = 128
TOP_K = 2
LANES = 128
VMEM_LIMIT_BYTES = 56 * 2**20


def _cparams(*sem):
    return pltpu.CompilerParams(dimension_semantics=sem, vmem_limit_bytes=VMEM_LIMIT_BYTES)


def _divisor_tile(n, cap, mult):
    best = None
    for d in range(mult, min(n, cap) + 1, mult):
        if n % d == 0:
            best = d
    assert best is not None, (n, cap, mult)
    return best


def _layer_norm(y, g, b):
    mu = jnp.mean(y, axis=-1, keepdims=True)
    yc = y - mu
    var = jnp.mean(yc * yc, axis=-1, keepdims=True)
    return yc * lax.rsqrt(var + LN_EPS) * g + b


def _sigmoid(x):
    return 1.0 / (1.0 + jnp.exp(-x))


def _silu(x):
    return x * _sigmoid(x)


def _mm_res_ln_kernel(z_ref, w_ref, h_ref, g_ref, b_ref, o_ref, ob_ref, *, nk, alpha):
    k = pl.program_id(1)
    part = jnp.dot(z_ref[...], w_ref[...], preferred_element_type=F32)

    @pl.when(k == 0)
    def _():
        o_ref[...] = part

    @pl.when(k > 0)
    def _():
        o_ref[...] += part

    @pl.when(k == nk - 1)
    def _():
        y = _layer_norm(alpha * h_ref[...] + o_ref[...], g_ref[...], b_ref[...])
        o_ref[...] = y
        ob_ref[...] = y.astype(BF16)


WEIGHT_SLICE_ROWS = 256


def _load_weight_bf16(w_hbm, wb_ref, stage_ref, sem):
    rows = stage_ref.shape[1]
    nslice = w_hbm.shape[0] // rows

    def copy(s):
        return pltpu.make_async_copy(w_hbm.at[pl.ds(s * rows, rows), :], stage_ref.at[s % 2], sem.at[s % 2])

    copy(0).start()
    for s in range(nslice):
        if s + 1 < nslice:
            copy(s + 1).start()
        copy(s).wait()
        wb_ref[s * rows:(s + 1) * rows, :] = stage_ref[s % 2].astype(BF16)


def _mm_res_ln_resident_kernel(z_ref, w_ref, h_ref, g_ref, b_ref, o_ref, ob_ref, wb_ref, stage_ref, sem,
                               *, alpha):
    @pl.when(pl.program_id(0) == 0)
    def _():
        _load_weight_bf16(w_ref, wb_ref, stage_ref, sem)

    y = alpha * h_ref[...] + jnp.dot(z_ref[...], wb_ref[...], preferred_element_type=F32)
    y = _layer_norm(y, g_ref[...], b_ref[...])
    o_ref[...] = y
    ob_ref[...] = y.astype(BF16)


RESIDENT_WEIGHT_BYTES = 16 * 2**20


def _mm_res_ln_resident(z, w, h, g, b, alpha, tm):
    n, kdim = z.shape
    d = w.shape[1]
    ws = _divisor_tile(kdim, WEIGHT_SLICE_ROWS, 8)
    const = lambda m: (0, 0)
    return pl.pallas_call(
        functools.partial(_mm_res_ln_resident_kernel, alpha=alpha),
        grid=(n // tm,),
        in_specs=[
            pl.BlockSpec((tm, kdim), lambda m: (m, 0)),
            pl.BlockSpec(memory_space=pl.ANY),
            pl.BlockSpec((tm, d), lambda m: (m, 0)),
            pl.BlockSpec((1, d), const),
            pl.BlockSpec((1, d), const),
        ],
        out_specs=[
            pl.BlockSpec((tm, d), lambda m: (m, 0)),
            pl.BlockSpec((tm, d), lambda m: (m, 0)),
        ],
        out_shape=[jax.ShapeDtypeStruct((n, d), F32), jax.ShapeDtypeStruct((n, d), BF16)],
        scratch_shapes=[pltpu.VMEM((kdim, d), BF16), pltpu.VMEM((2, ws, d), F32),
                        pltpu.SemaphoreType.DMA((2,))],
        compiler_params=_cparams("arbitrary"),
        name="mm_res_ln_resident",
    )(z, w, h, g.reshape(1, d), b.reshape(1, d))


def _mm_res_ln(z, w, h, g, b, alpha, tm):
    n, kdim = z.shape
    d = w.shape[1]
    if kdim * d * 4 <= RESIDENT_WEIGHT_BYTES:
        return _mm_res_ln_resident(z, w, h, g, b, alpha, tm)
    w = w.astype(BF16)
    tk = _divisor_tile(kdim, 1408, LANES)
    nk = kdim // tk
    return pl.pallas_call(
        functools.partial(_mm_res_ln_kernel, nk=nk, alpha=alpha),
        grid=(n // tm, nk),
        in_specs=[
            pl.BlockSpec((tm, tk), lambda m, k: (m, k)),
            pl.BlockSpec((tk, d), lambda m, k: (k, 0)),
            pl.BlockSpec((tm, d), lambda m, k: (m, 0)),
            pl.BlockSpec((1, d), lambda m, k: (0, 0)),
            pl.BlockSpec((1, d), lambda m, k: (0, 0)),
        ],
        out_specs=[
            pl.BlockSpec((tm, d), lambda m, k: (m, 0)),
            pl.BlockSpec((tm, d), lambda m, k: (m, 0)),
        ],
        out_shape=[jax.ShapeDtypeStruct((n, d), F32), jax.ShapeDtypeStruct((n, d), BF16)],
        compiler_params=_cparams("parallel", "arbitrary"),
        name="mm_res_ln",
    )(z, w, h, g.reshape(1, d), b.reshape(1, d))


def _ffn_up_kernel(exp_ref, nact_ref, x_ref, w1_ref, w3_ref, o_ref, w1b_ref, w3b_ref):
    c = pl.program_id(1)
    prev = exp_ref[jnp.maximum(c - 1, 0)]
    new_weights = jnp.logical_or(c == 0, exp_ref[c] != prev)

    @pl.when(new_weights)
    def _():
        w1b_ref[...] = w1_ref[...].astype(BF16)
        w3b_ref[...] = w3_ref[...].astype(BF16)

    @pl.when(c < nact_ref[0])
    def _():
        x = x_ref[...]
        a = jnp.dot(x, w1b_ref[...], preferred_element_type=F32)
        bb = jnp.dot(x, w3b_ref[...], preferred_element_type=F32)
        o_ref[...] = (_silu(a) * bb).astype(BF16)

    @pl.when(c >= nact_ref[0])
    def _():
        o_ref[...] = jnp.zeros_like(o_ref)


def _ffn_up(x, w1, w3, chunk_expert, n_active, rows):
    p, d = x.shape
    f = w1.shape[2]
    tf = _divisor_tile(f, 512, LANES)
    grid_spec = pltpu.PrefetchScalarGridSpec(
        num_scalar_prefetch=2,
        grid=(f // tf, p // rows),
        in_specs=[
            pl.BlockSpec((rows, d), lambda j, c, e, na: (c, 0)),
            pl.BlockSpec((None, d, tf), lambda j, c, e, na: (e[c], 0, j)),
            pl.BlockSpec((None, d, tf), lambda j, c, e, na: (e[c], 0, j)),
        ],
        out_specs=pl.BlockSpec((rows, tf), lambda j, c, e, na: (c, j)),
        scratch_shapes=[pltpu.VMEM((d, tf), BF16), pltpu.VMEM((d, tf), BF16)],
    )
    return pl.pallas_call(
        _ffn_up_kernel,
        grid_spec=grid_spec,
        out_shape=jax.ShapeDtypeStruct((p, f), BF16),
        compiler_params=_cparams("arbitrary", "arbitrary"),
        name="ffn_up",
    )(chunk_expert, n_active, x, w1, w3)


def _dense_ffn(hb, h, w1, w3, w2, g, b, alpha, tm):
    n = hb.shape[0]
    nchunks = n // tm
    hmid = _ffn_up(hb, w1[None], w3[None], jnp.zeros((nchunks,), jnp.int32),
                   jnp.full((1,), nchunks, jnp.int32), tm)
    return _mm_res_ln(hmid, w2, h, g, b, alpha, tm)


def _conv_proj_kernel(x_ref, wb_ref, wc_ref, wh_ref, cw_ref, cb_ref, o_ref,
                      wbb_ref, wcb_ref, whb_ref, carry_ref):
    bi = pl.program_id(1)
    ti = pl.program_id(2)

    @pl.when(jnp.logical_and(bi == 0, ti == 0))
    def _():
        wbb_ref[...] = wb_ref[...].astype(BF16)
        wcb_ref[...] = wc_ref[...].astype(BF16)
        whb_ref[...] = wh_ref[...].astype(BF16)

    @pl.when(ti == 0)
    def _():
        carry_ref[...] = jnp.zeros_like(carry_ref)

    x = x_ref[...]
    gate_b = jnp.dot(x, wbb_ref[...], preferred_element_type=F32)
    gate_c = jnp.dot(x, wcb_ref[...], preferred_element_type=F32)
    hh = jnp.dot(x, whb_ref[...], preferred_element_type=F32)
    u = gate_c * hh
    tm = u.shape[0]
    prev1 = carry_ref[7:8, :]
    prev2 = carry_ref[6:7, :]
    row = lax.broadcasted_iota(jnp.int32, (tm, 1), 0)
    r1 = jnp.where(row == 0, prev1, pltpu.roll(u, 1, axis=0))
    r2 = jnp.where(row == 0, prev2, jnp.where(row == 1, prev1, pltpu.roll(u, 2, axis=0)))
    v = cw_ref[0:1, :] * r2 + cw_ref[1:2, :] * r1 + cw_ref[2:3, :] * u + cb_ref[...]
    carry_ref[...] = u[tm - 8:, :]
    o_ref[...] = (gate_b * v).astype(BF16)


def _conv_proj(hb, w_in, conv_w, conv_b, batch, tm):
    n, d = hb.shape
    tn = _divisor_tile(d, 512, LANES)
    nd = d // tn
    tpb = n // batch // tm
    return pl.pallas_call(
        _conv_proj_kernel,
        grid=(nd, batch, tpb),
        in_specs=[
            pl.BlockSpec((tm, d), lambda j, bi, ti: (bi * tpb + ti, 0)),
            pl.BlockSpec((d, tn), lambda j, bi, ti: (0, j)),
            pl.BlockSpec((d, tn), lambda j, bi, ti: (0, nd + j)),
            pl.BlockSpec((d, tn), lambda j, bi, ti: (0, 2 * nd + j)),
            pl.BlockSpec((3, tn), lambda j, bi, ti: (0, j)),
            pl.BlockSpec((1, tn), lambda j, bi, ti: (0, j)),
        ],
        out_specs=pl.BlockSpec((tm, tn), lambda j, bi, ti: (bi * tpb + ti, j)),
        out_shape=jax.ShapeDtypeStruct((n, d), BF16),
        scratch_shapes=[pltpu.VMEM((d, tn), BF16)] * 3 + [pltpu.VMEM((8, tn), F32)],
        compiler_params=_cparams("arbitrary", "arbitrary", "arbitrary"),
        name="conv_proj",
    )(hb, w_in, w_in, w_in, conv_w, conv_b.reshape(1, d))


def _conv_mixer_layer(h, hb, p, g, b, alpha, batch, tm):
    z = _conv_proj(hb, p["conv_w_in"], p["conv_w"], p["conv_b"], batch, tm)
    return _mm_res_ln(z, p["conv_w_out"], h, g, b, alpha, tm)


MOE_ROWS = 128


def _split_bf16(a):
    hi = a.astype(BF16)
    lo = (a - hi.astype(F32)).astype(BF16)
    return hi, lo


def _dot_f32(a, b):
    ah, al = _split_bf16(a)
    bh, bl = _split_bf16(b)
    return (jnp.dot(ah, bh, preferred_element_type=F32)
            + (jnp.dot(ah, bl, preferred_element_type=F32)
               + jnp.dot(al, bh, preferred_element_type=F32)))


def _router_kernel(h_ref, w_ref, b_ref, info_ref, cnt_ref, carry_ref):
    i = pl.program_id(0)

    @pl.when(i == 0)
    def _():
        carry_ref[...] = jnp.zeros_like(carry_ref)

    logits = _dot_f32(h_ref[...], w_ref[...]) + b_ref[...]
    tm, ne = logits.shape
    lane = lax.broadcasted_iota(jnp.int32, (tm, ne), 1)
    m1 = jnp.max(logits, axis=-1, keepdims=True)
    i1 = jnp.min(jnp.where(logits == m1, lane, ne), axis=-1, keepdims=True)
    mask1 = lane == i1
    rest = jnp.where(mask1, -jnp.inf, logits)
    m2 = jnp.max(rest, axis=-1, keepdims=True)
    i2 = jnp.min(jnp.where(rest == m2, lane, ne), axis=-1, keepdims=True)
    mask2 = lane == i2
    dd = jnp.exp(m2 - m1)
    g1 = 1.0 / (1.0 + dd)
    g2 = dd / (1.0 + dd)
    sel = jnp.where(jnp.logical_or(mask1, mask2), 1.0, 0.0)
    r_i = lax.broadcasted_iota(jnp.int32, (tm, tm), 0)
    c_i = lax.broadcasted_iota(jnp.int32, (tm, tm), 1)
    tril = jnp.where(c_i < r_i, 1.0, 0.0).astype(BF16)
    rank = jnp.dot(tril, sel.astype(BF16), preferred_element_type=F32) + carry_ref[...]
    r1 = jnp.sum(jnp.where(mask1, rank, 0.0), axis=-1, keepdims=True)
    r2 = jnp.sum(jnp.where(mask2, rank, 0.0), axis=-1, keepdims=True)
    info = jnp.where(lane == 0, i1.astype(F32),
           jnp.where(lane == 1, i2.astype(F32),
           jnp.where(lane == 2, g1,
           jnp.where(lane == 3, g2,
           jnp.where(lane == 4, r1,
           jnp.where(lane == 5, r2, 0.0))))))
    info_ref[...] = info
    total = carry_ref[...] + jnp.sum(sel, axis=0, keepdims=True)
    carry_ref[...] = total
    cnt_ref[...] = total


def _router(h, w, b, tm):
    n, d = h.shape
    ne = w.shape[1]
    assert ne >= 6
    return pl.pallas_call(
        _router_kernel,
        grid=(n // tm,),
        in_specs=[
            pl.BlockSpec((tm, d), lambda i: (i, 0)),
            pl.BlockSpec((d, ne), lambda i: (0, 0)),
            pl.BlockSpec((1, ne), lambda i: (0, 0)),
        ],
        out_specs=[
            pl.BlockSpec((tm, ne), lambda i: (i, 0)),
            pl.BlockSpec((1, ne), lambda i: (0, 0)),
        ],
        out_shape=[jax.ShapeDtypeStruct((n, ne), F32), jax.ShapeDtypeStruct((1, ne), F32)],
        scratch_shapes=[pltpu.VMEM((1, ne), F32)],
        compiler_params=_cparams("arbitrary"),
        name="moe_router",
    )(h, w, b.reshape(1, ne))


def _row_copy(src_hbm, row, dst_vmem, r, sem):
    return pltpu.make_async_copy(src_hbm.at[pl.ds(row, 1), :], dst_vmem.at[pl.ds(r, 1), :], sem)


def _rows_wait(src_hbm, dst_vmem, sem):
    pltpu.make_async_copy(src_hbm.at[pl.ds(0, dst_vmem.shape[0]), :], dst_vmem, sem).wait()


def _moe_gather_kernel(src_ref, nact_ref, h_ref, o_ref, buf_ref, sem):
    c = pl.program_id(0)
    rows = buf_ref.shape[1]

    def issue(chunk):
        slot = chunk % 2

        def start(r, carry):
            _row_copy(h_ref, src_ref[chunk * rows + r], buf_ref.at[slot], r, sem.at[slot]).start()
            return carry

        lax.fori_loop(0, rows, start, 0, unroll=8)

    @pl.when(jnp.logical_and(c == 0, nact_ref[0] > 0))
    def _():
        issue(c)

    @pl.when(c + 1 < nact_ref[0])
    def _():
        issue(c + 1)

    @pl.when(c < nact_ref[0])
    def _():
        slot = c % 2
        _rows_wait(h_ref, buf_ref.at[slot], sem.at[slot])
        o_ref[...] = buf_ref[slot].astype(BF16)

    @pl.when(c >= nact_ref[0])
    def _():
        o_ref[...] = jnp.zeros_like(o_ref)


def _moe_gather(h, src, n_active, p, rows):
    n, d = h.shape
    grid_spec = pltpu.PrefetchScalarGridSpec(
        num_scalar_prefetch=2,
        grid=(p // rows,),
        in_specs=[pl.BlockSpec(memory_space=pl.ANY)],
        out_specs=pl.BlockSpec((rows, d), lambda c, s, na: (c, 0)),
        scratch_shapes=[pltpu.VMEM((2, rows, d), F32), pltpu.SemaphoreType.DMA((2,))],
    )
    return pl.pallas_call(
        _moe_gather_kernel,
        grid_spec=grid_spec,
        out_shape=jax.ShapeDtypeStruct((p, d), BF16),
        compiler_params=_cparams("arbitrary"),
        name="moe_gather",
    )(src, n_active, h)


def _moe_combine_kernel(p1_ref, p2_ref, y_ref, h_ref, info_ref, g_ref, b_ref, o_ref, ob_ref,
                        buf1_ref, buf2_ref, sem, *, alpha):
    i = pl.program_id(0)
    tm = buf1_ref.shape[1]

    def issue(tile):
        slot = tile % 2

        def start(r, carry):
            _row_copy(y_ref, p1_ref[tile * tm + r], buf1_ref.at[slot], r, sem.at[slot]).start()
            _row_copy(y_ref, p2_ref[tile * tm + r], buf2_ref.at[slot], r, sem.at[slot]).start()
            return carry

        lax.fori_loop(0, tm, start, 0, unroll=8)

    @pl.when(i == 0)
    def _():
        issue(i)

    @pl.when(i + 1 < pl.num_programs(0))
    def _():
        issue(i + 1)

    slot = i % 2
    _rows_wait(y_ref, buf1_ref.at[slot], sem.at[slot])
    _rows_wait(y_ref, buf2_ref.at[slot], sem.at[slot])
    info = info_ref[...]
    y = alpha * h_ref[...] + (info[:, 2:3] * buf1_ref[slot] + info[:, 3:4] * buf2_ref[slot])
    y = _layer_norm(y, g_ref[...], b_ref[...])
    o_ref[...] = y
    ob_ref[...] = y.astype(BF16)


def _moe_combine(y, h, info, p1, p2, g, b, alpha, tm):
    n, d = h.shape
    ne = info.shape[1]
    grid_spec = pltpu.PrefetchScalarGridSpec(
        num_scalar_prefetch=2,
        grid=(n // tm,),
        in_specs=[
            pl.BlockSpec(memory_space=pl.ANY),
            pl.BlockSpec((tm, d), lambda i, a, c: (i, 0)),
            pl.BlockSpec((tm, ne), lambda i, a, c: (i, 0)),
            pl.BlockSpec((1, d), lambda i, a, c: (0, 0)),
            pl.BlockSpec((1, d), lambda i, a, c: (0, 0)),
        ],
        out_specs=[
            pl.BlockSpec((tm, d), lambda i, a, c: (i, 0)),
            pl.BlockSpec((tm, d), lambda i, a, c: (i, 0)),
        ],
        scratch_shapes=[pltpu.VMEM((2, tm, d), F32), pltpu.VMEM((2, tm, d), F32),
                        pltpu.SemaphoreType.DMA((2,))],
    )
    return pl.pallas_call(
        functools.partial(_moe_combine_kernel, alpha=alpha),
        grid_spec=grid_spec,
        out_shape=[jax.ShapeDtypeStruct((n, d), F32), jax.ShapeDtypeStruct((n, d), BF16)],
        compiler_params=_cparams("arbitrary"),
        name="moe_combine",
    )(p1, p2, y, h, info, g.reshape(1, d), b.reshape(1, d))


MOE_PASS_BLOCKS = 18
MOE_F_TILE = 256


def _expert_ffn_kernel(pe_ref, ps_ref, pn_ref, nu_ref, xs_ref, w1_ref, w3_ref, w2_ref, y_ref,
                       x_buf, acc_ref, w1b_ref, w3b_ref, w2b_ref, sem):
    p = pl.program_id(0)
    f = pl.program_id(1)
    nf = pl.num_programs(1)
    nb = pn_ref[p]
    sb = MOE_ROWS
    start = pl.multiple_of(ps_ref[p], MOE_ROWS)
    npair = nb // 2

    @pl.when(nb > 0)
    def _():
        @pl.when(f == 0)
        def _():
            cp = pltpu.make_async_copy(xs_ref.at[pl.ds(start, x_buf.shape[0]), :], x_buf, sem.at[0])
            cp.start()
            cp.wait()

        w1b_ref[...] = w1_ref[...].astype(BF16)
        w3b_ref[...] = w3_ref[...].astype(BF16)
        w2b_ref[...] = w2_ref[...].astype(BF16)

        def up(unit, nunit):
            x = x_buf[pl.ds(pl.multiple_of(unit * sb, sb), nunit * sb), :]
            a = jnp.dot(x, w1b_ref[...], preferred_element_type=F32)
            bb = jnp.dot(x, w3b_ref[...], preferred_element_type=F32)
            return (_silu(a) * bb).astype(BF16)

        def down(unit, nunit, hmid, first):
            rows = pl.ds(pl.multiple_of(unit * sb, sb), nunit * sb)
            part = jnp.dot(hmid, w2b_ref[...], preferred_element_type=F32)
            if first:
                acc_ref[rows, :] = part
            else:
                acc_ref[rows, :] += part

        def sweep(first):
            @pl.when(npair > 0)
            def _():
                def body(i, hprev):
                    down(2 * (i - 1), 2, hprev, first)
                    return up(2 * i, 2)

                down(2 * (npair - 1), 2, lax.fori_loop(1, npair, body, up(0, 2)), first)

            @pl.when(nb % 2 == 1)
            def _():
                down(nb - 1, 1, up(nb - 1, 1), first)

        @pl.when(f == 0)
        def _():
            sweep(True)

        @pl.when(f > 0)
        def _():
            sweep(False)

        @pl.when(f == nf - 1)
        def _():
            def out_copy(i):
                rows = pl.ds(pl.multiple_of(i * sb, sb), sb)
                dst = pl.ds(pl.multiple_of(start + i * sb, sb), sb)
                return pltpu.make_async_copy(acc_ref.at[rows, :], y_ref.at[dst, :], sem.at[1])

            lax.fori_loop(0, nb, lambda i, c: (out_copy(i).start(), c)[1], 0)
            lax.fori_loop(0, nb, lambda i, c: (out_copy(i).wait(), c)[1], 0)

    @pl.when(jnp.logical_and(p == pl.num_programs(0) - 1, f == nf - 1))
    def _():
        acc_ref[0:sb, :] = jnp.zeros((sb, acc_ref.shape[1]), F32)

        def zero_copy(i):
            dst = pl.ds(pl.multiple_of(i * sb, sb), sb)
            return pltpu.make_async_copy(acc_ref.at[0:sb, :], y_ref.at[dst, :], sem.at[1])

        n_blocks = y_ref.shape[0] // sb
        lax.fori_loop(nu_ref[1], n_blocks, lambda i, c: (zero_copy(i).start(), c)[1], 0)
        lax.fori_loop(nu_ref[1], n_blocks, lambda i, c: (zero_copy(i).wait(), c)[1], 0)


def _expert_ffn(xs, w1, w3, w2, pass_expert, pass_start, pass_nb, n_used, p_rows):
    d = xs.shape[1]
    f = w1.shape[2]
    tf = _divisor_tile(f, MOE_F_TILE, LANES)
    nf = f // tf
    npass = pass_expert.shape[0]
    r = MOE_PASS_BLOCKS * MOE_ROWS

    def fidx(p, j, nu):
        return jnp.where(p < nu[0], j, nf - 1)

    grid_spec = pltpu.PrefetchScalarGridSpec(
        num_scalar_prefetch=4,
        grid=(npass, nf),
        in_specs=[
            pl.BlockSpec(memory_space=pl.ANY),
            pl.BlockSpec((None, d, tf), lambda p, j, pe, ps, pn, nu: (pe[p], 0, fidx(p, j, nu))),
            pl.BlockSpec((None, d, tf), lambda p, j, pe, ps, pn, nu: (pe[p], 0, fidx(p, j, nu))),
            pl.BlockSpec((None, tf, d), lambda p, j, pe, ps, pn, nu: (pe[p], fidx(p, j, nu), 0)),
        ],
        out_specs=pl.BlockSpec(memory_space=pl.ANY),
        scratch_shapes=[
            pltpu.VMEM((r, d), BF16), pltpu.VMEM((r, d), F32),
            pltpu.VMEM((d, tf), BF16), pltpu.VMEM((d, tf), BF16), pltpu.VMEM((tf, d), BF16),
            pltpu.SemaphoreType.DMA((2,)),
        ],
    )
    return pl.pallas_call(
        _expert_ffn_kernel,
        grid_spec=grid_spec,
        out_shape=jax.ShapeDtypeStruct((p_rows, d), F32),
        compiler_params=_cparams("arbitrary", "arbitrary"),
        name="expert_ffn",
    )(pass_expert, pass_start, pass_nb, n_used, xs, w1, w3, w2)


def _moe_ffn(h, hb, router_w, router_b, w1, w3, w2, g, b, alpha, tm):
    n, d = h.shape
    ne = router_w.shape[1]
    rows = MOE_ROWS
    info, counts = _router(h, router_w, router_b, tm)
    counts = counts[0].astype(jnp.int32)
    nblk_e = (counts + rows - 1) // rows
    blk_end = jnp.cumsum(nblk_e)
    starts = (blk_end - nblk_e) * rows
    n_blocks = (n * TOP_K + ne * (rows - 1)) // rows
    p = n_blocks * rows
    n_active = blk_end[-1:].astype(jnp.int32)
    npass_e = (nblk_e + MOE_PASS_BLOCKS - 1) // MOE_PASS_BLOCKS
    pass_end = jnp.cumsum(npass_e)
    max_pass = n_blocks // MOE_PASS_BLOCKS + ne
    pidx = jnp.arange(max_pass, dtype=jnp.int32)
    n_used = jnp.stack([pass_end[-1], blk_end[-1]]).astype(jnp.int32)
    last_expert = jnp.sum(pass_end < pass_end[-1]).astype(jnp.int32)
    pass_expert = jnp.minimum(jnp.sum(pidx[:, None] >= pass_end[None, :], axis=1), last_expert).astype(jnp.int32)
    local = pidx - (pass_end - npass_e)[pass_expert]
    pass_start = (starts[pass_expert] + local * (MOE_PASS_BLOCKS * rows)).astype(jnp.int32)
    pass_nb = jnp.where(pidx < n_used[0],
                        jnp.clip(nblk_e[pass_expert] - local * MOE_PASS_BLOCKS, 0, MOE_PASS_BLOCKS),
                        0).astype(jnp.int32)
    pass_start = jnp.where(pass_nb > 0, pass_start, 0).astype(jnp.int32)
    i1 = info[:, 0].astype(jnp.int32)
    i2 = info[:, 1].astype(jnp.int32)
    p1 = starts[i1] + info[:, 4].astype(jnp.int32)
    p2 = starts[i2] + info[:, 5].astype(jnp.int32)
    tok = jnp.arange(n, dtype=jnp.int32)
    p_in = p + MOE_PASS_BLOCKS * rows
    src = jnp.zeros((p_in,), jnp.int32).at[jnp.concatenate([p1, p2])].set(jnp.concatenate([tok, tok]))
    xs = _moe_gather(h, src, n_active, p_in, rows)
    y = _expert_ffn(xs, w1, w3, w2, pass_expert, pass_start, pass_nb, n_used, p)
    tmc = _divisor_tile(n, 384, 16)
    return _moe_combine(y, h, info, p1, p2, g, b, alpha, tmc)


def _proj_kernel(*refs, epilogue, n_extra, n_out):
    x_ref, w_ref = refs[0], refs[1]
    extra = refs[2:2 + n_extra]
    outs = refs[2 + n_extra:2 + n_extra + n_out]
    wb_ref = refs[2 + n_extra + n_out]

    @pl.when(pl.program_id(1) == 0)
    def _():
        wb_ref[...] = w_ref[...].astype(BF16)

    y = jnp.dot(x_ref[...], wb_ref[...], preferred_element_type=F32)
    res = epilogue(y, *[e[...] for e in extra])
    for o_ref, r in zip(outs, res):
        o_ref[...] = r.astype(o_ref.dtype)


def _proj(xb, w, col0, ncols, epilogue, extras, out_dtypes, tm, name, tn_cap=1024):
    n, kdim = xb.shape
    tn = _divisor_tile(ncols, tn_cap, LANES)
    assert col0 % tn == 0
    off = col0 // tn
    outs = pl.pallas_call(
        functools.partial(_proj_kernel, epilogue=epilogue, n_extra=len(extras), n_out=len(out_dtypes)),
        grid=(ncols // tn, n // tm),
        in_specs=[
            pl.BlockSpec((tm, kdim), lambda j, m: (m, 0)),
            pl.BlockSpec((kdim, tn), lambda j, m: (0, off + j)),
        ] + [pl.BlockSpec((1, tn), lambda j, m: (0, j))] * len(extras),
        out_specs=[pl.BlockSpec((tm, tn), lambda j, m: (m, j))] * len(out_dtypes),
        out_shape=[jax.ShapeDtypeStruct((n, ncols), dt) for dt in out_dtypes],
        scratch_shapes=[pltpu.VMEM((kdim, tn), BF16)],
        compiler_params=_cparams("arbitrary", "arbitrary"),
        name=name,
    )(xb, w, *[e.reshape(1, ncols) for e in extras])
    return outs


def _tril_bf16(c, inclusive):
    r_i = lax.broadcasted_iota(jnp.int32, (c, c), 0)
    c_i = lax.broadcasted_iota(jnp.int32, (c, c), 1)
    keep = (c_i <= r_i) if inclusive else (c_i < r_i)
    return jnp.where(keep, 1.0, 0.0).astype(BF16)


def _cumsum_rows(x, tril):
    hi = x.astype(BF16)
    r1 = x - hi.astype(F32)
    mid = r1.astype(BF16)
    lo = (r1 - mid.astype(F32)).astype(BF16)
    return (jnp.dot(tril, hi, preferred_element_type=F32)
            + (jnp.dot(tril, mid, preferred_element_type=F32)
               + jnp.dot(tril, lo, preferred_element_type=F32)))


def _dot_nt(a, b):
    return lax.dot_general(a.astype(BF16), b.astype(BF16), (((1,), (1,)), ((), ())),
                           preferred_element_type=F32)


def _dot_tn(a, b):
    return lax.dot_general(a.astype(BF16), b.astype(BF16), (((0,), (0,)), ((), ())),
                           preferred_element_type=F32)


def _dot_nn(a, b):
    return jnp.dot(a.astype(BF16), b.astype(BF16), preferred_element_type=F32)


def _chunk_len(t):
    return _divisor_tile(t, 64, 16)


def _round_robin(gens):
    done = [None] * len(gens)
    while any(d is None for d in done):
        for j, gen in enumerate(gens):
            if done[j] is None:
                done[j] = next(gen)
    return done


HGRN_SUB = 16


def _hgrn_scan_kernel(q_ref, lf_ref, v_ref, gs_ref, ng_ref, o_ref, st_ref, *, chunk):
    t, w = q_ref.shape
    nhead = w // HGRN_HEAD
    nsub = chunk // HGRN_SUB
    st_ref[...] = jnp.zeros_like(st_ref)
    tril = _tril_bf16(chunk, True)
    row16 = lax.broadcasted_iota(jnp.int32, (HGRN_SUB, 1), 0)

    def head_chunk(q, lf, v, st):
        k = 1.0 - jnp.exp(lf)
        cum = _cumsum_rows(lf, tril)
        yield None
        o_inter = _dot_nt(q * jnp.exp(cum), st)
        vb = v.astype(BF16)
        cl = cum[chunk - 1:chunk]
        kd = k * jnp.exp(cl - cum)
        st_new = st * jnp.exp(cl) + _dot_tn(v, kd)
        yield None
        outs = []
        for i in range(nsub):
            lo, hi = i * HGRN_SUB, (i + 1) * HGRN_SUB
            qi, ki, vi, cumi = q[lo:hi], k[lo:hi], v[lo:hi], cum[lo:hi]
            oi = o_inter[lo:hi]
            if i > 0:
                ci = cum[lo:lo + 1]
                qt = qi * jnp.exp(cumi - ci)
                kt = k[0:lo] * jnp.exp(ci - cum[0:lo])
                oi = oi + jnp.dot(_dot_nt(qt, kt).astype(BF16), vb[0:lo], preferred_element_type=F32)
            for s in range(HGRN_SUB):
                dec = jnp.exp(jnp.minimum(cumi - cumi[s:s + 1], 0.0))
                col = jnp.sum(qi * dec * ki[s:s + 1], axis=-1, keepdims=True)
                col = jnp.where(row16 >= s, col, 0.0)
                oi = oi + col * vi[s:s + 1]
                if s % 4 == 3:
                    yield None
            outs.append(oi)
        o = jnp.concatenate(outs, axis=0)
        o = o * lax.rsqrt(jnp.mean(o * o, axis=-1, keepdims=True) + RMS_EPS)
        yield o, st_new

    def body(c, carry):
        rows = pl.ds(pl.multiple_of(c * chunk, 16), chunk)
        q = q_ref[rows, :].astype(F32)
        lf = lf_ref[rows, :]
        v = v_ref[rows, :].astype(F32)
        heads = []
        for hh in range(nhead):
            cols = slice(hh * HGRN_HEAD, (hh + 1) * HGRN_HEAD)
            heads.append(head_chunk(q[:, cols], lf[:, cols], v[:, cols], st_ref[hh]))
        outs = []
        for hh, (o, st_new) in enumerate(_round_robin(heads)):
            st_ref[hh] = st_new
            outs.append(o)
        o = jnp.concatenate(outs, axis=1)
        o_ref[rows, :] = (o * ng_ref[...] * gs_ref[rows, :].astype(F32)).astype(BF16)
        return carry

    lax.fori_loop(0, t // chunk, body, 0)


HGRN_GROUP = 8


def _hgrn_scan(q, lf, v, gs, norm_g, batch):
    n, d = q.shape
    t = n // batch
    chunk = _chunk_len(t)
    w = min(d, HGRN_GROUP * HGRN_HEAD)
    blk = pl.BlockSpec((t, w), lambda b, j: (b, j))
    return pl.pallas_call(
        functools.partial(_hgrn_scan_kernel, chunk=chunk),
        grid=(batch, d // w),
        in_specs=[blk, blk, blk, blk, pl.BlockSpec((1, w), lambda b, j: (0, j))],
        out_specs=blk,
        out_shape=jax.ShapeDtypeStruct((n, d), BF16),
        scratch_shapes=[pltpu.VMEM((w // HGRN_HEAD, HGRN_HEAD, HGRN_HEAD), F32)],
        compiler_params=_cparams("parallel", "parallel"),
        name="hgrn_scan",
    )(q, lf, v, gs, norm_g.reshape(1, d))


def _hgrn_mixer_layer(h, hb, p, layer_idx, g, b, alpha, batch, tm):
    d = h.shape[1]
    w_in = p["hgrn_w_in"]
    lb = jnp.cumsum(jax.nn.softmax(p["hgrn_lb"].astype(F32), axis=0), axis=0)
    lb = lb[layer_idx] - lb[0]
    (q,) = _proj(hb, w_in, 0, d, lambda y: (_silu(y),), [], [BF16], tm, "hgrn_proj_q")
    (lf,) = _proj(hb, w_in, d, d, lambda y, lbv: (jnp.log(lbv + (1.0 - lbv) * _sigmoid(y)),),
                  [lb], [F32], tm, "hgrn_proj_f")
    (v,) = _proj(hb, w_in, 2 * d, d, lambda y: (y,), [], [BF16], tm, "hgrn_proj_i")
    (gs,) = _proj(hb, w_in, 3 * d, d, lambda y: (_silu(y),), [], [BF16], tm, "hgrn_proj_g")
    z = _hgrn_scan(q, lf, v, gs, p["hgrn_norm_g"], batch)
    return _mm_res_ln(z, p["hgrn_w_out"], h, g, b, alpha, tm)


def _fox_gate_kernel(h_ref, w_ref, bf_ref, c_ref, carry_ref):
    @pl.when(pl.program_id(1) == 0)
    def _():
        carry_ref[...] = jnp.zeros_like(carry_ref)

    x = _dot_f32(h_ref[...], w_ref[...]) + bf_ref[...]
    log_f = jnp.minimum(x, 0.0) - jnp.log(1.0 + jnp.exp(-jnp.abs(x)))
    tm = x.shape[0]
    c = _cumsum_rows(log_f, _tril_bf16(tm, True)) + carry_ref[...]
    c_ref[...] = c
    carry_ref[...] = c[tm - 1:tm, :]


def _fox_gate(h, w_f, b_f, batch, tm):
    n, d = h.shape
    nh = w_f.shape[1]
    tpb = n // batch // tm
    return pl.pallas_call(
        _fox_gate_kernel,
        grid=(batch, tpb),
        in_specs=[
            pl.BlockSpec((tm, d), lambda b, t: (b * tpb + t, 0)),
            pl.BlockSpec((d, nh), lambda b, t: (0, 0)),
            pl.BlockSpec((1, nh), lambda b, t: (0, 0)),
        ],
        out_specs=pl.BlockSpec((tm, nh), lambda b, t: (b * tpb + t, 0)),
        out_shape=jax.ShapeDtypeStruct((n, nh), F32),
        scratch_shapes=[pltpu.VMEM((1, nh), F32)],
        compiler_params=_cparams("arbitrary", "arbitrary"),
        name="fox_gate",
    )(h, w_f, b_f.reshape(1, nh))


def _fox_attn_kernel(q_ref, k_ref, v_ref, sg_ref, c_ref, ct_ref, o_ref, *, tq):
    hd = pl.program_id(1)
    t = q_ref.shape[0]
    nh = c_ref.shape[1]
    lane = lax.broadcasted_iota(jnp.int32, (t, nh), 1)
    c_col = jnp.sum(jnp.where(lane == hd, c_ref[...], 0.0), axis=-1, keepdims=True)
    c_row = ct_ref[pl.ds(hd, 1), :]
    for i in range(t // tq):
        lo, hi = i * tq, (i + 1) * tq
        s = lax.dot_general(q_ref[lo:hi, :], k_ref[0:hi, :], (((1,), (1,)), ((), ())),
                            preferred_element_type=F32)
        s = s + c_col[lo:hi] - c_row[:, 0:hi]
        r_i = lax.broadcasted_iota(jnp.int32, (tq, hi), 0) + lo
        c_i = lax.broadcasted_iota(jnp.int32, (tq, hi), 1)
        s = jnp.where(c_i <= r_i, s, -jnp.inf)
        m = jnp.max(s, axis=-1, keepdims=True)
        p = jnp.exp(s - m)
        l = jnp.sum(p, axis=-1, keepdims=True)
        o = jnp.dot(p.astype(BF16), v_ref[0:hi, :], preferred_element_type=F32) / l
        o_ref[lo:hi, :] = (o * sg_ref[lo:hi, :]).astype(BF16)


def _fox_attn(q, k, v, sg, c, ct, batch):
    n, d = q.shape
    t = n // batch
    nh = d // FOX_HEAD
    tq = _divisor_tile(t, 768, 16)
    blk = pl.BlockSpec((t, FOX_HEAD), lambda b, h: (b, h))
    return pl.pallas_call(
        functools.partial(_fox_attn_kernel, tq=tq),
        grid=(batch, nh),
        in_specs=[blk, blk, blk, blk,
                  pl.BlockSpec((t, nh), lambda b, h: (b, 0)),
                  pl.BlockSpec((None, nh, t), lambda b, h: (b, 0, 0))],
        out_specs=blk,
        out_shape=jax.ShapeDtypeStruct((n, d), BF16),
        compiler_params=_cparams("parallel", "parallel"),
        name="fox_attn",
    )(q, k, v, sg, c, ct)


def _head_rms_epilogue(scale):
    def epi(y, gain):
        outs = []
        for j in range(y.shape[1] // FOX_HEAD):
            yj = y[:, j * FOX_HEAD:(j + 1) * FOX_HEAD]
            yj = yj * lax.rsqrt(jnp.mean(yj * yj, axis=-1, keepdims=True) + RMS_EPS)
            outs.append(yj * gain[:, j * FOX_HEAD:(j + 1) * FOX_HEAD] * scale)
        return (jnp.concatenate(outs, axis=1),)
    return epi


def _fox_mixer_layer(h, hb, p, g, b, alpha, batch, tm):
    n, d = h.shape
    nh = d // FOX_HEAD
    w_in = p["fox_w_in"]
    qg = jnp.tile(p["fox_q_norm_g"], nh)
    kg = jnp.tile(p["fox_k_norm_g"], nh)
    (q,) = _proj(hb, w_in, 0, d, _head_rms_epilogue(FOX_HEAD ** -0.5), [qg], [BF16], tm, "fox_proj_q")
    (k,) = _proj(hb, w_in, d, d, _head_rms_epilogue(1.0), [kg], [BF16], tm, "fox_proj_k")
    (v,) = _proj(hb, w_in, 2 * d, d, lambda y: (y,), [], [BF16], tm, "fox_proj_v")
    (sg,) = _proj(hb, w_in, 3 * d, d, lambda y: (_sigmoid(y),), [], [F32], tm, "fox_proj_g")
    c = _fox_gate(h, w_in[:, 4 * d:], p["fox_b_f"], batch, tm)
    ct = c.reshape(batch, n // batch, nh).transpose(0, 2, 1)
    z = _fox_attn(q, k, v, sg, c, ct, batch)
    return _mm_res_ln(z, p["fox_w_out"], h, g, b, alpha, tm)


def _rwkv_mix_kernel(h_ref, mu_ref, *refs):
    outs, carry_ref = refs[:-1], refs[-1]

    @pl.when(pl.program_id(1) == 0)
    def _():
        carry_ref[...] = jnp.zeros_like(carry_ref)

    x = h_ref[...]
    tm = x.shape[0]
    row = lax.broadcasted_iota(jnp.int32, (tm, 1), 0)
    prev = jnp.where(row == 0, carry_ref[7:8, :], pltpu.roll(x, 1, axis=0))
    xx = prev - x
    carry_ref[...] = x[tm - 8:, :]
    for j, o_ref in enumerate(outs):
        o_ref[...] = (x + xx * mu_ref[j:j + 1, :]).astype(BF16)


def _rwkv_mix(h, mu, batch, tm):
    n, d = h.shape
    nmix = mu.shape[0]
    tpb = n // batch // tm
    blk = pl.BlockSpec((tm, d), lambda b, t: (b * tpb + t, 0))
    return pl.pallas_call(
        _rwkv_mix_kernel,
        grid=(batch, tpb),
        in_specs=[blk, pl.BlockSpec((nmix, d), lambda b, t: (0, 0))],
        out_specs=[blk] * nmix,
        out_shape=[jax.ShapeDtypeStruct((n, d), BF16)] * nmix,
        scratch_shapes=[pltpu.VMEM((8, d), F32)],
        compiler_params=_cparams("arbitrary", "arbitrary"),
        name="rwkv_mix",
    )(h, mu)


def _lora_kernel(x_ref, wa_ref, wb_ref, bias_ref, o_ref, wab_ref, wbb_ref, *, mid_act, out_act):
    @pl.when(pl.program_id(0) == 0)
    def _():
        wab_ref[...] = wa_ref[...].astype(BF16)
        wbb_ref[...] = wb_ref[...].astype(BF16)

    mid = mid_act(jnp.dot(x_ref[...], wab_ref[...], preferred_element_type=F32))
    y = jnp.dot(mid.astype(BF16), wbb_ref[...], preferred_element_type=F32)
    o_ref[...] = out_act(bias_ref[...] + y)


def _lora(xb, wa, wb, bias, mid_act, out_act, tm, name):
    n, d = xb.shape
    r = wa.shape[1]
    dout = wb.shape[1]
    return pl.pallas_call(
        functools.partial(_lora_kernel, mid_act=mid_act, out_act=out_act),
        grid=(n // tm,),
        in_specs=[
            pl.BlockSpec((tm, d), lambda i: (i, 0)),
            pl.BlockSpec((d, r), lambda i: (0, 0)),
            pl.BlockSpec((r, dout), lambda i: (0, 0)),
            pl.BlockSpec((1, dout), lambda i: (0, 0)),
        ],
        out_specs=pl.BlockSpec((tm, dout), lambda i: (i, 0)),
        out_shape=jax.ShapeDtypeStruct((n, dout), F32),
        scratch_shapes=[pltpu.VMEM((d, r), BF16), pltpu.VMEM((r, dout), BF16)],
        compiler_params=_cparams("arbitrary"),
        name=name,
    )(xb, wa, wb, bias.reshape(1, dout))


def _rwkv_log_decay(z):
    w_log = -(jnp.maximum(-z, 0.0) + jnp.log(1.0 + jnp.exp(-jnp.abs(z)))) - 0.5
    return -jnp.exp(w_log)


RWKV_GROUP = 4
RWKV_UNROLL = 8
RWKV_STAGGER = 0


def _seg_sum(x, bd):
    hi = x.astype(BF16)
    lo = (x - hi.astype(F32)).astype(BF16)
    return jnp.dot(hi, bd, preferred_element_type=F32) + jnp.dot(lo, bd, preferred_element_type=F32)


def _rwkv_scan_kernel(r_ref, kr_ref, v_ref, lw_ref, a_ref, g_ref, kk_p, ka_p, rk_p, gg_p, gb_p,
                      o_ref, kk_s, k_s, bonus_s, y_s, st_ref, *, chunk, ptile):
    t, w = r_ref.shape
    nhead = w // RWKV_HEAD
    sc = nhead * chunk
    lane_r = lax.broadcasted_iota(jnp.int32, (w, w), 0) // RWKV_HEAD
    lane_c = lax.broadcasted_iota(jnp.int32, (w, w), 1) // RWKV_HEAD
    bd = jnp.where(lane_r == lane_c, 1.0, 0.0).astype(BF16)

    def prologue(i, carry):
        rows = pl.ds(pl.multiple_of(i * ptile, 8), ptile)
        kr = kr_ref[rows, :]
        a = a_ref[rows, :]
        kkr = kr * kk_p[...]
        nrm = jnp.maximum(jnp.sqrt(_seg_sum(kkr * kkr, bd)), 1e-12)
        kk_s[rows, :] = kkr / nrm
        k = kr * (1.0 + (a - 1.0) * ka_p[...])
        k_s[rows, :] = k
        bonus_s[rows, :] = _seg_sum(r_ref[rows, :] * k * rk_p[...], bd) * v_ref[rows, :]
        return carry

    lax.fori_loop(0, t // ptile, prologue, 0)

    st_ref[...] = jnp.zeros_like(st_ref)
    tril = _tril_bf16(chunk, True)
    head_of_lane = lax.broadcasted_iota(jnp.int32, (chunk, w), 1) // RWKV_HEAD
    ri = lax.broadcasted_iota(jnp.int32, (2 * sc, sc), 0)
    ci = lax.broadcasted_iota(jnp.int32, (2 * sc, sc), 1)
    low_mask = ci < jnp.where(ri < sc, ri, ri - sc + 1)
    nsteps = max(1, (chunk - 1).bit_length())

    def stack(x):
        return jnp.concatenate([jnp.where(head_of_lane == hh, x, 0.0) for hh in range(nhead)], axis=0)

    eye = jnp.where(lax.broadcasted_iota(jnp.int32, (sc, sc), 0)
                    == lax.broadcasted_iota(jnp.int32, (sc, sc), 1), 1.0, 0.0)

    def prepare(c):
        start = c * chunk
        rows = pl.ds(start if isinstance(start, int) else pl.multiple_of(start, 16), chunk)
        r = r_ref[rows, :]
        v = v_ref[rows, :]
        lw = lw_ref[rows, :]
        a = a_ref[rows, :]
        kk = kk_s[rows, :]
        k = k_s[rows, :]
        cum = _cumsum_rows(lw, tril)
        e_neg = jnp.exp(-cum)
        at2 = stack(-kk * jnp.exp(cum - lw))
        rt2 = stack(r * jnp.exp(cum))
        bvec = kk * a
        bb2 = stack(bvec * e_neg)
        kb2 = stack(k * e_neg)
        v2 = stack(v)
        ar2 = jnp.concatenate([at2, rt2], axis=0).astype(BF16)
        cl = cum[chunk - 1:chunk]
        e_end = jnp.exp(cl - cum)
        khbh = jnp.concatenate([stack(k * e_end), stack(bvec * e_end)], axis=0).astype(BF16)
        yield None
        pb = jnp.where(low_mask, _dot_nt(ar2, bb2), 0.0)
        pk = jnp.where(low_mask, _dot_nt(ar2, kb2), 0.0)
        m_ab, m_rb = pb[:sc], pb[sc:]
        m_ak, m_rk = pk[:sc], pk[sc:]
        yield None
        u0 = _dot_nn(m_ak, v2)
        y0 = _dot_nn(m_rk, v2)
        tinv = eye + m_ab
        lpow = m_ab
        for _ in range(nsteps - 1):
            yield None
            lpow = _dot_nn(lpow, lpow)
            tinv = tinv + _dot_nn(tinv, lpow)
        return dict(rows=rows, ar2=ar2, tinv=tinv.astype(BF16), m_rb=m_rb.astype(BF16),
                    u0=u0, y0=y0, v2=v2.astype(BF16), khbh=khbh, decay=jnp.exp(cl))

    def chunk_steps(c, j, run):
        for _ in range(j * RWKV_STAGGER):
            yield None
        pc = yield from prepare(c)
        while run["turn"] != j:
            yield None
        st = run["st"]
        ps = _dot_nt(pc["ar2"], st)
        yield None
        u2 = jnp.dot(pc["tinv"], (ps[:sc] + pc["u0"]).astype(BF16), preferred_element_type=F32)
        u2b = u2.astype(BF16)
        yield None
        y2 = ps[sc:] + pc["y0"] + jnp.dot(pc["m_rb"], u2b, preferred_element_type=F32)
        y = y2[0:chunk]
        for hh in range(1, nhead):
            y = y + y2[hh * chunk:(hh + 1) * chunk]
        y_s[pc["rows"], :] = y
        run["st"] = st * pc["decay"] + _dot_tn(jnp.concatenate([pc["v2"], u2b], axis=0), pc["khbh"])
        run["turn"] = j + 1
        yield True

    def run_group(chunk_ids):
        run = dict(st=st_ref[...], turn=0)
        _round_robin([chunk_steps(c, j, run) for j, c in enumerate(chunk_ids)])
        st_ref[...] = run["st"]

    nchunks = t // chunk

    def body(i, carry):
        run_group([i * RWKV_UNROLL + j for j in range(RWKV_UNROLL)])
        return carry

    lax.fori_loop(0, nchunks // RWKV_UNROLL, body, 0)
    tail = list(range(nchunks - nchunks % RWKV_UNROLL, nchunks))
    if tail:
        run_group(tail)

    inv = 1.0 / RWKV_HEAD

    def epilogue(i, carry):
        rows = pl.ds(pl.multiple_of(i * ptile, 8), ptile)
        y = y_s[rows, :]
        mu = _seg_sum(y, bd) * inv
        yc = y - mu
        var = _seg_sum(yc * yc, bd) * inv
        yn = yc * lax.rsqrt(var + 1e-5 * RWKV_HEAD) * gg_p[...] + gb_p[...]
        o_ref[rows, :] = ((yn + bonus_s[rows, :]) * g_ref[rows, :]).astype(BF16)
        return carry

    lax.fori_loop(0, t // ptile, epilogue, 0)


def _rwkv_scan(r, kr, v, lw, a, g, p, batch):
    n, d = r.shape
    t = n // batch
    chunk = _chunk_len(t)
    w = min(d, RWKV_GROUP * RWKV_HEAD)
    ptile = _divisor_tile(t, 768, 16)
    blk = pl.BlockSpec((t, w), lambda b, j: (b, j))
    prm = pl.BlockSpec((1, w), lambda b, j: (0, j))
    params = [p["rwkv_k_k"], p["rwkv_k_a"], p["rwkv_r_k"], p["rwkv_gn_g"], p["rwkv_gn_b"]]
    return pl.pallas_call(
        functools.partial(_rwkv_scan_kernel, chunk=chunk, ptile=ptile),
        grid=(batch, d // w),
        in_specs=[blk] * 6 + [prm] * 5,
        out_specs=blk,
        out_shape=jax.ShapeDtypeStruct((n, d), BF16),
        scratch_shapes=[pltpu.VMEM((t, w), F32)] * 4 + [pltpu.VMEM((w, w), F32)],
        compiler_params=_cparams("parallel", "parallel"),
        name="rwkv_scan",
    )(r, kr, v, lw, a, g, *[x.reshape(1, d) for x in params])


def _rwkv_mixer_layer(h, hb, p, g, b, alpha, batch, tm):
    n, d = h.shape
    ident = lambda y: y
    xr, xw, xk, xv, xa, xg = _rwkv_mix(h, p["rwkv_mu"], batch, tm)
    (r,) = _proj(xr, p["rwkv_w_r"], 0, d, lambda y: (y,), [], [F32], tm, "rwkv_proj_r")
    (kr,) = _proj(xk, p["rwkv_w_k"], 0, d, lambda y: (y,), [], [F32], tm, "rwkv_proj_k")
    (v,) = _proj(xv, p["rwkv_w_v"], 0, d, lambda y: (y,), [], [F32], tm, "rwkv_proj_v")
    lw = _lora(xw, p["rwkv_w1"], p["rwkv_w2"], p["rwkv_w0"], jnp.tanh, _rwkv_log_decay, tm, "rwkv_lora_w")
    a = _lora(xa, p["rwkv_a1"], p["rwkv_a2"], p["rwkv_a0"], ident, _sigmoid, tm, "rwkv_lora_a")
    gate = _lora(xg, p["rwkv_g1"], p["rwkv_g2"], jnp.zeros((d,), F32), _sigmoid, ident, tm, "rwkv_lora_g")
    z = _rwkv_scan(r, kr, v, lw, a, gate, p, batch)
    return _mm_res_ln(z, p["rwkv_w_out"], h, g, b, alpha, tm)


def _embed_kernel(x_ref, meta_ref, h_ref, hb_ref, sem):
    b = pl.program_id(0)
    j = pl.program_id(1)
    tm = h_ref.shape[0]
    nmeta = meta_ref.shape[0]

    @pl.when(j == 0)
    def _():
        h_ref[0:nmeta, :] = meta_ref[...]
        cp = pltpu.make_async_copy(x_ref.at[b, pl.ds(0, tm - nmeta), :], h_ref.at[pl.ds(nmeta, tm - nmeta), :], sem)
        cp.start()
        cp.wait()

    @pl.when(j > 0)
    def _():
        first = pl.multiple_of(j * tm - nmeta, 8)
        cp = pltpu.make_async_copy(x_ref.at[b, pl.ds(first, tm), :], h_ref, sem)
        cp.start()
        cp.wait()

    hb_ref[...] = h_ref[...].astype(BF16)


def _embed(x, meta, tm):
    batch, seq, d = x.shape
    nmeta = meta.shape[0]
    t = nmeta + seq
    tpb = t // tm
    assert nmeta % 8 == 0 and tm % 8 == 0
    blk = pl.BlockSpec((tm, d), lambda b, j: (b * tpb + j, 0))
    return pl.pallas_call(
        _embed_kernel,
        grid=(batch, tpb),
        in_specs=[pl.BlockSpec(memory_space=pl.ANY), pl.BlockSpec((nmeta, d), lambda b, j: (0, 0))],
        out_specs=[blk, blk],
        out_shape=[jax.ShapeDtypeStruct((batch * t, d), F32), jax.ShapeDtypeStruct((batch * t, d), BF16)],
        scratch_shapes=[pltpu.SemaphoreType.DMA],
        compiler_params=_cparams("parallel", "arbitrary"),
        name="embed",
    )(x, meta.astype(x.dtype))


def kernel(x, meta, ln_mix_g, ln_mix_b, ln_ffn_g, ln_ffn_b, conv_w_in, conv_w, conv_b, conv_w_out, rwkv_mu, rwkv_w_r, rwkv_w_k, rwkv_w_v, rwkv_w0, rwkv_w1, rwkv_w2, rwkv_a0, rwkv_a1, rwkv_a2, rwkv_g1, rwkv_g2, rwkv_k_k, rwkv_k_a, rwkv_r_k, rwkv_gn_g, rwkv_gn_b, rwkv_w_out, hgrn_w_in, hgrn_lb, hgrn_norm_g, hgrn_w_out, fox_w_in, fox_b_f, fox_q_norm_g, fox_k_norm_g, fox_w_out, ffn0_w1, ffn0_w3, ffn0_w2, moe1_router, moe1_router_b, moe1_w1, moe1_w3, moe1_w2, ffn2_w1, ffn2_w3, ffn2_w2, moe3_router, moe3_router_b, moe3_w1, moe3_w3, moe3_w2):
    batch, seq, d = x.shape
    depth = ln_mix_g.shape[0]
    assert depth == 4
    alpha = (2.0 * depth) ** 0.25
    t = N_META + seq
    n = batch * t
    tm = _divisor_tile(t, 768, 16)
    p = dict(
        conv_w_in=conv_w_in, conv_w=conv_w, conv_b=conv_b, conv_w_out=conv_w_out,
        rwkv_mu=rwkv_mu, rwkv_w_r=rwkv_w_r, rwkv_w_k=rwkv_w_k, rwkv_w_v=rwkv_w_v, rwkv_w0=rwkv_w0,
        rwkv_w1=rwkv_w1, rwkv_w2=rwkv_w2, rwkv_a0=rwkv_a0, rwkv_a1=rwkv_a1, rwkv_a2=rwkv_a2,
        rwkv_g1=rwkv_g1, rwkv_g2=rwkv_g2, rwkv_k_k=rwkv_k_k, rwkv_k_a=rwkv_k_a, rwkv_r_k=rwkv_r_k,
        rwkv_gn_g=rwkv_gn_g, rwkv_gn_b=rwkv_gn_b, rwkv_w_out=rwkv_w_out,
        hgrn_w_in=hgrn_w_in, hgrn_lb=hgrn_lb, hgrn_norm_g=hgrn_norm_g, hgrn_w_out=hgrn_w_out,
        fox_w_in=fox_w_in, fox_b_f=fox_b_f, fox_q_norm_g=fox_q_norm_g, fox_k_norm_g=fox_k_norm_g,
        fox_w_out=fox_w_out,
    )
    assert meta.shape[0] == N_META
    h, hb = _embed(x, meta, tm)

    h, hb = _conv_mixer_layer(h, hb, p, ln_mix_g[0], ln_mix_b[0], alpha, batch, tm)
    h, hb = _dense_ffn(hb, h, ffn0_w1, ffn0_w3, ffn0_w2, ln_ffn_g[0], ln_ffn_b[0], alpha, tm)
    h, hb = _rwkv_mixer_layer(h, hb, p, ln_mix_g[1], ln_mix_b[1], alpha, batch, tm)
    h, hb = _moe_ffn(h, hb, moe1_router, moe1_router_b, moe1_w1, moe1_w3, moe1_w2,
                     ln_ffn_g[1], ln_ffn_b[1], alpha, tm)
    h, hb = _hgrn_mixer_layer(h, hb, p, 2, ln_mix_g[2], ln_mix_b[2], alpha, batch, tm)
    h, hb = _dense_ffn(hb, h, ffn2_w1, ffn2_w3, ffn2_w2, ln_ffn_g[2], ln_ffn_b[2], alpha, tm)
    h, hb = _fox_mixer_layer(h, hb, p, ln_mix_g[3], ln_mix_b[3], alpha, batch, tm)
    h, hb = _moe_ffn(h, hb, moe3_router, moe3_router_b, moe3_w1, moe3_w3, moe3_w2,
                     ln_ffn_g[3], ln_ffn_b[3], alpha, tm)
    return h.reshape(batch, t, d)[:, N_META:]
```

```python
import functools

import jax
import jax.numpy as jnp
from jax import lax
from jax.experimental import pallas as pl
from jax.experimental.pallas import tpu as pltpu

F32 = jnp.float32
BF16 = jnp.bfloat16

N_META = 16
LN_EPS = 1e-5
RMS_EPS = 1e-6
RWKV_HEAD = 64
HGRN_HEAD = 128
FOX_HEAD = 128
TOP_K = 2
LANES = 128
VMEM_LIMIT_BYTES = 56 * 2**20


def _cparams(*sem):
    return pltpu.CompilerParams(dimension_semantics=sem, vmem_limit_bytes=VMEM_LIMIT_BYTES)


def _divisor_tile(n, cap, mult):
    best = None
    for d in range(mult, min(n, cap) + 1, mult):
        if n % d == 0:
            best = d
    assert best is not None, (n, cap, mult)
    return best


def _layer_norm(y, g, b):
    mu = jnp.mean(y, axis=-1, keepdims=True)
    yc = y - mu
    var = jnp.mean(yc * yc, axis=-1, keepdims=True)
    return yc * lax.rsqrt(var + LN_EPS) * g + b


def _sigmoid(x):
    return 1.0 / (1.0 + jnp.exp(-x))


def _silu(x):
    return x * _sigmoid(x)


def _mm_res_ln_kernel(z_ref, w_ref, h_ref, g_ref, b_ref, o_ref, ob_ref, *, nk, alpha):
    k = pl.program_id(1)
    part = jnp.dot(z_ref[...], w_ref[...], preferred_element_type=F32)

    @pl.when(k == 0)
    def _():
        o_ref[...] = part

    @pl.when(k > 0)
    def _():
        o_ref[...] += part

    @pl.when(k == nk - 1)
    def _():
        y = _layer_norm(alpha * h_ref[...] + o_ref[...], g_ref[...], b_ref[...])
        o_ref[...] = y
        ob_ref[...] = y.astype(BF16)


WEIGHT_SLICE_ROWS = 256


def _load_weight_bf16(w_hbm, wb_ref, stage_ref, sem):
    rows = stage_ref.shape[1]
    nslice = w_hbm.shape[0] // rows

    def copy(s):
        return pltpu.make_async_copy(w_hbm.at[pl.ds(s * rows, rows), :], stage_ref.at[s % 2], sem.at[s % 2])

    copy(0).start()
    for s in range(nslice):
        if s + 1 < nslice:
            copy(s + 1).start()
        copy(s).wait()
        wb_ref[s * rows:(s + 1) * rows, :] = stage_ref[s % 2].astype(BF16)


def _mm_res_ln_resident_kernel(z_ref, w_ref, h_ref, g_ref, b_ref, o_ref, ob_ref, wb_ref, stage_ref, sem,
                               *, alpha):
    @pl.when(pl.program_id(0) == 0)
    def _():
        _load_weight_bf16(w_ref, wb_ref, stage_ref, sem)

    tm = z_ref.shape[0]
    split = min(tm, -(-tm // 32) * 16)
    spans = [(0, split), (split, tm)] if split < tm else [(0, tm)]
    dots = [jnp.dot(z_ref[lo:hi, :], wb_ref[...], preferred_element_type=F32) for lo, hi in spans]
    for (lo, hi), part in zip(spans, dots):
        y = _layer_norm(alpha * h_ref[lo:hi, :] + part, g_ref[...], b_ref[...])
        o_ref[lo:hi, :] = y
        ob_ref[lo:hi, :] = y.astype(BF16)


RESIDENT_WEIGHT_BYTES = 16 * 2**20


def _mm_res_ln_resident(z, w, h, g, b, alpha, tm):
    n, kdim = z.shape
    d = w.shape[1]
    ws = _divisor_tile(kdim, WEIGHT_SLICE_ROWS, 8)
    const = lambda m: (0, 0)
    return pl.pallas_call(
        functools.partial(_mm_res_ln_resident_kernel, alpha=alpha),
        grid=(n // tm,),
        in_specs=[
            pl.BlockSpec((tm, kdim), lambda m: (m, 0)),
            pl.BlockSpec(memory_space=pl.ANY),
            pl.BlockSpec((tm, d), lambda m: (m, 0)),
            pl.BlockSpec((1, d), const),
            pl.BlockSpec((1, d), const),
        ],
        out_specs=[
            pl.BlockSpec((tm, d), lambda m: (m, 0)),
            pl.BlockSpec((tm, d), lambda m: (m, 0)),
        ],
        out_shape=[jax.ShapeDtypeStruct((n, d), F32), jax.ShapeDtypeStruct((n, d), BF16)],
        scratch_shapes=[pltpu.VMEM((kdim, d), BF16), pltpu.VMEM((2, ws, d), F32),
                        pltpu.SemaphoreType.DMA((2,))],
        compiler_params=_cparams("arbitrary"),
        name="mm_res_ln_resident",
    )(z, w, h, g.reshape(1, d), b.reshape(1, d))


def _mm_res_ln(z, w, h, g, b, alpha, tm):
    n, kdim = z.shape
    d = w.shape[1]
    if kdim * d * 4 <= RESIDENT_WEIGHT_BYTES:
        return _mm_res_ln_resident(z, w, h, g, b, alpha, tm)
    w = w.astype(BF16)
    tk = _divisor_tile(kdim, 1408, LANES)
    nk = kdim // tk
    return pl.pallas_call(
        functools.partial(_mm_res_ln_kernel, nk=nk, alpha=alpha),
        grid=(n // tm, nk),
        in_specs=[
            pl.BlockSpec((tm, tk), lambda m, k: (m, k)),
            pl.BlockSpec((tk, d), lambda m, k: (k, 0)),
            pl.BlockSpec((tm, d), lambda m, k: (m, 0)),
            pl.BlockSpec((1, d), lambda m, k: (0, 0)),
            pl.BlockSpec((1, d), lambda m, k: (0, 0)),
        ],
        out_specs=[
            pl.BlockSpec((tm, d), lambda m, k: (m, 0)),
            pl.BlockSpec((tm, d), lambda m, k: (m, 0)),
        ],
        out_shape=[jax.ShapeDtypeStruct((n, d), F32), jax.ShapeDtypeStruct((n, d), BF16)],
        compiler_params=_cparams("parallel", "arbitrary"),
        name="mm_res_ln",
    )(z, w, h, g.reshape(1, d), b.reshape(1, d))


def _ffn_up_kernel(exp_ref, nact_ref, x_ref, w1_ref, w3_ref, o_ref, w1b_ref, w3b_ref):
    c = pl.program_id(1)
    prev = exp_ref[jnp.maximum(c - 1, 0)]
    new_weights = jnp.logical_or(c == 0, exp_ref[c] != prev)

    @pl.when(new_weights)
    def _():
        w1b_ref[...] = w1_ref[...].astype(BF16)
        w3b_ref[...] = w3_ref[...].astype(BF16)

    @pl.when(c < nact_ref[0])
    def _():
        x = x_ref[...]
        a = jnp.dot(x, w1b_ref[...], preferred_element_type=F32)
        bb = jnp.dot(x, w3b_ref[...], preferred_element_type=F32)
        o_ref[...] = (_silu(a) * bb).astype(BF16)

    @pl.when(c >= nact_ref[0])
    def _():
        o_ref[...] = jnp.zeros_like(o_ref)


def _ffn_up(x, w1, w3, chunk_expert, n_active, rows):
    p, d = x.shape
    f = w1.shape[2]
    tf = _divisor_tile(f, 512, LANES)
    grid_spec = pltpu.PrefetchScalarGridSpec(
        num_scalar_prefetch=2,
        grid=(f // tf, p // rows),
        in_specs=[
            pl.BlockSpec((rows, d), lambda j, c, e, na: (c, 0)),
            pl.BlockSpec((None, d, tf), lambda j, c, e, na: (e[c], 0, j)),
            pl.BlockSpec((None, d, tf), lambda j, c, e, na: (e[c], 0, j)),
        ],
        out_specs=pl.BlockSpec((rows, tf), lambda j, c, e, na: (c, j)),
        scratch_shapes=[pltpu.VMEM((d, tf), BF16), pltpu.VMEM((d, tf), BF16)],
    )
    return pl.pallas_call(
        _ffn_up_kernel,
        grid_spec=grid_spec,
        out_shape=jax.ShapeDtypeStruct((p, f), BF16),
        compiler_params=_cparams("arbitrary", "arbitrary"),
        name="ffn_up",
    )(chunk_expert, n_active, x, w1, w3)


def _dense_ffn(hb, h, w1, w3, w2, g, b, alpha, tm):
    n = hb.shape[0]
    nchunks = n // tm
    hmid = _ffn_up(hb, w1[None], w3[None], jnp.zeros((nchunks,), jnp.int32),
                   jnp.full((1,), nchunks, jnp.int32), tm)
    return _mm_res_ln(hmid, w2, h, g, b, alpha, tm)


def _conv_proj_kernel(x_ref, wb_ref, wc_ref, wh_ref, cw_ref, cb_ref, o_ref,
                      wbb_ref, wcb_ref, whb_ref, carry_ref):
    bi = pl.program_id(1)
    ti = pl.program_id(2)

    @pl.when(jnp.logical_and(bi == 0, ti == 0))
    def _():
        wbb_ref[...] = wb_ref[...].astype(BF16)
        wcb_ref[...] = wc_ref[...].astype(BF16)
        whb_ref[...] = wh_ref[...].astype(BF16)

    @pl.when(ti == 0)
    def _():
        carry_ref[...] = jnp.zeros_like(carry_ref)

    x = x_ref[...]
    gate_b = jnp.dot(x, wbb_ref[...], preferred_element_type=F32)
    gate_c = jnp.dot(x, wcb_ref[...], preferred_element_type=F32)
    hh = jnp.dot(x, whb_ref[...], preferred_element_type=F32)
    u = gate_c * hh
    tm = u.shape[0]
    prev1 = carry_ref[7:8, :]
    prev2 = carry_ref[6:7, :]
    row = lax.broadcasted_iota(jnp.int32, (tm, 1), 0)
    r1 = jnp.where(row == 0, prev1, pltpu.roll(u, 1, axis=0))
    r2 = jnp.where(row == 0, prev2, jnp.where(row == 1, prev1, pltpu.roll(u, 2, axis=0)))
    v = cw_ref[0:1, :] * r2 + cw_ref[1:2, :] * r1 + cw_ref[2:3, :] * u + cb_ref[...]
    carry_ref[...] = u[tm - 8:, :]
    o_ref[...] = (gate_b * v).astype(BF16)


def _conv_proj(hb, w_in, conv_w, conv_b, batch, tm):
    n, d = hb.shape
    tn = _divisor_tile(d, 512, LANES)
    nd = d // tn
    tpb = n // batch // tm
    return pl.pallas_call(
        _conv_proj_kernel,
        grid=(nd, batch, tpb),
        in_specs=[
            pl.BlockSpec((tm, d), lambda j, bi, ti: (bi * tpb + ti, 0)),
            pl.BlockSpec((d, tn), lambda j, bi, ti: (0, j)),
            pl.BlockSpec((d, tn), lambda j, bi, ti: (0, nd + j)),
            pl.BlockSpec((d, tn), lambda j, bi, ti: (0, 2 * nd + j)),
            pl.BlockSpec((3, tn), lambda j, bi, ti: (0, j)),
            pl.BlockSpec((1, tn), lambda j, bi, ti: (0, j)),
        ],
        out_specs=pl.BlockSpec((tm, tn), lambda j, bi, ti: (bi * tpb + ti, j)),
        out_shape=jax.ShapeDtypeStruct((n, d), BF16),
        scratch_shapes=[pltpu.VMEM((d, tn), BF16)] * 3 + [pltpu.VMEM((8, tn), F32)],
        compiler_params=_cparams("arbitrary", "arbitrary", "arbitrary"),
        name="conv_proj",
    )(hb, w_in, w_in, w_in, conv_w, conv_b.reshape(1, d))


def _conv_mixer_layer(h, hb, p, g, b, alpha, batch, tm):
    z = _conv_proj(hb, p["conv_w_in"], p["conv_w"], p["conv_b"], batch, tm)
    return _mm_res_ln(z, p["conv_w_out"], h, g, b, alpha, tm)


MOE_ROWS = 128


def _split_bf16(a):
    hi = a.astype(BF16)
    lo = (a - hi.astype(F32)).astype(BF16)
    return hi, lo


def _dot_f32(a, b):
    ah, al = _split_bf16(a)
    bh, bl = _split_bf16(b)
    return (jnp.dot(ah, bh, preferred_element_type=F32)
            + (jnp.dot(ah, bl, preferred_element_type=F32)
               + jnp.dot(al, bh, preferred_element_type=F32)))


def _router_kernel(h_ref, w_ref, b_ref, info_ref, cnt_ref, carry_ref):
    i = pl.program_id(0)

    @pl.when(i == 0)
    def _():
        carry_ref[...] = jnp.zeros_like(carry_ref)

    logits = _dot_f32(h_ref[...], w_ref[...]) + b_ref[...]
    tm, ne = logits.shape
    lane = lax.broadcasted_iota(jnp.int32, (tm, ne), 1)
    m1 = jnp.max(logits, axis=-1, keepdims=True)
    i1 = jnp.min(jnp.where(logits == m1, lane, ne), axis=-1, keepdims=True)
    mask1 = lane == i1
    rest = jnp.where(mask1, -jnp.inf, logits)
    m2 = jnp.max(rest, axis=-1, keepdims=True)
    i2 = jnp.min(jnp.where(rest == m2, lane, ne), axis=-1, keepdims=True)
    mask2 = lane == i2
    dd = jnp.exp(m2 - m1)
    g1 = 1.0 / (1.0 + dd)
    g2 = dd / (1.0 + dd)
    sel = jnp.where(jnp.logical_or(mask1, mask2), 1.0, 0.0)
    r_i = lax.broadcasted_iota(jnp.int32, (tm, tm), 0)
    c_i = lax.broadcasted_iota(jnp.int32, (tm, tm), 1)
    tril = jnp.where(c_i < r_i, 1.0, 0.0).astype(BF16)
    rank = jnp.dot(tril, sel.astype(BF16), preferred_element_type=F32) + carry_ref[...]
    r1 = jnp.sum(jnp.where(mask1, rank, 0.0), axis=-1, keepdims=True)
    r2 = jnp.sum(jnp.where(mask2, rank, 0.0), axis=-1, keepdims=True)
    info = jnp.where(lane == 0, i1.astype(F32),
           jnp.where(lane == 1, i2.astype(F32),
           jnp.where(lane == 2, g1,
           jnp.where(lane == 3, g2,
           jnp.where(lane == 4, r1,
           jnp.where(lane == 5, r2, 0.0))))))
    info_ref[...] = info
    total = carry_ref[...] + jnp.sum(sel, axis=0, keepdims=True)
    carry_ref[...] = total
    cnt_ref[...] = total


def _router(h, w, b, tm):
    n, d = h.shape
    ne = w.shape[1]
    assert ne >= 6
    return pl.pallas_call(
        _router_kernel,
        grid=(n // tm,),
        in_specs=[
            pl.BlockSpec((tm, d), lambda i: (i, 0)),
            pl.BlockSpec((d, ne), lambda i: (0, 0)),
            pl.BlockSpec((1, ne), lambda i: (0, 0)),
        ],
        out_specs=[
            pl.BlockSpec((tm, ne), lambda i: (i, 0)),
            pl.BlockSpec((1, ne), lambda i: (0, 0)),
        ],
        out_shape=[jax.ShapeDtypeStruct((n, ne), F32), jax.ShapeDtypeStruct((1, ne), F32)],
        scratch_shapes=[pltpu.VMEM((1, ne), F32)],
        compiler_params=_cparams("arbitrary"),
        name="moe_router",
    )(h, w, b.reshape(1, ne))


def _row_copy(src_hbm, row, dst_vmem, r, sem):
    return pltpu.make_async_copy(src_hbm.at[pl.ds(row, 1), :], dst_vmem.at[pl.ds(r, 1), :], sem)


def _rows_wait(src_hbm, dst_vmem, sem):
    pltpu.make_async_copy(src_hbm.at[pl.ds(0, dst_vmem.shape[0]), :], dst_vmem, sem).wait()


def _moe_gather_kernel(src_ref, nact_ref, h_ref, o_ref, buf_ref, sem):
    c = pl.program_id(0)
    rows = buf_ref.shape[1]

    def issue(chunk):
        slot = chunk % 2

        def start(r, carry):
            _row_copy(h_ref, src_ref[chunk * rows + r], buf_ref.at[slot], r, sem.at[slot]).start()
            return carry

        lax.fori_loop(0, rows, start, 0, unroll=8)

    @pl.when(jnp.logical_and(c == 0, nact_ref[0] > 0))
    def _():
        issue(c)

    @pl.when(c + 1 < nact_ref[0])
    def _():
        issue(c + 1)

    @pl.when(c < nact_ref[0])
    def _():
        slot = c % 2
        _rows_wait(h_ref, buf_ref.at[slot], sem.at[slot])
        o_ref[...] = buf_ref[slot].astype(BF16)

    @pl.when(c >= nact_ref[0])
    def _():
        o_ref[...] = jnp.zeros_like(o_ref)


def _moe_gather(h, src, n_active, p, rows):
    n, d = h.shape
    grid_spec = pltpu.PrefetchScalarGridSpec(
        num_scalar_prefetch=2,
        grid=(p // rows,),
        in_specs=[pl.BlockSpec(memory_space=pl.ANY)],
        out_specs=pl.BlockSpec((rows, d), lambda c, s, na: (c, 0)),
        scratch_shapes=[pltpu.VMEM((2, rows, d), F32), pltpu.SemaphoreType.DMA((2,))],
    )
    return pl.pallas_call(
        _moe_gather_kernel,
        grid_spec=grid_spec,
        out_shape=jax.ShapeDtypeStruct((p, d), BF16),
        compiler_params=_cparams("arbitrary"),
        name="moe_gather",
    )(src, n_active, h)


def _moe_combine_kernel(p1_ref, p2_ref, y_ref, h_ref, info_ref, g_ref, b_ref, o_ref, ob_ref,
                        buf1_ref, buf2_ref, sem, *, alpha):
    i = pl.program_id(0)
    tm = buf1_ref.shape[1]

    def issue(tile):
        slot = tile % 2

        def start(r, carry):
            _row_copy(y_ref, p1_ref[tile * tm + r], buf1_ref.at[slot], r, sem.at[slot]).start()
            _row_copy(y_ref, p2_ref[tile * tm + r], buf2_ref.at[slot], r, sem.at[slot]).start()
            return carry

        lax.fori_loop(0, tm, start, 0, unroll=8)

    @pl.when(i == 0)
    def _():
        issue(i)

    @pl.when(i + 1 < pl.num_programs(0))
    def _():
        issue(i + 1)

    slot = i % 2
    _rows_wait(y_ref, buf1_ref.at[slot], sem.at[slot])
    _rows_wait(y_ref, buf2_ref.at[slot], sem.at[slot])
    info = info_ref[...]
    y = alpha * h_ref[...] + (info[:, 2:3] * buf1_ref[slot] + info[:, 3:4] * buf2_ref[slot])
    y = _layer_norm(y, g_ref[...], b_ref[...])
    o_ref[...] = y
    ob_ref[...] = y.astype(BF16)


def _moe_combine(y, h, info, p1, p2, g, b, alpha, tm):
    n, d = h.shape
    ne = info.shape[1]
    grid_spec = pltpu.PrefetchScalarGridSpec(
        num_scalar_prefetch=2,
        grid=(n // tm,),
        in_specs=[
            pl.BlockSpec(memory_space=pl.ANY),
            pl.BlockSpec((tm, d), lambda i, a, c: (i, 0)),
            pl.BlockSpec((tm, ne), lambda i, a, c: (i, 0)),
            pl.BlockSpec((1, d), lambda i, a, c: (0, 0)),
            pl.BlockSpec((1, d), lambda i, a, c: (0, 0)),
        ],
        out_specs=[
            pl.BlockSpec((tm, d), lambda i, a, c: (i, 0)),
            pl.BlockSpec((tm, d), lambda i, a, c: (i, 0)),
        ],
        scratch_shapes=[pltpu.VMEM((2, tm, d), F32), pltpu.VMEM((2, tm, d), F32),
                        pltpu.SemaphoreType.DMA((2,))],
    )
    return pl.pallas_call(
        functools.partial(_moe_combine_kernel, alpha=alpha),
        grid_spec=grid_spec,
        out_shape=[jax.ShapeDtypeStruct((n, d), F32), jax.ShapeDtypeStruct((n, d), BF16)],
        compiler_params=_cparams("arbitrary"),
        name="moe_combine",
    )(p1, p2, y, h, info, g.reshape(1, d), b.reshape(1, d))


MOE_PASS_BLOCKS = 18
MOE_F_TILE = 256
MOE_BLOCK_UNITS = 4


def _expert_ffn_kernel(pe_ref, ps_ref, pn_ref, nu_ref, xs_ref, w1_ref, w3_ref, w2_ref, y_ref,
                       x_buf, acc_ref, w1b_ref, w3b_ref, w2b_ref, sem):
    p = pl.program_id(0)
    f = pl.program_id(1)
    nf = pl.num_programs(1)
    nb = pn_ref[p]
    sb = MOE_ROWS
    start = pl.multiple_of(ps_ref[p], MOE_ROWS)
    nblock = nb // MOE_BLOCK_UNITS

    @pl.when(nb > 0)
    def _():
        @pl.when(f == 0)
        def _():
            cp = pltpu.make_async_copy(xs_ref.at[pl.ds(start, x_buf.shape[0]), :], x_buf, sem.at[0])
            cp.start()
            cp.wait()

        w1b_ref[...] = w1_ref[...].astype(BF16)
        w3b_ref[...] = w3_ref[...].astype(BF16)
        w2b_ref[...] = w2_ref[...].astype(BF16)

        def up(unit, nunit):
            x = x_buf[pl.ds(pl.multiple_of(unit * sb, sb), nunit * sb), :]
            a = jnp.dot(x, w1b_ref[...], preferred_element_type=F32)
            bb = jnp.dot(x, w3b_ref[...], preferred_element_type=F32)
            return (_silu(a) * bb).astype(BF16)

        def down(unit, nunit, hmid, first):
            rows = pl.ds(pl.multiple_of(unit * sb, sb), nunit * sb)
            part = jnp.dot(hmid, w2b_ref[...], preferred_element_type=F32)
            if first:
                acc_ref[rows, :] = part
            else:
                acc_ref[rows, :] += part

        def sweep(first):
            bu = MOE_BLOCK_UNITS

            @pl.when(nblock > 0)
            def _():
                def body(i, hprev):
                    down(bu * (i - 1), bu, hprev, first)
                    return up(bu * i, bu)

                down(bu * (nblock - 1), bu, lax.fori_loop(1, nblock, body, up(0, bu)), first)

            done = nblock * bu
            size = bu // 2
            while size >= 1:
                @pl.when((nb - done) & size != 0)
                def _(done=done, size=size):
                    down(done, size, up(done, size), first)

                done = done + ((nb - done) & size)
                size //= 2

        @pl.when(f == 0)
        def _():
            sweep(True)

        @pl.when(f > 0)
        def _():
            sweep(False)

        @pl.when(f == nf - 1)
        def _():
            def out_copy(i):
                rows = pl.ds(pl.multiple_of(i * sb, sb), sb)
                dst = pl.ds(pl.multiple_of(start + i * sb, sb), sb)
                return pltpu.make_async_copy(acc_ref.at[rows, :], y_ref.at[dst, :], sem.at[1])

            lax.fori_loop(0, nb, lambda i, c: (out_copy(i).start(), c)[1], 0)
            lax.fori_loop(0, nb, lambda i, c: (out_copy(i).wait(), c)[1], 0)

    @pl.when(jnp.logical_and(p == pl.num_programs(0) - 1, f == nf - 1))
    def _():
        acc_ref[0:sb, :] = jnp.zeros((sb, acc_ref.shape[1]), F32)

        def zero_copy(i):
            dst = pl.ds(pl.multiple_of(i * sb, sb), sb)
            return pltpu.make_async_copy(acc_ref.at[0:sb, :], y_ref.at[dst, :], sem.at[1])

        n_blocks = y_ref.shape[0] // sb
        lax.fori_loop(nu_ref[1], n_blocks, lambda i, c: (zero_copy(i).start(), c)[1], 0)
        lax.fori_loop(nu_ref[1], n_blocks, lambda i, c: (zero_copy(i).wait(), c)[1], 0)


def _expert_ffn(xs, w1, w3, w2, pass_expert, pass_start, pass_nb, n_used, p_rows):
    d = xs.shape[1]
    f = w1.shape[2]
    tf = _divisor_tile(f, MOE_F_TILE, LANES)
    nf = f // tf
    npass = pass_expert.shape[0]
    r = MOE_PASS_BLOCKS * MOE_ROWS

    def fidx(p, j, nu):
        return jnp.where(p < nu[0], j, nf - 1)

    grid_spec = pltpu.PrefetchScalarGridSpec(
        num_scalar_prefetch=4,
        grid=(npass, nf),
        in_specs=[
            pl.BlockSpec(memory_space=pl.ANY),
            pl.BlockSpec((None, d, tf), lambda p, j, pe, ps, pn, nu: (pe[p], 0, fidx(p, j, nu))),
            pl.BlockSpec((None, d, tf), lambda p, j, pe, ps, pn, nu: (pe[p], 0, fidx(p, j, nu))),
            pl.BlockSpec((None, tf, d), lambda p, j, pe, ps, pn, nu: (pe[p], fidx(p, j, nu), 0)),
        ],
        out_specs=pl.BlockSpec(memory_space=pl.ANY),
        scratch_shapes=[
            pltpu.VMEM((r, d), BF16), pltpu.VMEM((r, d), F32),
            pltpu.VMEM((d, tf), BF16), pltpu.VMEM((d, tf), BF16), pltpu.VMEM((tf, d), BF16),
            pltpu.SemaphoreType.DMA((2,)),
        ],
    )
    return pl.pallas_call(
        _expert_ffn_kernel,
        grid_spec=grid_spec,
        out_shape=jax.ShapeDtypeStruct((p_rows, d), F32),
        compiler_params=_cparams("arbitrary", "arbitrary"),
        name="expert_ffn",
    )(pass_expert, pass_start, pass_nb, n_used, xs, w1, w3, w2)


def _moe_ffn(h, hb, router_w, router_b, w1, w3, w2, g, b, alpha, tm):
    n, d = h.shape
    ne = router_w.shape[1]
    rows = MOE_ROWS
    info, counts = _router(h, router_w, router_b, tm)
    counts = counts[0].astype(jnp.int32)
    nblk_e = (counts + rows - 1) // rows
    blk_end = jnp.cumsum(nblk_e)
    starts = (blk_end - nblk_e) * rows
    n_blocks = (n * TOP_K + ne * (rows - 1)) // rows
    p = n_blocks * rows
    n_active = blk_end[-1:].astype(jnp.int32)
    npass_e = (nblk_e + MOE_PASS_BLOCKS - 1) // MOE_PASS_BLOCKS
    pass_end = jnp.cumsum(npass_e)
    max_pass = n_blocks // MOE_PASS_BLOCKS + ne
    pidx = jnp.arange(max_pass, dtype=jnp.int32)
    n_used = jnp.stack([pass_end[-1], blk_end[-1]]).astype(jnp.int32)
    last_expert = jnp.sum(pass_end < pass_end[-1]).astype(jnp.int32)
    pass_expert = jnp.minimum(jnp.sum(pidx[:, None] >= pass_end[None, :], axis=1), last_expert).astype(jnp.int32)
    local = pidx - (pass_end - npass_e)[pass_expert]
    pass_start = (starts[pass_expert] + local * (MOE_PASS_BLOCKS * rows)).astype(jnp.int32)
    pass_nb = jnp.where(pidx < n_used[0],
                        jnp.clip(nblk_e[pass_expert] - local * MOE_PASS_BLOCKS, 0, MOE_PASS_BLOCKS),
                        0).astype(jnp.int32)
    pass_start = jnp.where(pass_nb > 0, pass_start, 0).astype(jnp.int32)
    i1 = info[:, 0].astype(jnp.int32)
    i2 = info[:, 1].astype(jnp.int32)
    p1 = starts[i1] + info[:, 4].astype(jnp.int32)
    p2 = starts[i2] + info[:, 5].astype(jnp.int32)
    tok = jnp.arange(n, dtype=jnp.int32)
    p_in = p + MOE_PASS_BLOCKS * rows
    src = jnp.zeros((p_in,), jnp.int32).at[jnp.concatenate([p1, p2])].set(jnp.concatenate([tok, tok]))
    xs = _moe_gather(h, src, n_active, p_in, rows)
    y = _expert_ffn(xs, w1, w3, w2, pass_expert, pass_start, pass_nb, n_used, p)
    tmc = _divisor_tile(n, 384, 16)
    return _moe_combine(y, h, info, p1, p2, g, b, alpha, tmc)


def _proj_kernel(*refs, epilogue, n_extra, n_out):
    x_ref, w_ref = refs[0], refs[1]
    extra = refs[2:2 + n_extra]
    outs = refs[2 + n_extra:2 + n_extra + n_out]
    wb_ref = refs[2 + n_extra + n_out]

    @pl.when(pl.program_id(1) == 0)
    def _():
        wb_ref[...] = w_ref[...].astype(BF16)

    y = jnp.dot(x_ref[...], wb_ref[...], preferred_element_type=F32)
    res = epilogue(y, *[e[...] for e in extra])
    for o_ref, r in zip(outs, res):
        o_ref[...] = r.astype(o_ref.dtype)


def _proj(xb, w, col0, ncols, epilogue, extras, out_dtypes, tm, name, tn_cap=1024):
    n, kdim = xb.shape
    tn = _divisor_tile(ncols, tn_cap, LANES)
    assert col0 % tn == 0
    off = col0 // tn
    outs = pl.pallas_call(
        functools.partial(_proj_kernel, epilogue=epilogue, n_extra=len(extras), n_out=len(out_dtypes)),
        grid=(ncols // tn, n // tm),
        in_specs=[
            pl.BlockSpec((tm, kdim), lambda j, m: (m, 0)),
            pl.BlockSpec((kdim, tn), lambda j, m: (0, off + j)),
        ] + [pl.BlockSpec((1, tn), lambda j, m: (0, j))] * len(extras),
        out_specs=[pl.BlockSpec((tm, tn), lambda j, m: (m, j))] * len(out_dtypes),
        out_shape=[jax.ShapeDtypeStruct((n, ncols), dt) for dt in out_dtypes],
        scratch_shapes=[pltpu.VMEM((kdim, tn), BF16)],
        compiler_params=_cparams("arbitrary", "arbitrary"),
        name=name,
    )(xb, w, *[e.reshape(1, ncols) for e in extras])
    return outs


def _tril_bf16(c, inclusive):
    r_i = lax.broadcasted_iota(jnp.int32, (c, c), 0)
    c_i = lax.broadcasted_iota(jnp.int32, (c, c), 1)
    keep = (c_i <= r_i) if inclusive else (c_i < r_i)
    return jnp.where(keep, 1.0, 0.0).astype(BF16)


def _cumsum_rows(x, tril):
    hi = x.astype(BF16)
    r1 = x - hi.astype(F32)
    mid = r1.astype(BF16)
    lo = (r1 - mid.astype(F32)).astype(BF16)
    return (jnp.dot(tril, hi, preferred_element_type=F32)
            + (jnp.dot(tril, mid, preferred_element_type=F32)
               + jnp.dot(tril, lo, preferred_element_type=F32)))


def _dot_nt(a, b):
    return lax.dot_general(a.astype(BF16), b.astype(BF16), (((1,), (1,)), ((), ())),
                           preferred_element_type=F32)


def _dot_tn(a, b):
    return lax.dot_general(a.astype(BF16), b.astype(BF16), (((0,), (0,)), ((), ())),
                           preferred_element_type=F32)


def _dot_nn(a, b):
    return jnp.dot(a.astype(BF16), b.astype(BF16), preferred_element_type=F32)


def _chunk_len(t):
    return _divisor_tile(t, 64, 16)


def _round_robin(gens):
    done = [None] * len(gens)
    while any(d is None for d in done):
        for j, gen in enumerate(gens):
            if done[j] is None:
                done[j] = next(gen)
    return done


HGRN_SUB = 16


def _hgrn_scan_kernel(q_ref, lf_ref, v_ref, gs_ref, ng_ref, o_ref, st_ref, *, chunk):
    t, w = q_ref.shape
    nhead = w // HGRN_HEAD
    nsub = chunk // HGRN_SUB
    st_ref[...] = jnp.zeros_like(st_ref)
    tril = _tril_bf16(chunk, True)
    row16 = lax.broadcasted_iota(jnp.int32, (HGRN_SUB, 1), 0)

    def head_chunk(q, lf, v, st):
        k = 1.0 - jnp.exp(lf)
        cum = _cumsum_rows(lf, tril)
        yield None
        o_inter = _dot_nt(q * jnp.exp(cum), st)
        vb = v.astype(BF16)
        cl = cum[chunk - 1:chunk]
        kd = k * jnp.exp(cl - cum)
        st_new = st * jnp.exp(cl) + _dot_tn(v, kd)
        yield None
        outs = []
        for i in range(nsub):
            lo, hi = i * HGRN_SUB, (i + 1) * HGRN_SUB
            qi, ki, vi, cumi = q[lo:hi], k[lo:hi], v[lo:hi], cum[lo:hi]
            oi = o_inter[lo:hi]
            if i > 0:
                ci = cum[lo:lo + 1]
                qt = qi * jnp.exp(cumi - ci)
                kt = k[0:lo] * jnp.exp(ci - cum[0:lo])
                oi = oi + jnp.dot(_dot_nt(qt, kt).astype(BF16), vb[0:lo], preferred_element_type=F32)
            for s in range(HGRN_SUB):
                dec = jnp.exp(jnp.minimum(cumi - cumi[s:s + 1], 0.0))
                col = jnp.sum(qi * dec * ki[s:s + 1], axis=-1, keepdims=True)
                col = jnp.where(row16 >= s, col, 0.0)
                oi = oi + col * vi[s:s + 1]
                if s % 4 == 3:
                    yield None
            outs.append(oi)
        o = jnp.concatenate(outs, axis=0)
        o = o * lax.rsqrt(jnp.mean(o * o, axis=-1, keepdims=True) + RMS_EPS)
        yield o, st_new

    def body(c, carry):
        rows = pl.ds(pl.multiple_of(c * chunk, 16), chunk)
        q = q_ref[rows, :].astype(F32)
        lf = lf_ref[rows, :]
        v = v_ref[rows, :].astype(F32)
        heads = []
        for hh in range(nhead):
            cols = slice(hh * HGRN_HEAD, (hh + 1) * HGRN_HEAD)
            heads.append(head_chunk(q[:, cols], lf[:, cols], v[:, cols], st_ref[hh]))
        outs = []
        for hh, (o, st_new) in enumerate(_round_robin(heads)):
            st_ref[hh] = st_new
            outs.append(o)
        o = jnp.concatenate(outs, axis=1)
        o_ref[rows, :] = (o * ng_ref[...] * gs_ref[rows, :].astype(F32)).astype(BF16)
        return carry

    lax.fori_loop(0, t // chunk, body, 0)


HGRN_GROUP = 8


def _hgrn_scan(q, lf, v, gs, norm_g, batch):
    n, d = q.shape
    t = n // batch
    chunk = _chunk_len(t)
    w = min(d, HGRN_GROUP * HGRN_HEAD)
    blk = pl.BlockSpec((t, w), lambda b, j: (b, j))
    return pl.pallas_call(
        functools.partial(_hgrn_scan_kernel, chunk=chunk),
        grid=(batch, d // w),
        in_specs=[blk, blk, blk, blk, pl.BlockSpec((1, w), lambda b, j: (0, j))],
        out_specs=blk,
        out_shape=jax.ShapeDtypeStruct((n, d), BF16),
        scratch_shapes=[pltpu.VMEM((w // HGRN_HEAD, HGRN_HEAD, HGRN_HEAD), F32)],
        compiler_params=_cparams("parallel", "parallel"),
        name="hgrn_scan",
    )(q, lf, v, gs, norm_g.reshape(1, d))


def _hgrn_mixer_layer(h, hb, p, layer_idx, g, b, alpha, batch, tm):
    d = h.shape[1]
    w_in = p["hgrn_w_in"]
    lb = jnp.cumsum(jax.nn.softmax(p["hgrn_lb"].astype(F32), axis=0), axis=0)
    lb = lb[layer_idx] - lb[0]
    (q,) = _proj(hb, w_in, 0, d, lambda y: (_silu(y),), [], [BF16], tm, "hgrn_proj_q")
    (lf,) = _proj(hb, w_in, d, d, lambda y, lbv: (jnp.log(lbv + (1.0 - lbv) * _sigmoid(y)),),
                  [lb], [F32], tm, "hgrn_proj_f")
    (v,) = _proj(hb, w_in, 2 * d, d, lambda y: (y,), [], [BF16], tm, "hgrn_proj_i")
    (gs,) = _proj(hb, w_in, 3 * d, d, lambda y: (_silu(y),), [], [BF16], tm, "hgrn_proj_g")
    z = _hgrn_scan(q, lf, v, gs, p["hgrn_norm_g"], batch)
    return _mm_res_ln(z, p["hgrn_w_out"], h, g, b, alpha, tm)


def _fox_gate_kernel(h_ref, w_ref, bf_ref, c_ref, carry_ref):
    @pl.when(pl.program_id(1) == 0)
    def _():
        carry_ref[...] = jnp.zeros_like(carry_ref)

    x = _dot_f32(h_ref[...], w_ref[...]) + bf_ref[...]
    log_f = jnp.minimum(x, 0.0) - jnp.log(1.0 + jnp.exp(-jnp.abs(x)))
    tm = x.shape[0]
    c = _cumsum_rows(log_f, _tril_bf16(tm, True)) + carry_ref[...]
    c_ref[...] = c
    carry_ref[...] = c[tm - 1:tm, :]


def _fox_gate(h, w_f, b_f, batch, tm):
    n, d = h.shape
    nh = w_f.shape[1]
    tpb = n // batch // tm
    return pl.pallas_call(
        _fox_gate_kernel,
        grid=(batch, tpb),
        in_specs=[
            pl.BlockSpec((tm, d), lambda b, t: (b * tpb + t, 0)),
            pl.BlockSpec((d, nh), lambda b, t: (0, 0)),
            pl.BlockSpec((1, nh), lambda b, t: (0, 0)),
        ],
        out_specs=pl.BlockSpec((tm, nh), lambda b, t: (b * tpb + t, 0)),
        out_shape=jax.ShapeDtypeStruct((n, nh), F32),
        scratch_shapes=[pltpu.VMEM((1, nh), F32)],
        compiler_params=_cparams("arbitrary", "arbitrary"),
        name="fox_gate",
    )(h, w_f, b_f.reshape(1, nh))


def _fox_attn_kernel(q_ref, k_ref, v_ref, sg_ref, c_ref, ct_ref, o_ref, *, tq):
    hd = pl.program_id(1)
    t = q_ref.shape[0]
    nh = c_ref.shape[1]
    lane = lax.broadcasted_iota(jnp.int32, (t, nh), 1)
    c_col = jnp.sum(jnp.where(lane == hd, c_ref[...], 0.0), axis=-1, keepdims=True)
    c_row = ct_ref[pl.ds(hd, 1), :]
    def query_tile(i):
        lo, hi = i * tq, (i + 1) * tq
        s = lax.dot_general(q_ref[lo:hi, :], k_ref[0:hi, :], (((1,), (1,)), ((), ())),
                            preferred_element_type=F32)
        yield None
        s = s + c_col[lo:hi] - c_row[:, 0:hi]
        r_i = lax.broadcasted_iota(jnp.int32, (tq, hi), 0) + lo
        c_i = lax.broadcasted_iota(jnp.int32, (tq, hi), 1)
        s = jnp.where(c_i <= r_i, s, -jnp.inf)
        m = jnp.max(s, axis=-1, keepdims=True)
        p = jnp.exp(s - m)
        l = jnp.sum(p, axis=-1, keepdims=True)
        yield None
        o = jnp.dot(p.astype(BF16), v_ref[0:hi, :], preferred_element_type=F32) / l
        o_ref[lo:hi, :] = (o * sg_ref[lo:hi, :]).astype(BF16)
        yield True

    _round_robin([query_tile(i) for i in range(t // tq)])


def _fox_attn(q, k, v, sg, c, ct, batch):
    n, d = q.shape
    t = n // batch
    nh = d // FOX_HEAD
    tq = _divisor_tile(t, 768, 16)
    blk = pl.BlockSpec((t, FOX_HEAD), lambda b, h: (b, h))
    return pl.pallas_call(
        functools.partial(_fox_attn_kernel, tq=tq),
        grid=(batch, nh),
        in_specs=[blk, blk, blk, blk,
                  pl.BlockSpec((t, nh), lambda b, h: (b, 0)),
                  pl.BlockSpec((None, nh, t), lambda b, h: (b, 0, 0))],
        out_specs=blk,
        out_shape=jax.ShapeDtypeStruct((n, d), BF16),
        compiler_params=_cparams("parallel", "parallel"),
        name="fox_attn",
    )(q, k, v, sg, c, ct)


def _head_rms_epilogue(scale):
    def epi(y, gain):
        outs = []
        for j in range(y.shape[1] // FOX_HEAD):
            yj = y[:, j * FOX_HEAD:(j + 1) * FOX_HEAD]
            yj = yj * lax.rsqrt(jnp.mean(yj * yj, axis=-1, keepdims=True) + RMS_EPS)
            outs.append(yj * gain[:, j * FOX_HEAD:(j + 1) * FOX_HEAD] * scale)
        return (jnp.concatenate(outs, axis=1),)
    return epi


def _fox_mixer_layer(h, hb, p, g, b, alpha, batch, tm):
    n, d = h.shape
    nh = d // FOX_HEAD
    w_in = p["fox_w_in"]
    qg = jnp.tile(p["fox_q_norm_g"], nh)
    kg = jnp.tile(p["fox_k_norm_g"], nh)
    (q,) = _proj(hb, w_in, 0, d, _head_rms_epilogue(FOX_HEAD ** -0.5), [qg], [BF16], tm, "fox_proj_q")
    (k,) = _proj(hb, w_in, d, d, _head_rms_epilogue(1.0), [kg], [BF16], tm, "fox_proj_k")
    (v,) = _proj(hb, w_in, 2 * d, d, lambda y: (y,), [], [BF16], tm, "fox_proj_v")
    (sg,) = _proj(hb, w_in, 3 * d, d, lambda y: (_sigmoid(y),), [], [F32], tm, "fox_proj_g")
    c = _fox_gate(h, w_in[:, 4 * d:], p["fox_b_f"], batch, tm)
    ct = c.reshape(batch, n // batch, nh).transpose(0, 2, 1)
    z = _fox_attn(q, k, v, sg, c, ct, batch)
    return _mm_res_ln(z, p["fox_w_out"], h, g, b, alpha, tm)


def _rwkv_mix_kernel(h_ref, mu_ref, *refs):
    outs, carry_ref = refs[:-1], refs[-1]

    @pl.when(pl.program_id(1) == 0)
    def _():
        carry_ref[...] = jnp.zeros_like(carry_ref)

    x = h_ref[...]
    tm = x.shape[0]
    row = lax.broadcasted_iota(jnp.int32, (tm, 1), 0)
    prev = jnp.where(row == 0, carry_ref[7:8, :], pltpu.roll(x, 1, axis=0))
    xx = prev - x
    carry_ref[...] = x[tm - 8:, :]
    for j, o_ref in enumerate(outs):
        o_ref[...] = (x + xx * mu_ref[j:j + 1, :]).astype(BF16)


def _rwkv_mix(h, mu, batch, tm):
    n, d = h.shape
    nmix = mu.shape[0]
    tpb = n // batch // tm
    blk = pl.BlockSpec((tm, d), lambda b, t: (b * tpb + t, 0))
    return pl.pallas_call(
        _rwkv_mix_kernel,
        grid=(batch, tpb),
        in_specs=[blk, pl.BlockSpec((nmix, d), lambda b, t: (0, 0))],
        out_specs=[blk] * nmix,
        out_shape=[jax.ShapeDtypeStruct((n, d), BF16)] * nmix,
        scratch_shapes=[pltpu.VMEM((8, d), F32)],
        compiler_params=_cparams("arbitrary", "arbitrary"),
        name="rwkv_mix",
    )(h, mu)


def _lora_kernel(x_ref, wa_ref, wb_ref, bias_ref, o_ref, wab_ref, wbb_ref, *, mid_act, out_act):
    @pl.when(pl.program_id(0) == 0)
    def _():
        wab_ref[...] = wa_ref[...].astype(BF16)
        wbb_ref[...] = wb_ref[...].astype(BF16)

    mid = mid_act(jnp.dot(x_ref[...], wab_ref[...], preferred_element_type=F32))
    y = jnp.dot(mid.astype(BF16), wbb_ref[...], preferred_element_type=F32)
    o_ref[...] = out_act(bias_ref[...] + y)


def _lora(xb, wa, wb, bias, mid_act, out_act, tm, name):
    n, d = xb.shape
    r = wa.shape[1]
    dout = wb.shape[1]
    return pl.pallas_call(
        functools.partial(_lora_kernel, mid_act=mid_act, out_act=out_act),
        grid=(n // tm,),
        in_specs=[
            pl.BlockSpec((tm, d), lambda i: (i, 0)),
            pl.BlockSpec((d, r), lambda i: (0, 0)),
            pl.BlockSpec((r, dout), lambda i: (0, 0)),
            pl.BlockSpec((1, dout), lambda i: (0, 0)),
        ],
        out_specs=pl.BlockSpec((tm, dout), lambda i: (i, 0)),
        out_shape=jax.ShapeDtypeStruct((n, dout), F32),
        scratch_shapes=[pltpu.VMEM((d, r), BF16), pltpu.VMEM((r, dout), BF16)],
        compiler_params=_cparams("arbitrary"),
        name=name,
    )(xb, wa, wb, bias.reshape(1, dout))


def _rwkv_log_decay(z):
    w_log = -(jnp.maximum(-z, 0.0) + jnp.log(1.0 + jnp.exp(-jnp.abs(z)))) - 0.5
    return -jnp.exp(w_log)


RWKV_GROUP = 4
RWKV_UNROLL = 8


def _seg_sum(x, bd):
    hi = x.astype(BF16)
    lo = (x - hi.astype(F32)).astype(BF16)
    return jnp.dot(hi, bd, preferred_element_type=F32) + jnp.dot(lo, bd, preferred_element_type=F32)


def _rwkv_scan_kernel(r_ref, kr_ref, v_ref, lw_ref, a_ref, g_ref, kk_p, ka_p, rk_p, gg_p, gb_p,
                      o_ref, kk_s, k_s, bonus_s, y_s, st_ref, *, chunk, ptile):
    t, w = r_ref.shape
    nhead = w // RWKV_HEAD
    sc = nhead * chunk
    lane_r = lax.broadcasted_iota(jnp.int32, (w, w), 0) // RWKV_HEAD
    lane_c = lax.broadcasted_iota(jnp.int32, (w, w), 1) // RWKV_HEAD
    bd = jnp.where(lane_r == lane_c, 1.0, 0.0).astype(BF16)

    def prologue(i, carry):
        rows = pl.ds(pl.multiple_of(i * ptile, 8), ptile)
        kr = kr_ref[rows, :]
        a = a_ref[rows, :]
        kkr = kr * kk_p[...]
        nrm = jnp.maximum(jnp.sqrt(_seg_sum(kkr * kkr, bd)), 1e-12)
        kk_s[rows, :] = kkr / nrm
        k = kr * (1.0 + (a - 1.0) * ka_p[...])
        k_s[rows, :] = k
        bonus_s[rows, :] = _seg_sum(r_ref[rows, :] * k * rk_p[...], bd) * v_ref[rows, :]
        return carry

    lax.fori_loop(0, t // ptile, prologue, 0)

    st_ref[...] = jnp.zeros_like(st_ref)
    tril = _tril_bf16(chunk, True)
    head_of_lane = lax.broadcasted_iota(jnp.int32, (chunk, w), 1) // RWKV_HEAD
    ri = lax.broadcasted_iota(jnp.int32, (2 * sc, sc), 0)
    ci = lax.broadcasted_iota(jnp.int32, (2 * sc, sc), 1)
    low_mask = ci < jnp.where(ri < sc, ri, ri - sc + 1)
    nsteps = max(1, (chunk - 1).bit_length())

    def stack(x):
        return jnp.concatenate([jnp.where(head_of_lane == hh, x, 0.0) for hh in range(nhead)], axis=0)

    eye = jnp.where(lax.broadcasted_iota(jnp.int32, (sc, sc), 0)
                    == lax.broadcasted_iota(jnp.int32, (sc, sc), 1), 1.0, 0.0)

    def prepare(c):
        start = c * chunk
        rows = pl.ds(start if isinstance(start, int) else pl.multiple_of(start, 16), chunk)
        r = r_ref[rows, :]
        v = v_ref[rows, :]
        lw = lw_ref[rows, :]
        a = a_ref[rows, :]
        kk = kk_s[rows, :]
        k = k_s[rows, :]
        cum = _cumsum_rows(lw, tril)
        e_neg = jnp.exp(-cum)
        at2 = stack(-kk * jnp.exp(cum - lw))
        rt2 = stack(r * jnp.exp(cum))
        bvec = kk * a
        bb2 = stack(bvec * e_neg)
        kb2 = stack(k * e_neg)
        v2 = stack(v)
        ar2 = jnp.concatenate([at2, rt2], axis=0).astype(BF16)
        cl = cum[chunk - 1:chunk]
        e_end = jnp.exp(cl - cum)
        khbh = jnp.concatenate([stack(k * e_end), stack(bvec * e_end)], axis=0).astype(BF16)
        yield None
        pb = jnp.where(low_mask, _dot_nt(ar2, bb2), 0.0)
        pk = jnp.where(low_mask, _dot_nt(ar2, kb2), 0.0)
        m_ab, m_rb = pb[:sc], pb[sc:]
        m_ak, m_rk = pk[:sc], pk[sc:]
        yield None
        u0 = _dot_nn(m_ak, v2)
        y0 = _dot_nn(m_rk, v2)
        tinv = eye + m_ab
        lpow = m_ab
        for _ in range(nsteps - 1):
            yield None
            lpow = _dot_nn(lpow, lpow)
            tinv = tinv + _dot_nn(tinv, lpow)
        return dict(rows=rows, ar2=ar2, tinv=tinv.astype(BF16), m_rb=m_rb.astype(BF16),
                    u0=u0, y0=y0, v2=v2.astype(BF16), khbh=khbh, decay=jnp.exp(cl))

    def chunk_steps(c, j, run):
        pc = yield from prepare(c)
        while run["turn"] != j:
            yield None
        st = run["st"]
        ps = _dot_nt(pc["ar2"], st)
        yield None
        u2 = jnp.dot(pc["tinv"], (ps[:sc] + pc["u0"]).astype(BF16), preferred_element_type=F32)
        u2b = u2.astype(BF16)
        yield None
        y2 = ps[sc:] + pc["y0"] + jnp.dot(pc["m_rb"], u2b, preferred_element_type=F32)
        y = y2[0:chunk]
        for hh in range(1, nhead):
            y = y + y2[hh * chunk:(hh + 1) * chunk]
        y_s[pc["rows"], :] = y
        run["st"] = st * pc["decay"] + _dot_tn(jnp.concatenate([pc["v2"], u2b], axis=0), pc["khbh"])
        run["turn"] = j + 1
        yield True

    def run_group(chunk_ids):
        run = dict(st=st_ref[...], turn=0)
        _round_robin([chunk_steps(c, j, run) for j, c in enumerate(chunk_ids)])
        st_ref[...] = run["st"]

    nchunks = t // chunk

    def body(i, carry):
        run_group([i * RWKV_UNROLL + j for j in range(RWKV_UNROLL)])
        return carry

    lax.fori_loop(0, nchunks // RWKV_UNROLL, body, 0)
    tail = list(range(nchunks - nchunks % RWKV_UNROLL, nchunks))
    if tail:
        run_group(tail)

    inv = 1.0 / RWKV_HEAD

    def epilogue(i, carry):
        rows = pl.ds(pl.multiple_of(i * ptile, 8), ptile)
        y = y_s[rows, :]
        mu = _seg_sum(y, bd) * inv
        yc = y - mu
        var = _seg_sum(yc * yc, bd) * inv
        yn = yc * lax.rsqrt(var + 1e-5 * RWKV_HEAD) * gg_p[...] + gb_p[...]
        o_ref[rows, :] = ((yn + bonus_s[rows, :]) * g_ref[rows, :]).astype(BF16)
        return carry

    lax.fori_loop(0, t // ptile, epilogue, 0)


def _rwkv_scan(r, kr, v, lw, a, g, p, batch):
    n, d = r.shape
    t = n // batch
    chunk = _chunk_len(t)
    w = min(d, RWKV_GROUP * RWKV_HEAD)
    ptile = _divisor_tile(t, 768, 16)
    blk = pl.BlockSpec((t, w), lambda b, j: (b, j))
    prm = pl.BlockSpec((1, w), lambda b, j: (0, j))
    params = [p["rwkv_k_k"], p["rwkv_k_a"], p["rwkv_r_k"], p["rwkv_gn_g"], p["rwkv_gn_b"]]
    return pl.pallas_call(
        functools.partial(_rwkv_scan_kernel, chunk=chunk, ptile=ptile),
        grid=(batch, d // w),
        in_specs=[blk] * 6 + [prm] * 5,
        out_specs=blk,
        out_shape=jax.ShapeDtypeStruct((n, d), BF16),
        scratch_shapes=[pltpu.VMEM((t, w), F32)] * 4 + [pltpu.VMEM((w, w), F32)],
        compiler_params=_cparams("parallel", "parallel"),
        name="rwkv_scan",
    )(r, kr, v, lw, a, g, *[x.reshape(1, d) for x in params])


def _rwkv_mixer_layer(h, hb, p, g, b, alpha, batch, tm):
    n, d = h.shape
    ident = lambda y: y
    xr, xw, xk, xv, xa, xg = _rwkv_mix(h, p["rwkv_mu"], batch, tm)
    (r,) = _proj(xr, p["rwkv_w_r"], 0, d, lambda y: (y,), [], [F32], tm, "rwkv_proj_r")
    (kr,) = _proj(xk, p["rwkv_w_k"], 0, d, lambda y: (y,), [], [F32], tm, "rwkv_proj_k")
    (v,) = _proj(xv, p["rwkv_w_v"], 0, d, lambda y: (y,), [], [F32], tm, "rwkv_proj_v")
    lw = _lora(xw, p["rwkv_w1"], p["rwkv_w2"], p["rwkv_w0"], jnp.tanh, _rwkv_log_decay, tm, "rwkv_lora_w")
    a = _lora(xa, p["rwkv_a1"], p["rwkv_a2"], p["rwkv_a0"], ident, _sigmoid, tm, "rwkv_lora_a")
    gate = _lora(xg, p["rwkv_g1"], p["rwkv_g2"], jnp.zeros((d,), F32), _sigmoid, ident, tm, "rwkv_lora_g")
    z = _rwkv_scan(r, kr, v, lw, a, gate, p, batch)
    return _mm_res_ln(z, p["rwkv_w_out"], h, g, b, alpha, tm)


def _embed_kernel(x_ref, meta_ref, h_ref, hb_ref, sem):
    b = pl.program_id(0)
    j = pl.program_id(1)
    tm = h_ref.shape[0]
    nmeta = meta_ref.shape[0]

    @pl.when(j == 0)
    def _():
        h_ref[0:nmeta, :] = meta_ref[...]
        cp = pltpu.make_async_copy(x_ref.at[b, pl.ds(0, tm - nmeta), :], h_ref.at[pl.ds(nmeta, tm - nmeta), :], sem)
        cp.start()
        cp.wait()

    @pl.when(j > 0)
    def _():
        first = pl.multiple_of(j * tm - nmeta, 8)
        cp = pltpu.make_async_copy(x_ref.at[b, pl.ds(first, tm), :], h_ref, sem)
        cp.start()
        cp.wait()

    hb_ref[...] = h_ref[...].astype(BF16)


def _embed(x, meta, tm):
    batch, seq, d = x.shape
    nmeta = meta.shape[0]
    t = nmeta + seq
    tpb = t // tm
    assert nmeta % 8 == 0 and tm % 8 == 0
    blk = pl.BlockSpec((tm, d), lambda b, j: (b * tpb + j, 0))
    return pl.pallas_call(
        _embed_kernel,
        grid=(batch, tpb),
        in_specs=[pl.BlockSpec(memory_space=pl.ANY), pl.BlockSpec((nmeta, d), lambda b, j: (0, 0))],
        out_specs=[blk, blk],
        out_shape=[jax.ShapeDtypeStruct((batch * t, d), F32), jax.ShapeDtypeStruct((batch * t, d), BF16)],
        scratch_shapes=[pltpu.SemaphoreType.DMA],
        compiler_params=_cparams("parallel", "arbitrary"),
        name="embed",
    )(x, meta.astype(x.dtype))


def kernel(x, meta, ln_mix_g, ln_mix_b, ln_ffn_g, ln_ffn_b, conv_w_in, conv_w, conv_b, conv_w_out, rwkv_mu, rwkv_w_r, rwkv_w_k, rwkv_w_v, rwkv_w0, rwkv_w1, rwkv_w2, rwkv_a0, rwkv_a1, rwkv_a2, rwkv_g1, rwkv_g2, rwkv_k_k, rwkv_k_a, rwkv_r_k, rwkv_gn_g, rwkv_gn_b, rwkv_w_out, hgrn_w_in, hgrn_lb, hgrn_norm_g, hgrn_w_out, fox_w_in, fox_b_f, fox_q_norm_g, fox_k_norm_g, fox_w_out, ffn0_w1, ffn0_w3, ffn0_w2, moe1_router, moe1_router_b, moe1_w1, moe1_w3, moe1_w2, ffn2_w1, ffn2_w3, ffn2_w2, moe3_router, moe3_router_b, moe3_w1, moe3_w3, moe3_w2):
    batch, seq, d = x.shape
    depth = ln_mix_g.shape[0]
    assert depth == 4
    alpha = (2.0 * depth) ** 0.25
    t = N_META + seq
    n = batch * t
    tm = _divisor_tile(t, 768, 16)
    p = dict(
        conv_w_in=conv_w_in, conv_w=conv_w, conv_b=conv_b, conv_w_out=conv_w_out,
        rwkv_mu=rwkv_mu, rwkv_w_r=rwkv_w_r, rwkv_w_k=rwkv_w_k, rwkv_w_v=rwkv_w_v, rwkv_w0=rwkv_w0,
        rwkv_w1=rwkv_w1, rwkv_w2=rwkv_w2, rwkv_a0=rwkv_a0, rwkv_a1=rwkv_a1, rwkv_a2=rwkv_a2,
        rwkv_g1=rwkv_g1, rwkv_g2=rwkv_g2, rwkv_k_k=rwkv_k_k, rwkv_k_a=rwkv_k_a, rwkv_r_k=rwkv_r_k,
        rwkv_gn_g=rwkv_gn_g, rwkv_gn_b=rwkv_gn_b, rwkv_w_out=rwkv_w_out,
        hgrn_w_in=hgrn_w_in, hgrn_lb=hgrn_lb, hgrn_norm_g=hgrn_norm_g, hgrn_w_out=hgrn_w_out,
        fox_w_in=fox_w_in, fox_b_f=fox_b_f, fox_q_norm_g=fox_q_norm_g, fox_k_norm_g=fox_k_norm_g,
        fox_w_out=fox_w_out,
    )
    assert meta.shape[0] == N_META
    h, hb = _embed(x, meta, tm)

    h, hb = _conv_mixer_layer(h, hb, p, ln_mix_g[0], ln_mix_b[0], alpha, batch, tm)
    h, hb = _dense_ffn(hb, h, ffn0_w1, ffn0_w3, ffn0_w2, ln_ffn_g[0], ln_ffn_b[0], alpha, tm)
    h, hb = _rwkv_mixer_layer(h, hb, p, ln_mix_g[1], ln_mix_b[1], alpha, batch, tm)
    h, hb = _moe_ffn(h, hb, moe1_router, moe1_router_b, moe1_w1, moe1_w3, moe1_w2,
                     ln_ffn_g[1], ln_ffn_b[1], alpha, tm)
    h, hb = _hgrn_mixer_layer(h, hb, p, 2, ln_mix_g[2], ln_mix_b[2], alpha, batch, tm)
    h, hb = _dense_ffn(hb, h, ffn2_w1, ffn2_w3, ffn2_w2, ln_ffn_g[2], ln_ffn_b[2], alpha, tm)
    h, hb = _fox_mixer_layer(h, hb, p, ln_mix_g[3], ln_mix_b[3], alpha, batch, tm)
    h, hb = _moe_ffn(h, hb, moe3_router, moe3_router_b, moe3_w1, moe3_w3, moe3_w2,
                     ln_ffn_g[3], ln_ffn_b[3], alpha, tm)
    return h.reshape(batch, t, d)[:, N_META:]
```

```python
import functools

import jax
import jax.numpy as jnp
from jax import lax
from jax.experimental import pallas as pl
from jax.experimental.pallas import tpu as pltpu

F32 = jnp.float32
BF16 = jnp.bfloat16

N_META = 16
LN_EPS = 1e-5
RMS_EPS = 1e-6
RWKV_HEAD = 64
HGRN_HEAD = 128
FOX_HEAD = 128
TOP_K = 2
LANES = 128
VMEM_LIMIT_BYTES = 56 * 2**20


def _cparams(*sem):
    return pltpu.CompilerParams(dimension_semantics=sem, vmem_limit_bytes=VMEM_LIMIT_BYTES)


def _divisor_tile(n, cap, mult):
    best = None
    for d in range(mult, min(n, cap) + 1, mult):
        if n % d == 0:
            best = d
    assert best is not None, (n, cap, mult)
    return best


def _layer_norm(y, g, b):
    mu = jnp.mean(y, axis=-1, keepdims=True)
    yc = y - mu
    var = jnp.mean(yc * yc, axis=-1, keepdims=True)
    return yc * lax.rsqrt(var + LN_EPS) * g + b


def _sigmoid(x):
    return 1.0 / (1.0 + jnp.exp(-x))


def _silu(x):
    return x * _sigmoid(x)


def _mm_res_ln_kernel(z_ref, w_ref, h_ref, g_ref, b_ref, o_ref, ob_ref, *, nk, alpha):
    k = pl.program_id(1)
    part = jnp.dot(z_ref[...], w_ref[...], preferred_element_type=F32)

    @pl.when(k == 0)
    def _():
        o_ref[...] = part

    @pl.when(k > 0)
    def _():
        o_ref[...] += part

    @pl.when(k == nk - 1)
    def _():
        y = _layer_norm(alpha * h_ref[...] + o_ref[...], g_ref[...], b_ref[...])
        o_ref[...] = y
        ob_ref[...] = y.astype(BF16)


WEIGHT_SLICE_ROWS = 256


def _load_weight_bf16(w_hbm, wb_ref, stage_ref, sem):
    rows = stage_ref.shape[1]
    nslice = w_hbm.shape[0] // rows

    def copy(s):
        return pltpu.make_async_copy(w_hbm.at[pl.ds(s * rows, rows), :], stage_ref.at[s % 2], sem.at[s % 2])

    copy(0).start()
    for s in range(nslice):
        if s + 1 < nslice:
            copy(s + 1).start()
        copy(s).wait()
        wb_ref[s * rows:(s + 1) * rows, :] = stage_ref[s % 2].astype(BF16)


def _mm_res_ln_resident_kernel(z_ref, w_ref, h_ref, g_ref, b_ref, o_ref, ob_ref, wb_ref, stage_ref, sem,
                               *, alpha):
    @pl.when(pl.program_id(0) == 0)
    def _():
        _load_weight_bf16(w_ref, wb_ref, stage_ref, sem)

    tm = z_ref.shape[0]
    split = min(tm, -(-tm // 32) * 16)
    spans = [(0, split), (split, tm)] if split < tm else [(0, tm)]
    dots = [jnp.dot(z_ref[lo:hi, :], wb_ref[...], preferred_element_type=F32) for lo, hi in spans]
    for (lo, hi), part in zip(spans, dots):
        y = _layer_norm(alpha * h_ref[lo:hi, :] + part, g_ref[...], b_ref[...])
        o_ref[lo:hi, :] = y
        ob_ref[lo:hi, :] = y.astype(BF16)


RESIDENT_WEIGHT_BYTES = 16 * 2**20


def _mm_res_ln_resident(z, w, h, g, b, alpha, tm):
    n, kdim = z.shape
    d = w.shape[1]
    ws = _divisor_tile(kdim, WEIGHT_SLICE_ROWS, 8)
    const = lambda m: (0, 0)
    return pl.pallas_call(
        functools.partial(_mm_res_ln_resident_kernel, alpha=alpha),
        grid=(n // tm,),
        in_specs=[
            pl.BlockSpec((tm, kdim), lambda m: (m, 0)),
            pl.BlockSpec(memory_space=pl.ANY),
            pl.BlockSpec((tm, d), lambda m: (m, 0)),
            pl.BlockSpec((1, d), const),
            pl.BlockSpec((1, d), const),
        ],
        out_specs=[
            pl.BlockSpec((tm, d), lambda m: (m, 0)),
            pl.BlockSpec((tm, d), lambda m: (m, 0)),
        ],
        out_shape=[jax.ShapeDtypeStruct((n, d), F32), jax.ShapeDtypeStruct((n, d), BF16)],
        scratch_shapes=[pltpu.VMEM((kdim, d), BF16), pltpu.VMEM((2, ws, d), F32),
                        pltpu.SemaphoreType.DMA((2,))],
        compiler_params=_cparams("arbitrary"),
        name="mm_res_ln_resident",
    )(z, w, h, g.reshape(1, d), b.reshape(1, d))


def _mm_res_ln(z, w, h, g, b, alpha, tm):
    n, kdim = z.shape
    d = w.shape[1]
    if kdim * d * 4 <= RESIDENT_WEIGHT_BYTES:
        return _mm_res_ln_resident(z, w, h, g, b, alpha, tm)
    w = w.astype(BF16)
    tk = _divisor_tile(kdim, 1408, LANES)
    nk = kdim // tk
    return pl.pallas_call(
        functools.partial(_mm_res_ln_kernel, nk=nk, alpha=alpha),
        grid=(n // tm, nk),
        in_specs=[
            pl.BlockSpec((tm, tk), lambda m, k: (m, k)),
            pl.BlockSpec((tk, d), lambda m, k: (k, 0)),
            pl.BlockSpec((tm, d), lambda m, k: (m, 0)),
            pl.BlockSpec((1, d), lambda m, k: (0, 0)),
            pl.BlockSpec((1, d), lambda m, k: (0, 0)),
        ],
        out_specs=[
            pl.BlockSpec((tm, d), lambda m, k: (m, 0)),
            pl.BlockSpec((tm, d), lambda m, k: (m, 0)),
        ],
        out_shape=[jax.ShapeDtypeStruct((n, d), F32), jax.ShapeDtypeStruct((n, d), BF16)],
        compiler_params=_cparams("parallel", "arbitrary"),
        name="mm_res_ln",
    )(z, w, h, g.reshape(1, d), b.reshape(1, d))


def _ffn_up_kernel(exp_ref, nact_ref, x_ref, w1_ref, w3_ref, o_ref, w1b_ref, w3b_ref):
    c = pl.program_id(1)
    prev = exp_ref[jnp.maximum(c - 1, 0)]
    new_weights = jnp.logical_or(c == 0, exp_ref[c] != prev)

    @pl.when(new_weights)
    def _():
        w1b_ref[...] = w1_ref[...].astype(BF16)
        w3b_ref[...] = w3_ref[...].astype(BF16)

    @pl.when(c < nact_ref[0])
    def _():
        x = x_ref[...]
        a = jnp.dot(x, w1b_ref[...], preferred_element_type=F32)
        bb = jnp.dot(x, w3b_ref[...], preferred_element_type=F32)
        o_ref[...] = (_silu(a) * bb).astype(BF16)

    @pl.when(c >= nact_ref[0])
    def _():
        o_ref[...] = jnp.zeros_like(o_ref)


def _ffn_up(x, w1, w3, chunk_expert, n_active, rows):
    p, d = x.shape
    f = w1.shape[2]
    tf = _divisor_tile(f, 512, LANES)
    grid_spec = pltpu.PrefetchScalarGridSpec(
        num_scalar_prefetch=2,
        grid=(f // tf, p // rows),
        in_specs=[
            pl.BlockSpec((rows, d), lambda j, c, e, na: (c, 0)),
            pl.BlockSpec((None, d, tf), lambda j, c, e, na: (e[c], 0, j)),
            pl.BlockSpec((None, d, tf), lambda j, c, e, na: (e[c], 0, j)),
        ],
        out_specs=pl.BlockSpec((rows, tf), lambda j, c, e, na: (c, j)),
        scratch_shapes=[pltpu.VMEM((d, tf), BF16), pltpu.VMEM((d, tf), BF16)],
    )
    return pl.pallas_call(
        _ffn_up_kernel,
        grid_spec=grid_spec,
        out_shape=jax.ShapeDtypeStruct((p, f), BF16),
        compiler_params=_cparams("arbitrary", "arbitrary"),
        name="ffn_up",
    )(chunk_expert, n_active, x, w1, w3)


def _dense_ffn(hb, h, w1, w3, w2, g, b, alpha, tm):
    n = hb.shape[0]
    nchunks = n // tm
    hmid = _ffn_up(hb, w1[None], w3[None], jnp.zeros((nchunks,), jnp.int32),
                   jnp.full((1,), nchunks, jnp.int32), tm)
    return _mm_res_ln(hmid, w2, h, g, b, alpha, tm)


def _conv_proj_kernel(x_ref, wb_ref, wc_ref, wh_ref, cw_ref, cb_ref, o_ref,
                      wbb_ref, wcb_ref, whb_ref, carry_ref):
    bi = pl.program_id(1)
    ti = pl.program_id(2)

    @pl.when(jnp.logical_and(bi == 0, ti == 0))
    def _():
        wbb_ref[...] = wb_ref[...].astype(BF16)
        wcb_ref[...] = wc_ref[...].astype(BF16)
        whb_ref[...] = wh_ref[...].astype(BF16)

    @pl.when(ti == 0)
    def _():
        carry_ref[...] = jnp.zeros_like(carry_ref)

    x = x_ref[...]
    gate_b = jnp.dot(x, wbb_ref[...], preferred_element_type=F32)
    gate_c = jnp.dot(x, wcb_ref[...], preferred_element_type=F32)
    hh = jnp.dot(x, whb_ref[...], preferred_element_type=F32)
    u = gate_c * hh
    tm = u.shape[0]
    prev1 = carry_ref[7:8, :]
    prev2 = carry_ref[6:7, :]
    row = lax.broadcasted_iota(jnp.int32, (tm, 1), 0)
    r1 = jnp.where(row == 0, prev1, pltpu.roll(u, 1, axis=0))
    r2 = jnp.where(row == 0, prev2, jnp.where(row == 1, prev1, pltpu.roll(u, 2, axis=0)))
    v = cw_ref[0:1, :] * r2 + cw_ref[1:2, :] * r1 + cw_ref[2:3, :] * u + cb_ref[...]
    carry_ref[...] = u[tm - 8:, :]
    o_ref[...] = (gate_b * v).astype(BF16)


def _conv_proj(hb, w_in, conv_w, conv_b, batch, tm):
    n, d = hb.shape
    tn = _divisor_tile(d, 512, LANES)
    nd = d // tn
    tpb = n // batch // tm
    return pl.pallas_call(
        _conv_proj_kernel,
        grid=(nd, batch, tpb),
        in_specs=[
            pl.BlockSpec((tm, d), lambda j, bi, ti: (bi * tpb + ti, 0)),
            pl.BlockSpec((d, tn), lambda j, bi, ti: (0, j)),
            pl.BlockSpec((d, tn), lambda j, bi, ti: (0, nd + j)),
            pl.BlockSpec((d, tn), lambda j, bi, ti: (0, 2 * nd + j)),
            pl.BlockSpec((3, tn), lambda j, bi, ti: (0, j)),
            pl.BlockSpec((1, tn), lambda j, bi, ti: (0, j)),
        ],
        out_specs=pl.BlockSpec((tm, tn), lambda j, bi, ti: (bi * tpb + ti, j)),
        out_shape=jax.ShapeDtypeStruct((n, d), BF16),
        scratch_shapes=[pltpu.VMEM((d, tn), BF16)] * 3 + [pltpu.VMEM((8, tn), F32)],
        compiler_params=_cparams("arbitrary", "arbitrary", "arbitrary"),
        name="conv_proj",
    )(hb, w_in, w_in, w_in, conv_w, conv_b.reshape(1, d))


def _conv_mixer_layer(h, hb, p, g, b, alpha, batch, tm):
    z = _conv_proj(hb, p["conv_w_in"], p["conv_w"], p["conv_b"], batch, tm)
    return _mm_res_ln(z, p["conv_w_out"], h, g, b, alpha, tm)


MOE_ROWS = 128


def _split_bf16(a):
    hi = a.astype(BF16)
    lo = (a - hi.astype(F32)).astype(BF16)
    return hi, lo


def _dot_f32(a, b):
    ah, al = _split_bf16(a)
    bh, bl = _split_bf16(b)
    return (jnp.dot(ah, bh, preferred_element_type=F32)
            + (jnp.dot(ah, bl, preferred_element_type=F32)
               + jnp.dot(al, bh, preferred_element_type=F32)))


def _router_kernel(h_ref, w_ref, b_ref, info_ref, cnt_ref, carry_ref):
    i = pl.program_id(0)

    @pl.when(i == 0)
    def _():
        carry_ref[...] = jnp.zeros_like(carry_ref)

    logits = _dot_f32(h_ref[...], w_ref[...]) + b_ref[...]
    tm, ne = logits.shape
    lane = lax.broadcasted_iota(jnp.int32, (tm, ne), 1)
    m1 = jnp.max(logits, axis=-1, keepdims=True)
    i1 = jnp.min(jnp.where(logits == m1, lane, ne), axis=-1, keepdims=True)
    mask1 = lane == i1
    rest = jnp.where(mask1, -jnp.inf, logits)
    m2 = jnp.max(rest, axis=-1, keepdims=True)
    i2 = jnp.min(jnp.where(rest == m2, lane, ne), axis=-1, keepdims=True)
    mask2 = lane == i2
    dd = jnp.exp(m2 - m1)
    g1 = 1.0 / (1.0 + dd)
    g2 = dd / (1.0 + dd)
    sel = jnp.where(jnp.logical_or(mask1, mask2), 1.0, 0.0)
    r_i = lax.broadcasted_iota(jnp.int32, (tm, tm), 0)
    c_i = lax.broadcasted_iota(jnp.int32, (tm, tm), 1)
    tril = jnp.where(c_i < r_i, 1.0, 0.0).astype(BF16)
    rank = jnp.dot(tril, sel.astype(BF16), preferred_element_type=F32) + carry_ref[...]
    r1 = jnp.sum(jnp.where(mask1, rank, 0.0), axis=-1, keepdims=True)
    r2 = jnp.sum(jnp.where(mask2, rank, 0.0), axis=-1, keepdims=True)
    info = jnp.where(lane == 0, i1.astype(F32),
           jnp.where(lane == 1, i2.astype(F32),
           jnp.where(lane == 2, g1,
           jnp.where(lane == 3, g2,
           jnp.where(lane == 4, r1,
           jnp.where(lane == 5, r2, 0.0))))))
    info_ref[...] = info
    total = carry_ref[...] + jnp.sum(sel, axis=0, keepdims=True)
    carry_ref[...] = total
    cnt_ref[...] = total


def _router(h, w, b, tm):
    n, d = h.shape
    ne = w.shape[1]
    assert ne >= 6
    return pl.pallas_call(
        _router_kernel,
        grid=(n // tm,),
        in_specs=[
            pl.BlockSpec((tm, d), lambda i: (i, 0)),
            pl.BlockSpec((d, ne), lambda i: (0, 0)),
            pl.BlockSpec((1, ne), lambda i: (0, 0)),
        ],
        out_specs=[
            pl.BlockSpec((tm, ne), lambda i: (i, 0)),
            pl.BlockSpec((1, ne), lambda i: (0, 0)),
        ],
        out_shape=[jax.ShapeDtypeStruct((n, ne), F32), jax.ShapeDtypeStruct((1, ne), F32)],
        scratch_shapes=[pltpu.VMEM((1, ne), F32)],
        compiler_params=_cparams("arbitrary"),
        name="moe_router",
    )(h, w, b.reshape(1, ne))


def _row_copy(src_hbm, row, dst_vmem, r, sem):
    return pltpu.make_async_copy(src_hbm.at[pl.ds(row, 1), :], dst_vmem.at[pl.ds(r, 1), :], sem)


def _rows_wait(src_hbm, dst_vmem, sem):
    pltpu.make_async_copy(src_hbm.at[pl.ds(0, dst_vmem.shape[0]), :], dst_vmem, sem).wait()


def _moe_gather_kernel(src_ref, nact_ref, h_ref, o_ref, buf_ref, sem):
    c = pl.program_id(0)
    rows = buf_ref.shape[1]

    def issue(chunk):
        slot = chunk % 2

        def start(r, carry):
            _row_copy(h_ref, src_ref[chunk * rows + r], buf_ref.at[slot], r, sem.at[slot]).start()
            return carry

        lax.fori_loop(0, rows, start, 0, unroll=8)

    @pl.when(jnp.logical_and(c == 0, nact_ref[0] > 0))
    def _():
        issue(c)

    @pl.when(c + 1 < nact_ref[0])
    def _():
        issue(c + 1)

    @pl.when(c < nact_ref[0])
    def _():
        slot = c % 2
        _rows_wait(h_ref, buf_ref.at[slot], sem.at[slot])
        o_ref[...] = buf_ref[slot].astype(BF16)

    @pl.when(c >= nact_ref[0])
    def _():
        o_ref[...] = jnp.zeros_like(o_ref)


def _moe_gather(h, src, n_active, p, rows):
    n, d = h.shape
    grid_spec = pltpu.PrefetchScalarGridSpec(
        num_scalar_prefetch=2,
        grid=(p // rows,),
        in_specs=[pl.BlockSpec(memory_space=pl.ANY)],
        out_specs=pl.BlockSpec((rows, d), lambda c, s, na: (c, 0)),
        scratch_shapes=[pltpu.VMEM((2, rows, d), F32), pltpu.SemaphoreType.DMA((2,))],
    )
    return pl.pallas_call(
        _moe_gather_kernel,
        grid_spec=grid_spec,
        out_shape=jax.ShapeDtypeStruct((p, d), BF16),
        compiler_params=_cparams("arbitrary"),
        name="moe_gather",
    )(src, n_active, h)


def _moe_combine_kernel(p1_ref, p2_ref, y_ref, h_ref, info_ref, g_ref, b_ref, o_ref, ob_ref,
                        buf1_ref, buf2_ref, sem, *, alpha):
    i = pl.program_id(0)
    tm = buf1_ref.shape[1]

    def issue(tile):
        slot = tile % 2

        def start(r, carry):
            _row_copy(y_ref, p1_ref[tile * tm + r], buf1_ref.at[slot], r, sem.at[slot]).start()
            _row_copy(y_ref, p2_ref[tile * tm + r], buf2_ref.at[slot], r, sem.at[slot]).start()
            return carry

        lax.fori_loop(0, tm, start, 0, unroll=8)

    @pl.when(i == 0)
    def _():
        issue(i)

    @pl.when(i + 1 < pl.num_programs(0))
    def _():
        issue(i + 1)

    slot = i % 2
    _rows_wait(y_ref, buf1_ref.at[slot], sem.at[slot])
    _rows_wait(y_ref, buf2_ref.at[slot], sem.at[slot])
    info = info_ref[...]
    y = alpha * h_ref[...] + (info[:, 2:3] * buf1_ref[slot] + info[:, 3:4] * buf2_ref[slot])
    y = _layer_norm(y, g_ref[...], b_ref[...])
    o_ref[...] = y
    ob_ref[...] = y.astype(BF16)


def _moe_combine(y, h, info, p1, p2, g, b, alpha, tm):
    n, d = h.shape
    ne = info.shape[1]
    grid_spec = pltpu.PrefetchScalarGridSpec(
        num_scalar_prefetch=2,
        grid=(n // tm,),
        in_specs=[
            pl.BlockSpec(memory_space=pl.ANY),
            pl.BlockSpec((tm, d), lambda i, a, c: (i, 0)),
            pl.BlockSpec((tm, ne), lambda i, a, c: (i, 0)),
            pl.BlockSpec((1, d), lambda i, a, c: (0, 0)),
            pl.BlockSpec((1, d), lambda i, a, c: (0, 0)),
        ],
        out_specs=[
            pl.BlockSpec((tm, d), lambda i, a, c: (i, 0)),
            pl.BlockSpec((tm, d), lambda i, a, c: (i, 0)),
        ],
        scratch_shapes=[pltpu.VMEM((2, tm, d), F32), pltpu.VMEM((2, tm, d), F32),
                        pltpu.SemaphoreType.DMA((2,))],
    )
    return pl.pallas_call(
        functools.partial(_moe_combine_kernel, alpha=alpha),
        grid_spec=grid_spec,
        out_shape=[jax.ShapeDtypeStruct((n, d), F32), jax.ShapeDtypeStruct((n, d), BF16)],
        compiler_params=_cparams("arbitrary"),
        name="moe_combine",
    )(p1, p2, y, h, info, g.reshape(1, d), b.reshape(1, d))


MOE_PASS_BLOCKS = 18
MOE_F_TILE = 256
MOE_BLOCK_UNITS = 8


def _expert_ffn_kernel(pe_ref, ps_ref, pn_ref, nu_ref, xs_ref, w1_ref, w3_ref, w2_ref, y_ref,
                       x_buf, acc_ref, w1b_ref, w3b_ref, w2b_ref, sem):
    p = pl.program_id(0)
    f = pl.program_id(1)
    nf = pl.num_programs(1)
    nb = pn_ref[p]
    sb = MOE_ROWS
    start = pl.multiple_of(ps_ref[p], MOE_ROWS)
    nblock = nb // MOE_BLOCK_UNITS

    @pl.when(nb > 0)
    def _():
        @pl.when(f == 0)
        def _():
            cp = pltpu.make_async_copy(xs_ref.at[pl.ds(start, x_buf.shape[0]), :], x_buf, sem.at[0])
            cp.start()
            cp.wait()

        w1b_ref[...] = w1_ref[...].astype(BF16)
        w3b_ref[...] = w3_ref[...].astype(BF16)
        w2b_ref[...] = w2_ref[...].astype(BF16)

        def up(unit, nunit):
            x = x_buf[pl.ds(pl.multiple_of(unit * sb, sb), nunit * sb), :]
            a = jnp.dot(x, w1b_ref[...], preferred_element_type=F32)
            bb = jnp.dot(x, w3b_ref[...], preferred_element_type=F32)
            return (_silu(a) * bb).astype(BF16)

        def out_copy(i):
            rows = pl.ds(pl.multiple_of(i * sb, sb), sb)
            dst = pl.ds(pl.multiple_of(start + i * sb, sb), sb)
            return pltpu.make_async_copy(acc_ref.at[rows, :], y_ref.at[dst, :], sem.at[1])

        def down(unit, nunit, hmid, first):
            rows = pl.ds(pl.multiple_of(unit * sb, sb), nunit * sb)
            part = jnp.dot(hmid, w2b_ref[...], preferred_element_type=F32)
            if first:
                acc_ref[rows, :] = part
            else:
                acc_ref[rows, :] += part

            @pl.when(f == nf - 1)
            def _():
                for u in range(nunit):
                    out_copy(unit + u).start()

        def sweep(first):
            bu = MOE_BLOCK_UNITS

            @pl.when(nblock > 0)
            def _():
                def body(i, hprev):
                    down(bu * (i - 1), bu, hprev, first)
                    return up(bu * i, bu)

                down(bu * (nblock - 1), bu, lax.fori_loop(1, nblock, body, up(0, bu)), first)

            done = nblock * bu
            size = bu // 2
            while size >= 1:
                @pl.when((nb - done) & size != 0)
                def _(done=done, size=size):
                    down(done, size, up(done, size), first)

                done = done + ((nb - done) & size)
                size //= 2

        @pl.when(f == 0)
        def _():
            sweep(True)

        @pl.when(f > 0)
        def _():
            sweep(False)

        @pl.when(f == nf - 1)
        def _():
            lax.fori_loop(0, nb, lambda i, c: (out_copy(i).wait(), c)[1], 0)

    @pl.when(jnp.logical_and(p == pl.num_programs(0) - 1, f == nf - 1))
    def _():
        acc_ref[0:sb, :] = jnp.zeros((sb, acc_ref.shape[1]), F32)

        def zero_copy(i):
            dst = pl.ds(pl.multiple_of(i * sb, sb), sb)
            return pltpu.make_async_copy(acc_ref.at[0:sb, :], y_ref.at[dst, :], sem.at[1])

        n_blocks = y_ref.shape[0] // sb
        lax.fori_loop(nu_ref[1], n_blocks, lambda i, c: (zero_copy(i).start(), c)[1], 0)
        lax.fori_loop(nu_ref[1], n_blocks, lambda i, c: (zero_copy(i).wait(), c)[1], 0)


def _expert_ffn(xs, w1, w3, w2, pass_expert, pass_start, pass_nb, n_used, p_rows):
    d = xs.shape[1]
    f = w1.shape[2]
    tf = _divisor_tile(f, MOE_F_TILE, LANES)
    nf = f // tf
    npass = pass_expert.shape[0]
    r = MOE_PASS_BLOCKS * MOE_ROWS

    def fidx(p, j, nu):
        return jnp.where(p < nu[0], j, nf - 1)

    grid_spec = pltpu.PrefetchScalarGridSpec(
        num_scalar_prefetch=4,
        grid=(npass, nf),
        in_specs=[
            pl.BlockSpec(memory_space=pl.ANY),
            pl.BlockSpec((None, d, tf), lambda p, j, pe, ps, pn, nu: (pe[p], 0, fidx(p, j, nu))),
            pl.BlockSpec((None, d, tf), lambda p, j, pe, ps, pn, nu: (pe[p], 0, fidx(p, j, nu))),
            pl.BlockSpec((None, tf, d), lambda p, j, pe, ps, pn, nu: (pe[p], fidx(p, j, nu), 0)),
        ],
        out_specs=pl.BlockSpec(memory_space=pl.ANY),
        scratch_shapes=[
            pltpu.VMEM((r, d), BF16), pltpu.VMEM((r, d), F32),
            pltpu.VMEM((d, tf), BF16), pltpu.VMEM((d, tf), BF16), pltpu.VMEM((tf, d), BF16),
            pltpu.SemaphoreType.DMA((2,)),
        ],
    )
    return pl.pallas_call(
        _expert_ffn_kernel,
        grid_spec=grid_spec,
        out_shape=jax.ShapeDtypeStruct((p_rows, d), F32),
        compiler_params=_cparams("arbitrary", "arbitrary"),
        name="expert_ffn",
    )(pass_expert, pass_start, pass_nb, n_used, xs, w1, w3, w2)


def _moe_ffn(h, hb, router_w, router_b, w1, w3, w2, g, b, alpha, tm):
    n, d = h.shape
    ne = router_w.shape[1]
    rows = MOE_ROWS
    info, counts = _router(h, router_w, router_b, tm)
    counts = counts[0].astype(jnp.int32)
    nblk_e = (counts + rows - 1) // rows
    blk_end = jnp.cumsum(nblk_e)
    starts = (blk_end - nblk_e) * rows
    n_blocks = (n * TOP_K + ne * (rows - 1)) // rows
    p = n_blocks * rows
    n_active = blk_end[-1:].astype(jnp.int32)
    npass_e = (nblk_e + MOE_PASS_BLOCKS - 1) // MOE_PASS_BLOCKS
    pass_end = jnp.cumsum(npass_e)
    max_pass = n_blocks // MOE_PASS_BLOCKS + ne
    pidx = jnp.arange(max_pass, dtype=jnp.int32)
    n_used = jnp.stack([pass_end[-1], blk_end[-1]]).astype(jnp.int32)
    last_expert = jnp.sum(pass_end < pass_end[-1]).astype(jnp.int32)
    pass_expert = jnp.minimum(jnp.sum(pidx[:, None] >= pass_end[None, :], axis=1), last_expert).astype(jnp.int32)
    local = pidx - (pass_end - npass_e)[pass_expert]
    pass_start = (starts[pass_expert] + local * (MOE_PASS_BLOCKS * rows)).astype(jnp.int32)
    pass_nb = jnp.where(pidx < n_used[0],
                        jnp.clip(nblk_e[pass_expert] - local * MOE_PASS_BLOCKS, 0, MOE_PASS_BLOCKS),
                        0).astype(jnp.int32)
    pass_start = jnp.where(pass_nb > 0, pass_start, 0).astype(jnp.int32)
    i1 = info[:, 0].astype(jnp.int32)
    i2 = info[:, 1].astype(jnp.int32)
    p1 = starts[i1] + info[:, 4].astype(jnp.int32)
    p2 = starts[i2] + info[:, 5].astype(jnp.int32)
    tok = jnp.arange(n, dtype=jnp.int32)
    p_in = p + MOE_PASS_BLOCKS * rows
    src = jnp.zeros((p_in,), jnp.int32).at[jnp.concatenate([p1, p2])].set(jnp.concatenate([tok, tok]))
    xs = _moe_gather(h, src, n_active, p_in, rows)
    y = _expert_ffn(xs, w1, w3, w2, pass_expert, pass_start, pass_nb, n_used, p)
    tmc = _divisor_tile(n, 384, 16)
    return _moe_combine(y, h, info, p1, p2, g, b, alpha, tmc)


def _proj_kernel(*refs, epilogue, n_extra, n_out):
    x_ref, w_ref = refs[0], refs[1]
    extra = refs[2:2 + n_extra]
    outs = refs[2 + n_extra:2 + n_extra + n_out]
    wb_ref = refs[2 + n_extra + n_out]

    @pl.when(pl.program_id(1) == 0)
    def _():
        wb_ref[...] = w_ref[...].astype(BF16)

    y = jnp.dot(x_ref[...], wb_ref[...], preferred_element_type=F32)
    res = epilogue(y, *[e[...] for e in extra])
    for o_ref, r in zip(outs, res):
        o_ref[...] = r.astype(o_ref.dtype)


def _proj(xb, w, col0, ncols, epilogue, extras, out_dtypes, tm, name, tn_cap=1024):
    n, kdim = xb.shape
    tn = _divisor_tile(ncols, tn_cap, LANES)
    assert col0 % tn == 0
    off = col0 // tn
    outs = pl.pallas_call(
        functools.partial(_proj_kernel, epilogue=epilogue, n_extra=len(extras), n_out=len(out_dtypes)),
        grid=(ncols // tn, n // tm),
        in_specs=[
            pl.BlockSpec((tm, kdim), lambda j, m: (m, 0)),
            pl.BlockSpec((kdim, tn), lambda j, m: (0, off + j)),
        ] + [pl.BlockSpec((1, tn), lambda j, m: (0, j))] * len(extras),
        out_specs=[pl.BlockSpec((tm, tn), lambda j, m: (m, j))] * len(out_dtypes),
        out_shape=[jax.ShapeDtypeStruct((n, ncols), dt) for dt in out_dtypes],
        scratch_shapes=[pltpu.VMEM((kdim, tn), BF16)],
        compiler_params=_cparams("arbitrary", "arbitrary"),
        name=name,
    )(xb, w, *[e.reshape(1, ncols) for e in extras])
    return outs


def _tril_bf16(c, inclusive):
    r_i = lax.broadcasted_iota(jnp.int32, (c, c), 0)
    c_i = lax.broadcasted_iota(jnp.int32, (c, c), 1)
    keep = (c_i <= r_i) if inclusive else (c_i < r_i)
    return jnp.where(keep, 1.0, 0.0).astype(BF16)


def _cumsum_rows(x, tril):
    hi = x.astype(BF16)
    r1 = x - hi.astype(F32)
    mid = r1.astype(BF16)
    lo = (r1 - mid.astype(F32)).astype(BF16)
    return (jnp.dot(tril, hi, preferred_element_type=F32)
            + (jnp.dot(tril, mid, preferred_element_type=F32)
               + jnp.dot(tril, lo, preferred_element_type=F32)))


def _dot_nt(a, b):
    return lax.dot_general(a.astype(BF16), b.astype(BF16), (((1,), (1,)), ((), ())),
                           preferred_element_type=F32)


def _dot_tn(a, b):
    return lax.dot_general(a.astype(BF16), b.astype(BF16), (((0,), (0,)), ((), ())),
                           preferred_element_type=F32)


def _dot_nn(a, b):
    return jnp.dot(a.astype(BF16), b.astype(BF16), preferred_element_type=F32)


def _chunk_len(t):
    return _divisor_tile(t, 64, 16)


def _round_robin(gens):
    done = [None] * len(gens)
    while any(d is None for d in done):
        for j, gen in enumerate(gens):
            if done[j] is None:
                done[j] = next(gen)
    return done


HGRN_SUB = 16


def _hgrn_scan_kernel(q_ref, lf_ref, v_ref, gs_ref, ng_ref, o_ref, st_ref, *, chunk):
    t, w = q_ref.shape
    nhead = w // HGRN_HEAD
    nsub = chunk // HGRN_SUB
    st_ref[...] = jnp.zeros_like(st_ref)
    tril = _tril_bf16(chunk, True)
    row16 = lax.broadcasted_iota(jnp.int32, (HGRN_SUB, 1), 0)

    def head_chunk(q, lf, v, st):
        k = 1.0 - jnp.exp(lf)
        cum = _cumsum_rows(lf, tril)
        yield None
        o_inter = _dot_nt(q * jnp.exp(cum), st)
        vb = v.astype(BF16)
        cl = cum[chunk - 1:chunk]
        kd = k * jnp.exp(cl - cum)
        st_new = st * jnp.exp(cl) + _dot_tn(v, kd)
        yield None
        outs = []
        for i in range(nsub):
            lo, hi = i * HGRN_SUB, (i + 1) * HGRN_SUB
            qi, ki, vi, cumi = q[lo:hi], k[lo:hi], v[lo:hi], cum[lo:hi]
            oi = o_inter[lo:hi]
            if i > 0:
                ci = cum[lo:lo + 1]
                qt = qi * jnp.exp(cumi - ci)
                kt = k[0:lo] * jnp.exp(ci - cum[0:lo])
                oi = oi + jnp.dot(_dot_nt(qt, kt).astype(BF16), vb[0:lo], preferred_element_type=F32)
            for s in range(HGRN_SUB):
                dec = jnp.exp(jnp.minimum(cumi - cumi[s:s + 1], 0.0))
                col = jnp.sum(qi * dec * ki[s:s + 1], axis=-1, keepdims=True)
                col = jnp.where(row16 >= s, col, 0.0)
                oi = oi + col * vi[s:s + 1]
                if s % 4 == 3:
                    yield None
            outs.append(oi)
        o = jnp.concatenate(outs, axis=0)
        o = o * lax.rsqrt(jnp.mean(o * o, axis=-1, keepdims=True) + RMS_EPS)
        yield o, st_new

    def body(c, carry):
        rows = pl.ds(pl.multiple_of(c * chunk, 16), chunk)
        q = q_ref[rows, :].astype(F32)
        lf = lf_ref[rows, :]
        v = v_ref[rows, :].astype(F32)
        heads = []
        for hh in range(nhead):
            cols = slice(hh * HGRN_HEAD, (hh + 1) * HGRN_HEAD)
            heads.append(head_chunk(q[:, cols], lf[:, cols], v[:, cols], st_ref[hh]))
        outs = []
        for hh, (o, st_new) in enumerate(_round_robin(heads)):
            st_ref[hh] = st_new
            outs.append(o)
        o = jnp.concatenate(outs, axis=1)
        o_ref[rows, :] = (o * ng_ref[...] * gs_ref[rows, :].astype(F32)).astype(BF16)
        return carry

    lax.fori_loop(0, t // chunk, body, 0)


HGRN_GROUP = 8


def _hgrn_scan(q, lf, v, gs, norm_g, batch):
    n, d = q.shape
    t = n // batch
    chunk = _chunk_len(t)
    w = min(d, HGRN_GROUP * HGRN_HEAD)
    blk = pl.BlockSpec((t, w), lambda b, j: (b, j))
    return pl.pallas_call(
        functools.partial(_hgrn_scan_kernel, chunk=chunk),
        grid=(batch, d // w),
        in_specs=[blk, blk, blk, blk, pl.BlockSpec((1, w), lambda b, j: (0, j))],
        out_specs=blk,
        out_shape=jax.ShapeDtypeStruct((n, d), BF16),
        scratch_shapes=[pltpu.VMEM((w // HGRN_HEAD, HGRN_HEAD, HGRN_HEAD), F32)],
        compiler_params=_cparams("parallel", "parallel"),
        name="hgrn_scan",
    )(q, lf, v, gs, norm_g.reshape(1, d))


def _hgrn_mixer_layer(h, hb, p, layer_idx, g, b, alpha, batch, tm):
    d = h.shape[1]
    w_in = p["hgrn_w_in"]
    lb = jnp.cumsum(jax.nn.softmax(p["hgrn_lb"].astype(F32), axis=0), axis=0)
    lb = lb[layer_idx] - lb[0]
    (q,) = _proj(hb, w_in, 0, d, lambda y: (_silu(y),), [], [BF16], tm, "hgrn_proj_q")
    (lf,) = _proj(hb, w_in, d, d, lambda y, lbv: (jnp.log(lbv + (1.0 - lbv) * _sigmoid(y)),),
                  [lb], [F32], tm, "hgrn_proj_f")
    (v,) = _proj(hb, w_in, 2 * d, d, lambda y: (y,), [], [BF16], tm, "hgrn_proj_i")
    (gs,) = _proj(hb, w_in, 3 * d, d, lambda y: (_silu(y),), [], [BF16], tm, "hgrn_proj_g")
    z = _hgrn_scan(q, lf, v, gs, p["hgrn_norm_g"], batch)
    return _mm_res_ln(z, p["hgrn_w_out"], h, g, b, alpha, tm)


def _fox_gate_kernel(h_ref, w_ref, bf_ref, c_ref, carry_ref):
    @pl.when(pl.program_id(1) == 0)
    def _():
        carry_ref[...] = jnp.zeros_like(carry_ref)

    x = _dot_f32(h_ref[...], w_ref[...]) + bf_ref[...]
    log_f = jnp.minimum(x, 0.0) - jnp.log(1.0 + jnp.exp(-jnp.abs(x)))
    tm = x.shape[0]
    c = _cumsum_rows(log_f, _tril_bf16(tm, True)) + carry_ref[...]
    c_ref[...] = c
    carry_ref[...] = c[tm - 1:tm, :]


def _fox_gate(h, w_f, b_f, batch, tm):
    n, d = h.shape
    nh = w_f.shape[1]
    tpb = n // batch // tm
    return pl.pallas_call(
        _fox_gate_kernel,
        grid=(batch, tpb),
        in_specs=[
            pl.BlockSpec((tm, d), lambda b, t: (b * tpb + t, 0)),
            pl.BlockSpec((d, nh), lambda b, t: (0, 0)),
            pl.BlockSpec((1, nh), lambda b, t: (0, 0)),
        ],
        out_specs=pl.BlockSpec((tm, nh), lambda b, t: (b * tpb + t, 0)),
        out_shape=jax.ShapeDtypeStruct((n, nh), F32),
        scratch_shapes=[pltpu.VMEM((1, nh), F32)],
        compiler_params=_cparams("arbitrary", "arbitrary"),
        name="fox_gate",
    )(h, w_f, b_f.reshape(1, nh))


def _fox_attn_kernel(q_ref, k_ref, v_ref, sg_ref, c_ref, ct_ref, o_ref, *, tq):
    hd = pl.program_id(1)
    t = q_ref.shape[0]
    nh = c_ref.shape[1]
    lane = lax.broadcasted_iota(jnp.int32, (t, nh), 1)
    c_col = jnp.sum(jnp.where(lane == hd, c_ref[...], 0.0), axis=-1, keepdims=True)
    c_row = ct_ref[pl.ds(hd, 1), :]
    def query_tile(i):
        lo, hi = i * tq, (i + 1) * tq
        s = lax.dot_general(q_ref[lo:hi, :], k_ref[0:hi, :], (((1,), (1,)), ((), ())),
                            preferred_element_type=F32)
        yield None
        s = s + c_col[lo:hi] - c_row[:, 0:hi]
        r_i = lax.broadcasted_iota(jnp.int32, (tq, hi), 0) + lo
        c_i = lax.broadcasted_iota(jnp.int32, (tq, hi), 1)
        s = jnp.where(c_i <= r_i, s, -jnp.inf)
        m = jnp.max(s, axis=-1, keepdims=True)
        p = jnp.exp(s - m)
        l = jnp.sum(p, axis=-1, keepdims=True)
        yield None
        o = jnp.dot(p.astype(BF16), v_ref[0:hi, :], preferred_element_type=F32) / l
        o_ref[lo:hi, :] = (o * sg_ref[lo:hi, :]).astype(BF16)
        yield True

    _round_robin([query_tile(i) for i in range(t // tq)])


def _fox_attn(q, k, v, sg, c, ct, batch):
    n, d = q.shape
    t = n // batch
    nh = d // FOX_HEAD
    tq = _divisor_tile(t, 768, 16)
    blk = pl.BlockSpec((t, FOX_HEAD), lambda b, h: (b, h))
    return pl.pallas_call(
        functools.partial(_fox_attn_kernel, tq=tq),
        grid=(batch, nh),
        in_specs=[blk, blk, blk, blk,
                  pl.BlockSpec((t, nh), lambda b, h: (b, 0)),
                  pl.BlockSpec((None, nh, t), lambda b, h: (b, 0, 0))],
        out_specs=blk,
        out_shape=jax.ShapeDtypeStruct((n, d), BF16),
        compiler_params=_cparams("parallel", "parallel"),
        name="fox_attn",
    )(q, k, v, sg, c, ct)


def _head_rms_epilogue(scale):
    def epi(y, gain):
        outs = []
        for j in range(y.shape[1] // FOX_HEAD):
            yj = y[:, j * FOX_HEAD:(j + 1) * FOX_HEAD]
            yj = yj * lax.rsqrt(jnp.mean(yj * yj, axis=-1, keepdims=True) + RMS_EPS)
            outs.append(yj * gain[:, j * FOX_HEAD:(j + 1) * FOX_HEAD] * scale)
        return (jnp.concatenate(outs, axis=1),)
    return epi


def _fox_mixer_layer(h, hb, p, g, b, alpha, batch, tm):
    n, d = h.shape
    nh = d // FOX_HEAD
    w_in = p["fox_w_in"]
    qg = jnp.tile(p["fox_q_norm_g"], nh)
    kg = jnp.tile(p["fox_k_norm_g"], nh)
    (q,) = _proj(hb, w_in, 0, d, _head_rms_epilogue(FOX_HEAD ** -0.5), [qg], [BF16], tm, "fox_proj_q")
    (k,) = _proj(hb, w_in, d, d, _head_rms_epilogue(1.0), [kg], [BF16], tm, "fox_proj_k")
    (v,) = _proj(hb, w_in, 2 * d, d, lambda y: (y,), [], [BF16], tm, "fox_proj_v")
    (sg,) = _proj(hb, w_in, 3 * d, d, lambda y: (_sigmoid(y),), [], [F32], tm, "fox_proj_g")
    c = _fox_gate(h, w_in[:, 4 * d:], p["fox_b_f"], batch, tm)
    ct = c.reshape(batch, n // batch, nh).transpose(0, 2, 1)
    z = _fox_attn(q, k, v, sg, c, ct, batch)
    return _mm_res_ln(z, p["fox_w_out"], h, g, b, alpha, tm)


def _rwkv_mix_kernel(h_ref, mu_ref, *refs):
    outs, carry_ref = refs[:-1], refs[-1]

    @pl.when(pl.program_id(1) == 0)
    def _():
        carry_ref[...] = jnp.zeros_like(carry_ref)

    x = h_ref[...]
    tm = x.shape[0]
    row = lax.broadcasted_iota(jnp.int32, (tm, 1), 0)
    prev = jnp.where(row == 0, carry_ref[7:8, :], pltpu.roll(x, 1, axis=0))
    xx = prev - x
    carry_ref[...] = x[tm - 8:, :]
    for j, o_ref in enumerate(outs):
        o_ref[...] = (x + xx * mu_ref[j:j + 1, :]).astype(BF16)


def _rwkv_mix(h, mu, batch, tm):
    n, d = h.shape
    nmix = mu.shape[0]
    tpb = n // batch // tm
    blk = pl.BlockSpec((tm, d), lambda b, t: (b * tpb + t, 0))
    return pl.pallas_call(
        _rwkv_mix_kernel,
        grid=(batch, tpb),
        in_specs=[blk, pl.BlockSpec((nmix, d), lambda b, t: (0, 0))],
        out_specs=[blk] * nmix,
        out_shape=[jax.ShapeDtypeStruct((n, d), BF16)] * nmix,
        scratch_shapes=[pltpu.VMEM((8, d), F32)],
        compiler_params=_cparams("arbitrary", "arbitrary"),
        name="rwkv_mix",
    )(h, mu)


def _lora_kernel(x_ref, wa_ref, wb_ref, bias_ref, o_ref, wab_ref, wbb_ref, *, mid_act, out_act):
    @pl.when(pl.program_id(0) == 0)
    def _():
        wab_ref[...] = wa_ref[...].astype(BF16)
        wbb_ref[...] = wb_ref[...].astype(BF16)

    mid = mid_act(jnp.dot(x_ref[...], wab_ref[...], preferred_element_type=F32))
    y = jnp.dot(mid.astype(BF16), wbb_ref[...], preferred_element_type=F32)
    o_ref[...] = out_act(bias_ref[...] + y)


def _lora(xb, wa, wb, bias, mid_act, out_act, tm, name):
    n, d = xb.shape
    r = wa.shape[1]
    dout = wb.shape[1]
    return pl.pallas_call(
        functools.partial(_lora_kernel, mid_act=mid_act, out_act=out_act),
        grid=(n // tm,),
        in_specs=[
            pl.BlockSpec((tm, d), lambda i: (i, 0)),
            pl.BlockSpec((d, r), lambda i: (0, 0)),
            pl.BlockSpec((r, dout), lambda i: (0, 0)),
            pl.BlockSpec((1, dout), lambda i: (0, 0)),
        ],
        out_specs=pl.BlockSpec((tm, dout), lambda i: (i, 0)),
        out_shape=jax.ShapeDtypeStruct((n, dout), F32),
        scratch_shapes=[pltpu.VMEM((d, r), BF16), pltpu.VMEM((r, dout), BF16)],
        compiler_params=_cparams("arbitrary"),
        name=name,
    )(xb, wa, wb, bias.reshape(1, dout))


def _rwkv_log_decay(z):
    w_log = -(jnp.maximum(-z, 0.0) + jnp.log(1.0 + jnp.exp(-jnp.abs(z)))) - 0.5
    return -jnp.exp(w_log)


RWKV_GROUP = 4
RWKV_UNROLL = 8


def _seg_sum(x, bd):
    hi = x.astype(BF16)
    lo = (x - hi.astype(F32)).astype(BF16)
    return jnp.dot(hi, bd, preferred_element_type=F32) + jnp.dot(lo, bd, preferred_element_type=F32)


def _rwkv_scan_kernel(r_ref, kr_ref, v_ref, lw_ref, a_ref, g_ref, kk_p, ka_p, rk_p, gg_p, gb_p,
                      o_ref, kk_s, k_s, bonus_s, y_s, st_ref, *, chunk, ptile):
    t, w = r_ref.shape
    nhead = w // RWKV_HEAD
    sc = nhead * chunk
    lane_r = lax.broadcasted_iota(jnp.int32, (w, w), 0) // RWKV_HEAD
    lane_c = lax.broadcasted_iota(jnp.int32, (w, w), 1) // RWKV_HEAD
    bd = jnp.where(lane_r == lane_c, 1.0, 0.0).astype(BF16)

    def prologue(i, carry):
        rows = pl.ds(pl.multiple_of(i * ptile, 8), ptile)
        kr = kr_ref[rows, :]
        a = a_ref[rows, :]
        kkr = kr * kk_p[...]
        nrm = jnp.maximum(jnp.sqrt(_seg_sum(kkr * kkr, bd)), 1e-12)
        kk_s[rows, :] = kkr / nrm
        k = kr * (1.0 + (a - 1.0) * ka_p[...])
        k_s[rows, :] = k
        bonus_s[rows, :] = _seg_sum(r_ref[rows, :] * k * rk_p[...], bd) * v_ref[rows, :]
        return carry

    lax.fori_loop(0, t // ptile, prologue, 0)

    st_ref[...] = jnp.zeros_like(st_ref)
    tril = _tril_bf16(chunk, True)
    head_of_lane = lax.broadcasted_iota(jnp.int32, (chunk, w), 1) // RWKV_HEAD
    ri = lax.broadcasted_iota(jnp.int32, (2 * sc, sc), 0)
    ci = lax.broadcasted_iota(jnp.int32, (2 * sc, sc), 1)
    low_mask = ci < jnp.where(ri < sc, ri, ri - sc + 1)
    nsteps = max(1, (chunk - 1).bit_length())

    def stack(x):
        return jnp.concatenate([jnp.where(head_of_lane == hh, x, 0.0) for hh in range(nhead)], axis=0)

    eye = jnp.where(lax.broadcasted_iota(jnp.int32, (sc, sc), 0)
                    == lax.broadcasted_iota(jnp.int32, (sc, sc), 1), 1.0, 0.0)

    def prepare(c):
        start = c * chunk
        rows = pl.ds(start if isinstance(start, int) else pl.multiple_of(start, 16), chunk)
        r = r_ref[rows, :]
        v = v_ref[rows, :]
        lw = lw_ref[rows, :]
        a = a_ref[rows, :]
        kk = kk_s[rows, :]
        k = k_s[rows, :]
        cum = _cumsum_rows(lw, tril)
        e_neg = jnp.exp(-cum)
        at2 = stack(-kk * jnp.exp(cum - lw))
        rt2 = stack(r * jnp.exp(cum))
        bvec = kk * a
        bb2 = stack(bvec * e_neg)
        kb2 = stack(k * e_neg)
        v2 = stack(v)
        ar2 = jnp.concatenate([at2, rt2], axis=0).astype(BF16)
        cl = cum[chunk - 1:chunk]
        e_end = jnp.exp(cl - cum)
        khbh = jnp.concatenate([stack(k * e_end), stack(bvec * e_end)], axis=0).astype(BF16)
        yield None
        pb = jnp.where(low_mask, _dot_nt(ar2, bb2), 0.0)
        pk = jnp.where(low_mask, _dot_nt(ar2, kb2), 0.0)
        m_ab, m_rb = pb[:sc], pb[sc:]
        m_ak, m_rk = pk[:sc], pk[sc:]
        yield None
        u0 = _dot_nn(m_ak, v2)
        y0 = _dot_nn(m_rk, v2)
        tinv = eye + m_ab
        lpow = m_ab
        for _ in range(nsteps - 1):
            yield None
            lpow = _dot_nn(lpow, lpow)
            tinv = tinv + _dot_nn(tinv, lpow)
        return dict(rows=rows, ar2=ar2, tinv=tinv.astype(BF16), m_rb=m_rb.astype(BF16),
                    u0=u0, y0=y0, v2=v2.astype(BF16), khbh=khbh, decay=jnp.exp(cl))

    def chunk_steps(c, j, run):
        pc = yield from prepare(c)
        while run["turn"] != j:
            yield None
        st = run["st"]
        ps = _dot_nt(pc["ar2"], st)
        yield None
        u2 = jnp.dot(pc["tinv"], (ps[:sc] + pc["u0"]).astype(BF16), preferred_element_type=F32)
        u2b = u2.astype(BF16)
        yield None
        y2 = ps[sc:] + pc["y0"] + jnp.dot(pc["m_rb"], u2b, preferred_element_type=F32)
        y = y2[0:chunk]
        for hh in range(1, nhead):
            y = y + y2[hh * chunk:(hh + 1) * chunk]
        y_s[pc["rows"], :] = y
        run["st"] = st * pc["decay"] + _dot_tn(jnp.concatenate([pc["v2"], u2b], axis=0), pc["khbh"])
        run["turn"] = j + 1
        yield True

    def run_group(chunk_ids):
        run = dict(st=st_ref[...], turn=0)
        _round_robin([chunk_steps(c, j, run) for j, c in enumerate(chunk_ids)])
        st_ref[...] = run["st"]

    nchunks = t // chunk

    def body(i, carry):
        run_group([i * RWKV_UNROLL + j for j in range(RWKV_UNROLL)])
        return carry

    lax.fori_loop(0, nchunks // RWKV_UNROLL, body, 0)
    tail = list(range(nchunks - nchunks % RWKV_UNROLL, nchunks))
    if tail:
        run_group(tail)

    inv = 1.0 / RWKV_HEAD

    def epilogue(i, carry):
        rows = pl.ds(pl.multiple_of(i * ptile, 8), ptile)
        y = y_s[rows, :]
        mu = _seg_sum(y, bd) * inv
        yc = y - mu
        var = _seg_sum(yc * yc, bd) * inv
        yn = yc * lax.rsqrt(var + 1e-5 * RWKV_HEAD) * gg_p[...] + gb_p[...]
        o_ref[rows, :] = ((yn + bonus_s[rows, :]) * g_ref[rows, :]).astype(BF16)
        return carry

    lax.fori_loop(0, t // ptile, epilogue, 0)


def _rwkv_scan(r, kr, v, lw, a, g, p, batch):
    n, d = r.shape
    t = n // batch
    chunk = _chunk_len(t)
    w = min(d, RWKV_GROUP * RWKV_HEAD)
    ptile = _divisor_tile(t, 768, 16)
    blk = pl.BlockSpec((t, w), lambda b, j: (b, j))
    prm = pl.BlockSpec((1, w), lambda b, j: (0, j))
    params = [p["rwkv_k_k"], p["rwkv_k_a"], p["rwkv_r_k"], p["rwkv_gn_g"], p["rwkv_gn_b"]]
    return pl.pallas_call(
        functools.partial(_rwkv_scan_kernel, chunk=chunk, ptile=ptile),
        grid=(batch, d // w),
        in_specs=[blk] * 6 + [prm] * 5,
        out_specs=blk,
        out_shape=jax.ShapeDtypeStruct((n, d), BF16),
        scratch_shapes=[pltpu.VMEM((t, w), F32)] * 4 + [pltpu.VMEM((w, w), F32)],
        compiler_params=_cparams("parallel", "parallel"),
        name="rwkv_scan",
    )(r, kr, v, lw, a, g, *[x.reshape(1, d) for x in params])


def _rwkv_mixer_layer(h, hb, p, g, b, alpha, batch, tm):
    n, d = h.shape
    ident = lambda y: y
    xr, xw, xk, xv, xa, xg = _rwkv_mix(h, p["rwkv_mu"], batch, tm)
    (r,) = _proj(xr, p["rwkv_w_r"], 0, d, lambda y: (y,), [], [F32], tm, "rwkv_proj_r")
    (kr,) = _proj(xk, p["rwkv_w_k"], 0, d, lambda y: (y,), [], [F32], tm, "rwkv_proj_k")
    (v,) = _proj(xv, p["rwkv_w_v"], 0, d, lambda y: (y,), [], [F32], tm, "rwkv_proj_v")
    lw = _lora(xw, p["rwkv_w1"], p["rwkv_w2"], p["rwkv_w0"], jnp.tanh, _rwkv_log_decay, tm, "rwkv_lora_w")
    a = _lora(xa, p["rwkv_a1"], p["rwkv_a2"], p["rwkv_a0"], ident, _sigmoid, tm, "rwkv_lora_a")
    gate = _lora(xg, p["rwkv_g1"], p["rwkv_g2"], jnp.zeros((d,), F32), _sigmoid, ident, tm, "rwkv_lora_g")
    z = _rwkv_scan(r, kr, v, lw, a, gate, p, batch)
    return _mm_res_ln(z, p["rwkv_w_out"], h, g, b, alpha, tm)


def _embed_kernel(x_ref, meta_ref, h_ref, hb_ref, sem):
    b = pl.program_id(0)
    j = pl.program_id(1)
    tm = h_ref.shape[0]
    nmeta = meta_ref.shape[0]

    @pl.when(j == 0)
    def _():
        h_ref[0:nmeta, :] = meta_ref[...]
        cp = pltpu.make_async_copy(x_ref.at[b, pl.ds(0, tm - nmeta), :], h_ref.at[pl.ds(nmeta, tm - nmeta), :], sem)
        cp.start()
        cp.wait()

    @pl.when(j > 0)
    def _():
        first = pl.multiple_of(j * tm - nmeta, 8)
        cp = pltpu.make_async_copy(x_ref.at[b, pl.ds(first, tm), :], h_ref, sem)
        cp.start()
        cp.wait()

    hb_ref[...] = h_ref[...].astype(BF16)


def _embed(x, meta, tm):
    batch, seq, d = x.shape
    nmeta = meta.shape[0]
    t = nmeta + seq
    tpb = t // tm
    assert nmeta % 8 == 0 and tm % 8 == 0
    blk = pl.BlockSpec((tm, d), lambda b, j: (b * tpb + j, 0))
    return pl.pallas_call(
        _embed_kernel,
        grid=(batch, tpb),
        in_specs=[pl.BlockSpec(memory_space=pl.ANY), pl.BlockSpec((nmeta, d), lambda b, j: (0, 0))],
        out_specs=[blk, blk],
        out_shape=[jax.ShapeDtypeStruct((batch * t, d), F32), jax.ShapeDtypeStruct((batch * t, d), BF16)],
        scratch_shapes=[pltpu.SemaphoreType.DMA],
        compiler_params=_cparams("parallel", "arbitrary"),
        name="embed",
    )(x, meta.astype(x.dtype))


def kernel(x, meta, ln_mix_g, ln_mix_b, ln_ffn_g, ln_ffn_b, conv_w_in, conv_w, conv_b, conv_w_out, rwkv_mu, rwkv_w_r, rwkv_w_k, rwkv_w_v, rwkv_w0, rwkv_w1, rwkv_w2, rwkv_a0, rwkv_a1, rwkv_a2, rwkv_g1, rwkv_g2, rwkv_k_k, rwkv_k_a, rwkv_r_k, rwkv_gn_g, rwkv_gn_b, rwkv_w_out, hgrn_w_in, hgrn_lb, hgrn_norm_g, hgrn_w_out, fox_w_in, fox_b_f, fox_q_norm_g, fox_k_norm_g, fox_w_out, ffn0_w1, ffn0_w3, ffn0_w2, moe1_router, moe1_router_b, moe1_w1, moe1_w3, moe1_w2, ffn2_w1, ffn2_w3, ffn2_w2, moe3_router, moe3_router_b, moe3_w1, moe3_w3, moe3_w2):
    batch, seq, d = x.shape
    depth = ln_mix_g.shape[0]
    assert depth == 4
    alpha = (2.0 * depth) ** 0.25
    t = N_META + seq
    n = batch * t
    tm = _divisor_tile(t, 768, 16)
    p = dict(
        conv_w_in=conv_w_in, conv_w=conv_w, conv_b=conv_b, conv_w_out=conv_w_out,
        rwkv_mu=rwkv_mu, rwkv_w_r=rwkv_w_r, rwkv_w_k=rwkv_w_k, rwkv_w_v=rwkv_w_v, rwkv_w0=rwkv_w0,
        rwkv_w1=rwkv_w1, rwkv_w2=rwkv_w2, rwkv_a0=rwkv_a0, rwkv_a1=rwkv_a1, rwkv_a2=rwkv_a2,
        rwkv_g1=rwkv_g1, rwkv_g2=rwkv_g2, rwkv_k_k=rwkv_k_k, rwkv_k_a=rwkv_k_a, rwkv_r_k=rwkv_r_k,
        rwkv_gn_g=rwkv_gn_g, rwkv_gn_b=rwkv_gn_b, rwkv_w_out=rwkv_w_out,
        hgrn_w_in=hgrn_w_in, hgrn_lb=hgrn_lb, hgrn_norm_g=hgrn_norm_g, hgrn_w_out=hgrn_w_out,
        fox_w_in=fox_w_in, fox_b_f=fox_b_f, fox_q_norm_g=fox_q_norm_g, fox_k_norm_g=fox_k_norm_g,
        fox_w_out=fox_w_out,
    )
    assert meta.shape[0] == N_META
    h, hb = _embed(x, meta, tm)

    h, hb = _conv_mixer_layer(h, hb, p, ln_mix_g[0], ln_mix_b[0], alpha, batch, tm)
    h, hb = _dense_ffn(hb, h, ffn0_w1, ffn0_w3, ffn0_w2, ln_ffn_g[0], ln_ffn_b[0], alpha, tm)
    h, hb = _rwkv_mixer_layer(h, hb, p, ln_mix_g[1], ln_mix_b[1], alpha, batch, tm)
    h, hb = _moe_ffn(h, hb, moe1_router, moe1_router_b, moe1_w1, moe1_w3, moe1_w2,
                     ln_ffn_g[1], ln_ffn_b[1], alpha, tm)
    h, hb = _hgrn_mixer_layer(h, hb, p, 2, ln_mix_g[2], ln_mix_b[2], alpha, batch, tm)
    h, hb = _dense_ffn(hb, h, ffn2_w1, ffn2_w3, ffn2_w2, ln_ffn_g[2], ln_ffn_b[2], alpha, tm)
    h, hb = _fox_mixer_layer(h, hb, p, ln_mix_g[3], ln_mix_b[3], alpha, batch, tm)
    h, hb = _moe_ffn(h, hb, moe3_router, moe3_router_b, moe3_w1, moe3_w3, moe3_w2,
                     ln_ffn_g[3], ln_ffn_b[3], alpha, tm)
    return h.reshape(batch, t, d)[:, N_META:]
```

```python
import functools

import jax
import jax.numpy as jnp
from jax import lax
from jax.experimental import pallas as pl
from jax.experimental.pallas import tpu as pltpu

F32 = jnp.float32
BF16 = jnp.bfloat16

N_META = 16
LN_EPS = 1e-5
RMS_EPS = 1e-6
RWKV_HEAD = 64
HGRN_HEAD = 128
FOX_HEAD = 128
TOP_K = 2
LANES = 128
VMEM_LIMIT_BYTES = 56 * 2**20


def _cparams(*sem):
    return pltpu.CompilerParams(dimension_semantics=sem, vmem_limit_bytes=VMEM_LIMIT_BYTES)


def _divisor_tile(n, cap, mult):
    best = None
    for d in range(mult, min(n, cap) + 1, mult):
        if n % d == 0:
            best = d
    assert best is not None, (n, cap, mult)
    return best


def _layer_norm(y, g, b):
    mu = jnp.mean(y, axis=-1, keepdims=True)
    yc = y - mu
    var = jnp.mean(yc * yc, axis=-1, keepdims=True)
    return yc * lax.rsqrt(var + LN_EPS) * g + b


def _sigmoid(x):
    return 1.0 / (1.0 + jnp.exp(-x))


def _silu(x):
    return x * _sigmoid(x)


def _mm_res_ln_kernel(z_ref, w_ref, h_ref, g_ref, b_ref, o_ref, ob_ref, *, nk, alpha):
    k = pl.program_id(1)
    part = jnp.dot(z_ref[...], w_ref[...], preferred_element_type=F32)

    @pl.when(k == 0)
    def _():
        o_ref[...] = part

    @pl.when(k > 0)
    def _():
        o_ref[...] += part

    @pl.when(k == nk - 1)
    def _():
        y = _layer_norm(alpha * h_ref[...] + o_ref[...], g_ref[...], b_ref[...])
        o_ref[...] = y
        ob_ref[...] = y.astype(BF16)


WEIGHT_SLICE_ROWS = 256


def _load_weight_bf16(w_hbm, wb_ref, stage_ref, sem):
    rows = stage_ref.shape[1]
    nslice = w_hbm.shape[0] // rows

    def copy(s):
        return pltpu.make_async_copy(w_hbm.at[pl.ds(s * rows, rows), :], stage_ref.at[s % 2], sem.at[s % 2])

    copy(0).start()
    for s in range(nslice):
        if s + 1 < nslice:
            copy(s + 1).start()
        copy(s).wait()
        wb_ref[s * rows:(s + 1) * rows, :] = stage_ref[s % 2].astype(BF16)


def _mm_res_ln_resident_kernel(z_ref, w_ref, h_ref, g_ref, b_ref, o_ref, ob_ref, wb_ref, stage_ref, sem,
                               *, alpha):
    @pl.when(pl.program_id(0) == 0)
    def _():
        _load_weight_bf16(w_ref, wb_ref, stage_ref, sem)

    tm = z_ref.shape[0]
    split = min(tm, -(-tm // 32) * 16)
    spans = [(0, split), (split, tm)] if split < tm else [(0, tm)]
    dots = [jnp.dot(z_ref[lo:hi, :], wb_ref[...], preferred_element_type=F32) for lo, hi in spans]
    for (lo, hi), part in zip(spans, dots):
        y = _layer_norm(alpha * h_ref[lo:hi, :] + part, g_ref[...], b_ref[...])
        o_ref[lo:hi, :] = y
        ob_ref[lo:hi, :] = y.astype(BF16)


RESIDENT_WEIGHT_BYTES = 16 * 2**20


def _mm_res_ln_resident(z, w, h, g, b, alpha, tm):
    n, kdim = z.shape
    d = w.shape[1]
    ws = _divisor_tile(kdim, WEIGHT_SLICE_ROWS, 8)
    const = lambda m: (0, 0)
    return pl.pallas_call(
        functools.partial(_mm_res_ln_resident_kernel, alpha=alpha),
        grid=(n // tm,),
        in_specs=[
            pl.BlockSpec((tm, kdim), lambda m: (m, 0)),
            pl.BlockSpec(memory_space=pl.ANY),
            pl.BlockSpec((tm, d), lambda m: (m, 0)),
            pl.BlockSpec((1, d), const),
            pl.BlockSpec((1, d), const),
        ],
        out_specs=[
            pl.BlockSpec((tm, d), lambda m: (m, 0)),
            pl.BlockSpec((tm, d), lambda m: (m, 0)),
        ],
        out_shape=[jax.ShapeDtypeStruct((n, d), F32), jax.ShapeDtypeStruct((n, d), BF16)],
        scratch_shapes=[pltpu.VMEM((kdim, d), BF16), pltpu.VMEM((2, ws, d), F32),
                        pltpu.SemaphoreType.DMA((2,))],
        compiler_params=_cparams("arbitrary"),
        name="mm_res_ln_resident",
    )(z, w, h, g.reshape(1, d), b.reshape(1, d))


def _mm_res_ln(z, w, h, g, b, alpha, tm):
    n, kdim = z.shape
    d = w.shape[1]
    if kdim * d * 4 <= RESIDENT_WEIGHT_BYTES:
        return _mm_res_ln_resident(z, w, h, g, b, alpha, tm)
    w = w.astype(BF16)
    tk = _divisor_tile(kdim, 1408, LANES)
    nk = kdim // tk
    return pl.pallas_call(
        functools.partial(_mm_res_ln_kernel, nk=nk, alpha=alpha),
        grid=(n // tm, nk),
        in_specs=[
            pl.BlockSpec((tm, tk), lambda m, k: (m, k)),
            pl.BlockSpec((tk, d), lambda m, k: (k, 0)),
            pl.BlockSpec((tm, d), lambda m, k: (m, 0)),
            pl.BlockSpec((1, d), lambda m, k: (0, 0)),
            pl.BlockSpec((1, d), lambda m, k: (0, 0)),
        ],
        out_specs=[
            pl.BlockSpec((tm, d), lambda m, k: (m, 0)),
            pl.BlockSpec((tm, d), lambda m, k: (m, 0)),
        ],
        out_shape=[jax.ShapeDtypeStruct((n, d), F32), jax.ShapeDtypeStruct((n, d), BF16)],
        compiler_params=_cparams("parallel", "arbitrary"),
        name="mm_res_ln",
    )(z, w, h, g.reshape(1, d), b.reshape(1, d))


def _ffn_up_kernel(exp_ref, nact_ref, x_ref, w1_ref, w3_ref, o_ref, w1b_ref, w3b_ref):
    c = pl.program_id(1)
    prev = exp_ref[jnp.maximum(c - 1, 0)]
    new_weights = jnp.logical_or(c == 0, exp_ref[c] != prev)

    @pl.when(new_weights)
    def _():
        w1b_ref[...] = w1_ref[...].astype(BF16)
        w3b_ref[...] = w3_ref[...].astype(BF16)

    @pl.when(c < nact_ref[0])
    def _():
        x = x_ref[...]
        a = jnp.dot(x, w1b_ref[...], preferred_element_type=F32)
        bb = jnp.dot(x, w3b_ref[...], preferred_element_type=F32)
        o_ref[...] = (_silu(a) * bb).astype(BF16)

    @pl.when(c >= nact_ref[0])
    def _():
        o_ref[...] = jnp.zeros_like(o_ref)


def _ffn_up(x, w1, w3, chunk_expert, n_active, rows):
    p, d = x.shape
    f = w1.shape[2]
    tf = _divisor_tile(f, 512, LANES)
    grid_spec = pltpu.PrefetchScalarGridSpec(
        num_scalar_prefetch=2,
        grid=(f // tf, p // rows),
        in_specs=[
            pl.BlockSpec((rows, d), lambda j, c, e, na: (c, 0)),
            pl.BlockSpec((None, d, tf), lambda j, c, e, na: (e[c], 0, j)),
            pl.BlockSpec((None, d, tf), lambda j, c, e, na: (e[c], 0, j)),
        ],
        out_specs=pl.BlockSpec((rows, tf), lambda j, c, e, na: (c, j)),
        scratch_shapes=[pltpu.VMEM((d, tf), BF16), pltpu.VMEM((d, tf), BF16)],
    )
    return pl.pallas_call(
        _ffn_up_kernel,
        grid_spec=grid_spec,
        out_shape=jax.ShapeDtypeStruct((p, f), BF16),
        compiler_params=_cparams("arbitrary", "arbitrary"),
        name="ffn_up",
    )(chunk_expert, n_active, x, w1, w3)


def _dense_ffn(hb, h, w1, w3, w2, g, b, alpha, tm):
    n = hb.shape[0]
    nchunks = n // tm
    hmid = _ffn_up(hb, w1[None], w3[None], jnp.zeros((nchunks,), jnp.int32),
                   jnp.full((1,), nchunks, jnp.int32), tm)
    return _mm_res_ln(hmid, w2, h, g, b, alpha, tm)


def _conv_proj_kernel(x_ref, wb_ref, wc_ref, wh_ref, cw_ref, cb_ref, o_ref,
                      wbb_ref, wcb_ref, whb_ref, carry_ref):
    bi = pl.program_id(1)
    ti = pl.program_id(2)

    @pl.when(jnp.logical_and(bi == 0, ti == 0))
    def _():
        wbb_ref[...] = wb_ref[...].astype(BF16)
        wcb_ref[...] = wc_ref[...].astype(BF16)
        whb_ref[...] = wh_ref[...].astype(BF16)

    @pl.when(ti == 0)
    def _():
        carry_ref[...] = jnp.zeros_like(carry_ref)

    x = x_ref[...]
    gate_b = jnp.dot(x, wbb_ref[...], preferred_element_type=F32)
    gate_c = jnp.dot(x, wcb_ref[...], preferred_element_type=F32)
    hh = jnp.dot(x, whb_ref[...], preferred_element_type=F32)
    u = gate_c * hh
    tm = u.shape[0]
    prev1 = carry_ref[7:8, :]
    prev2 = carry_ref[6:7, :]
    row = lax.broadcasted_iota(jnp.int32, (tm, 1), 0)
    r1 = jnp.where(row == 0, prev1, pltpu.roll(u, 1, axis=0))
    r2 = jnp.where(row == 0, prev2, jnp.where(row == 1, prev1, pltpu.roll(u, 2, axis=0)))
    v = cw_ref[0:1, :] * r2 + cw_ref[1:2, :] * r1 + cw_ref[2:3, :] * u + cb_ref[...]
    carry_ref[...] = u[tm - 8:, :]
    o_ref[...] = (gate_b * v).astype(BF16)


def _conv_proj(hb, w_in, conv_w, conv_b, batch, tm):
    n, d = hb.shape
    tn = _divisor_tile(d, 512, LANES)
    nd = d // tn
    tpb = n // batch // tm
    return pl.pallas_call(
        _conv_proj_kernel,
        grid=(nd, batch, tpb),
        in_specs=[
            pl.BlockSpec((tm, d), lambda j, bi, ti: (bi * tpb + ti, 0)),
            pl.BlockSpec((d, tn), lambda j, bi, ti: (0, j)),
            pl.BlockSpec((d, tn), lambda j, bi, ti: (0, nd + j)),
            pl.BlockSpec((d, tn), lambda j, bi, ti: (0, 2 * nd + j)),
            pl.BlockSpec((3, tn), lambda j, bi, ti: (0, j)),
            pl.BlockSpec((1, tn), lambda j, bi, ti: (0, j)),
        ],
        out_specs=pl.BlockSpec((tm, tn), lambda j, bi, ti: (bi * tpb + ti, j)),
        out_shape=jax.ShapeDtypeStruct((n, d), BF16),
        scratch_shapes=[pltpu.VMEM((d, tn), BF16)] * 3 + [pltpu.VMEM((8, tn), F32)],
        compiler_params=_cparams("arbitrary", "arbitrary", "arbitrary"),
        name="conv_proj",
    )(hb, w_in, w_in, w_in, conv_w, conv_b.reshape(1, d))


def _conv_mixer_layer(h, hb, p, g, b, alpha, batch, tm):
    z = _conv_proj(hb, p["conv_w_in"], p["conv_w"], p["conv_b"], batch, tm)
    return _mm_res_ln(z, p["conv_w_out"], h, g, b, alpha, tm)


MOE_ROWS = 128


def _split_bf16(a):
    hi = a.astype(BF16)
    lo = (a - hi.astype(F32)).astype(BF16)
    return hi, lo


def _dot_f32(a, b):
    ah, al = _split_bf16(a)
    bh, bl = _split_bf16(b)
    return (jnp.dot(ah, bh, preferred_element_type=F32)
            + (jnp.dot(ah, bl, preferred_element_type=F32)
               + jnp.dot(al, bh, preferred_element_type=F32)))


def _router_kernel(h_ref, w_ref, b_ref, info_ref, cnt_ref, carry_ref):
    i = pl.program_id(0)

    @pl.when(i == 0)
    def _():
        carry_ref[...] = jnp.zeros_like(carry_ref)

    logits = _dot_f32(h_ref[...], w_ref[...]) + b_ref[...]
    tm, ne = logits.shape
    lane = lax.broadcasted_iota(jnp.int32, (tm, ne), 1)
    m1 = jnp.max(logits, axis=-1, keepdims=True)
    i1 = jnp.min(jnp.where(logits == m1, lane, ne), axis=-1, keepdims=True)
    mask1 = lane == i1
    rest = jnp.where(mask1, -jnp.inf, logits)
    m2 = jnp.max(rest, axis=-1, keepdims=True)
    i2 = jnp.min(jnp.where(rest == m2, lane, ne), axis=-1, keepdims=True)
    mask2 = lane == i2
    dd = jnp.exp(m2 - m1)
    g1 = 1.0 / (1.0 + dd)
    g2 = dd / (1.0 + dd)
    sel = jnp.where(jnp.logical_or(mask1, mask2), 1.0, 0.0)
    r_i = lax.broadcasted_iota(jnp.int32, (tm, tm), 0)
    c_i = lax.broadcasted_iota(jnp.int32, (tm, tm), 1)
    tril = jnp.where(c_i < r_i, 1.0, 0.0).astype(BF16)
    rank = jnp.dot(tril, sel.astype(BF16), preferred_element_type=F32) + carry_ref[...]
    r1 = jnp.sum(jnp.where(mask1, rank, 0.0), axis=-1, keepdims=True)
    r2 = jnp.sum(jnp.where(mask2, rank, 0.0), axis=-1, keepdims=True)
    info = jnp.where(lane == 0, i1.astype(F32),
           jnp.where(lane == 1, i2.astype(F32),
           jnp.where(lane == 2, g1,
           jnp.where(lane == 3, g2,
           jnp.where(lane == 4, r1,
           jnp.where(lane == 5, r2, 0.0))))))
    info_ref[...] = info
    total = carry_ref[...] + jnp.sum(sel, axis=0, keepdims=True)
    carry_ref[...] = total
    cnt_ref[...] = total


def _router(h, w, b, tm):
    n, d = h.shape
    ne = w.shape[1]
    assert ne >= 6
    return pl.pallas_call(
        _router_kernel,
        grid=(n // tm,),
        in_specs=[
            pl.BlockSpec((tm, d), lambda i: (i, 0)),
            pl.BlockSpec((d, ne), lambda i: (0, 0)),
            pl.BlockSpec((1, ne), lambda i: (0, 0)),
        ],
        out_specs=[
            pl.BlockSpec((tm, ne), lambda i: (i, 0)),
            pl.BlockSpec((1, ne), lambda i: (0, 0)),
        ],
        out_shape=[jax.ShapeDtypeStruct((n, ne), F32), jax.ShapeDtypeStruct((1, ne), F32)],
        scratch_shapes=[pltpu.VMEM((1, ne), F32)],
        compiler_params=_cparams("arbitrary"),
        name="moe_router",
    )(h, w, b.reshape(1, ne))


def _row_copy(src_hbm, row, dst_vmem, r, sem):
    return pltpu.make_async_copy(src_hbm.at[pl.ds(row, 1), :], dst_vmem.at[pl.ds(r, 1), :], sem)


def _rows_wait(src_hbm, dst_vmem, sem):
    pltpu.make_async_copy(src_hbm.at[pl.ds(0, dst_vmem.shape[0]), :], dst_vmem, sem).wait()


def _moe_gather_kernel(src_ref, nact_ref, h_ref, o_ref, buf_ref, sem):
    c = pl.program_id(0)
    rows = buf_ref.shape[1]

    nact = nact_ref[0]
    slot = c % 2

    def start(chunk, to_slot, r):
        _row_copy(h_ref, src_ref[chunk * rows + r], buf_ref.at[to_slot], r, sem.at[to_slot]).start()

    @pl.when(jnp.logical_and(c == 0, nact > 0))
    def _():
        lax.fori_loop(0, rows, lambda r, carry: (start(c, slot, r), carry)[1], 0, unroll=8)

    @pl.when(c < nact)
    def _():
        _rows_wait(h_ref, buf_ref.at[slot], sem.at[slot])
        nxt = jnp.minimum(c + 1, nact - 1)
        for r in range(rows):
            start(nxt, 1 - slot, r)
        o_ref[...] = buf_ref[slot].astype(BF16)

        @pl.when(c == nact - 1)
        def _():
            _rows_wait(h_ref, buf_ref.at[1 - slot], sem.at[1 - slot])

    @pl.when(c >= nact)
    def _():
        o_ref[...] = jnp.zeros_like(o_ref)


def _moe_gather(h, src, n_active, p, rows):
    n, d = h.shape
    grid_spec = pltpu.PrefetchScalarGridSpec(
        num_scalar_prefetch=2,
        grid=(p // rows,),
        in_specs=[pl.BlockSpec(memory_space=pl.ANY)],
        out_specs=pl.BlockSpec((rows, d), lambda c, s, na: (c, 0)),
        scratch_shapes=[pltpu.VMEM((2, rows, d), F32), pltpu.SemaphoreType.DMA((2,))],
    )
    return pl.pallas_call(
        _moe_gather_kernel,
        grid_spec=grid_spec,
        out_shape=jax.ShapeDtypeStruct((p, d), BF16),
        compiler_params=_cparams("arbitrary"),
        name="moe_gather",
    )(src, n_active, h)


def _moe_combine_kernel(p1_ref, p2_ref, y_ref, h_ref, info_ref, g_ref, b_ref, o_ref, ob_ref,
                        buf1_ref, buf2_ref, sem, *, alpha):
    i = pl.program_id(0)
    tm = buf1_ref.shape[1]

    last = pl.num_programs(0) - 1
    slot = i % 2

    def start(tile, to_slot, r):
        _row_copy(y_ref, p1_ref[tile * tm + r], buf1_ref.at[to_slot], r, sem.at[to_slot]).start()
        _row_copy(y_ref, p2_ref[tile * tm + r], buf2_ref.at[to_slot], r, sem.at[to_slot]).start()

    def wait(from_slot):
        _rows_wait(y_ref, buf1_ref.at[from_slot], sem.at[from_slot])
        _rows_wait(y_ref, buf2_ref.at[from_slot], sem.at[from_slot])

    @pl.when(i == 0)
    def _():
        lax.fori_loop(0, tm, lambda r, c: (start(i, slot, r), c)[1], 0, unroll=8)

    wait(slot)
    nxt = jnp.minimum(i + 1, last)
    for r in range(tm):
        start(nxt, 1 - slot, r)
    info = info_ref[...]
    y = alpha * h_ref[...] + (info[:, 2:3] * buf1_ref[slot] + info[:, 3:4] * buf2_ref[slot])
    y = _layer_norm(y, g_ref[...], b_ref[...])
    o_ref[...] = y
    ob_ref[...] = y.astype(BF16)

    @pl.when(i == last)
    def _():
        wait(1 - slot)


def _moe_combine(y, h, info, p1, p2, g, b, alpha, tm):
    n, d = h.shape
    ne = info.shape[1]
    grid_spec = pltpu.PrefetchScalarGridSpec(
        num_scalar_prefetch=2,
        grid=(n // tm,),
        in_specs=[
            pl.BlockSpec(memory_space=pl.ANY),
            pl.BlockSpec((tm, d), lambda i, a, c: (i, 0)),
            pl.BlockSpec((tm, ne), lambda i, a, c: (i, 0)),
            pl.BlockSpec((1, d), lambda i, a, c: (0, 0)),
            pl.BlockSpec((1, d), lambda i, a, c: (0, 0)),
        ],
        out_specs=[
            pl.BlockSpec((tm, d), lambda i, a, c: (i, 0)),
            pl.BlockSpec((tm, d), lambda i, a, c: (i, 0)),
        ],
        scratch_shapes=[pltpu.VMEM((2, tm, d), F32), pltpu.VMEM((2, tm, d), F32),
                        pltpu.SemaphoreType.DMA((2,))],
    )
    return pl.pallas_call(
        functools.partial(_moe_combine_kernel, alpha=alpha),
        grid_spec=grid_spec,
        out_shape=[jax.ShapeDtypeStruct((n, d), F32), jax.ShapeDtypeStruct((n, d), BF16)],
        compiler_params=_cparams("arbitrary"),
        name="moe_combine",
    )(p1, p2, y, h, info, g.reshape(1, d), b.reshape(1, d))


MOE_PASS_BLOCKS = 18
MOE_F_TILE = 256
MOE_BLOCK_UNITS = 8


def _expert_ffn_kernel(pe_ref, ps_ref, pn_ref, nu_ref, xs_ref, w1_ref, w3_ref, w2_ref, y_ref,
                       x_buf, acc_ref, w1b_ref, w3b_ref, w2b_ref, sem):
    p = pl.program_id(0)
    f = pl.program_id(1)
    nf = pl.num_programs(1)
    nb = pn_ref[p]
    sb = MOE_ROWS
    start = pl.multiple_of(ps_ref[p], MOE_ROWS)
    nblock = nb // MOE_BLOCK_UNITS

    @pl.when(nb > 0)
    def _():
        @pl.when(f == 0)
        def _():
            cp = pltpu.make_async_copy(xs_ref.at[pl.ds(start, x_buf.shape[0]), :], x_buf, sem.at[0])
            cp.start()
            cp.wait()

        w1b_ref[...] = w1_ref[...].astype(BF16)
        w3b_ref[...] = w3_ref[...].astype(BF16)

        @pl.when(nblock == 0)
        def _():
            w2b_ref[...] = w2_ref[...].astype(BF16)

        def up(unit, nunit):
            x = x_buf[pl.ds(pl.multiple_of(unit * sb, sb), nunit * sb), :]
            a = jnp.dot(x, w1b_ref[...], preferred_element_type=F32)
            bb = jnp.dot(x, w3b_ref[...], preferred_element_type=F32)
            return (_silu(a) * bb).astype(BF16)

        def out_copy(i):
            rows = pl.ds(pl.multiple_of(i * sb, sb), sb)
            dst = pl.ds(pl.multiple_of(start + i * sb, sb), sb)
            return pltpu.make_async_copy(acc_ref.at[rows, :], y_ref.at[dst, :], sem.at[1])

        def down(unit, nunit, hmid, first):
            rows = pl.ds(pl.multiple_of(unit * sb, sb), nunit * sb)
            part = jnp.dot(hmid, w2b_ref[...], preferred_element_type=F32)
            if first:
                acc_ref[rows, :] = part
            else:
                acc_ref[rows, :] += part

            @pl.when(f == nf - 1)
            def _():
                for u in range(nunit):
                    out_copy(unit + u).start()

        def sweep(first):
            bu = MOE_BLOCK_UNITS

            @pl.when(nblock > 0)
            def _():
                def body(i, hprev):
                    down(bu * (i - 1), bu, hprev, first)
                    return up(bu * i, bu)

                h0 = up(0, bu)
                w2b_ref[...] = w2_ref[...].astype(BF16)
                down(bu * (nblock - 1), bu, lax.fori_loop(1, nblock, body, h0), first)

            done = nblock * bu
            size = bu // 2
            while size >= 1:
                @pl.when((nb - done) & size != 0)
                def _(done=done, size=size):
                    down(done, size, up(done, size), first)

                done = done + ((nb - done) & size)
                size //= 2

        @pl.when(f == 0)
        def _():
            sweep(True)

        @pl.when(f > 0)
        def _():
            sweep(False)

        @pl.when(f == nf - 1)
        def _():
            lax.fori_loop(0, nb, lambda i, c: (out_copy(i).wait(), c)[1], 0)

    @pl.when(jnp.logical_and(p == pl.num_programs(0) - 1, f == nf - 1))
    def _():
        acc_ref[0:sb, :] = jnp.zeros((sb, acc_ref.shape[1]), F32)

        def zero_copy(i):
            dst = pl.ds(pl.multiple_of(i * sb, sb), sb)
            return pltpu.make_async_copy(acc_ref.at[0:sb, :], y_ref.at[dst, :], sem.at[1])

        n_blocks = y_ref.shape[0] // sb
        lax.fori_loop(nu_ref[1], n_blocks, lambda i, c: (zero_copy(i).start(), c)[1], 0)
        lax.fori_loop(nu_ref[1], n_blocks, lambda i, c: (zero_copy(i).wait(), c)[1], 0)


def _expert_ffn(xs, w1, w3, w2, pass_expert, pass_start, pass_nb, n_used, p_rows):
    d = xs.shape[1]
    f = w1.shape[2]
    tf = _divisor_tile(f, MOE_F_TILE, LANES)
    nf = f // tf
    npass = pass_expert.shape[0]
    r = MOE_PASS_BLOCKS * MOE_ROWS

    def fidx(p, j, nu):
        return jnp.where(p < nu[0], j, nf - 1)

    grid_spec = pltpu.PrefetchScalarGridSpec(
        num_scalar_prefetch=4,
        grid=(npass, nf),
        in_specs=[
            pl.BlockSpec(memory_space=pl.ANY),
            pl.BlockSpec((None, d, tf), lambda p, j, pe, ps, pn, nu: (pe[p], 0, fidx(p, j, nu))),
            pl.BlockSpec((None, d, tf), lambda p, j, pe, ps, pn, nu: (pe[p], 0, fidx(p, j, nu))),
            pl.BlockSpec((None, tf, d), lambda p, j, pe, ps, pn, nu: (pe[p], fidx(p, j, nu), 0)),
        ],
        out_specs=pl.BlockSpec(memory_space=pl.ANY),
        scratch_shapes=[
            pltpu.VMEM((r, d), BF16), pltpu.VMEM((r, d), F32),
            pltpu.VMEM((d, tf), BF16), pltpu.VMEM((d, tf), BF16), pltpu.VMEM((tf, d), BF16),
            pltpu.SemaphoreType.DMA((2,)),
        ],
    )
    return pl.pallas_call(
        _expert_ffn_kernel,
        grid_spec=grid_spec,
        out_shape=jax.ShapeDtypeStruct((p_rows, d), F32),
        compiler_params=_cparams("arbitrary", "arbitrary"),
        name="expert_ffn",
    )(pass_expert, pass_start, pass_nb, n_used, xs, w1, w3, w2)


def _moe_ffn(h, hb, router_w, router_b, w1, w3, w2, g, b, alpha, tm):
    n, d = h.shape
    ne = router_w.shape[1]
    rows = MOE_ROWS
    info, counts = _router(h, router_w, router_b, tm)
    counts = counts[0].astype(jnp.int32)
    nblk_e = (counts + rows - 1) // rows
    blk_end = jnp.cumsum(nblk_e)
    starts = (blk_end - nblk_e) * rows
    n_blocks = (n * TOP_K + ne * (rows - 1)) // rows
    p = n_blocks * rows
    n_active = blk_end[-1:].astype(jnp.int32)
    npass_e = (nblk_e + MOE_PASS_BLOCKS - 1) // MOE_PASS_BLOCKS
    pass_end = jnp.cumsum(npass_e)
    max_pass = n_blocks // MOE_PASS_BLOCKS + ne
    pidx = jnp.arange(max_pass, dtype=jnp.int32)
    n_used = jnp.stack([pass_end[-1], blk_end[-1]]).astype(jnp.int32)
    last_expert = jnp.sum(pass_end < pass_end[-1]).astype(jnp.int32)
    pass_expert = jnp.minimum(jnp.sum(pidx[:, None] >= pass_end[None, :], axis=1), last_expert).astype(jnp.int32)
    local = pidx - (pass_end - npass_e)[pass_expert]
    pass_start = (starts[pass_expert] + local * (MOE_PASS_BLOCKS * rows)).astype(jnp.int32)
    pass_nb = jnp.where(pidx < n_used[0],
                        jnp.clip(nblk_e[pass_expert] - local * MOE_PASS_BLOCKS, 0, MOE_PASS_BLOCKS),
                        0).astype(jnp.int32)
    pass_start = jnp.where(pass_nb > 0, pass_start, 0).astype(jnp.int32)
    i1 = info[:, 0].astype(jnp.int32)
    i2 = info[:, 1].astype(jnp.int32)
    p1 = starts[i1] + info[:, 4].astype(jnp.int32)
    p2 = starts[i2] + info[:, 5].astype(jnp.int32)
    tok = jnp.arange(n, dtype=jnp.int32)
    p_in = p + MOE_PASS_BLOCKS * rows
    src = jnp.zeros((p_in,), jnp.int32).at[jnp.concatenate([p1, p2])].set(jnp.concatenate([tok, tok]))
    xs = _moe_gather(h, src, n_active, p_in, rows)
    y = _expert_ffn(xs, w1, w3, w2, pass_expert, pass_start, pass_nb, n_used, p)
    tmc = _divisor_tile(n, 384, 16)
    return _moe_combine(y, h, info, p1, p2, g, b, alpha, tmc)


def _proj_kernel(*refs, epilogue, n_extra, n_out):
    x_ref, w_ref = refs[0], refs[1]
    extra = refs[2:2 + n_extra]
    outs = refs[2 + n_extra:2 + n_extra + n_out]
    wb_ref = refs[2 + n_extra + n_out]

    @pl.when(pl.program_id(1) == 0)
    def _():
        wb_ref[...] = w_ref[...].astype(BF16)

    y = jnp.dot(x_ref[...], wb_ref[...], preferred_element_type=F32)
    res = epilogue(y, *[e[...] for e in extra])
    for o_ref, r in zip(outs, res):
        o_ref[...] = r.astype(o_ref.dtype)


def _proj(xb, w, col0, ncols, epilogue, extras, out_dtypes, tm, name, tn_cap=1024):
    n, kdim = xb.shape
    tn = _divisor_tile(ncols, tn_cap, LANES)
    assert col0 % tn == 0
    off = col0 // tn
    outs = pl.pallas_call(
        functools.partial(_proj_kernel, epilogue=epilogue, n_extra=len(extras), n_out=len(out_dtypes)),
        grid=(ncols // tn, n // tm),
        in_specs=[
            pl.BlockSpec((tm, kdim), lambda j, m: (m, 0)),
            pl.BlockSpec((kdim, tn), lambda j, m: (0, off + j)),
        ] + [pl.BlockSpec((1, tn), lambda j, m: (0, j))] * len(extras),
        out_specs=[pl.BlockSpec((tm, tn), lambda j, m: (m, j))] * len(out_dtypes),
        out_shape=[jax.ShapeDtypeStruct((n, ncols), dt) for dt in out_dtypes],
        scratch_shapes=[pltpu.VMEM((kdim, tn), BF16)],
        compiler_params=_cparams("arbitrary", "arbitrary"),
        name=name,
    )(xb, w, *[e.reshape(1, ncols) for e in extras])
    return outs


def _tril_bf16(c, inclusive):
    r_i = lax.broadcasted_iota(jnp.int32, (c, c), 0)
    c_i = lax.broadcasted_iota(jnp.int32, (c, c), 1)
    keep = (c_i <= r_i) if inclusive else (c_i < r_i)
    return jnp.where(keep, 1.0, 0.0).astype(BF16)


def _cumsum_rows(x, tril):
    hi = x.astype(BF16)
    r1 = x - hi.astype(F32)
    mid = r1.astype(BF16)
    lo = (r1 - mid.astype(F32)).astype(BF16)
    return (jnp.dot(tril, hi, preferred_element_type=F32)
            + (jnp.dot(tril, mid, preferred_element_type=F32)
               + jnp.dot(tril, lo, preferred_element_type=F32)))


def _dot_nt(a, b):
    return lax.dot_general(a.astype(BF16), b.astype(BF16), (((1,), (1,)), ((), ())),
                           preferred_element_type=F32)


def _dot_tn(a, b):
    return lax.dot_general(a.astype(BF16), b.astype(BF16), (((0,), (0,)), ((), ())),
                           preferred_element_type=F32)


def _dot_nn(a, b):
    return jnp.dot(a.astype(BF16), b.astype(BF16), preferred_element_type=F32)


def _chunk_len(t):
    return _divisor_tile(t, 64, 16)


def _round_robin(gens):
    done = [None] * len(gens)
    while any(d is None for d in done):
        for j, gen in enumerate(gens):
            if done[j] is None:
                done[j] = next(gen)
    return done


HGRN_SUB = 16


def _hgrn_scan_kernel(q_ref, lf_ref, v_ref, gs_ref, ng_ref, o_ref, st_ref, *, chunk):
    t, w = q_ref.shape
    nhead = w // HGRN_HEAD
    nsub = chunk // HGRN_SUB
    st_ref[...] = jnp.zeros_like(st_ref)
    tril = _tril_bf16(chunk, True)
    row16 = lax.broadcasted_iota(jnp.int32, (HGRN_SUB, 1), 0)

    def head_chunk(q, lf, v, st):
        k = 1.0 - jnp.exp(lf)
        cum = _cumsum_rows(lf, tril)
        yield None
        o_inter = _dot_nt(q * jnp.exp(cum), st)
        vb = v.astype(BF16)
        cl = cum[chunk - 1:chunk]
        kd = k * jnp.exp(cl - cum)
        st_new = st * jnp.exp(cl) + _dot_tn(v, kd)
        yield None
        outs = []
        for i in range(nsub):
            lo, hi = i * HGRN_SUB, (i + 1) * HGRN_SUB
            qi, ki, vi, cumi = q[lo:hi], k[lo:hi], v[lo:hi], cum[lo:hi]
            oi = o_inter[lo:hi]
            if i > 0:
                ci = cum[lo:lo + 1]
                qt = qi * jnp.exp(cumi - ci)
                kt = k[0:lo] * jnp.exp(ci - cum[0:lo])
                oi = oi + jnp.dot(_dot_nt(qt, kt).astype(BF16), vb[0:lo], preferred_element_type=F32)
            for s in range(HGRN_SUB):
                dec = jnp.exp(jnp.minimum(cumi - cumi[s:s + 1], 0.0))
                col = jnp.sum(qi * dec * ki[s:s + 1], axis=-1, keepdims=True)
                col = jnp.where(row16 >= s, col, 0.0)
                oi = oi + col * vi[s:s + 1]
                if s % 4 == 3:
                    yield None
            outs.append(oi)
        o = jnp.concatenate(outs, axis=0)
        o = o * lax.rsqrt(jnp.mean(o * o, axis=-1, keepdims=True) + RMS_EPS)
        yield o, st_new

    def body(c, carry):
        rows = pl.ds(pl.multiple_of(c * chunk, 16), chunk)
        q = q_ref[rows, :].astype(F32)
        lf = lf_ref[rows, :]
        v = v_ref[rows, :].astype(F32)
        heads = []
        for hh in range(nhead):
            cols = slice(hh * HGRN_HEAD, (hh + 1) * HGRN_HEAD)
            heads.append(head_chunk(q[:, cols], lf[:, cols], v[:, cols], st_ref[hh]))
        outs = []
        for hh, (o, st_new) in enumerate(_round_robin(heads)):
            st_ref[hh] = st_new
            outs.append(o)
        o = jnp.concatenate(outs, axis=1)
        o_ref[rows, :] = (o * ng_ref[...] * gs_ref[rows, :].astype(F32)).astype(BF16)
        return carry

    lax.fori_loop(0, t // chunk, body, 0)


HGRN_GROUP = 8


def _hgrn_scan(q, lf, v, gs, norm_g, batch):
    n, d = q.shape
    t = n // batch
    chunk = _chunk_len(t)
    w = min(d, HGRN_GROUP * HGRN_HEAD)
    blk = pl.BlockSpec((t, w), lambda b, j: (b, j))
    return pl.pallas_call(
        functools.partial(_hgrn_scan_kernel, chunk=chunk),
        grid=(batch, d // w),
        in_specs=[blk, blk, blk, blk, pl.BlockSpec((1, w), lambda b, j: (0, j))],
        out_specs=blk,
        out_shape=jax.ShapeDtypeStruct((n, d), BF16),
        scratch_shapes=[pltpu.VMEM((w // HGRN_HEAD, HGRN_HEAD, HGRN_HEAD), F32)],
        compiler_params=_cparams("parallel", "parallel"),
        name="hgrn_scan",
    )(q, lf, v, gs, norm_g.reshape(1, d))


def _hgrn_mixer_layer(h, hb, p, layer_idx, g, b, alpha, batch, tm):
    d = h.shape[1]
    w_in = p["hgrn_w_in"]
    lb = jnp.cumsum(jax.nn.softmax(p["hgrn_lb"].astype(F32), axis=0), axis=0)
    lb = lb[layer_idx] - lb[0]
    (q,) = _proj(hb, w_in, 0, d, lambda y: (_silu(y),), [], [BF16], tm, "hgrn_proj_q")
    (lf,) = _proj(hb, w_in, d, d, lambda y, lbv: (jnp.log(lbv + (1.0 - lbv) * _sigmoid(y)),),
                  [lb], [F32], tm, "hgrn_proj_f")
    (v,) = _proj(hb, w_in, 2 * d, d, lambda y: (y,), [], [BF16], tm, "hgrn_proj_i")
    (gs,) = _proj(hb, w_in, 3 * d, d, lambda y: (_silu(y),), [], [BF16], tm, "hgrn_proj_g")
    z = _hgrn_scan(q, lf, v, gs, p["hgrn_norm_g"], batch)
    return _mm_res_ln(z, p["hgrn_w_out"], h, g, b, alpha, tm)


def _fox_gate_kernel(h_ref, w_ref, bf_ref, c_ref, carry_ref):
    @pl.when(pl.program_id(1) == 0)
    def _():
        carry_ref[...] = jnp.zeros_like(carry_ref)

    x = _dot_f32(h_ref[...], w_ref[...]) + bf_ref[...]
    log_f = jnp.minimum(x, 0.0) - jnp.log(1.0 + jnp.exp(-jnp.abs(x)))
    tm = x.shape[0]
    c = _cumsum_rows(log_f, _tril_bf16(tm, True)) + carry_ref[...]
    c_ref[...] = c
    carry_ref[...] = c[tm - 1:tm, :]


def _fox_gate(h, w_f, b_f, batch, tm):
    n, d = h.shape
    nh = w_f.shape[1]
    tpb = n // batch // tm
    return pl.pallas_call(
        _fox_gate_kernel,
        grid=(batch, tpb),
        in_specs=[
            pl.BlockSpec((tm, d), lambda b, t: (b * tpb + t, 0)),
            pl.BlockSpec((d, nh), lambda b, t: (0, 0)),
            pl.BlockSpec((1, nh), lambda b, t: (0, 0)),
        ],
        out_specs=pl.BlockSpec((tm, nh), lambda b, t: (b * tpb + t, 0)),
        out_shape=jax.ShapeDtypeStruct((n, nh), F32),
        scratch_shapes=[pltpu.VMEM((1, nh), F32)],
        compiler_params=_cparams("arbitrary", "arbitrary"),
        name="fox_gate",
    )(h, w_f, b_f.reshape(1, nh))


def _fox_attn_kernel(q_ref, k_ref, v_ref, sg_ref, c_ref, ct_ref, o_ref, *, tq):
    hd = pl.program_id(1)
    t = q_ref.shape[0]
    nh = c_ref.shape[1]
    lane = lax.broadcasted_iota(jnp.int32, (t, nh), 1)
    c_col = jnp.sum(jnp.where(lane == hd, c_ref[...], 0.0), axis=-1, keepdims=True)
    c_row = ct_ref[pl.ds(hd, 1), :]
    def query_tile(i):
        lo, hi = i * tq, (i + 1) * tq
        s = lax.dot_general(q_ref[lo:hi, :], k_ref[0:hi, :], (((1,), (1,)), ((), ())),
                            preferred_element_type=F32)
        yield None
        s = s + c_col[lo:hi] - c_row[:, 0:hi]
        r_i = lax.broadcasted_iota(jnp.int32, (tq, hi), 0) + lo
        c_i = lax.broadcasted_iota(jnp.int32, (tq, hi), 1)
        s = jnp.where(c_i <= r_i, s, -jnp.inf)
        m = jnp.max(s, axis=-1, keepdims=True)
        p = jnp.exp(s - m)
        l = jnp.sum(p, axis=-1, keepdims=True)
        yield None
        o = jnp.dot(p.astype(BF16), v_ref[0:hi, :], preferred_element_type=F32) / l
        o_ref[lo:hi, :] = (o * sg_ref[lo:hi, :]).astype(BF16)
        yield True

    _round_robin([query_tile(i) for i in range(t // tq)])


def _fox_attn(q, k, v, sg, c, ct, batch):
    n, d = q.shape
    t = n // batch
    nh = d // FOX_HEAD
    tq = _divisor_tile(t, 768, 16)
    blk = pl.BlockSpec((t, FOX_HEAD), lambda b, h: (b, h))
    return pl.pallas_call(
        functools.partial(_fox_attn_kernel, tq=tq),
        grid=(batch, nh),
        in_specs=[blk, blk, blk, blk,
                  pl.BlockSpec((t, nh), lambda b, h: (b, 0)),
                  pl.BlockSpec((None, nh, t), lambda b, h: (b, 0, 0))],
        out_specs=blk,
        out_shape=jax.ShapeDtypeStruct((n, d), BF16),
        compiler_params=_cparams("parallel", "parallel"),
        name="fox_attn",
    )(q, k, v, sg, c, ct)


def _head_rms_epilogue(scale):
    def epi(y, gain):
        outs = []
        for j in range(y.shape[1] // FOX_HEAD):
            yj = y[:, j * FOX_HEAD:(j + 1) * FOX_HEAD]
            yj = yj * lax.rsqrt(jnp.mean(yj * yj, axis=-1, keepdims=True) + RMS_EPS)
            outs.append(yj * gain[:, j * FOX_HEAD:(j + 1) * FOX_HEAD] * scale)
        return (jnp.concatenate(outs, axis=1),)
    return epi


def _fox_mixer_layer(h, hb, p, g, b, alpha, batch, tm):
    n, d = h.shape
    nh = d // FOX_HEAD
    w_in = p["fox_w_in"]
    qg = jnp.tile(p["fox_q_norm_g"], nh)
    kg = jnp.tile(p["fox_k_norm_g"], nh)
    (q,) = _proj(hb, w_in, 0, d, _head_rms_epilogue(FOX_HEAD ** -0.5), [qg], [BF16], tm, "fox_proj_q")
    (k,) = _proj(hb, w_in, d, d, _head_rms_epilogue(1.0), [kg], [BF16], tm, "fox_proj_k")
    (v,) = _proj(hb, w_in, 2 * d, d, lambda y: (y,), [], [BF16], tm, "fox_proj_v")
    (sg,) = _proj(hb, w_in, 3 * d, d, lambda y: (_sigmoid(y),), [], [F32], tm, "fox_proj_g")
    c = _fox_gate(h, w_in[:, 4 * d:], p["fox_b_f"], batch, tm)
    ct = c.reshape(batch, n // batch, nh).transpose(0, 2, 1)
    z = _fox_attn(q, k, v, sg, c, ct, batch)
    return _mm_res_ln(z, p["fox_w_out"], h, g, b, alpha, tm)


def _rwkv_mix_kernel(h_ref, mu_ref, *refs):
    outs, carry_ref = refs[:-1], refs[-1]

    @pl.when(pl.program_id(1) == 0)
    def _():
        carry_ref[...] = jnp.zeros_like(carry_ref)

    x = h_ref[...]
    tm = x.shape[0]
    row = lax.broadcasted_iota(jnp.int32, (tm, 1), 0)
    prev = jnp.where(row == 0, carry_ref[7:8, :], pltpu.roll(x, 1, axis=0))
    xx = prev - x
    carry_ref[...] = x[tm - 8:, :]
    for j, o_ref in enumerate(outs):
        o_ref[...] = (x + xx * mu_ref[j:j + 1, :]).astype(BF16)


def _rwkv_mix(h, mu, batch, tm):
    n, d = h.shape
    nmix = mu.shape[0]
    tpb = n // batch // tm
    blk = pl.BlockSpec((tm, d), lambda b, t: (b * tpb + t, 0))
    return pl.pallas_call(
        _rwkv_mix_kernel,
        grid=(batch, tpb),
        in_specs=[blk, pl.BlockSpec((nmix, d), lambda b, t: (0, 0))],
        out_specs=[blk] * nmix,
        out_shape=[jax.ShapeDtypeStruct((n, d), BF16)] * nmix,
        scratch_shapes=[pltpu.VMEM((8, d), F32)],
        compiler_params=_cparams("arbitrary", "arbitrary"),
        name="rwkv_mix",
    )(h, mu)


def _lora_kernel(x_ref, wa_ref, wb_ref, bias_ref, o_ref, wab_ref, wbb_ref, *, mid_act, out_act):
    @pl.when(pl.program_id(0) == 0)
    def _():
        wab_ref[...] = wa_ref[...].astype(BF16)
        wbb_ref[...] = wb_ref[...].astype(BF16)

    mid = mid_act(jnp.dot(x_ref[...], wab_ref[...], preferred_element_type=F32))
    y = jnp.dot(mid.astype(BF16), wbb_ref[...], preferred_element_type=F32)
    o_ref[...] = out_act(bias_ref[...] + y)


def _lora(xb, wa, wb, bias, mid_act, out_act, tm, name):
    n, d = xb.shape
    r = wa.shape[1]
    dout = wb.shape[1]
    return pl.pallas_call(
        functools.partial(_lora_kernel, mid_act=mid_act, out_act=out_act),
        grid=(n // tm,),
        in_specs=[
            pl.BlockSpec((tm, d), lambda i: (i, 0)),
            pl.BlockSpec((d, r), lambda i: (0, 0)),
            pl.BlockSpec((r, dout), lambda i: (0, 0)),
            pl.BlockSpec((1, dout), lambda i: (0, 0)),
        ],
        out_specs=pl.BlockSpec((tm, dout), lambda i: (i, 0)),
        out_shape=jax.ShapeDtypeStruct((n, dout), F32),
        scratch_shapes=[pltpu.VMEM((d, r), BF16), pltpu.VMEM((r, dout), BF16)],
        compiler_params=_cparams("arbitrary"),
        name=name,
    )(xb, wa, wb, bias.reshape(1, dout))


def _rwkv_log_decay(z):
    w_log = -(jnp.maximum(-z, 0.0) + jnp.log(1.0 + jnp.exp(-jnp.abs(z)))) - 0.5
    return -jnp.exp(w_log)


RWKV_GROUP = 4
RWKV_UNROLL = 8


def _seg_sum(x, bd):
    hi = x.astype(BF16)
    lo = (x - hi.astype(F32)).astype(BF16)
    return jnp.dot(hi, bd, preferred_element_type=F32) + jnp.dot(lo, bd, preferred_element_type=F32)


def _rwkv_scan_kernel(r_ref, kr_ref, v_ref, lw_ref, a_ref, g_ref, kk_p, ka_p, rk_p, gg_p, gb_p,
                      o_ref, kk_s, k_s, bonus_s, y_s, st_ref, *, chunk, ptile):
    t, w = r_ref.shape
    nhead = w // RWKV_HEAD
    sc = nhead * chunk
    lane_r = lax.broadcasted_iota(jnp.int32, (w, w), 0) // RWKV_HEAD
    lane_c = lax.broadcasted_iota(jnp.int32, (w, w), 1) // RWKV_HEAD
    bd = jnp.where(lane_r == lane_c, 1.0, 0.0).astype(BF16)

    def prologue(i, carry):
        rows = pl.ds(pl.multiple_of(i * ptile, 8), ptile)
        kr = kr_ref[rows, :]
        a = a_ref[rows, :]
        kkr = kr * kk_p[...]
        nrm = jnp.maximum(jnp.sqrt(_seg_sum(kkr * kkr, bd)), 1e-12)
        kk_s[rows, :] = kkr / nrm
        k = kr * (1.0 + (a - 1.0) * ka_p[...])
        k_s[rows, :] = k
        bonus_s[rows, :] = _seg_sum(r_ref[rows, :] * k * rk_p[...], bd) * v_ref[rows, :]
        return carry

    lax.fori_loop(0, t // ptile, prologue, 0)

    st_ref[...] = jnp.zeros_like(st_ref)
    tril = _tril_bf16(chunk, True)
    head_of_lane = lax.broadcasted_iota(jnp.int32, (chunk, w), 1) // RWKV_HEAD
    ri = lax.broadcasted_iota(jnp.int32, (2 * sc, sc), 0)
    ci = lax.broadcasted_iota(jnp.int32, (2 * sc, sc), 1)
    low_mask = ci < jnp.where(ri < sc, ri, ri - sc + 1)
    nsteps = max(1, (chunk - 1).bit_length())

    def stack(x):
        return jnp.concatenate([jnp.where(head_of_lane == hh, x, 0.0) for hh in range(nhead)], axis=0)

    eye = jnp.where(lax.broadcasted_iota(jnp.int32, (sc, sc), 0)
                    == lax.broadcasted_iota(jnp.int32, (sc, sc), 1), 1.0, 0.0)

    def prepare(c):
        start = c * chunk
        rows = pl.ds(start if isinstance(start, int) else pl.multiple_of(start, 16), chunk)
        r = r_ref[rows, :]
        v = v_ref[rows, :]
        lw = lw_ref[rows, :]
        a = a_ref[rows, :]
        kk = kk_s[rows, :]
        k = k_s[rows, :]
        cum = _cumsum_rows(lw, tril)
        e_neg = jnp.exp(-cum)
        at2 = stack(-kk * jnp.exp(cum - lw))
        rt2 = stack(r * jnp.exp(cum))
        bvec = kk * a
        bb2 = stack(bvec * e_neg)
        kb2 = stack(k * e_neg)
        v2 = stack(v)
        ar2 = jnp.concatenate([at2, rt2], axis=0).astype(BF16)
        cl = cum[chunk - 1:chunk]
        e_end = jnp.exp(cl - cum)
        khbh = jnp.concatenate([stack(k * e_end), stack(bvec * e_end)], axis=0).astype(BF16)
        yield None
        pb = jnp.where(low_mask, _dot_nt(ar2, bb2), 0.0)
        pk = jnp.where(low_mask, _dot_nt(ar2, kb2), 0.0)
        m_ab, m_rb = pb[:sc], pb[sc:]
        m_ak, m_rk = pk[:sc], pk[sc:]
        yield None
        u0 = _dot_nn(m_ak, v2)
        y0 = _dot_nn(m_rk, v2)
        tinv = eye + m_ab
        lpow = m_ab
        for _ in range(nsteps - 1):
            yield None
            lpow = _dot_nn(lpow, lpow)
            tinv = tinv + _dot_nn(tinv, lpow)
        return dict(rows=rows, ar2=ar2, tinv=tinv.astype(BF16), m_rb=m_rb.astype(BF16),
                    u0=u0, y0=y0, v2=v2.astype(BF16), khbh=khbh, decay=jnp.exp(cl))

    def chunk_steps(c, j, run):
        pc = yield from prepare(c)
        while run["turn"] != j:
            yield None
        st = run["st"]
        ps = _dot_nt(pc["ar2"], st)
        yield None
        u2 = jnp.dot(pc["tinv"], (ps[:sc] + pc["u0"]).astype(BF16), preferred_element_type=F32)
        u2b = u2.astype(BF16)
        yield None
        y2 = ps[sc:] + pc["y0"] + jnp.dot(pc["m_rb"], u2b, preferred_element_type=F32)
        y = y2[0:chunk]
        for hh in range(1, nhead):
            y = y + y2[hh * chunk:(hh + 1) * chunk]
        y_s[pc["rows"], :] = y
        run["st"] = st * pc["decay"] + _dot_tn(jnp.concatenate([pc["v2"], u2b], axis=0), pc["khbh"])
        run["turn"] = j + 1
        yield True

    def run_group(chunk_ids):
        run = dict(st=st_ref[...], turn=0)
        _round_robin([chunk_steps(c, j, run) for j, c in enumerate(chunk_ids)])
        st_ref[...] = run["st"]

    nchunks = t // chunk

    def body(i, carry):
        run_group([i * RWKV_UNROLL + j for j in range(RWKV_UNROLL)])
        return carry

    lax.fori_loop(0, nchunks // RWKV_UNROLL, body, 0)
    tail = list(range(nchunks - nchunks % RWKV_UNROLL, nchunks))
    if tail:
        run_group(tail)

    inv = 1.0 / RWKV_HEAD

    def epilogue(i, carry):
        rows = pl.ds(pl.multiple_of(i * ptile, 8), ptile)
        y = y_s[rows, :]
        mu = _seg_sum(y, bd) * inv
        yc = y - mu
        var = _seg_sum(yc * yc, bd) * inv
        yn = yc * lax.rsqrt(var + 1e-5 * RWKV_HEAD) * gg_p[...] + gb_p[...]
        o_ref[rows, :] = ((yn + bonus_s[rows, :]) * g_ref[rows, :]).astype(BF16)
        return carry

    lax.fori_loop(0, t // ptile, epilogue, 0)


def _rwkv_scan(r, kr, v, lw, a, g, p, batch):
    n, d = r.shape
    t = n // batch
    chunk = _chunk_len(t)
    w = min(d, RWKV_GROUP * RWKV_HEAD)
    ptile = _divisor_tile(t, 768, 16)
    blk = pl.BlockSpec((t, w), lambda b, j: (b, j))
    prm = pl.BlockSpec((1, w), lambda b, j: (0, j))
    params = [p["rwkv_k_k"], p["rwkv_k_a"], p["rwkv_r_k"], p["rwkv_gn_g"], p["rwkv_gn_b"]]
    return pl.pallas_call(
        functools.partial(_rwkv_scan_kernel, chunk=chunk, ptile=ptile),
        grid=(batch, d // w),
        in_specs=[blk] * 6 + [prm] * 5,
        out_specs=blk,
        out_shape=jax.ShapeDtypeStruct((n, d), BF16),
        scratch_shapes=[pltpu.VMEM((t, w), F32)] * 4 + [pltpu.VMEM((w, w), F32)],
        compiler_params=_cparams("parallel", "parallel"),
        name="rwkv_scan",
    )(r, kr, v, lw, a, g, *[x.reshape(1, d) for x in params])


def _rwkv_mixer_layer(h, hb, p, g, b, alpha, batch, tm):
    n, d = h.shape
    ident = lambda y: y
    xr, xw, xk, xv, xa, xg = _rwkv_mix(h, p["rwkv_mu"], batch, tm)
    (r,) = _proj(xr, p["rwkv_w_r"], 0, d, lambda y: (y,), [], [F32], tm, "rwkv_proj_r")
    (kr,) = _proj(xk, p["rwkv_w_k"], 0, d, lambda y: (y,), [], [F32], tm, "rwkv_proj_k")
    (v,) = _proj(xv, p["rwkv_w_v"], 0, d, lambda y: (y,), [], [F32], tm, "rwkv_proj_v")
    lw = _lora(xw, p["rwkv_w1"], p["rwkv_w2"], p["rwkv_w0"], jnp.tanh, _rwkv_log_decay, tm, "rwkv_lora_w")
    a = _lora(xa, p["rwkv_a1"], p["rwkv_a2"], p["rwkv_a0"], ident, _sigmoid, tm, "rwkv_lora_a")
    gate = _lora(xg, p["rwkv_g1"], p["rwkv_g2"], jnp.zeros((d,), F32), _sigmoid, ident, tm, "rwkv_lora_g")
    z = _rwkv_scan(r, kr, v, lw, a, gate, p, batch)
    return _mm_res_ln(z, p["rwkv_w_out"], h, g, b, alpha, tm)


def _embed_kernel(x_ref, meta_ref, h_ref, hb_ref, sem):
    b = pl.program_id(0)
    j = pl.program_id(1)
    tm = h_ref.shape[0]
    nmeta = meta_ref.shape[0]

    @pl.when(j == 0)
    def _():
        h_ref[0:nmeta, :] = meta_ref[...]
        cp = pltpu.make_async_copy(x_ref.at[b, pl.ds(0, tm - nmeta), :], h_ref.at[pl.ds(nmeta, tm - nmeta), :], sem)
        cp.start()
        cp.wait()

    @pl.when(j > 0)
    def _():
        first = pl.multiple_of(j * tm - nmeta, 8)
        cp = pltpu.make_async_copy(x_ref.at[b, pl.ds(first, tm), :], h_ref, sem)
        cp.start()
        cp.wait()

    hb_ref[...] = h_ref[...].astype(BF16)


def _embed(x, meta, tm):
    batch, seq, d = x.shape
    nmeta = meta.shape[0]
    t = nmeta + seq
    tpb = t // tm
    assert nmeta % 8 == 0 and tm % 8 == 0
    blk = pl.BlockSpec((tm, d), lambda b, j: (b * tpb + j, 0))
    return pl.pallas_call(
        _embed_kernel,
        grid=(batch, tpb),
        in_specs=[pl.BlockSpec(memory_space=pl.ANY), pl.BlockSpec((nmeta, d), lambda b, j: (0, 0))],
        out_specs=[blk, blk],
        out_shape=[jax.ShapeDtypeStruct((batch * t, d), F32), jax.ShapeDtypeStruct((batch * t, d), BF16)],
        scratch_shapes=[pltpu.SemaphoreType.DMA],
        compiler_params=_cparams("parallel", "arbitrary"),
        name="embed",
    )(x, meta.astype(x.dtype))


def kernel(x, meta, ln_mix_g, ln_mix_b, ln_ffn_g, ln_ffn_b, conv_w_in, conv_w, conv_b, conv_w_out, rwkv_mu, rwkv_w_r, rwkv_w_k, rwkv_w_v, rwkv_w0, rwkv_w1, rwkv_w2, rwkv_a0, rwkv_a1, rwkv_a2, rwkv_g1, rwkv_g2, rwkv_k_k, rwkv_k_a, rwkv_r_k, rwkv_gn_g, rwkv_gn_b, rwkv_w_out, hgrn_w_in, hgrn_lb, hgrn_norm_g, hgrn_w_out, fox_w_in, fox_b_f, fox_q_norm_g, fox_k_norm_g, fox_w_out, ffn0_w1, ffn0_w3, ffn0_w2, moe1_router, moe1_router_b, moe1_w1, moe1_w3, moe1_w2, ffn2_w1, ffn2_w3, ffn2_w2, moe3_router, moe3_router_b, moe3_w1, moe3_w3, moe3_w2):
    batch, seq, d = x.shape
    depth = ln_mix_g.shape[0]
    assert depth == 4
    alpha = (2.0 * depth) ** 0.25
    t = N_META + seq
    n = batch * t
    tm = _divisor_tile(t, 768, 16)
    p = dict(
        conv_w_in=conv_w_in, conv_w=conv_w, conv_b=conv_b, conv_w_out=conv_w_out,
        rwkv_mu=rwkv_mu, rwkv_w_r=rwkv_w_r, rwkv_w_k=rwkv_w_k, rwkv_w_v=rwkv_w_v, rwkv_w0=rwkv_w0,
        rwkv_w1=rwkv_w1, rwkv_w2=rwkv_w2, rwkv_a0=rwkv_a0, rwkv_a1=rwkv_a1, rwkv_a2=rwkv_a2,
        rwkv_g1=rwkv_g1, rwkv_g2=rwkv_g2, rwkv_k_k=rwkv_k_k, rwkv_k_a=rwkv_k_a, rwkv_r_k=rwkv_r_k,
        rwkv_gn_g=rwkv_gn_g, rwkv_gn_b=rwkv_gn_b, rwkv_w_out=rwkv_w_out,
        hgrn_w_in=hgrn_w_in, hgrn_lb=hgrn_lb, hgrn_norm_g=hgrn_norm_g, hgrn_w_out=hgrn_w_out,
        fox_w_in=fox_w_in, fox_b_f=fox_b_f, fox_q_norm_g=fox_q_norm_g, fox_k_norm_g=fox_k_norm_g,
        fox_w_out=fox_w_out,
    )
    assert meta.shape[0] == N_META
    h, hb = _embed(x, meta, tm)

    h, hb = _conv_mixer_layer(h, hb, p, ln_mix_g[0], ln_mix_b[0], alpha, batch, tm)
    h, hb = _dense_ffn(hb, h, ffn0_w1, ffn0_w3, ffn0_w2, ln_ffn_g[0], ln_ffn_b[0], alpha, tm)
    h, hb = _rwkv_mixer_layer(h, hb, p, ln_mix_g[1], ln_mix_b[1], alpha, batch, tm)
    h, hb = _moe_ffn(h, hb, moe1_router, moe1_router_b, moe1_w1, moe1_w3, moe1_w2,
                     ln_ffn_g[1], ln_ffn_b[1], alpha, tm)
    h, hb = _hgrn_mixer_layer(h, hb, p, 2, ln_mix_g[2], ln_mix_b[2], alpha, batch, tm)
    h, hb = _dense_ffn(hb, h, ffn2_w1, ffn2_w3, ffn2_w2, ln_ffn_g[2], ln_ffn_b[2], alpha, tm)
    h, hb = _fox_mixer_layer(h, hb, p, ln_mix_g[3], ln_mix_b[3], alpha, batch, tm)
    h, hb = _moe_ffn(h, hb, moe3_router, moe3_router_b, moe3_w1, moe3_w3, moe3_w2,
                     ln_ffn_g[3], ln_ffn_b[3], alpha, tm)
    return h.reshape(batch, t, d)[:, N_META:]
```

```python
import functools

import jax
import jax.numpy as jnp
from jax import lax
from jax.experimental import pallas as pl
from jax.experimental.pallas import tpu as pltpu

F32 = jnp.float32
BF16 = jnp.bfloat16

N_META = 16
LN_EPS = 1e-5
RMS_EPS = 1e-6
RWKV_HEAD = 64
HGRN_HEAD = 128
FOX_HEAD = 128
TOP_K = 2
LANES = 128
VMEM_LIMIT_BYTES = 56 * 2**20


def _cparams(*sem):
    return pltpu.CompilerParams(dimension_semantics=sem, vmem_limit_bytes=VMEM_LIMIT_BYTES)


def _divisor_tile(n, cap, mult):
    best = None
    for d in range(mult, min(n, cap) + 1, mult):
        if n % d == 0:
            best = d
    assert best is not None, (n, cap, mult)
    return best


def _layer_norm(y, g, b):
    mu = jnp.mean(y, axis=-1, keepdims=True)
    yc = y - mu
    var = jnp.mean(yc * yc, axis=-1, keepdims=True)
    return yc * lax.rsqrt(var + LN_EPS) * g + b


def _sigmoid(x):
    return 1.0 / (1.0 + jnp.exp(-x))


def _silu(x):
    return x * _sigmoid(x)


def _mm_res_ln_kernel(z_ref, w_ref, h_ref, g_ref, b_ref, o_ref, ob_ref, *, nk, alpha):
    k = pl.program_id(1)
    part = jnp.dot(z_ref[...], w_ref[...], preferred_element_type=F32)

    @pl.when(k == 0)
    def _():
        o_ref[...] = part

    @pl.when(k > 0)
    def _():
        o_ref[...] += part

    @pl.when(k == nk - 1)
    def _():
        y = _layer_norm(alpha * h_ref[...] + o_ref[...], g_ref[...], b_ref[...])
        o_ref[...] = y
        ob_ref[...] = y.astype(BF16)


WEIGHT_SLICE_ROWS = 256


def _load_weight_bf16(w_hbm, wb_ref, stage_ref, sem):
    rows = stage_ref.shape[1]
    nslice = w_hbm.shape[0] // rows

    def copy(s):
        return pltpu.make_async_copy(w_hbm.at[pl.ds(s * rows, rows), :], stage_ref.at[s % 2], sem.at[s % 2])

    copy(0).start()
    for s in range(nslice):
        if s + 1 < nslice:
            copy(s + 1).start()
        copy(s).wait()
        wb_ref[s * rows:(s + 1) * rows, :] = stage_ref[s % 2].astype(BF16)


def _mm_res_ln_resident_kernel(z_ref, w_ref, h_ref, g_ref, b_ref, o_ref, ob_ref, wb_ref, stage_ref, sem,
                               *, alpha):
    @pl.when(pl.program_id(0) == 0)
    def _():
        _load_weight_bf16(w_ref, wb_ref, stage_ref, sem)

    tm = z_ref.shape[0]
    split = min(tm, -(-tm // 32) * 16)
    spans = [(0, split), (split, tm)] if split < tm else [(0, tm)]
    dots = [jnp.dot(z_ref[lo:hi, :], wb_ref[...], preferred_element_type=F32) for lo, hi in spans]
    for (lo, hi), part in zip(spans, dots):
        y = _layer_norm(alpha * h_ref[lo:hi, :] + part, g_ref[...], b_ref[...])
        o_ref[lo:hi, :] = y
        ob_ref[lo:hi, :] = y.astype(BF16)


RESIDENT_WEIGHT_BYTES = 16 * 2**20


def _mm_res_ln_resident(z, w, h, g, b, alpha, tm):
    n, kdim = z.shape
    d = w.shape[1]
    ws = _divisor_tile(kdim, WEIGHT_SLICE_ROWS, 8)
    const = lambda m: (0, 0)
    return pl.pallas_call(
        functools.partial(_mm_res_ln_resident_kernel, alpha=alpha),
        grid=(n // tm,),
        in_specs=[
            pl.BlockSpec((tm, kdim), lambda m: (m, 0)),
            pl.BlockSpec(memory_space=pl.ANY),
            pl.BlockSpec((tm, d), lambda m: (m, 0)),
            pl.BlockSpec((1, d), const),
            pl.BlockSpec((1, d), const),
        ],
        out_specs=[
            pl.BlockSpec((tm, d), lambda m: (m, 0)),
            pl.BlockSpec((tm, d), lambda m: (m, 0)),
        ],
        out_shape=[jax.ShapeDtypeStruct((n, d), F32), jax.ShapeDtypeStruct((n, d), BF16)],
        scratch_shapes=[pltpu.VMEM((kdim, d), BF16), pltpu.VMEM((2, ws, d), F32),
                        pltpu.SemaphoreType.DMA((2,))],
        compiler_params=_cparams("arbitrary"),
        name="mm_res_ln_resident",
    )(z, w, h, g.reshape(1, d), b.reshape(1, d))


def _mm_res_ln(z, w, h, g, b, alpha, tm):
    n, kdim = z.shape
    d = w.shape[1]
    if kdim * d * 4 <= RESIDENT_WEIGHT_BYTES:
        return _mm_res_ln_resident(z, w, h, g, b, alpha, tm)
    w = w.astype(BF16)
    tk = _divisor_tile(kdim, 1408, LANES)
    nk = kdim // tk
    return pl.pallas_call(
        functools.partial(_mm_res_ln_kernel, nk=nk, alpha=alpha),
        grid=(n // tm, nk),
        in_specs=[
            pl.BlockSpec((tm, tk), lambda m, k: (m, k)),
            pl.BlockSpec((tk, d), lambda m, k: (k, 0)),
            pl.BlockSpec((tm, d), lambda m, k: (m, 0)),
            pl.BlockSpec((1, d), lambda m, k: (0, 0)),
            pl.BlockSpec((1, d), lambda m, k: (0, 0)),
        ],
        out_specs=[
            pl.BlockSpec((tm, d), lambda m, k: (m, 0)),
            pl.BlockSpec((tm, d), lambda m, k: (m, 0)),
        ],
        out_shape=[jax.ShapeDtypeStruct((n, d), F32), jax.ShapeDtypeStruct((n, d), BF16)],
        compiler_params=_cparams("parallel", "arbitrary"),
        name="mm_res_ln",
    )(z, w, h, g.reshape(1, d), b.reshape(1, d))


def _ffn_up_kernel(exp_ref, nact_ref, x_ref, w1_ref, w3_ref, o_ref, w1b_ref, w3b_ref):
    c = pl.program_id(1)
    prev = exp_ref[jnp.maximum(c - 1, 0)]
    new_weights = jnp.logical_or(c == 0, exp_ref[c] != prev)

    @pl.when(new_weights)
    def _():
        w1b_ref[...] = w1_ref[...].astype(BF16)
        w3b_ref[...] = w3_ref[...].astype(BF16)

    @pl.when(c < nact_ref[0])
    def _():
        x = x_ref[...]
        a = jnp.dot(x, w1b_ref[...], preferred_element_type=F32)
        bb = jnp.dot(x, w3b_ref[...], preferred_element_type=F32)
        o_ref[...] = (_silu(a) * bb).astype(BF16)

    @pl.when(c >= nact_ref[0])
    def _():
        o_ref[...] = jnp.zeros_like(o_ref)


def _ffn_up(x, w1, w3, chunk_expert, n_active, rows):
    p, d = x.shape
    f = w1.shape[2]
    tf = _divisor_tile(f, 512, LANES)
    grid_spec = pltpu.PrefetchScalarGridSpec(
        num_scalar_prefetch=2,
        grid=(f // tf, p // rows),
        in_specs=[
            pl.BlockSpec((rows, d), lambda j, c, e, na: (c, 0)),
            pl.BlockSpec((None, d, tf), lambda j, c, e, na: (e[c], 0, j)),
            pl.BlockSpec((None, d, tf), lambda j, c, e, na: (e[c], 0, j)),
        ],
        out_specs=pl.BlockSpec((rows, tf), lambda j, c, e, na: (c, j)),
        scratch_shapes=[pltpu.VMEM((d, tf), BF16), pltpu.VMEM((d, tf), BF16)],
    )
    return pl.pallas_call(
        _ffn_up_kernel,
        grid_spec=grid_spec,
        out_shape=jax.ShapeDtypeStruct((p, f), BF16),
        compiler_params=_cparams("arbitrary", "arbitrary"),
        name="ffn_up",
    )(chunk_expert, n_active, x, w1, w3)


def _dense_ffn(hb, h, w1, w3, w2, g, b, alpha, tm):
    n = hb.shape[0]
    nchunks = n // tm
    hmid = _ffn_up(hb, w1[None], w3[None], jnp.zeros((nchunks,), jnp.int32),
                   jnp.full((1,), nchunks, jnp.int32), tm)
    return _mm_res_ln(hmid, w2, h, g, b, alpha, tm)


def _conv_proj_kernel(x_ref, wb_ref, wc_ref, wh_ref, cw_ref, cb_ref, o_ref,
                      wbb_ref, wcb_ref, whb_ref, carry_ref):
    bi = pl.program_id(1)
    ti = pl.program_id(2)

    @pl.when(jnp.logical_and(bi == 0, ti == 0))
    def _():
        wbb_ref[...] = wb_ref[...].astype(BF16)
        wcb_ref[...] = wc_ref[...].astype(BF16)
        whb_ref[...] = wh_ref[...].astype(BF16)

    @pl.when(ti == 0)
    def _():
        carry_ref[...] = jnp.zeros_like(carry_ref)

    x = x_ref[...]
    gate_b = jnp.dot(x, wbb_ref[...], preferred_element_type=F32)
    gate_c = jnp.dot(x, wcb_ref[...], preferred_element_type=F32)
    hh = jnp.dot(x, whb_ref[...], preferred_element_type=F32)
    u = gate_c * hh
    tm = u.shape[0]
    prev1 = carry_ref[7:8, :]
    prev2 = carry_ref[6:7, :]
    row = lax.broadcasted_iota(jnp.int32, (tm, 1), 0)
    r1 = jnp.where(row == 0, prev1, pltpu.roll(u, 1, axis=0))
    r2 = jnp.where(row == 0, prev2, jnp.where(row == 1, prev1, pltpu.roll(u, 2, axis=0)))
    v = cw_ref[0:1, :] * r2 + cw_ref[1:2, :] * r1 + cw_ref[2:3, :] * u + cb_ref[...]
    carry_ref[...] = u[tm - 8:, :]
    o_ref[...] = (gate_b * v).astype(BF16)


def _conv_proj(hb, w_in, conv_w, conv_b, batch, tm):
    n, d = hb.shape
    tn = _divisor_tile(d, 512, LANES)
    nd = d // tn
    tpb = n // batch // tm
    return pl.pallas_call(
        _conv_proj_kernel,
        grid=(nd, batch, tpb),
        in_specs=[
            pl.BlockSpec((tm, d), lambda j, bi, ti: (bi * tpb + ti, 0)),
            pl.BlockSpec((d, tn), lambda j, bi, ti: (0, j)),
            pl.BlockSpec((d, tn), lambda j, bi, ti: (0, nd + j)),
            pl.BlockSpec((d, tn), lambda j, bi, ti: (0, 2 * nd + j)),
            pl.BlockSpec((3, tn), lambda j, bi, ti: (0, j)),
            pl.BlockSpec((1, tn), lambda j, bi, ti: (0, j)),
        ],
        out_specs=pl.BlockSpec((tm, tn), lambda j, bi, ti: (bi * tpb + ti, j)),
        out_shape=jax.ShapeDtypeStruct((n, d), BF16),
        scratch_shapes=[pltpu.VMEM((d, tn), BF16)] * 3 + [pltpu.VMEM((8, tn), F32)],
        compiler_params=_cparams("arbitrary", "arbitrary", "arbitrary"),
        name="conv_proj",
    )(hb, w_in, w_in, w_in, conv_w, conv_b.reshape(1, d))


def _conv_mixer_layer(h, hb, p, g, b, alpha, batch, tm):
    z = _conv_proj(hb, p["conv_w_in"], p["conv_w"], p["conv_b"], batch, tm)
    return _mm_res_ln(z, p["conv_w_out"], h, g, b, alpha, tm)


MOE_ROWS = 128


def _split_bf16(a):
    hi = a.astype(BF16)
    lo = (a - hi.astype(F32)).astype(BF16)
    return hi, lo


def _dot_f32(a, b):
    ah, al = _split_bf16(a)
    bh, bl = _split_bf16(b)
    return (jnp.dot(ah, bh, preferred_element_type=F32)
            + (jnp.dot(ah, bl, preferred_element_type=F32)
               + jnp.dot(al, bh, preferred_element_type=F32)))


def _router_kernel(h_ref, w_ref, b_ref, info_ref, cnt_ref, carry_ref):
    i = pl.program_id(0)

    @pl.when(i == 0)
    def _():
        carry_ref[...] = jnp.zeros_like(carry_ref)

    logits = _dot_f32(h_ref[...], w_ref[...]) + b_ref[...]
    tm, ne = logits.shape
    lane = lax.broadcasted_iota(jnp.int32, (tm, ne), 1)
    m1 = jnp.max(logits, axis=-1, keepdims=True)
    i1 = jnp.min(jnp.where(logits == m1, lane, ne), axis=-1, keepdims=True)
    mask1 = lane == i1
    rest = jnp.where(mask1, -jnp.inf, logits)
    m2 = jnp.max(rest, axis=-1, keepdims=True)
    i2 = jnp.min(jnp.where(rest == m2, lane, ne), axis=-1, keepdims=True)
    mask2 = lane == i2
    dd = jnp.exp(m2 - m1)
    g1 = 1.0 / (1.0 + dd)
    g2 = dd / (1.0 + dd)
    sel = jnp.where(jnp.logical_or(mask1, mask2), 1.0, 0.0)
    r_i = lax.broadcasted_iota(jnp.int32, (tm, tm), 0)
    c_i = lax.broadcasted_iota(jnp.int32, (tm, tm), 1)
    tril = jnp.where(c_i < r_i, 1.0, 0.0).astype(BF16)
    rank = jnp.dot(tril, sel.astype(BF16), preferred_element_type=F32) + carry_ref[...]
    r1 = jnp.sum(jnp.where(mask1, rank, 0.0), axis=-1, keepdims=True)
    r2 = jnp.sum(jnp.where(mask2, rank, 0.0), axis=-1, keepdims=True)
    info = jnp.where(lane == 0, i1.astype(F32),
           jnp.where(lane == 1, i2.astype(F32),
           jnp.where(lane == 2, g1,
           jnp.where(lane == 3, g2,
           jnp.where(lane == 4, r1,
           jnp.where(lane == 5, r2, 0.0))))))
    info_ref[...] = info
    total = carry_ref[...] + jnp.sum(sel, axis=0, keepdims=True)
    carry_ref[...] = total
    cnt_ref[...] = total


def _router(h, w, b, tm):
    n, d = h.shape
    ne = w.shape[1]
    assert ne >= 6
    return pl.pallas_call(
        _router_kernel,
        grid=(n // tm,),
        in_specs=[
            pl.BlockSpec((tm, d), lambda i: (i, 0)),
            pl.BlockSpec((d, ne), lambda i: (0, 0)),
            pl.BlockSpec((1, ne), lambda i: (0, 0)),
        ],
        out_specs=[
            pl.BlockSpec((tm, ne), lambda i: (i, 0)),
            pl.BlockSpec((1, ne), lambda i: (0, 0)),
        ],
        out_shape=[jax.ShapeDtypeStruct((n, ne), F32), jax.ShapeDtypeStruct((1, ne), F32)],
        scratch_shapes=[pltpu.VMEM((1, ne), F32)],
        compiler_params=_cparams("arbitrary"),
        name="moe_router",
    )(h, w, b.reshape(1, ne))


def _row_copy(src_hbm, row, dst_vmem, r, sem):
    return pltpu.make_async_copy(src_hbm.at[pl.ds(row, 1), :], dst_vmem.at[pl.ds(r, 1), :], sem)


def _rows_wait(src_hbm, dst_vmem, sem):
    pltpu.make_async_copy(src_hbm.at[pl.ds(0, dst_vmem.shape[0]), :], dst_vmem, sem).wait()


GATHER_SLOTS = 3


def _moe_gather_kernel(src_ref, nact_ref, h_ref, o_ref, buf_ref, sem):
    c = pl.program_id(0)
    rows = buf_ref.shape[1]

    nact = nact_ref[0]
    slot = c % GATHER_SLOTS
    slot1 = (c + 1) % GATHER_SLOTS
    slot2 = (c + 2) % GATHER_SLOTS

    def start(chunk, to_slot, r):
        _row_copy(h_ref, src_ref[chunk * rows + r], buf_ref.at[to_slot], r, sem.at[to_slot]).start()

    @pl.when(jnp.logical_and(c == 0, nact > 0))
    def _():
        lax.fori_loop(0, rows, lambda r, carry: (start(0, slot, r), carry)[1], 0, unroll=8)

    @pl.when(jnp.logical_and(c == 0, nact > 1))
    def _():
        lax.fori_loop(0, rows, lambda r, carry: (start(1, slot1, r), carry)[1], 0, unroll=8)

    @pl.when(c < nact)
    def _():
        _rows_wait(h_ref, buf_ref.at[slot], sem.at[slot])
        nxt = jnp.minimum(c + 2, nact - 1)
        for r in range(rows):
            start(nxt, slot2, r)
        o_ref[...] = buf_ref[slot].astype(BF16)

        @pl.when(c == nact - 1)
        def _():
            _rows_wait(h_ref, buf_ref.at[slot2], sem.at[slot2])

            @pl.when(nact > 1)
            def _():
                _rows_wait(h_ref, buf_ref.at[slot1], sem.at[slot1])

    @pl.when(c >= nact)
    def _():
        o_ref[...] = jnp.zeros_like(o_ref)


def _moe_gather(h, src, n_active, p, rows):
    n, d = h.shape
    grid_spec = pltpu.PrefetchScalarGridSpec(
        num_scalar_prefetch=2,
        grid=(p // rows,),
        in_specs=[pl.BlockSpec(memory_space=pl.ANY)],
        out_specs=pl.BlockSpec((rows, d), lambda c, s, na: (c, 0)),
        scratch_shapes=[pltpu.VMEM((GATHER_SLOTS, rows, d), F32), pltpu.SemaphoreType.DMA((GATHER_SLOTS,))],
    )
    return pl.pallas_call(
        _moe_gather_kernel,
        grid_spec=grid_spec,
        out_shape=jax.ShapeDtypeStruct((p, d), BF16),
        compiler_params=_cparams("arbitrary"),
        name="moe_gather",
    )(src, n_active, h)


def _moe_combine_kernel(p1_ref, p2_ref, y_ref, h_ref, info_ref, g_ref, b_ref, o_ref, ob_ref,
                        buf1_ref, buf2_ref, sem, *, alpha):
    i = pl.program_id(0)
    tm = buf1_ref.shape[1]
    ntile = pl.num_programs(0)
    last = ntile - 1
    slot = i % GATHER_SLOTS
    slot1 = (i + 1) % GATHER_SLOTS
    slot2 = (i + 2) % GATHER_SLOTS

    def start(tile, to_slot, r):
        _row_copy(y_ref, p1_ref[tile * tm + r], buf1_ref.at[to_slot], r, sem.at[to_slot]).start()
        _row_copy(y_ref, p2_ref[tile * tm + r], buf2_ref.at[to_slot], r, sem.at[to_slot]).start()

    def wait(from_slot):
        _rows_wait(y_ref, buf1_ref.at[from_slot], sem.at[from_slot])
        _rows_wait(y_ref, buf2_ref.at[from_slot], sem.at[from_slot])

    @pl.when(i == 0)
    def _():
        lax.fori_loop(0, tm, lambda r, c: (start(0, slot, r), c)[1], 0, unroll=8)

    @pl.when(jnp.logical_and(i == 0, ntile > 1))
    def _():
        lax.fori_loop(0, tm, lambda r, c: (start(1, slot1, r), c)[1], 0, unroll=8)

    wait(slot)
    nxt = jnp.minimum(i + 2, last)
    for r in range(tm):
        start(nxt, slot2, r)
    info = info_ref[...]
    y = alpha * h_ref[...] + (info[:, 2:3] * buf1_ref[slot] + info[:, 3:4] * buf2_ref[slot])
    y = _layer_norm(y, g_ref[...], b_ref[...])
    o_ref[...] = y
    ob_ref[...] = y.astype(BF16)

    @pl.when(i == last)
    def _():
        wait(slot2)

        @pl.when(ntile > 1)
        def _():
            wait(slot1)


def _moe_combine(y, h, info, p1, p2, g, b, alpha, tm):
    n, d = h.shape
    ne = info.shape[1]
    grid_spec = pltpu.PrefetchScalarGridSpec(
        num_scalar_prefetch=2,
        grid=(n // tm,),
        in_specs=[
            pl.BlockSpec(memory_space=pl.ANY),
            pl.BlockSpec((tm, d), lambda i, a, c: (i, 0)),
            pl.BlockSpec((tm, ne), lambda i, a, c: (i, 0)),
            pl.BlockSpec((1, d), lambda i, a, c: (0, 0)),
            pl.BlockSpec((1, d), lambda i, a, c: (0, 0)),
        ],
        out_specs=[
            pl.BlockSpec((tm, d), lambda i, a, c: (i, 0)),
            pl.BlockSpec((tm, d), lambda i, a, c: (i, 0)),
        ],
        scratch_shapes=[pltpu.VMEM((GATHER_SLOTS, tm, d), F32), pltpu.VMEM((GATHER_SLOTS, tm, d), F32),
                        pltpu.SemaphoreType.DMA((GATHER_SLOTS,))],
    )
    return pl.pallas_call(
        functools.partial(_moe_combine_kernel, alpha=alpha),
        grid_spec=grid_spec,
        out_shape=[jax.ShapeDtypeStruct((n, d), F32), jax.ShapeDtypeStruct((n, d), BF16)],
        compiler_params=_cparams("arbitrary"),
        name="moe_combine",
    )(p1, p2, y, h, info, g.reshape(1, d), b.reshape(1, d))


MOE_PASS_BLOCKS = 18
MOE_F_TILE = 256
MOE_BLOCK_UNITS = 8


def _expert_ffn_kernel(pe_ref, ps_ref, pn_ref, nu_ref, xs_ref, w1_ref, w3_ref, w2_ref, y_ref,
                       x_buf, acc_ref, w1b_ref, w3b_ref, w2b_ref, sem):
    p = pl.program_id(0)
    f = pl.program_id(1)
    nf = pl.num_programs(1)
    nb = pn_ref[p]
    sb = MOE_ROWS
    start = pl.multiple_of(ps_ref[p], MOE_ROWS)
    nblock = nb // MOE_BLOCK_UNITS

    @pl.when(nb > 0)
    def _():
        @pl.when(f == 0)
        def _():
            cp = pltpu.make_async_copy(xs_ref.at[pl.ds(start, x_buf.shape[0]), :], x_buf, sem.at[0])
            cp.start()
            cp.wait()

        w1b_ref[...] = w1_ref[...].astype(BF16)
        w3b_ref[...] = w3_ref[...].astype(BF16)

        @pl.when(nblock == 0)
        def _():
            w2b_ref[...] = w2_ref[...].astype(BF16)

        def up(unit, nunit):
            x = x_buf[pl.ds(pl.multiple_of(unit * sb, sb), nunit * sb), :]
            a = jnp.dot(x, w1b_ref[...], preferred_element_type=F32)
            bb = jnp.dot(x, w3b_ref[...], preferred_element_type=F32)
            return (_silu(a) * bb).astype(BF16)

        def out_copy(i):
            rows = pl.ds(pl.multiple_of(i * sb, sb), sb)
            dst = pl.ds(pl.multiple_of(start + i * sb, sb), sb)
            return pltpu.make_async_copy(acc_ref.at[rows, :], y_ref.at[dst, :], sem.at[1])

        def down(unit, nunit, hmid, first):
            rows = pl.ds(pl.multiple_of(unit * sb, sb), nunit * sb)
            part = jnp.dot(hmid, w2b_ref[...], preferred_element_type=F32)
            if first:
                acc_ref[rows, :] = part
            else:
                acc_ref[rows, :] += part

            @pl.when(f == nf - 1)
            def _():
                for u in range(nunit):
                    out_copy(unit + u).start()

        def sweep(first):
            bu = MOE_BLOCK_UNITS

            @pl.when(nblock > 0)
            def _():
                def body(i, hprev):
                    down(bu * (i - 1), bu, hprev, first)
                    return up(bu * i, bu)

                h0 = up(0, bu)
                w2b_ref[...] = w2_ref[...].astype(BF16)
                down(bu * (nblock - 1), bu, lax.fori_loop(1, nblock, body, h0), first)

            done = nblock * bu
            size = bu // 2
            while size >= 1:
                @pl.when((nb - done) & size != 0)
                def _(done=done, size=size):
                    down(done, size, up(done, size), first)

                done = done + ((nb - done) & size)
                size //= 2

        @pl.when(f == 0)
        def _():
            sweep(True)

        @pl.when(f > 0)
        def _():
            sweep(False)

        @pl.when(f == nf - 1)
        def _():
            lax.fori_loop(0, nb, lambda i, c: (out_copy(i).wait(), c)[1], 0)

    @pl.when(jnp.logical_and(p == pl.num_programs(0) - 1, f == nf - 1))
    def _():
        acc_ref[0:sb, :] = jnp.zeros((sb, acc_ref.shape[1]), F32)

        def zero_copy(i):
            dst = pl.ds(pl.multiple_of(i * sb, sb), sb)
            return pltpu.make_async_copy(acc_ref.at[0:sb, :], y_ref.at[dst, :], sem.at[1])

        n_blocks = y_ref.shape[0] // sb
        lax.fori_loop(nu_ref[1], n_blocks, lambda i, c: (zero_copy(i).start(), c)[1], 0)
        lax.fori_loop(nu_ref[1], n_blocks, lambda i, c: (zero_copy(i).wait(), c)[1], 0)


def _expert_ffn(xs, w1, w3, w2, pass_expert, pass_start, pass_nb, n_used, p_rows):
    d = xs.shape[1]
    f = w1.shape[2]
    tf = _divisor_tile(f, MOE_F_TILE, LANES)
    nf = f // tf
    npass = pass_expert.shape[0]
    r = MOE_PASS_BLOCKS * MOE_ROWS

    def fidx(p, j, nu):
        return jnp.where(p < nu[0], j, nf - 1)

    grid_spec = pltpu.PrefetchScalarGridSpec(
        num_scalar_prefetch=4,
        grid=(npass, nf),
        in_specs=[
            pl.BlockSpec(memory_space=pl.ANY),
            pl.BlockSpec((None, d, tf), lambda p, j, pe, ps, pn, nu: (pe[p], 0, fidx(p, j, nu))),
            pl.BlockSpec((None, d, tf), lambda p, j, pe, ps, pn, nu: (pe[p], 0, fidx(p, j, nu))),
            pl.BlockSpec((None, tf, d), lambda p, j, pe, ps, pn, nu: (pe[p], fidx(p, j, nu), 0)),
        ],
        out_specs=pl.BlockSpec(memory_space=pl.ANY),
        scratch_shapes=[
            pltpu.VMEM((r, d), BF16), pltpu.VMEM((r, d), F32),
            pltpu.VMEM((d, tf), BF16), pltpu.VMEM((d, tf), BF16), pltpu.VMEM((tf, d), BF16),
            pltpu.SemaphoreType.DMA((2,)),
        ],
    )
    return pl.pallas_call(
        _expert_ffn_kernel,
        grid_spec=grid_spec,
        out_shape=jax.ShapeDtypeStruct((p_rows, d), F32),
        compiler_params=_cparams("arbitrary", "arbitrary"),
        name="expert_ffn",
    )(pass_expert, pass_start, pass_nb, n_used, xs, w1, w3, w2)


def _moe_ffn(h, hb, router_w, router_b, w1, w3, w2, g, b, alpha, tm):
    n, d = h.shape
    ne = router_w.shape[1]
    rows = MOE_ROWS
    info, counts = _router(h, router_w, router_b, tm)
    counts = counts[0].astype(jnp.int32)
    nblk_e = (counts + rows - 1) // rows
    blk_end = jnp.cumsum(nblk_e)
    starts = (blk_end - nblk_e) * rows
    n_blocks = (n * TOP_K + ne * (rows - 1)) // rows
    p = n_blocks * rows
    n_active = blk_end[-1:].astype(jnp.int32)
    npass_e = (nblk_e + MOE_PASS_BLOCKS - 1) // MOE_PASS_BLOCKS
    pass_end = jnp.cumsum(npass_e)
    max_pass = n_blocks // MOE_PASS_BLOCKS + ne
    pidx = jnp.arange(max_pass, dtype=jnp.int32)
    n_used = jnp.stack([pass_end[-1], blk_end[-1]]).astype(jnp.int32)
    last_expert = jnp.sum(pass_end < pass_end[-1]).astype(jnp.int32)
    pass_expert = jnp.minimum(jnp.sum(pidx[:, None] >= pass_end[None, :], axis=1), last_expert).astype(jnp.int32)
    local = pidx - (pass_end - npass_e)[pass_expert]
    pass_start = (starts[pass_expert] + local * (MOE_PASS_BLOCKS * rows)).astype(jnp.int32)
    pass_nb = jnp.where(pidx < n_used[0],
                        jnp.clip(nblk_e[pass_expert] - local * MOE_PASS_BLOCKS, 0, MOE_PASS_BLOCKS),
                        0).astype(jnp.int32)
    pass_start = jnp.where(pass_nb > 0, pass_start, 0).astype(jnp.int32)
    i1 = info[:, 0].astype(jnp.int32)
    i2 = info[:, 1].astype(jnp.int32)
    p1 = starts[i1] + info[:, 4].astype(jnp.int32)
    p2 = starts[i2] + info[:, 5].astype(jnp.int32)
    tok = jnp.arange(n, dtype=jnp.int32)
    p_in = p + MOE_PASS_BLOCKS * rows
    src = jnp.zeros((p_in,), jnp.int32).at[jnp.concatenate([p1, p2])].set(jnp.concatenate([tok, tok]))
    xs = _moe_gather(h, src, n_active, p_in, rows)
    y = _expert_ffn(xs, w1, w3, w2, pass_expert, pass_start, pass_nb, n_used, p)
    tmc = _divisor_tile(n, 384, 16)
    return _moe_combine(y, h, info, p1, p2, g, b, alpha, tmc)


def _proj_kernel(*refs, epilogue, n_extra, n_out):
    x_ref, w_ref = refs[0], refs[1]
    extra = refs[2:2 + n_extra]
    outs = refs[2 + n_extra:2 + n_extra + n_out]
    wb_ref = refs[2 + n_extra + n_out]

    @pl.when(pl.program_id(1) == 0)
    def _():
        wb_ref[...] = w_ref[...].astype(BF16)

    y = jnp.dot(x_ref[...], wb_ref[...], preferred_element_type=F32)
    res = epilogue(y, *[e[...] for e in extra])
    for o_ref, r in zip(outs, res):
        o_ref[...] = r.astype(o_ref.dtype)


def _proj(xb, w, col0, ncols, epilogue, extras, out_dtypes, tm, name, tn_cap=1024):
    n, kdim = xb.shape
    tn = _divisor_tile(ncols, tn_cap, LANES)
    assert col0 % tn == 0
    off = col0 // tn
    outs = pl.pallas_call(
        functools.partial(_proj_kernel, epilogue=epilogue, n_extra=len(extras), n_out=len(out_dtypes)),
        grid=(ncols // tn, n // tm),
        in_specs=[
            pl.BlockSpec((tm, kdim), lambda j, m: (m, 0)),
            pl.BlockSpec((kdim, tn), lambda j, m: (0, off + j)),
        ] + [pl.BlockSpec((1, tn), lambda j, m: (0, j))] * len(extras),
        out_specs=[pl.BlockSpec((tm, tn), lambda j, m: (m, j))] * len(out_dtypes),
        out_shape=[jax.ShapeDtypeStruct((n, ncols), dt) for dt in out_dtypes],
        scratch_shapes=[pltpu.VMEM((kdim, tn), BF16)],
        compiler_params=_cparams("arbitrary", "arbitrary"),
        name=name,
    )(xb, w, *[e.reshape(1, ncols) for e in extras])
    return outs


def _tril_bf16(c, inclusive):
    r_i = lax.broadcasted_iota(jnp.int32, (c, c), 0)
    c_i = lax.broadcasted_iota(jnp.int32, (c, c), 1)
    keep = (c_i <= r_i) if inclusive else (c_i < r_i)
    return jnp.where(keep, 1.0, 0.0).astype(BF16)


def _cumsum_rows(x, tril):
    hi = x.astype(BF16)
    r1 = x - hi.astype(F32)
    mid = r1.astype(BF16)
    lo = (r1 - mid.astype(F32)).astype(BF16)
    return (jnp.dot(tril, hi, preferred_element_type=F32)
            + (jnp.dot(tril, mid, preferred_element_type=F32)
               + jnp.dot(tril, lo, preferred_element_type=F32)))


def _dot_nt(a, b):
    return lax.dot_general(a.astype(BF16), b.astype(BF16), (((1,), (1,)), ((), ())),
                           preferred_element_type=F32)


def _dot_tn(a, b):
    return lax.dot_general(a.astype(BF16), b.astype(BF16), (((0,), (0,)), ((), ())),
                           preferred_element_type=F32)


def _dot_nn(a, b):
    return jnp.dot(a.astype(BF16), b.astype(BF16), preferred_element_type=F32)


def _chunk_len(t):
    return _divisor_tile(t, 64, 16)


def _round_robin(gens):
    done = [None] * len(gens)
    while any(d is None for d in done):
        for j, gen in enumerate(gens):
            if done[j] is None:
                done[j] = next(gen)
    return done


HGRN_SUB = 16


def _hgrn_scan_kernel(q_ref, lf_ref, v_ref, gs_ref, ng_ref, o_ref, st_ref, *, chunk):
    t, w = q_ref.shape
    nhead = w // HGRN_HEAD
    nsub = chunk // HGRN_SUB
    st_ref[...] = jnp.zeros_like(st_ref)
    tril = _tril_bf16(chunk, True)
    row16 = lax.broadcasted_iota(jnp.int32, (HGRN_SUB, 1), 0)

    def head_chunk(q, lf, v, st):
        k = 1.0 - jnp.exp(lf)
        cum = _cumsum_rows(lf, tril)
        yield None
        o_inter = _dot_nt(q * jnp.exp(cum), st)
        vb = v.astype(BF16)
        cl = cum[chunk - 1:chunk]
        kd = k * jnp.exp(cl - cum)
        st_new = st * jnp.exp(cl) + _dot_tn(v, kd)
        yield None
        outs = []
        for i in range(nsub):
            lo, hi = i * HGRN_SUB, (i + 1) * HGRN_SUB
            qi, ki, vi, cumi = q[lo:hi], k[lo:hi], v[lo:hi], cum[lo:hi]
            oi = o_inter[lo:hi]
            if i > 0:
                ci = cum[lo:lo + 1]
                qt = qi * jnp.exp(cumi - ci)
                kt = k[0:lo] * jnp.exp(ci - cum[0:lo])
                oi = oi + jnp.dot(_dot_nt(qt, kt).astype(BF16), vb[0:lo], preferred_element_type=F32)
            for s in range(HGRN_SUB):
                dec = jnp.exp(jnp.minimum(cumi - cumi[s:s + 1], 0.0))
                col = jnp.sum(qi * dec * ki[s:s + 1], axis=-1, keepdims=True)
                col = jnp.where(row16 >= s, col, 0.0)
                oi = oi + col * vi[s:s + 1]
                if s % 4 == 3:
                    yield None
            outs.append(oi)
        o = jnp.concatenate(outs, axis=0)
        o = o * lax.rsqrt(jnp.mean(o * o, axis=-1, keepdims=True) + RMS_EPS)
        yield o, st_new

    def body(c, carry):
        rows = pl.ds(pl.multiple_of(c * chunk, 16), chunk)
        q = q_ref[rows, :].astype(F32)
        lf = lf_ref[rows, :]
        v = v_ref[rows, :].astype(F32)
        heads = []
        for hh in range(nhead):
            cols = slice(hh * HGRN_HEAD, (hh + 1) * HGRN_HEAD)
            heads.append(head_chunk(q[:, cols], lf[:, cols], v[:, cols], st_ref[hh]))
        outs = []
        for hh, (o, st_new) in enumerate(_round_robin(heads)):
            st_ref[hh] = st_new
            outs.append(o)
        o = jnp.concatenate(outs, axis=1)
        o_ref[rows, :] = (o * ng_ref[...] * gs_ref[rows, :].astype(F32)).astype(BF16)
        return carry

    lax.fori_loop(0, t // chunk, body, 0)


HGRN_GROUP = 8


def _hgrn_scan(q, lf, v, gs, norm_g, batch):
    n, d = q.shape
    t = n // batch
    chunk = _chunk_len(t)
    w = min(d, HGRN_GROUP * HGRN_HEAD)
    blk = pl.BlockSpec((t, w), lambda b, j: (b, j))
    return pl.pallas_call(
        functools.partial(_hgrn_scan_kernel, chunk=chunk),
        grid=(batch, d // w),
        in_specs=[blk, blk, blk, blk, pl.BlockSpec((1, w), lambda b, j: (0, j))],
        out_specs=blk,
        out_shape=jax.ShapeDtypeStruct((n, d), BF16),
        scratch_shapes=[pltpu.VMEM((w // HGRN_HEAD, HGRN_HEAD, HGRN_HEAD), F32)],
        compiler_params=_cparams("parallel", "parallel"),
        name="hgrn_scan",
    )(q, lf, v, gs, norm_g.reshape(1, d))


def _hgrn_mixer_layer(h, hb, p, layer_idx, g, b, alpha, batch, tm):
    d = h.shape[1]
    w_in = p["hgrn_w_in"]
    lb = jnp.cumsum(jax.nn.softmax(p["hgrn_lb"].astype(F32), axis=0), axis=0)
    lb = lb[layer_idx] - lb[0]
    (q,) = _proj(hb, w_in, 0, d, lambda y: (_silu(y),), [], [BF16], tm, "hgrn_proj_q")
    (lf,) = _proj(hb, w_in, d, d, lambda y, lbv: (jnp.log(lbv + (1.0 - lbv) * _sigmoid(y)),),
                  [lb], [F32], tm, "hgrn_proj_f")
    (v,) = _proj(hb, w_in, 2 * d, d, lambda y: (y,), [], [BF16], tm, "hgrn_proj_i")
    (gs,) = _proj(hb, w_in, 3 * d, d, lambda y: (_silu(y),), [], [BF16], tm, "hgrn_proj_g")
    z = _hgrn_scan(q, lf, v, gs, p["hgrn_norm_g"], batch)
    return _mm_res_ln(z, p["hgrn_w_out"], h, g, b, alpha, tm)


def _fox_gate_kernel(h_ref, w_ref, bf_ref, c_ref, carry_ref):
    @pl.when(pl.program_id(1) == 0)
    def _():
        carry_ref[...] = jnp.zeros_like(carry_ref)

    x = _dot_f32(h_ref[...], w_ref[...]) + bf_ref[...]
    log_f = jnp.minimum(x, 0.0) - jnp.log(1.0 + jnp.exp(-jnp.abs(x)))
    tm = x.shape[0]
    c = _cumsum_rows(log_f, _tril_bf16(tm, True)) + carry_ref[...]
    c_ref[...] = c
    carry_ref[...] = c[tm - 1:tm, :]


def _fox_gate(h, w_f, b_f, batch, tm):
    n, d = h.shape
    nh = w_f.shape[1]
    tpb = n // batch // tm
    return pl.pallas_call(
        _fox_gate_kernel,
        grid=(batch, tpb),
        in_specs=[
            pl.BlockSpec((tm, d), lambda b, t: (b * tpb + t, 0)),
            pl.BlockSpec((d, nh), lambda b, t: (0, 0)),
            pl.BlockSpec((1, nh), lambda b, t: (0, 0)),
        ],
        out_specs=pl.BlockSpec((tm, nh), lambda b, t: (b * tpb + t, 0)),
        out_shape=jax.ShapeDtypeStruct((n, nh), F32),
        scratch_shapes=[pltpu.VMEM((1, nh), F32)],
        compiler_params=_cparams("arbitrary", "arbitrary"),
        name="fox_gate",
    )(h, w_f, b_f.reshape(1, nh))


def _fox_attn_kernel(q_ref, k_ref, v_ref, sg_ref, c_ref, ct_ref, o_ref, *, tq):
    hd = pl.program_id(1)
    t = q_ref.shape[0]
    nh = c_ref.shape[1]
    lane = lax.broadcasted_iota(jnp.int32, (t, nh), 1)
    c_col = jnp.sum(jnp.where(lane == hd, c_ref[...], 0.0), axis=-1, keepdims=True)
    c_row = ct_ref[pl.ds(hd, 1), :]
    def query_tile(i):
        lo, hi = i * tq, (i + 1) * tq
        s = lax.dot_general(q_ref[lo:hi, :], k_ref[0:hi, :], (((1,), (1,)), ((), ())),
                            preferred_element_type=F32)
        yield None
        s = s + c_col[lo:hi] - c_row[:, 0:hi]
        r_i = lax.broadcasted_iota(jnp.int32, (tq, hi), 0) + lo
        c_i = lax.broadcasted_iota(jnp.int32, (tq, hi), 1)
        s = jnp.where(c_i <= r_i, s, -jnp.inf)
        m = jnp.max(s, axis=-1, keepdims=True)
        p = jnp.exp(s - m)
        l = jnp.sum(p, axis=-1, keepdims=True)
        yield None
        o = jnp.dot(p.astype(BF16), v_ref[0:hi, :], preferred_element_type=F32) / l
        o_ref[lo:hi, :] = (o * sg_ref[lo:hi, :]).astype(BF16)
        yield True

    _round_robin([query_tile(i) for i in range(t // tq)])


def _fox_attn(q, k, v, sg, c, ct, batch):
    n, d = q.shape
    t = n // batch
    nh = d // FOX_HEAD
    tq = _divisor_tile(t, 768, 16)
    blk = pl.BlockSpec((t, FOX_HEAD), lambda b, h: (b, h))
    return pl.pallas_call(
        functools.partial(_fox_attn_kernel, tq=tq),
        grid=(batch, nh),
        in_specs=[blk, blk, blk, blk,
                  pl.BlockSpec((t, nh), lambda b, h: (b, 0)),
                  pl.BlockSpec((None, nh, t), lambda b, h: (b, 0, 0))],
        out_specs=blk,
        out_shape=jax.ShapeDtypeStruct((n, d), BF16),
        compiler_params=_cparams("parallel", "parallel"),
        name="fox_attn",
    )(q, k, v, sg, c, ct)


def _head_rms_epilogue(scale):
    def epi(y, gain):
        outs = []
        for j in range(y.shape[1] // FOX_HEAD):
            yj = y[:, j * FOX_HEAD:(j + 1) * FOX_HEAD]
            yj = yj * lax.rsqrt(jnp.mean(yj * yj, axis=-1, keepdims=True) + RMS_EPS)
            outs.append(yj * gain[:, j * FOX_HEAD:(j + 1) * FOX_HEAD] * scale)
        return (jnp.concatenate(outs, axis=1),)
    return epi


def _fox_mixer_layer(h, hb, p, g, b, alpha, batch, tm):
    n, d = h.shape
    nh = d // FOX_HEAD
    w_in = p["fox_w_in"]
    qg = jnp.tile(p["fox_q_norm_g"], nh)
    kg = jnp.tile(p["fox_k_norm_g"], nh)
    (q,) = _proj(hb, w_in, 0, d, _head_rms_epilogue(FOX_HEAD ** -0.5), [qg], [BF16], tm, "fox_proj_q")
    (k,) = _proj(hb, w_in, d, d, _head_rms_epilogue(1.0), [kg], [BF16], tm, "fox_proj_k")
    (v,) = _proj(hb, w_in, 2 * d, d, lambda y: (y,), [], [BF16], tm, "fox_proj_v")
    (sg,) = _proj(hb, w_in, 3 * d, d, lambda y: (_sigmoid(y),), [], [F32], tm, "fox_proj_g")
    c = _fox_gate(h, w_in[:, 4 * d:], p["fox_b_f"], batch, tm)
    ct = c.reshape(batch, n // batch, nh).transpose(0, 2, 1)
    z = _fox_attn(q, k, v, sg, c, ct, batch)
    return _mm_res_ln(z, p["fox_w_out"], h, g, b, alpha, tm)


def _rwkv_mix_kernel(h_ref, mu_ref, *refs):
    outs, carry_ref = refs[:-1], refs[-1]

    @pl.when(pl.program_id(1) == 0)
    def _():
        carry_ref[...] = jnp.zeros_like(carry_ref)

    x = h_ref[...]
    tm = x.shape[0]
    row = lax.broadcasted_iota(jnp.int32, (tm, 1), 0)
    prev = jnp.where(row == 0, carry_ref[7:8, :], pltpu.roll(x, 1, axis=0))
    xx = prev - x
    carry_ref[...] = x[tm - 8:, :]
    for j, o_ref in enumerate(outs):
        o_ref[...] = (x + xx * mu_ref[j:j + 1, :]).astype(BF16)


def _rwkv_mix(h, mu, batch, tm):
    n, d = h.shape
    nmix = mu.shape[0]
    tpb = n // batch // tm
    blk = pl.BlockSpec((tm, d), lambda b, t: (b * tpb + t, 0))
    return pl.pallas_call(
        _rwkv_mix_kernel,
        grid=(batch, tpb),
        in_specs=[blk, pl.BlockSpec((nmix, d), lambda b, t: (0, 0))],
        out_specs=[blk] * nmix,
        out_shape=[jax.ShapeDtypeStruct((n, d), BF16)] * nmix,
        scratch_shapes=[pltpu.VMEM((8, d), F32)],
        compiler_params=_cparams("arbitrary", "arbitrary"),
        name="rwkv_mix",
    )(h, mu)


def _lora_kernel(x_ref, wa_ref, wb_ref, bias_ref, o_ref, wab_ref, wbb_ref, *, mid_act, out_act):
    @pl.when(pl.program_id(0) == 0)
    def _():
        wab_ref[...] = wa_ref[...].astype(BF16)
        wbb_ref[...] = wb_ref[...].astype(BF16)

    mid = mid_act(jnp.dot(x_ref[...], wab_ref[...], preferred_element_type=F32))
    y = jnp.dot(mid.astype(BF16), wbb_ref[...], preferred_element_type=F32)
    o_ref[...] = out_act(bias_ref[...] + y)


def _lora(xb, wa, wb, bias, mid_act, out_act, tm, name):
    n, d = xb.shape
    r = wa.shape[1]
    dout = wb.shape[1]
    return pl.pallas_call(
        functools.partial(_lora_kernel, mid_act=mid_act, out_act=out_act),
        grid=(n // tm,),
        in_specs=[
            pl.BlockSpec((tm, d), lambda i: (i, 0)),
            pl.BlockSpec((d, r), lambda i: (0, 0)),
            pl.BlockSpec((r, dout), lambda i: (0, 0)),
            pl.BlockSpec((1, dout), lambda i: (0, 0)),
        ],
        out_specs=pl.BlockSpec((tm, dout), lambda i: (i, 0)),
        out_shape=jax.ShapeDtypeStruct((n, dout), F32),
        scratch_shapes=[pltpu.VMEM((d, r), BF16), pltpu.VMEM((r, dout), BF16)],
        compiler_params=_cparams("arbitrary"),
        name=name,
    )(xb, wa, wb, bias.reshape(1, dout))


def _rwkv_log_decay(z):
    w_log = -(jnp.maximum(-z, 0.0) + jnp.log(1.0 + jnp.exp(-jnp.abs(z)))) - 0.5
    return -jnp.exp(w_log)


RWKV_GROUP = 4
RWKV_UNROLL = 8


def _seg_sum(x, bd):
    hi = x.astype(BF16)
    lo = (x - hi.astype(F32)).astype(BF16)
    return jnp.dot(hi, bd, preferred_element_type=F32) + jnp.dot(lo, bd, preferred_element_type=F32)


def _rwkv_scan_kernel(r_ref, kr_ref, v_ref, lw_ref, a_ref, g_ref, kk_p, ka_p, rk_p, gg_p, gb_p,
                      o_ref, kk_s, k_s, bonus_s, y_s, st_ref, *, chunk, ptile):
    t, w = r_ref.shape
    nhead = w // RWKV_HEAD
    sc = nhead * chunk
    lane_r = lax.broadcasted_iota(jnp.int32, (w, w), 0) // RWKV_HEAD
    lane_c = lax.broadcasted_iota(jnp.int32, (w, w), 1) // RWKV_HEAD
    bd = jnp.where(lane_r == lane_c, 1.0, 0.0).astype(BF16)

    def prologue(i, carry):
        rows = pl.ds(pl.multiple_of(i * ptile, 8), ptile)
        kr = kr_ref[rows, :]
        a = a_ref[rows, :]
        kkr = kr * kk_p[...]
        nrm = jnp.maximum(jnp.sqrt(_seg_sum(kkr * kkr, bd)), 1e-12)
        kk_s[rows, :] = kkr / nrm
        k = kr * (1.0 + (a - 1.0) * ka_p[...])
        k_s[rows, :] = k
        bonus_s[rows, :] = _seg_sum(r_ref[rows, :] * k * rk_p[...], bd) * v_ref[rows, :]
        return carry

    lax.fori_loop(0, t // ptile, prologue, 0)

    st_ref[...] = jnp.zeros_like(st_ref)
    tril = _tril_bf16(chunk, True)
    head_of_lane = lax.broadcasted_iota(jnp.int32, (chunk, w), 1) // RWKV_HEAD
    ri = lax.broadcasted_iota(jnp.int32, (2 * sc, sc), 0)
    ci = lax.broadcasted_iota(jnp.int32, (2 * sc, sc), 1)
    low_mask = ci < jnp.where(ri < sc, ri, ri - sc + 1)
    nsteps = max(1, (chunk - 1).bit_length())

    def stack(x):
        return jnp.concatenate([jnp.where(head_of_lane == hh, x, 0.0) for hh in range(nhead)], axis=0)

    eye = jnp.where(lax.broadcasted_iota(jnp.int32, (sc, sc), 0)
                    == lax.broadcasted_iota(jnp.int32, (sc, sc), 1), 1.0, 0.0)

    def prepare(c):
        start = c * chunk
        rows = pl.ds(start if isinstance(start, int) else pl.multiple_of(start, 16), chunk)
        r = r_ref[rows, :]
        v = v_ref[rows, :]
        lw = lw_ref[rows, :]
        a = a_ref[rows, :]
        kk = kk_s[rows, :]
        k = k_s[rows, :]
        cum = _cumsum_rows(lw, tril)
        e_neg = jnp.exp(-cum)
        at2 = stack(-kk * jnp.exp(cum - lw))
        rt2 = stack(r * jnp.exp(cum))
        bvec = kk * a
        bb2 = stack(bvec * e_neg)
        kb2 = stack(k * e_neg)
        v2 = stack(v)
        ar2 = jnp.concatenate([at2, rt2], axis=0).astype(BF16)
        cl = cum[chunk - 1:chunk]
        e_end = jnp.exp(cl - cum)
        khbh = jnp.concatenate([stack(k * e_end), stack(bvec * e_end)], axis=0).astype(BF16)
        yield None
        pb = jnp.where(low_mask, _dot_nt(ar2, bb2), 0.0)
        pk = jnp.where(low_mask, _dot_nt(ar2, kb2), 0.0)
        m_ab, m_rb = pb[:sc], pb[sc:]
        m_ak, m_rk = pk[:sc], pk[sc:]
        yield None
        u0 = _dot_nn(m_ak, v2)
        y0 = _dot_nn(m_rk, v2)
        tinv = eye + m_ab
        lpow = m_ab
        for _ in range(nsteps - 1):
            yield None
            lpow = _dot_nn(lpow, lpow)
            tinv = tinv + _dot_nn(tinv, lpow)
        return dict(rows=rows, ar2=ar2, tinv=tinv.astype(BF16), m_rb=m_rb.astype(BF16),
                    u0=u0, y0=y0, v2=v2.astype(BF16), khbh=khbh, decay=jnp.exp(cl))

    def chunk_steps(c, j, run):
        pc = yield from prepare(c)
        while run["turn"] != j:
            yield None
        st = run["st"]
        ps = _dot_nt(pc["ar2"], st)
        yield None
        u2 = jnp.dot(pc["tinv"], (ps[:sc] + pc["u0"]).astype(BF16), preferred_element_type=F32)
        u2b = u2.astype(BF16)
        yield None
        y2 = ps[sc:] + pc["y0"] + jnp.dot(pc["m_rb"], u2b, preferred_element_type=F32)
        y = y2[0:chunk]
        for hh in range(1, nhead):
            y = y + y2[hh * chunk:(hh + 1) * chunk]
        y_s[pc["rows"], :] = y
        run["st"] = st * pc["decay"] + _dot_tn(jnp.concatenate([pc["v2"], u2b], axis=0), pc["khbh"])
        run["turn"] = j + 1
        yield True

    def run_group(chunk_ids):
        run = dict(st=st_ref[...], turn=0)
        _round_robin([chunk_steps(c, j, run) for j, c in enumerate(chunk_ids)])
        st_ref[...] = run["st"]

    nchunks = t // chunk

    def body(i, carry):
        run_group([i * RWKV_UNROLL + j for j in range(RWKV_UNROLL)])
        return carry

    lax.fori_loop(0, nchunks // RWKV_UNROLL, body, 0)
    tail = list(range(nchunks - nchunks % RWKV_UNROLL, nchunks))
    if tail:
        run_group(tail)

    inv = 1.0 / RWKV_HEAD

    def epilogue(i, carry):
        rows = pl.ds(pl.multiple_of(i * ptile, 8), ptile)
        y = y_s[rows, :]
        mu = _seg_sum(y, bd) * inv
        yc = y - mu
        var = _seg_sum(yc * yc, bd) * inv
        yn = yc * lax.rsqrt(var + 1e-5 * RWKV_HEAD) * gg_p[...] + gb_p[...]
        o_ref[rows, :] = ((yn + bonus_s[rows, :]) * g_ref[rows, :]).astype(BF16)
        return carry

    lax.fori_loop(0, t // ptile, epilogue, 0)


def _rwkv_scan(r, kr, v, lw, a, g, p, batch):
    n, d = r.shape
    t = n // batch
    chunk = _chunk_len(t)
    w = min(d, RWKV_GROUP * RWKV_HEAD)
    ptile = _divisor_tile(t, 768, 16)
    blk = pl.BlockSpec((t, w), lambda b, j: (b, j))
    prm = pl.BlockSpec((1, w), lambda b, j: (0, j))
    params = [p["rwkv_k_k"], p["rwkv_k_a"], p["rwkv_r_k"], p["rwkv_gn_g"], p["rwkv_gn_b"]]
    return pl.pallas_call(
        functools.partial(_rwkv_scan_kernel, chunk=chunk, ptile=ptile),
        grid=(batch, d // w),
        in_specs=[blk] * 6 + [prm] * 5,
        out_specs=blk,
        out_shape=jax.ShapeDtypeStruct((n, d), BF16),
        scratch_shapes=[pltpu.VMEM((t, w), F32)] * 4 + [pltpu.VMEM((w, w), F32)],
        compiler_params=_cparams("parallel", "parallel"),
        name="rwkv_scan",
    )(r, kr, v, lw, a, g, *[x.reshape(1, d) for x in params])


def _rwkv_mixer_layer(h, hb, p, g, b, alpha, batch, tm):
    n, d = h.shape
    ident = lambda y: y
    xr, xw, xk, xv, xa, xg = _rwkv_mix(h, p["rwkv_mu"], batch, tm)
    (r,) = _proj(xr, p["rwkv_w_r"], 0, d, lambda y: (y,), [], [F32], tm, "rwkv_proj_r")
    (kr,) = _proj(xk, p["rwkv_w_k"], 0, d, lambda y: (y,), [], [F32], tm, "rwkv_proj_k")
    (v,) = _proj(xv, p["rwkv_w_v"], 0, d, lambda y: (y,), [], [F32], tm, "rwkv_proj_v")
    lw = _lora(xw, p["rwkv_w1"], p["rwkv_w2"], p["rwkv_w0"], jnp.tanh, _rwkv_log_decay, tm, "rwkv_lora_w")
    a = _lora(xa, p["rwkv_a1"], p["rwkv_a2"], p["rwkv_a0"], ident, _sigmoid, tm, "rwkv_lora_a")
    gate = _lora(xg, p["rwkv_g1"], p["rwkv_g2"], jnp.zeros((d,), F32), _sigmoid, ident, tm, "rwkv_lora_g")
    z = _rwkv_scan(r, kr, v, lw, a, gate, p, batch)
    return _mm_res_ln(z, p["rwkv_w_out"], h, g, b, alpha, tm)


def _embed_kernel(x_ref, meta_ref, h_ref, hb_ref, sem):
    b = pl.program_id(0)
    j = pl.program_id(1)
    tm = h_ref.shape[0]
    nmeta = meta_ref.shape[0]

    @pl.when(j == 0)
    def _():
        h_ref[0:nmeta, :] = meta_ref[...]
        cp = pltpu.make_async_copy(x_ref.at[b, pl.ds(0, tm - nmeta), :], h_ref.at[pl.ds(nmeta, tm - nmeta), :], sem)
        cp.start()
        cp.wait()

    @pl.when(j > 0)
    def _():
        first = pl.multiple_of(j * tm - nmeta, 8)
        cp = pltpu.make_async_copy(x_ref.at[b, pl.ds(first, tm), :], h_ref, sem)
        cp.start()
        cp.wait()

    hb_ref[...] = h_ref[...].astype(BF16)


def _embed(x, meta, tm):
    batch, seq, d = x.shape
    nmeta = meta.shape[0]
    t = nmeta + seq
    tpb = t // tm
    assert nmeta % 8 == 0 and tm % 8 == 0
    blk = pl.BlockSpec((tm, d), lambda b, j: (b * tpb + j, 0))
    return pl.pallas_call(
        _embed_kernel,
        grid=(batch, tpb),
        in_specs=[pl.BlockSpec(memory_space=pl.ANY), pl.BlockSpec((nmeta, d), lambda b, j: (0, 0))],
        out_specs=[blk, blk],
        out_shape=[jax.ShapeDtypeStruct((batch * t, d), F32), jax.ShapeDtypeStruct((batch * t, d), BF16)],
        scratch_shapes=[pltpu.SemaphoreType.DMA],
        compiler_params=_cparams("parallel", "arbitrary"),
        name="embed",
    )(x, meta.astype(x.dtype))


def kernel(x, meta, ln_mix_g, ln_mix_b, ln_ffn_g, ln_ffn_b, conv_w_in, conv_w, conv_b, conv_w_out, rwkv_mu, rwkv_w_r, rwkv_w_k, rwkv_w_v, rwkv_w0, rwkv_w1, rwkv_w2, rwkv_a0, rwkv_a1, rwkv_a2, rwkv_g1, rwkv_g2, rwkv_k_k, rwkv_k_a, rwkv_r_k, rwkv_gn_g, rwkv_gn_b, rwkv_w_out, hgrn_w_in, hgrn_lb, hgrn_norm_g, hgrn_w_out, fox_w_in, fox_b_f, fox_q_norm_g, fox_k_norm_g, fox_w_out, ffn0_w1, ffn0_w3, ffn0_w2, moe1_router, moe1_router_b, moe1_w1, moe1_w3, moe1_w2, ffn2_w1, ffn2_w3, ffn2_w2, moe3_router, moe3_router_b, moe3_w1, moe3_w3, moe3_w2):
    batch, seq, d = x.shape
    depth = ln_mix_g.shape[0]
    assert depth == 4
    alpha = (2.0 * depth) ** 0.25
    t = N_META + seq
    n = batch * t
    tm = _divisor_tile(t, 768, 16)
    p = dict(
        conv_w_in=conv_w_in, conv_w=conv_w, conv_b=conv_b, conv_w_out=conv_w_out,
        rwkv_mu=rwkv_mu, rwkv_w_r=rwkv_w_r, rwkv_w_k=rwkv_w_k, rwkv_w_v=rwkv_w_v, rwkv_w0=rwkv_w0,
        rwkv_w1=rwkv_w1, rwkv_w2=rwkv_w2, rwkv_a0=rwkv_a0, rwkv_a1=rwkv_a1, rwkv_a2=rwkv_a2,
        rwkv_g1=rwkv_g1, rwkv_g2=rwkv_g2, rwkv_k_k=rwkv_k_k, rwkv_k_a=rwkv_k_a, rwkv_r_k=rwkv_r_k,
        rwkv_gn_g=rwkv_gn_g, rwkv_gn_b=rwkv_gn_b, rwkv_w_out=rwkv_w_out,
        hgrn_w_in=hgrn_w_in, hgrn_lb=hgrn_lb, hgrn_norm_g=hgrn_norm_g, hgrn_w_out=hgrn_w_out,
        fox_w_in=fox_w_in, fox_b_f=fox_b_f, fox_q_norm_g=fox_q_norm_g, fox_k_norm_g=fox_k_norm_g,
        fox_w_out=fox_w_out,
    )
    assert meta.shape[0] == N_META
    h, hb = _embed(x, meta, tm)

    h, hb = _conv_mixer_layer(h, hb, p, ln_mix_g[0], ln_mix_b[0], alpha, batch, tm)
    h, hb = _dense_ffn(hb, h, ffn0_w1, ffn0_w3, ffn0_w2, ln_ffn_g[0], ln_ffn_b[0], alpha, tm)
    h, hb = _rwkv_mixer_layer(h, hb, p, ln_mix_g[1], ln_mix_b[1], alpha, batch, tm)
    h, hb = _moe_ffn(h, hb, moe1_router, moe1_router_b, moe1_w1, moe1_w3, moe1_w2,
                     ln_ffn_g[1], ln_ffn_b[1], alpha, tm)
    h, hb = _hgrn_mixer_layer(h, hb, p, 2, ln_mix_g[2], ln_mix_b[2], alpha, batch, tm)
    h, hb = _dense_ffn(hb, h, ffn2_w1, ffn2_w3, ffn2_w2, ln_ffn_g[2], ln_ffn_b[2], alpha, tm)
    h, hb = _fox_mixer_layer(h, hb, p, ln_mix_g[3], ln_mix_b[3], alpha, batch, tm)
    h, hb = _moe_ffn(h, hb, moe3_router, moe3_router_b, moe3_w1, moe3_w3, moe3_w2,
                     ln_ffn_g[3], ln_ffn_b[3], alpha, tm)
    return h.reshape(batch, t, d)[:, N_META:]
```

```python
import functools

import jax
import jax.numpy as jnp
from jax import lax
from jax.experimental import pallas as pl
from jax.experimental.pallas import tpu as pltpu

F32 = jnp.float32
BF16 = jnp.bfloat16

N_META = 16
LN_EPS = 1e-5
RMS_EPS = 1e-6
RWKV_HEAD = 64
HGRN_HEAD = 128
FOX_HEAD = 128
TOP_K = 2
LANES = 128
VMEM_LIMIT_BYTES = 56 * 2**20


def _cparams(*sem):
    return pltpu.CompilerParams(dimension_semantics=sem, vmem_limit_bytes=VMEM_LIMIT_BYTES)


def _divisor_tile(n, cap, mult):
    best = None
    for d in range(mult, min(n, cap) + 1, mult):
        if n % d == 0:
            best = d
    assert best is not None, (n, cap, mult)
    return best


def _layer_norm(y, g, b):
    mu = jnp.mean(y, axis=-1, keepdims=True)
    yc = y - mu
    var = jnp.mean(yc * yc, axis=-1, keepdims=True)
    return yc * lax.rsqrt(var + LN_EPS) * g + b


def _row_halves(tm):
    split = -(-tm // 32) * 16
    return [(0, split), (split, tm)] if split < tm else [(0, tm)]


def _sigmoid(x):
    return 1.0 / (1.0 + jnp.exp(-x))


def _silu(x):
    return x * _sigmoid(x)


def _mm_res_ln_kernel(z_ref, w_ref, h_ref, g_ref, b_ref, o_ref, ob_ref, *, nk, alpha):
    k = pl.program_id(1)

    @pl.when(k == 0)
    def _():
        o_ref[...] = jnp.dot(z_ref[...], w_ref[...], preferred_element_type=F32)

    @pl.when(jnp.logical_and(k > 0, k < nk - 1))
    def _():
        o_ref[...] += jnp.dot(z_ref[...], w_ref[...], preferred_element_type=F32)

    @pl.when(k == nk - 1)
    def _():
        spans = _row_halves(z_ref.shape[0])
        dots = [jnp.dot(z_ref[lo:hi, :], w_ref[...], preferred_element_type=F32) for lo, hi in spans]
        for (lo, hi), part in zip(spans, dots):
            y = _layer_norm(alpha * h_ref[lo:hi, :] + (o_ref[lo:hi, :] + part), g_ref[...], b_ref[...])
            o_ref[lo:hi, :] = y
            ob_ref[lo:hi, :] = y.astype(BF16)


WEIGHT_SLICE_ROWS = 256


def _load_weight_bf16(w_hbm, wb_ref, stage_ref, sem):
    rows = stage_ref.shape[1]
    nslice = w_hbm.shape[0] // rows

    def copy(s):
        return pltpu.make_async_copy(w_hbm.at[pl.ds(s * rows, rows), :], stage_ref.at[s % 2], sem.at[s % 2])

    copy(0).start()
    for s in range(nslice):
        if s + 1 < nslice:
            copy(s + 1).start()
        copy(s).wait()
        wb_ref[s * rows:(s + 1) * rows, :] = stage_ref[s % 2].astype(BF16)


def _mm_res_ln_resident_kernel(z_ref, w_ref, h_ref, g_ref, b_ref, o_ref, ob_ref, wb_ref, stage_ref, sem,
                               *, alpha):
    @pl.when(pl.program_id(0) == 0)
    def _():
        _load_weight_bf16(w_ref, wb_ref, stage_ref, sem)

    spans = _row_halves(z_ref.shape[0])
    dots = [jnp.dot(z_ref[lo:hi, :], wb_ref[...], preferred_element_type=F32) for lo, hi in spans]
    for (lo, hi), part in zip(spans, dots):
        y = _layer_norm(alpha * h_ref[lo:hi, :] + part, g_ref[...], b_ref[...])
        o_ref[lo:hi, :] = y
        ob_ref[lo:hi, :] = y.astype(BF16)


RESIDENT_WEIGHT_BYTES = 16 * 2**20


def _mm_res_ln_resident(z, w, h, g, b, alpha, tm):
    n, kdim = z.shape
    d = w.shape[1]
    ws = _divisor_tile(kdim, WEIGHT_SLICE_ROWS, 8)
    const = lambda m: (0, 0)
    return pl.pallas_call(
        functools.partial(_mm_res_ln_resident_kernel, alpha=alpha),
        grid=(n // tm,),
        in_specs=[
            pl.BlockSpec((tm, kdim), lambda m: (m, 0)),
            pl.BlockSpec(memory_space=pl.ANY),
            pl.BlockSpec((tm, d), lambda m: (m, 0)),
            pl.BlockSpec((1, d), const),
            pl.BlockSpec((1, d), const),
        ],
        out_specs=[
            pl.BlockSpec((tm, d), lambda m: (m, 0)),
            pl.BlockSpec((tm, d), lambda m: (m, 0)),
        ],
        out_shape=[jax.ShapeDtypeStruct((n, d), F32), jax.ShapeDtypeStruct((n, d), BF16)],
        scratch_shapes=[pltpu.VMEM((kdim, d), BF16), pltpu.VMEM((2, ws, d), F32),
                        pltpu.SemaphoreType.DMA((2,))],
        compiler_params=_cparams("arbitrary"),
        name="mm_res_ln_resident",
    )(z, w, h, g.reshape(1, d), b.reshape(1, d))


def _mm_res_ln(z, w, h, g, b, alpha, tm):
    n, kdim = z.shape
    d = w.shape[1]
    if kdim * d * 4 <= RESIDENT_WEIGHT_BYTES:
        return _mm_res_ln_resident(z, w, h, g, b, alpha, tm)
    w = w.astype(BF16)
    tk = _divisor_tile(kdim, 1408, LANES)
    nk = kdim // tk
    assert nk >= 2
    return pl.pallas_call(
        functools.partial(_mm_res_ln_kernel, nk=nk, alpha=alpha),
        grid=(n // tm, nk),
        in_specs=[
            pl.BlockSpec((tm, tk), lambda m, k: (m, k)),
            pl.BlockSpec((tk, d), lambda m, k: (k, 0)),
            pl.BlockSpec((tm, d), lambda m, k: (m, 0)),
            pl.BlockSpec((1, d), lambda m, k: (0, 0)),
            pl.BlockSpec((1, d), lambda m, k: (0, 0)),
        ],
        out_specs=[
            pl.BlockSpec((tm, d), lambda m, k: (m, 0)),
            pl.BlockSpec((tm, d), lambda m, k: (m, 0)),
        ],
        out_shape=[jax.ShapeDtypeStruct((n, d), F32), jax.ShapeDtypeStruct((n, d), BF16)],
        compiler_params=_cparams("parallel", "arbitrary"),
        name="mm_res_ln",
    )(z, w, h, g.reshape(1, d), b.reshape(1, d))


def _ffn_up_kernel(exp_ref, nact_ref, x_ref, w1_ref, w3_ref, o_ref, w1b_ref, w3b_ref):
    c = pl.program_id(1)
    prev = exp_ref[jnp.maximum(c - 1, 0)]
    new_weights = jnp.logical_or(c == 0, exp_ref[c] != prev)

    @pl.when(new_weights)
    def _():
        w1b_ref[...] = w1_ref[...].astype(BF16)
        w3b_ref[...] = w3_ref[...].astype(BF16)

    @pl.when(c < nact_ref[0])
    def _():
        x = x_ref[...]
        a = jnp.dot(x, w1b_ref[...], preferred_element_type=F32)
        bb = jnp.dot(x, w3b_ref[...], preferred_element_type=F32)
        o_ref[...] = (_silu(a) * bb).astype(BF16)

    @pl.when(c >= nact_ref[0])
    def _():
        o_ref[...] = jnp.zeros_like(o_ref)


def _ffn_up(x, w1, w3, chunk_expert, n_active, rows):
    p, d = x.shape
    f = w1.shape[2]
    tf = _divisor_tile(f, 512, LANES)
    grid_spec = pltpu.PrefetchScalarGridSpec(
        num_scalar_prefetch=2,
        grid=(f // tf, p // rows),
        in_specs=[
            pl.BlockSpec((rows, d), lambda j, c, e, na: (c, 0)),
            pl.BlockSpec((None, d, tf), lambda j, c, e, na: (e[c], 0, j)),
            pl.BlockSpec((None, d, tf), lambda j, c, e, na: (e[c], 0, j)),
        ],
        out_specs=pl.BlockSpec((rows, tf), lambda j, c, e, na: (c, j)),
        scratch_shapes=[pltpu.VMEM((d, tf), BF16), pltpu.VMEM((d, tf), BF16)],
    )
    return pl.pallas_call(
        _ffn_up_kernel,
        grid_spec=grid_spec,
        out_shape=jax.ShapeDtypeStruct((p, f), BF16),
        compiler_params=_cparams("arbitrary", "arbitrary"),
        name="ffn_up",
    )(chunk_expert, n_active, x, w1, w3)


def _dense_ffn(hb, h, w1, w3, w2, g, b, alpha, tm):
    n = hb.shape[0]
    nchunks = n // tm
    hmid = _ffn_up(hb, w1[None], w3[None], jnp.zeros((nchunks,), jnp.int32),
                   jnp.full((1,), nchunks, jnp.int32), tm)
    return _mm_res_ln(hmid, w2, h, g, b, alpha, tm)


def _conv_proj_kernel(x_ref, wb_ref, wc_ref, wh_ref, cw_ref, cb_ref, o_ref,
                      wbb_ref, wcb_ref, whb_ref, carry_ref):
    bi = pl.program_id(1)
    ti = pl.program_id(2)

    @pl.when(jnp.logical_and(bi == 0, ti == 0))
    def _():
        wbb_ref[...] = wb_ref[...].astype(BF16)
        wcb_ref[...] = wc_ref[...].astype(BF16)
        whb_ref[...] = wh_ref[...].astype(BF16)

    @pl.when(ti == 0)
    def _():
        carry_ref[...] = jnp.zeros_like(carry_ref)

    x = x_ref[...]
    gate_b = jnp.dot(x, wbb_ref[...], preferred_element_type=F32)
    gate_c = jnp.dot(x, wcb_ref[...], preferred_element_type=F32)
    hh = jnp.dot(x, whb_ref[...], preferred_element_type=F32)
    u = gate_c * hh
    tm = u.shape[0]
    prev1 = carry_ref[7:8, :]
    prev2 = carry_ref[6:7, :]
    row = lax.broadcasted_iota(jnp.int32, (tm, 1), 0)
    r1 = jnp.where(row == 0, prev1, pltpu.roll(u, 1, axis=0))
    r2 = jnp.where(row == 0, prev2, jnp.where(row == 1, prev1, pltpu.roll(u, 2, axis=0)))
    v = cw_ref[0:1, :] * r2 + cw_ref[1:2, :] * r1 + cw_ref[2:3, :] * u + cb_ref[...]
    carry_ref[...] = u[tm - 8:, :]
    o_ref[...] = (gate_b * v).astype(BF16)


def _conv_proj(hb, w_in, conv_w, conv_b, batch, tm):
    n, d = hb.shape
    tn = _divisor_tile(d, 512, LANES)
    nd = d // tn
    tpb = n // batch // tm
    return pl.pallas_call(
        _conv_proj_kernel,
        grid=(nd, batch, tpb),
        in_specs=[
            pl.BlockSpec((tm, d), lambda j, bi, ti: (bi * tpb + ti, 0)),
            pl.BlockSpec((d, tn), lambda j, bi, ti: (0, j)),
            pl.BlockSpec((d, tn), lambda j, bi, ti: (0, nd + j)),
            pl.BlockSpec((d, tn), lambda j, bi, ti: (0, 2 * nd + j)),
            pl.BlockSpec((3, tn), lambda j, bi, ti: (0, j)),
            pl.BlockSpec((1, tn), lambda j, bi, ti: (0, j)),
        ],
        out_specs=pl.BlockSpec((tm, tn), lambda j, bi, ti: (bi * tpb + ti, j)),
        out_shape=jax.ShapeDtypeStruct((n, d), BF16),
        scratch_shapes=[pltpu.VMEM((d, tn), BF16)] * 3 + [pltpu.VMEM((8, tn), F32)],
        compiler_params=_cparams("arbitrary", "arbitrary", "arbitrary"),
        name="conv_proj",
    )(hb, w_in, w_in, w_in, conv_w, conv_b.reshape(1, d))


def _conv_mixer_layer(h, hb, p, g, b, alpha, batch, tm):
    z = _conv_proj(hb, p["conv_w_in"], p["conv_w"], p["conv_b"], batch, tm)
    return _mm_res_ln(z, p["conv_w_out"], h, g, b, alpha, tm)


MOE_ROWS = 128


def _split_bf16(a):
    hi = a.astype(BF16)
    lo = (a - hi.astype(F32)).astype(BF16)
    return hi, lo


def _dot_f32(a, b):
    ah, al = _split_bf16(a)
    bh, bl = _split_bf16(b)
    return (jnp.dot(ah, bh, preferred_element_type=F32)
            + (jnp.dot(ah, bl, preferred_element_type=F32)
               + jnp.dot(al, bh, preferred_element_type=F32)))


def _router_kernel(h_ref, w_ref, b_ref, info_ref, cnt_ref, carry_ref):
    i = pl.program_id(0)

    @pl.when(i == 0)
    def _():
        carry_ref[...] = jnp.zeros_like(carry_ref)

    logits = _dot_f32(h_ref[...], w_ref[...]) + b_ref[...]
    tm, ne = logits.shape
    lane = lax.broadcasted_iota(jnp.int32, (tm, ne), 1)
    m1 = jnp.max(logits, axis=-1, keepdims=True)
    i1 = jnp.min(jnp.where(logits == m1, lane, ne), axis=-1, keepdims=True)
    mask1 = lane == i1
    rest = jnp.where(mask1, -jnp.inf, logits)
    m2 = jnp.max(rest, axis=-1, keepdims=True)
    i2 = jnp.min(jnp.where(rest == m2, lane, ne), axis=-1, keepdims=True)
    mask2 = lane == i2
    dd = jnp.exp(m2 - m1)
    g1 = 1.0 / (1.0 + dd)
    g2 = dd / (1.0 + dd)
    sel = jnp.where(jnp.logical_or(mask1, mask2), 1.0, 0.0)
    r_i = lax.broadcasted_iota(jnp.int32, (tm, tm), 0)
    c_i = lax.broadcasted_iota(jnp.int32, (tm, tm), 1)
    tril = jnp.where(c_i < r_i, 1.0, 0.0).astype(BF16)
    rank = jnp.dot(tril, sel.astype(BF16), preferred_element_type=F32) + carry_ref[...]
    r1 = jnp.sum(jnp.where(mask1, rank, 0.0), axis=-1, keepdims=True)
    r2 = jnp.sum(jnp.where(mask2, rank, 0.0), axis=-1, keepdims=True)
    info = jnp.where(lane == 0, i1.astype(F32),
           jnp.where(lane == 1, i2.astype(F32),
           jnp.where(lane == 2, g1,
           jnp.where(lane == 3, g2,
           jnp.where(lane == 4, r1,
           jnp.where(lane == 5, r2, 0.0))))))
    info_ref[...] = info
    total = carry_ref[...] + jnp.sum(sel, axis=0, keepdims=True)
    carry_ref[...] = total
    cnt_ref[...] = total


def _router(h, w, b, tm):
    n, d = h.shape
    ne = w.shape[1]
    assert ne >= 6
    return pl.pallas_call(
        _router_kernel,
        grid=(n // tm,),
        in_specs=[
            pl.BlockSpec((tm, d), lambda i: (i, 0)),
            pl.BlockSpec((d, ne), lambda i: (0, 0)),
            pl.BlockSpec((1, ne), lambda i: (0, 0)),
        ],
        out_specs=[
            pl.BlockSpec((tm, ne), lambda i: (i, 0)),
            pl.BlockSpec((1, ne), lambda i: (0, 0)),
        ],
        out_shape=[jax.ShapeDtypeStruct((n, ne), F32), jax.ShapeDtypeStruct((1, ne), F32)],
        scratch_shapes=[pltpu.VMEM((1, ne), F32)],
        compiler_params=_cparams("arbitrary"),
        name="moe_router",
    )(h, w, b.reshape(1, ne))


def _row_copy(src_hbm, row, dst_vmem, r, sem):
    return pltpu.make_async_copy(src_hbm.at[pl.ds(row, 1), :], dst_vmem.at[pl.ds(r, 1), :], sem)


def _rows_wait(src_hbm, dst_vmem, sem):
    pltpu.make_async_copy(src_hbm.at[pl.ds(0, dst_vmem.shape[0]), :], dst_vmem, sem).wait()


GATHER_SLOTS = 3
GATHER_UNITS = 2


def _moe_gather_kernel(src_ref, nact_ref, h_ref, o_ref, buf_ref, sem):
    c = pl.program_id(0)
    rows = buf_ref.shape[1]

    nact = nact_ref[0]
    slot = c % GATHER_SLOTS
    slot1 = (c + 1) % GATHER_SLOTS
    slot2 = (c + 2) % GATHER_SLOTS

    def start(chunk, to_slot, r):
        _row_copy(h_ref, src_ref[chunk * rows + r], buf_ref.at[to_slot], r, sem.at[to_slot]).start()

    @pl.when(jnp.logical_and(c == 0, nact > 0))
    def _():
        lax.fori_loop(0, rows, lambda r, carry: (start(0, slot, r), carry)[1], 0, unroll=8)

    @pl.when(jnp.logical_and(c == 0, nact > 1))
    def _():
        lax.fori_loop(0, rows, lambda r, carry: (start(1, slot1, r), carry)[1], 0, unroll=8)

    @pl.when(c < nact)
    def _():
        _rows_wait(h_ref, buf_ref.at[slot], sem.at[slot])
        nxt = jnp.minimum(c + 2, nact - 1)
        for r in range(rows):
            start(nxt, slot2, r)
        o_ref[...] = buf_ref[slot].astype(BF16)

        @pl.when(c == nact - 1)
        def _():
            _rows_wait(h_ref, buf_ref.at[slot2], sem.at[slot2])

            @pl.when(nact > 1)
            def _():
                _rows_wait(h_ref, buf_ref.at[slot1], sem.at[slot1])

    @pl.when(c >= nact)
    def _():
        o_ref[...] = jnp.zeros_like(o_ref)


def _moe_gather(h, src, n_active, p, rows):
    n, d = h.shape
    grid_spec = pltpu.PrefetchScalarGridSpec(
        num_scalar_prefetch=2,
        grid=(p // rows,),
        in_specs=[pl.BlockSpec(memory_space=pl.ANY)],
        out_specs=pl.BlockSpec((rows, d), lambda c, s, na: (c, 0)),
        scratch_shapes=[pltpu.VMEM((GATHER_SLOTS, rows, d), F32), pltpu.SemaphoreType.DMA((GATHER_SLOTS,))],
    )
    return pl.pallas_call(
        _moe_gather_kernel,
        grid_spec=grid_spec,
        out_shape=jax.ShapeDtypeStruct((p, d), BF16),
        compiler_params=_cparams("arbitrary"),
        name="moe_gather",
    )(src, n_active, h)


def _moe_combine_kernel(p1_ref, p2_ref, y_ref, h_ref, info_ref, g_ref, b_ref, o_ref, ob_ref,
                        buf1_ref, buf2_ref, sem, *, alpha):
    i = pl.program_id(0)
    tm = buf1_ref.shape[1]
    ntile = pl.num_programs(0)
    last = ntile - 1
    slot = i % GATHER_SLOTS
    slot1 = (i + 1) % GATHER_SLOTS
    slot2 = (i + 2) % GATHER_SLOTS

    def start(tile, to_slot, r):
        _row_copy(y_ref, p1_ref[tile * tm + r], buf1_ref.at[to_slot], r, sem.at[to_slot]).start()
        _row_copy(y_ref, p2_ref[tile * tm + r], buf2_ref.at[to_slot], r, sem.at[to_slot]).start()

    def wait(from_slot):
        _rows_wait(y_ref, buf1_ref.at[from_slot], sem.at[from_slot])
        _rows_wait(y_ref, buf2_ref.at[from_slot], sem.at[from_slot])

    @pl.when(i == 0)
    def _():
        lax.fori_loop(0, tm, lambda r, c: (start(0, slot, r), c)[1], 0, unroll=8)

    @pl.when(jnp.logical_and(i == 0, ntile > 1))
    def _():
        lax.fori_loop(0, tm, lambda r, c: (start(1, slot1, r), c)[1], 0, unroll=8)

    wait(slot)
    nxt = jnp.minimum(i + 2, last)
    for r in range(tm):
        start(nxt, slot2, r)
    info = info_ref[...]
    y = alpha * h_ref[...] + (info[:, 2:3] * buf1_ref[slot] + info[:, 3:4] * buf2_ref[slot])
    y = _layer_norm(y, g_ref[...], b_ref[...])
    o_ref[...] = y
    ob_ref[...] = y.astype(BF16)

    @pl.when(i == last)
    def _():
        wait(slot2)

        @pl.when(ntile > 1)
        def _():
            wait(slot1)


def _moe_combine(y, h, info, p1, p2, g, b, alpha, tm):
    n, d = h.shape
    ne = info.shape[1]
    grid_spec = pltpu.PrefetchScalarGridSpec(
        num_scalar_prefetch=2,
        grid=(n // tm,),
        in_specs=[
            pl.BlockSpec(memory_space=pl.ANY),
            pl.BlockSpec((tm, d), lambda i, a, c: (i, 0)),
            pl.BlockSpec((tm, ne), lambda i, a, c: (i, 0)),
            pl.BlockSpec((1, d), lambda i, a, c: (0, 0)),
            pl.BlockSpec((1, d), lambda i, a, c: (0, 0)),
        ],
        out_specs=[
            pl.BlockSpec((tm, d), lambda i, a, c: (i, 0)),
            pl.BlockSpec((tm, d), lambda i, a, c: (i, 0)),
        ],
        scratch_shapes=[pltpu.VMEM((GATHER_SLOTS, tm, d), F32), pltpu.VMEM((GATHER_SLOTS, tm, d), F32),
                        pltpu.SemaphoreType.DMA((GATHER_SLOTS,))],
    )
    return pl.pallas_call(
        functools.partial(_moe_combine_kernel, alpha=alpha),
        grid_spec=grid_spec,
        out_shape=[jax.ShapeDtypeStruct((n, d), F32), jax.ShapeDtypeStruct((n, d), BF16)],
        compiler_params=_cparams("arbitrary"),
        name="moe_combine",
    )(p1, p2, y, h, info, g.reshape(1, d), b.reshape(1, d))


MOE_PASS_BLOCKS = 18
MOE_F_TILE = 256
MOE_BLOCK_UNITS = 8


def _expert_ffn_kernel(pe_ref, ps_ref, pn_ref, nu_ref, xs_ref, w1_ref, w3_ref, w2_ref, y_ref,
                       x_buf, acc_ref, w1b_ref, w3b_ref, w2b_ref, sem):
    p = pl.program_id(0)
    f = pl.program_id(1)
    nf = pl.num_programs(1)
    nb = pn_ref[p]
    sb = MOE_ROWS
    start = pl.multiple_of(ps_ref[p], MOE_ROWS)
    nblock = nb // MOE_BLOCK_UNITS

    @pl.when(nb > 0)
    def _():
        @pl.when(f == 0)
        def _():
            cp = pltpu.make_async_copy(xs_ref.at[pl.ds(start, x_buf.shape[0]), :], x_buf, sem.at[0])
            cp.start()
            cp.wait()

        w1b_ref[...] = w1_ref[...].astype(BF16)
        w3b_ref[...] = w3_ref[...].astype(BF16)

        @pl.when(nblock == 0)
        def _():
            w2b_ref[...] = w2_ref[...].astype(BF16)

        def up(unit, nunit):
            x = x_buf[pl.ds(pl.multiple_of(unit * sb, sb), nunit * sb), :]
            a = jnp.dot(x, w1b_ref[...], preferred_element_type=F32)
            bb = jnp.dot(x, w3b_ref[...], preferred_element_type=F32)
            return (_silu(a) * bb).astype(BF16)

        def out_copy(i):
            rows = pl.ds(pl.multiple_of(i * sb, sb), sb)
            dst = pl.ds(pl.multiple_of(start + i * sb, sb), sb)
            return pltpu.make_async_copy(acc_ref.at[rows, :], y_ref.at[dst, :], sem.at[1])

        def down(unit, nunit, hmid, first):
            rows = pl.ds(pl.multiple_of(unit * sb, sb), nunit * sb)
            part = jnp.dot(hmid, w2b_ref[...], preferred_element_type=F32)
            if first:
                acc_ref[rows, :] = part
            else:
                acc_ref[rows, :] += part

            @pl.when(f == nf - 1)
            def _():
                for u in range(nunit):
                    out_copy(unit + u).start()

        def sweep(first):
            bu = MOE_BLOCK_UNITS

            @pl.when(nblock > 0)
            def _():
                def body(i, hprev):
                    down(bu * (i - 1), bu, hprev, first)
                    return up(bu * i, bu)

                h0 = up(0, bu)
                w2b_ref[...] = w2_ref[...].astype(BF16)
                down(bu * (nblock - 1), bu, lax.fori_loop(1, nblock, body, h0), first)

            done = nblock * bu
            size = bu // 2
            while size >= 1:
                @pl.when((nb - done) & size != 0)
                def _(done=done, size=size):
                    down(done, size, up(done, size), first)

                done = done + ((nb - done) & size)
                size //= 2

        @pl.when(f == 0)
        def _():
            sweep(True)

        @pl.when(f > 0)
        def _():
            sweep(False)

        @pl.when(f == nf - 1)
        def _():
            lax.fori_loop(0, nb, lambda i, c: (out_copy(i).wait(), c)[1], 0)

    @pl.when(jnp.logical_and(p == pl.num_programs(0) - 1, f == nf - 1))
    def _():
        acc_ref[0:sb, :] = jnp.zeros((sb, acc_ref.shape[1]), F32)

        def zero_copy(i):
            dst = pl.ds(pl.multiple_of(i * sb, sb), sb)
            return pltpu.make_async_copy(acc_ref.at[0:sb, :], y_ref.at[dst, :], sem.at[1])

        n_blocks = y_ref.shape[0] // sb
        lax.fori_loop(nu_ref[1], n_blocks, lambda i, c: (zero_copy(i).start(), c)[1], 0)
        lax.fori_loop(nu_ref[1], n_blocks, lambda i, c: (zero_copy(i).wait(), c)[1], 0)


def _expert_ffn(xs, w1, w3, w2, pass_expert, pass_start, pass_nb, n_used, p_rows):
    d = xs.shape[1]
    f = w1.shape[2]
    tf = _divisor_tile(f, MOE_F_TILE, LANES)
    nf = f // tf
    npass = pass_expert.shape[0]
    r = MOE_PASS_BLOCKS * MOE_ROWS

    def fidx(p, j, nu):
        return jnp.where(p < nu[0], j, nf - 1)

    grid_spec = pltpu.PrefetchScalarGridSpec(
        num_scalar_prefetch=4,
        grid=(npass, nf),
        in_specs=[
            pl.BlockSpec(memory_space=pl.ANY),
            pl.BlockSpec((None, d, tf), lambda p, j, pe, ps, pn, nu: (pe[p], 0, fidx(p, j, nu))),
            pl.BlockSpec((None, d, tf), lambda p, j, pe, ps, pn, nu: (pe[p], 0, fidx(p, j, nu))),
            pl.BlockSpec((None, tf, d), lambda p, j, pe, ps, pn, nu: (pe[p], fidx(p, j, nu), 0)),
        ],
        out_specs=pl.BlockSpec(memory_space=pl.ANY),
        scratch_shapes=[
            pltpu.VMEM((r, d), BF16), pltpu.VMEM((r, d), F32),
            pltpu.VMEM((d, tf), BF16), pltpu.VMEM((d, tf), BF16), pltpu.VMEM((tf, d), BF16),
            pltpu.SemaphoreType.DMA((2,)),
        ],
    )
    return pl.pallas_call(
        _expert_ffn_kernel,
        grid_spec=grid_spec,
        out_shape=jax.ShapeDtypeStruct((p_rows, d), F32),
        compiler_params=_cparams("arbitrary", "arbitrary"),
        name="expert_ffn",
    )(pass_expert, pass_start, pass_nb, n_used, xs, w1, w3, w2)


def _moe_ffn(h, hb, router_w, router_b, w1, w3, w2, g, b, alpha, tm):
    n, d = h.shape
    ne = router_w.shape[1]
    rows = MOE_ROWS
    info, counts = _router(h, router_w, router_b, tm)
    counts = counts[0].astype(jnp.int32)
    nblk_e = (counts + rows - 1) // rows
    blk_end = jnp.cumsum(nblk_e)
    starts = (blk_end - nblk_e) * rows
    n_blocks = (n * TOP_K + ne * (rows - 1)) // rows
    p = n_blocks * rows
    n_active = blk_end[-1:].astype(jnp.int32)
    npass_e = (nblk_e + MOE_PASS_BLOCKS - 1) // MOE_PASS_BLOCKS
    pass_end = jnp.cumsum(npass_e)
    max_pass = n_blocks // MOE_PASS_BLOCKS + ne
    pidx = jnp.arange(max_pass, dtype=jnp.int32)
    n_used = jnp.stack([pass_end[-1], blk_end[-1]]).astype(jnp.int32)
    last_expert = jnp.sum(pass_end < pass_end[-1]).astype(jnp.int32)
    pass_expert = jnp.minimum(jnp.sum(pidx[:, None] >= pass_end[None, :], axis=1), last_expert).astype(jnp.int32)
    local = pidx - (pass_end - npass_e)[pass_expert]
    pass_start = (starts[pass_expert] + local * (MOE_PASS_BLOCKS * rows)).astype(jnp.int32)
    pass_nb = jnp.where(pidx < n_used[0],
                        jnp.clip(nblk_e[pass_expert] - local * MOE_PASS_BLOCKS, 0, MOE_PASS_BLOCKS),
                        0).astype(jnp.int32)
    pass_start = jnp.where(pass_nb > 0, pass_start, 0).astype(jnp.int32)
    i1 = info[:, 0].astype(jnp.int32)
    i2 = info[:, 1].astype(jnp.int32)
    p1 = starts[i1] + info[:, 4].astype(jnp.int32)
    p2 = starts[i2] + info[:, 5].astype(jnp.int32)
    tok = jnp.arange(n, dtype=jnp.int32)
    grows = GATHER_UNITS * rows
    p_in = -(-(p + MOE_PASS_BLOCKS * rows) // grows) * grows
    src = jnp.zeros((p_in,), jnp.int32).at[jnp.concatenate([p1, p2])].set(jnp.concatenate([tok, tok]))
    xs = _moe_gather(h, src, (n_active + GATHER_UNITS - 1) // GATHER_UNITS, p_in, grows)
    y = _expert_ffn(xs, w1, w3, w2, pass_expert, pass_start, pass_nb, n_used, p)
    tmc = _divisor_tile(n, 384, 16)
    return _moe_combine(y, h, info, p1, p2, g, b, alpha, tmc)


def _proj_kernel(*refs, epilogue, n_extra, n_out):
    x_ref, w_ref = refs[0], refs[1]
    extra = refs[2:2 + n_extra]
    outs = refs[2 + n_extra:2 + n_extra + n_out]
    wb_ref = refs[2 + n_extra + n_out]

    @pl.when(pl.program_id(1) == 0)
    def _():
        wb_ref[...] = w_ref[...].astype(BF16)

    y = jnp.dot(x_ref[...], wb_ref[...], preferred_element_type=F32)
    res = epilogue(y, *[e[...] for e in extra])
    for o_ref, r in zip(outs, res):
        o_ref[...] = r.astype(o_ref.dtype)


def _proj(xb, w, col0, ncols, epilogue, extras, out_dtypes, tm, name, tn_cap=1024):
    n, kdim = xb.shape
    tn = _divisor_tile(ncols, tn_cap, LANES)
    assert col0 % tn == 0
    off = col0 // tn
    outs = pl.pallas_call(
        functools.partial(_proj_kernel, epilogue=epilogue, n_extra=len(extras), n_out=len(out_dtypes)),
        grid=(ncols // tn, n // tm),
        in_specs=[
            pl.BlockSpec((tm, kdim), lambda j, m: (m, 0)),
            pl.BlockSpec((kdim, tn), lambda j, m: (0, off + j)),
        ] + [pl.BlockSpec((1, tn), lambda j, m: (0, j))] * len(extras),
        out_specs=[pl.BlockSpec((tm, tn), lambda j, m: (m, j))] * len(out_dtypes),
        out_shape=[jax.ShapeDtypeStruct((n, ncols), dt) for dt in out_dtypes],
        scratch_shapes=[pltpu.VMEM((kdim, tn), BF16)],
        compiler_params=_cparams("arbitrary", "arbitrary"),
        name=name,
    )(xb, w, *[e.reshape(1, ncols) for e in extras])
    return outs


def _tril_bf16(c, inclusive):
    r_i = lax.broadcasted_iota(jnp.int32, (c, c), 0)
    c_i = lax.broadcasted_iota(jnp.int32, (c, c), 1)
    keep = (c_i <= r_i) if inclusive else (c_i < r_i)
    return jnp.where(keep, 1.0, 0.0).astype(BF16)


def _cumsum_rows(x, tril):
    hi = x.astype(BF16)
    r1 = x - hi.astype(F32)
    mid = r1.astype(BF16)
    lo = (r1 - mid.astype(F32)).astype(BF16)
    return (jnp.dot(tril, hi, preferred_element_type=F32)
            + (jnp.dot(tril, mid, preferred_element_type=F32)
               + jnp.dot(tril, lo, preferred_element_type=F32)))


def _dot_nt(a, b):
    return lax.dot_general(a.astype(BF16), b.astype(BF16), (((1,), (1,)), ((), ())),
                           preferred_element_type=F32)


def _dot_tn(a, b):
    return lax.dot_general(a.astype(BF16), b.astype(BF16), (((0,), (0,)), ((), ())),
                           preferred_element_type=F32)


def _dot_nn(a, b):
    return jnp.dot(a.astype(BF16), b.astype(BF16), preferred_element_type=F32)


def _chunk_len(t):
    return _divisor_tile(t, 64, 16)


def _round_robin(gens):
    done = [None] * len(gens)
    while any(d is None for d in done):
        for j, gen in enumerate(gens):
            if done[j] is None:
                done[j] = next(gen)
    return done


HGRN_SUB = 16


def _hgrn_scan_kernel(q_ref, lf_ref, v_ref, gs_ref, ng_ref, o_ref, st_ref, *, chunk):
    t, w = q_ref.shape
    nhead = w // HGRN_HEAD
    nsub = chunk // HGRN_SUB
    st_ref[...] = jnp.zeros_like(st_ref)
    tril = _tril_bf16(chunk, True)
    row16 = lax.broadcasted_iota(jnp.int32, (HGRN_SUB, 1), 0)

    def head_chunk(q, lf, v, st):
        k = 1.0 - jnp.exp(lf)
        cum = _cumsum_rows(lf, tril)
        yield None
        o_inter = _dot_nt(q * jnp.exp(cum), st)
        vb = v.astype(BF16)
        cl = cum[chunk - 1:chunk]
        kd = k * jnp.exp(cl - cum)
        st_new = st * jnp.exp(cl) + _dot_tn(v, kd)
        yield None
        outs = []
        for i in range(nsub):
            lo, hi = i * HGRN_SUB, (i + 1) * HGRN_SUB
            qi, ki, vi, cumi = q[lo:hi], k[lo:hi], v[lo:hi], cum[lo:hi]
            oi = o_inter[lo:hi]
            if i > 0:
                ci = cum[lo:lo + 1]
                qt = qi * jnp.exp(cumi - ci)
                kt = k[0:lo] * jnp.exp(ci - cum[0:lo])
                oi = oi + jnp.dot(_dot_nt(qt, kt).astype(BF16), vb[0:lo], preferred_element_type=F32)
            for s in range(HGRN_SUB):
                dec = jnp.exp(jnp.minimum(cumi - cumi[s:s + 1], 0.0))
                col = jnp.sum(qi * dec * ki[s:s + 1], axis=-1, keepdims=True)
                col = jnp.where(row16 >= s, col, 0.0)
                oi = oi + col * vi[s:s + 1]
                if s % 4 == 3:
                    yield None
            outs.append(oi)
        o = jnp.concatenate(outs, axis=0)
        o = o * lax.rsqrt(jnp.mean(o * o, axis=-1, keepdims=True) + RMS_EPS)
        yield o, st_new

    def body(c, carry):
        rows = pl.ds(pl.multiple_of(c * chunk, 16), chunk)
        q = q_ref[rows, :].astype(F32)
        lf = lf_ref[rows, :]
        v = v_ref[rows, :].astype(F32)
        heads = []
        for hh in range(nhead):
            cols = slice(hh * HGRN_HEAD, (hh + 1) * HGRN_HEAD)
            heads.append(head_chunk(q[:, cols], lf[:, cols], v[:, cols], st_ref[hh]))
        outs = []
        for hh, (o, st_new) in enumerate(_round_robin(heads)):
            st_ref[hh] = st_new
            outs.append(o)
        o = jnp.concatenate(outs, axis=1)
        o_ref[rows, :] = (o * ng_ref[...] * gs_ref[rows, :].astype(F32)).astype(BF16)
        return carry

    lax.fori_loop(0, t // chunk, body, 0)


HGRN_GROUP = 8


def _hgrn_scan(q, lf, v, gs, norm_g, batch):
    n, d = q.shape
    t = n // batch
    chunk = _chunk_len(t)
    w = min(d, HGRN_GROUP * HGRN_HEAD)
    blk = pl.BlockSpec((t, w), lambda b, j: (b, j))
    return pl.pallas_call(
        functools.partial(_hgrn_scan_kernel, chunk=chunk),
        grid=(batch, d // w),
        in_specs=[blk, blk, blk, blk, pl.BlockSpec((1, w), lambda b, j: (0, j))],
        out_specs=blk,
        out_shape=jax.ShapeDtypeStruct((n, d), BF16),
        scratch_shapes=[pltpu.VMEM((w // HGRN_HEAD, HGRN_HEAD, HGRN_HEAD), F32)],
        compiler_params=_cparams("parallel", "parallel"),
        name="hgrn_scan",
    )(q, lf, v, gs, norm_g.reshape(1, d))


def _hgrn_mixer_layer(h, hb, p, layer_idx, g, b, alpha, batch, tm):
    d = h.shape[1]
    w_in = p["hgrn_w_in"]
    lb = jnp.cumsum(jax.nn.softmax(p["hgrn_lb"].astype(F32), axis=0), axis=0)
    lb = lb[layer_idx] - lb[0]
    (q,) = _proj(hb, w_in, 0, d, lambda y: (_silu(y),), [], [BF16], tm, "hgrn_proj_q")
    (lf,) = _proj(hb, w_in, d, d, lambda y, lbv: (jnp.log(lbv + (1.0 - lbv) * _sigmoid(y)),),
                  [lb], [F32], tm, "hgrn_proj_f")
    (v,) = _proj(hb, w_in, 2 * d, d, lambda y: (y,), [], [BF16], tm, "hgrn_proj_i")
    (gs,) = _proj(hb, w_in, 3 * d, d, lambda y: (_silu(y),), [], [BF16], tm, "hgrn_proj_g")
    z = _hgrn_scan(q, lf, v, gs, p["hgrn_norm_g"], batch)
    return _mm_res_ln(z, p["hgrn_w_out"], h, g, b, alpha, tm)


def _fox_gate_kernel(h_ref, w_ref, bf_ref, c_ref, carry_ref):
    @pl.when(pl.program_id(1) == 0)
    def _():
        carry_ref[...] = jnp.zeros_like(carry_ref)

    x = _dot_f32(h_ref[...], w_ref[...]) + bf_ref[...]
    log_f = jnp.minimum(x, 0.0) - jnp.log(1.0 + jnp.exp(-jnp.abs(x)))
    tm = x.shape[0]
    c = _cumsum_rows(log_f, _tril_bf16(tm, True)) + carry_ref[...]
    c_ref[...] = c
    carry_ref[...] = c[tm - 1:tm, :]


def _fox_gate(h, w_f, b_f, batch, tm):
    n, d = h.shape
    nh = w_f.shape[1]
    tpb = n // batch // tm
    return pl.pallas_call(
        _fox_gate_kernel,
        grid=(batch, tpb),
        in_specs=[
            pl.BlockSpec((tm, d), lambda b, t: (b * tpb + t, 0)),
            pl.BlockSpec((d, nh), lambda b, t: (0, 0)),
            pl.BlockSpec((1, nh), lambda b, t: (0, 0)),
        ],
        out_specs=pl.BlockSpec((tm, nh), lambda b, t: (b * tpb + t, 0)),
        out_shape=jax.ShapeDtypeStruct((n, nh), F32),
        scratch_shapes=[pltpu.VMEM((1, nh), F32)],
        compiler_params=_cparams("arbitrary", "arbitrary"),
        name="fox_gate",
    )(h, w_f, b_f.reshape(1, nh))


def _fox_attn_kernel(q_ref, k_ref, v_ref, sg_ref, c_ref, ct_ref, o_ref, *, tq):
    hd = pl.program_id(1)
    t = q_ref.shape[0]
    nh = c_ref.shape[1]
    lane = lax.broadcasted_iota(jnp.int32, (t, nh), 1)
    c_col = jnp.sum(jnp.where(lane == hd, c_ref[...], 0.0), axis=-1, keepdims=True)
    c_row = ct_ref[pl.ds(hd, 1), :]
    def query_tile(i):
        lo, hi = i * tq, (i + 1) * tq
        s = lax.dot_general(q_ref[lo:hi, :], k_ref[0:hi, :], (((1,), (1,)), ((), ())),
                            preferred_element_type=F32)
        yield None
        s = s + c_col[lo:hi] - c_row[:, 0:hi]
        r_i = lax.broadcasted_iota(jnp.int32, (tq, hi), 0) + lo
        c_i = lax.broadcasted_iota(jnp.int32, (tq, hi), 1)
        s = jnp.where(c_i <= r_i, s, -jnp.inf)
        m = jnp.max(s, axis=-1, keepdims=True)
        p = jnp.exp(s - m)
        l = jnp.sum(p, axis=-1, keepdims=True)
        yield None
        o = jnp.dot(p.astype(BF16), v_ref[0:hi, :], preferred_element_type=F32) / l
        o_ref[lo:hi, :] = (o * sg_ref[lo:hi, :]).astype(BF16)
        yield True

    _round_robin([query_tile(i) for i in range(t // tq)])


def _fox_attn(q, k, v, sg, c, ct, batch):
    n, d = q.shape
    t = n // batch
    nh = d // FOX_HEAD
    tq = _divisor_tile(t, 768, 16)
    blk = pl.BlockSpec((t, FOX_HEAD), lambda b, h: (b, h))
    return pl.pallas_call(
        functools.partial(_fox_attn_kernel, tq=tq),
        grid=(batch, nh),
        in_specs=[blk, blk, blk, blk,
                  pl.BlockSpec((t, nh), lambda b, h: (b, 0)),
                  pl.BlockSpec((None, nh, t), lambda b, h: (b, 0, 0))],
        out_specs=blk,
        out_shape=jax.ShapeDtypeStruct((n, d), BF16),
        compiler_params=_cparams("parallel", "parallel"),
        name="fox_attn",
    )(q, k, v, sg, c, ct)


def _head_rms_epilogue(scale):
    def epi(y, gain):
        outs = []
        for j in range(y.shape[1] // FOX_HEAD):
            yj = y[:, j * FOX_HEAD:(j + 1) * FOX_HEAD]
            yj = yj * lax.rsqrt(jnp.mean(yj * yj, axis=-1, keepdims=True) + RMS_EPS)
            outs.append(yj * gain[:, j * FOX_HEAD:(j + 1) * FOX_HEAD] * scale)
        return (jnp.concatenate(outs, axis=1),)
    return epi


def _fox_mixer_layer(h, hb, p, g, b, alpha, batch, tm):
    n, d = h.shape
    nh = d // FOX_HEAD
    w_in = p["fox_w_in"]
    qg = jnp.tile(p["fox_q_norm_g"], nh)
    kg = jnp.tile(p["fox_k_norm_g"], nh)
    (q,) = _proj(hb, w_in, 0, d, _head_rms_epilogue(FOX_HEAD ** -0.5), [qg], [BF16], tm, "fox_proj_q")
    (k,) = _proj(hb, w_in, d, d, _head_rms_epilogue(1.0), [kg], [BF16], tm, "fox_proj_k")
    (v,) = _proj(hb, w_in, 2 * d, d, lambda y: (y,), [], [BF16], tm, "fox_proj_v")
    (sg,) = _proj(hb, w_in, 3 * d, d, lambda y: (_sigmoid(y),), [], [F32], tm, "fox_proj_g")
    c = _fox_gate(h, w_in[:, 4 * d:], p["fox_b_f"], batch, tm)
    ct = c.reshape(batch, n // batch, nh).transpose(0, 2, 1)
    z = _fox_attn(q, k, v, sg, c, ct, batch)
    return _mm_res_ln(z, p["fox_w_out"], h, g, b, alpha, tm)


def _rwkv_mix_kernel(h_ref, mu_ref, *refs):
    outs, carry_ref = refs[:-1], refs[-1]

    @pl.when(pl.program_id(1) == 0)
    def _():
        carry_ref[...] = jnp.zeros_like(carry_ref)

    x = h_ref[...]
    tm = x.shape[0]
    row = lax.broadcasted_iota(jnp.int32, (tm, 1), 0)
    prev = jnp.where(row == 0, carry_ref[7:8, :], pltpu.roll(x, 1, axis=0))
    xx = prev - x
    carry_ref[...] = x[tm - 8:, :]
    for j, o_ref in enumerate(outs):
        o_ref[...] = (x + xx * mu_ref[j:j + 1, :]).astype(BF16)


def _rwkv_mix(h, mu, batch, tm):
    n, d = h.shape
    nmix = mu.shape[0]
    tpb = n // batch // tm
    blk = pl.BlockSpec((tm, d), lambda b, t: (b * tpb + t, 0))
    return pl.pallas_call(
        _rwkv_mix_kernel,
        grid=(batch, tpb),
        in_specs=[blk, pl.BlockSpec((nmix, d), lambda b, t: (0, 0))],
        out_specs=[blk] * nmix,
        out_shape=[jax.ShapeDtypeStruct((n, d), BF16)] * nmix,
        scratch_shapes=[pltpu.VMEM((8, d), F32)],
        compiler_params=_cparams("arbitrary", "arbitrary"),
        name="rwkv_mix",
    )(h, mu)


def _lora_kernel(x_ref, wa_ref, wb_ref, bias_ref, o_ref, wab_ref, wbb_ref, *, mid_act, out_act):
    @pl.when(pl.program_id(0) == 0)
    def _():
        wab_ref[...] = wa_ref[...].astype(BF16)
        wbb_ref[...] = wb_ref[...].astype(BF16)

    mid = mid_act(jnp.dot(x_ref[...], wab_ref[...], preferred_element_type=F32))
    y = jnp.dot(mid.astype(BF16), wbb_ref[...], preferred_element_type=F32)
    o_ref[...] = out_act(bias_ref[...] + y)


def _lora(xb, wa, wb, bias, mid_act, out_act, tm, name):
    n, d = xb.shape
    r = wa.shape[1]
    dout = wb.shape[1]
    return pl.pallas_call(
        functools.partial(_lora_kernel, mid_act=mid_act, out_act=out_act),
        grid=(n // tm,),
        in_specs=[
            pl.BlockSpec((tm, d), lambda i: (i, 0)),
            pl.BlockSpec((d, r), lambda i: (0, 0)),
            pl.BlockSpec((r, dout), lambda i: (0, 0)),
            pl.BlockSpec((1, dout), lambda i: (0, 0)),
        ],
        out_specs=pl.BlockSpec((tm, dout), lambda i: (i, 0)),
        out_shape=jax.ShapeDtypeStruct((n, dout), F32),
        scratch_shapes=[pltpu.VMEM((d, r), BF16), pltpu.VMEM((r, dout), BF16)],
        compiler_params=_cparams("arbitrary"),
        name=name,
    )(xb, wa, wb, bias.reshape(1, dout))


def _rwkv_log_decay(z):
    w_log = -(jnp.maximum(-z, 0.0) + jnp.log(1.0 + jnp.exp(-jnp.abs(z)))) - 0.5
    return -jnp.exp(w_log)


RWKV_GROUP = 4
RWKV_UNROLL = 8


def _seg_sum(x, bd):
    hi = x.astype(BF16)
    lo = (x - hi.astype(F32)).astype(BF16)
    return jnp.dot(hi, bd, preferred_element_type=F32) + jnp.dot(lo, bd, preferred_element_type=F32)


def _rwkv_scan_kernel(r_ref, kr_ref, v_ref, lw_ref, a_ref, g_ref, kk_p, ka_p, rk_p, gg_p, gb_p,
                      o_ref, kk_s, k_s, bonus_s, y_s, st_ref, *, chunk, ptile):
    t, w = r_ref.shape
    nhead = w // RWKV_HEAD
    sc = nhead * chunk
    lane_r = lax.broadcasted_iota(jnp.int32, (w, w), 0) // RWKV_HEAD
    lane_c = lax.broadcasted_iota(jnp.int32, (w, w), 1) // RWKV_HEAD
    bd = jnp.where(lane_r == lane_c, 1.0, 0.0).astype(BF16)

    def prologue(i, carry):
        rows = pl.ds(pl.multiple_of(i * ptile, 8), ptile)
        kr = kr_ref[rows, :]
        a = a_ref[rows, :]
        kkr = kr * kk_p[...]
        nrm = jnp.maximum(jnp.sqrt(_seg_sum(kkr * kkr, bd)), 1e-12)
        kk_s[rows, :] = kkr / nrm
        k = kr * (1.0 + (a - 1.0) * ka_p[...])
        k_s[rows, :] = k
        bonus_s[rows, :] = _seg_sum(r_ref[rows, :] * k * rk_p[...], bd) * v_ref[rows, :]
        return carry

    lax.fori_loop(0, t // ptile, prologue, 0)

    st_ref[...] = jnp.zeros_like(st_ref)
    tril = _tril_bf16(chunk, True)
    head_of_lane = lax.broadcasted_iota(jnp.int32, (chunk, w), 1) // RWKV_HEAD
    ri = lax.broadcasted_iota(jnp.int32, (2 * sc, sc), 0)
    ci = lax.broadcasted_iota(jnp.int32, (2 * sc, sc), 1)
    low_mask = ci < jnp.where(ri < sc, ri, ri - sc + 1)
    nsteps = max(1, (chunk - 1).bit_length())

    def stack(x):
        return jnp.concatenate([jnp.where(head_of_lane == hh, x, 0.0) for hh in range(nhead)], axis=0)

    eye = jnp.where(lax.broadcasted_iota(jnp.int32, (sc, sc), 0)
                    == lax.broadcasted_iota(jnp.int32, (sc, sc), 1), 1.0, 0.0)

    def prepare(c):
        start = c * chunk
        rows = pl.ds(start if isinstance(start, int) else pl.multiple_of(start, 16), chunk)
        r = r_ref[rows, :]
        v = v_ref[rows, :]
        lw = lw_ref[rows, :]
        a = a_ref[rows, :]
        kk = kk_s[rows, :]
        k = k_s[rows, :]
        cum = _cumsum_rows(lw, tril)
        e_neg = jnp.exp(-cum)
        at2 = stack(-kk * jnp.exp(cum - lw))
        rt2 = stack(r * jnp.exp(cum))
        bvec = kk * a
        bb2 = stack(bvec * e_neg)
        kb2 = stack(k * e_neg)
        v2 = stack(v)
        ar2 = jnp.concatenate([at2, rt2], axis=0).astype(BF16)
        cl = cum[chunk - 1:chunk]
        e_end = jnp.exp(cl - cum)
        khbh = jnp.concatenate([stack(k * e_end), stack(bvec * e_end)], axis=0).astype(BF16)
        yield None
        pb = jnp.where(low_mask, _dot_nt(ar2, bb2), 0.0)
        pk = jnp.where(low_mask, _dot_nt(ar2, kb2), 0.0)
        m_ab, m_rb = pb[:sc], pb[sc:]
        m_ak, m_rk = pk[:sc], pk[sc:]
        yield None
        u0 = _dot_nn(m_ak, v2)
        y0 = _dot_nn(m_rk, v2)
        tinv = eye + m_ab
        lpow = m_ab
        for _ in range(nsteps - 1):
            yield None
            lpow = _dot_nn(lpow, lpow)
            tinv = tinv + _dot_nn(tinv, lpow)
        return dict(rows=rows, ar2=ar2, tinv=tinv.astype(BF16), m_rb=m_rb.astype(BF16),
                    u0=u0, y0=y0, v2=v2.astype(BF16), khbh=khbh, decay=jnp.exp(cl))

    def chunk_steps(c, j, run):
        pc = yield from prepare(c)
        while run["turn"] != j:
            yield None
        st = run["st"]
        ps = _dot_nt(pc["ar2"], st)
        yield None
        u2 = jnp.dot(pc["tinv"], (ps[:sc] + pc["u0"]).astype(BF16), preferred_element_type=F32)
        u2b = u2.astype(BF16)
        yield None
        y2 = ps[sc:] + pc["y0"] + jnp.dot(pc["m_rb"], u2b, preferred_element_type=F32)
        y = y2[0:chunk]
        for hh in range(1, nhead):
            y = y + y2[hh * chunk:(hh + 1) * chunk]
        y_s[pc["rows"], :] = y
        run["st"] = st * pc["decay"] + _dot_tn(jnp.concatenate([pc["v2"], u2b], axis=0), pc["khbh"])
        run["turn"] = j + 1
        yield True

    def run_group(chunk_ids):
        run = dict(st=st_ref[...], turn=0)
        _round_robin([chunk_steps(c, j, run) for j, c in enumerate(chunk_ids)])
        st_ref[...] = run["st"]

    nchunks = t // chunk

    def body(i, carry):
        run_group([i * RWKV_UNROLL + j for j in range(RWKV_UNROLL)])
        return carry

    lax.fori_loop(0, nchunks // RWKV_UNROLL, body, 0)
    tail = list(range(nchunks - nchunks % RWKV_UNROLL, nchunks))
    if tail:
        run_group(tail)

    inv = 1.0 / RWKV_HEAD

    def epilogue(i, carry):
        rows = pl.ds(pl.multiple_of(i * ptile, 8), ptile)
        y = y_s[rows, :]
        mu = _seg_sum(y, bd) * inv
        yc = y - mu
        var = _seg_sum(yc * yc, bd) * inv
        yn = yc * lax.rsqrt(var + 1e-5 * RWKV_HEAD) * gg_p[...] + gb_p[...]
        o_ref[rows, :] = ((yn + bonus_s[rows, :]) * g_ref[rows, :]).astype(BF16)
        return carry

    lax.fori_loop(0, t // ptile, epilogue, 0)


def _rwkv_scan(r, kr, v, lw, a, g, p, batch):
    n, d = r.shape
    t = n // batch
    chunk = _chunk_len(t)
    w = min(d, RWKV_GROUP * RWKV_HEAD)
    ptile = _divisor_tile(t, 768, 16)
    blk = pl.BlockSpec((t, w), lambda b, j: (b, j))
    prm = pl.BlockSpec((1, w), lambda b, j: (0, j))
    params = [p["rwkv_k_k"], p["rwkv_k_a"], p["rwkv_r_k"], p["rwkv_gn_g"], p["rwkv_gn_b"]]
    return pl.pallas_call(
        functools.partial(_rwkv_scan_kernel, chunk=chunk, ptile=ptile),
        grid=(batch, d // w),
        in_specs=[blk] * 6 + [prm] * 5,
        out_specs=blk,
        out_shape=jax.ShapeDtypeStruct((n, d), BF16),
        scratch_shapes=[pltpu.VMEM((t, w), F32)] * 4 + [pltpu.VMEM((w, w), F32)],
        compiler_params=_cparams("parallel", "parallel"),
        name="rwkv_scan",
    )(r, kr, v, lw, a, g, *[x.reshape(1, d) for x in params])


def _rwkv_mixer_layer(h, hb, p, g, b, alpha, batch, tm):
    n, d = h.shape
    ident = lambda y: y
    xr, xw, xk, xv, xa, xg = _rwkv_mix(h, p["rwkv_mu"], batch, tm)
    (r,) = _proj(xr, p["rwkv_w_r"], 0, d, lambda y: (y,), [], [F32], tm, "rwkv_proj_r")
    (kr,) = _proj(xk, p["rwkv_w_k"], 0, d, lambda y: (y,), [], [F32], tm, "rwkv_proj_k")
    (v,) = _proj(xv, p["rwkv_w_v"], 0, d, lambda y: (y,), [], [F32], tm, "rwkv_proj_v")
    lw = _lora(xw, p["rwkv_w1"], p["rwkv_w2"], p["rwkv_w0"], jnp.tanh, _rwkv_log_decay, tm, "rwkv_lora_w")
    a = _lora(xa, p["rwkv_a1"], p["rwkv_a2"], p["rwkv_a0"], ident, _sigmoid, tm, "rwkv_lora_a")
    gate = _lora(xg, p["rwkv_g1"], p["rwkv_g2"], jnp.zeros((d,), F32), _sigmoid, ident, tm, "rwkv_lora_g")
    z = _rwkv_scan(r, kr, v, lw, a, gate, p, batch)
    return _mm_res_ln(z, p["rwkv_w_out"], h, g, b, alpha, tm)


def _embed_kernel(x_ref, meta_ref, h_ref, hb_ref, sem):
    b = pl.program_id(0)
    j = pl.program_id(1)
    tm = h_ref.shape[0]
    nmeta = meta_ref.shape[0]

    @pl.when(j == 0)
    def _():
        h_ref[0:nmeta, :] = meta_ref[...]
        cp = pltpu.make_async_copy(x_ref.at[b, pl.ds(0, tm - nmeta), :], h_ref.at[pl.ds(nmeta, tm - nmeta), :], sem)
        cp.start()
        cp.wait()

    @pl.when(j > 0)
    def _():
        first = pl.multiple_of(j * tm - nmeta, 8)
        cp = pltpu.make_async_copy(x_ref.at[b, pl.ds(first, tm), :], h_ref, sem)
        cp.start()
        cp.wait()

    hb_ref[...] = h_ref[...].astype(BF16)


def _embed(x, meta, tm):
    batch, seq, d = x.shape
    nmeta = meta.shape[0]
    t = nmeta + seq
    tpb = t // tm
    assert nmeta % 8 == 0 and tm % 8 == 0
    blk = pl.BlockSpec((tm, d), lambda b, j: (b * tpb + j, 0))
    return pl.pallas_call(
        _embed_kernel,
        grid=(batch, tpb),
        in_specs=[pl.BlockSpec(memory_space=pl.ANY), pl.BlockSpec((nmeta, d), lambda b, j: (0, 0))],
        out_specs=[blk, blk],
        out_shape=[jax.ShapeDtypeStruct((batch * t, d), F32), jax.ShapeDtypeStruct((batch * t, d), BF16)],
        scratch_shapes=[pltpu.SemaphoreType.DMA],
        compiler_params=_cparams("parallel", "arbitrary"),
        name="embed",
    )(x, meta.astype(x.dtype))


def kernel(x, meta, ln_mix_g, ln_mix_b, ln_ffn_g, ln_ffn_b, conv_w_in, conv_w, conv_b, conv_w_out, rwkv_mu, rwkv_w_r, rwkv_w_k, rwkv_w_v, rwkv_w0, rwkv_w1, rwkv_w2, rwkv_a0, rwkv_a1, rwkv_a2, rwkv_g1, rwkv_g2, rwkv_k_k, rwkv_k_a, rwkv_r_k, rwkv_gn_g, rwkv_gn_b, rwkv_w_out, hgrn_w_in, hgrn_lb, hgrn_norm_g, hgrn_w_out, fox_w_in, fox_b_f, fox_q_norm_g, fox_k_norm_g, fox_w_out, ffn0_w1, ffn0_w3, ffn0_w2, moe1_router, moe1_router_b, moe1_w1, moe1_w3, moe1_w2, ffn2_w1, ffn2_w3, ffn2_w2, moe3_router, moe3_router_b, moe3_w1, moe3_w3, moe3_w2):
    batch, seq, d = x.shape
    depth = ln_mix_g.shape[0]
    assert depth == 4
    alpha = (2.0 * depth) ** 0.25
    t = N_META + seq
    n = batch * t
    tm = _divisor_tile(t, 768, 16)
    p = dict(
        conv_w_in=conv_w_in, conv_w=conv_w, conv_b=conv_b, conv_w_out=conv_w_out,
        rwkv_mu=rwkv_mu, rwkv_w_r=rwkv_w_r, rwkv_w_k=rwkv_w_k, rwkv_w_v=rwkv_w_v, rwkv_w0=rwkv_w0,
        rwkv_w1=rwkv_w1, rwkv_w2=rwkv_w2, rwkv_a0=rwkv_a0, rwkv_a1=rwkv_a1, rwkv_a2=rwkv_a2,
        rwkv_g1=rwkv_g1, rwkv_g2=rwkv_g2, rwkv_k_k=rwkv_k_k, rwkv_k_a=rwkv_k_a, rwkv_r_k=rwkv_r_k,
        rwkv_gn_g=rwkv_gn_g, rwkv_gn_b=rwkv_gn_b, rwkv_w_out=rwkv_w_out,
        hgrn_w_in=hgrn_w_in, hgrn_lb=hgrn_lb, hgrn_norm_g=hgrn_norm_g, hgrn_w_out=hgrn_w_out,
        fox_w_in=fox_w_in, fox_b_f=fox_b_f, fox_q_norm_g=fox_q_norm_g, fox_k_norm_g=fox_k_norm_g,
        fox_w_out=fox_w_out,
    )
    assert meta.shape[0] == N_META
    h, hb = _embed(x, meta, tm)

    h, hb = _conv_mixer_layer(h, hb, p, ln_mix_g[0], ln_mix_b[0], alpha, batch, tm)
    h, hb = _dense_ffn(hb, h, ffn0_w1, ffn0_w3, ffn0_w2, ln_ffn_g[0], ln_ffn_b[0], alpha, tm)
    h, hb = _rwkv_mixer_layer(h, hb, p, ln_mix_g[1], ln_mix_b[1], alpha, batch, tm)
    h, hb = _moe_ffn(h, hb, moe1_router, moe1_router_b, moe1_w1, moe1_w3, moe1_w2,
                     ln_ffn_g[1], ln_ffn_b[1], alpha, tm)
    h, hb = _hgrn_mixer_layer(h, hb, p, 2, ln_mix_g[2], ln_mix_b[2], alpha, batch, tm)
    h, hb = _dense_ffn(hb, h, ffn2_w1, ffn2_w3, ffn2_w2, ln_ffn_g[2], ln_ffn_b[2], alpha, tm)
    h, hb = _fox_mixer_layer(h, hb, p, ln_mix_g[3], ln_mix_b[3], alpha, batch, tm)
    h, hb = _moe_ffn(h, hb, moe3_router, moe3_router_b, moe3_w1, moe3_w3, moe3_w2,
                     ln_ffn_g[3], ln_ffn_b[3], alpha, tm)
    return h.reshape(batch, t, d)[:, N_META:]
```

```python
import functools

import jax
import jax.numpy as jnp
from jax import lax
from jax.experimental import pallas as pl
from jax.experimental.pallas import tpu as pltpu

F32 = jnp.float32
BF16 = jnp.bfloat16

N_META = 16
LN_EPS = 1e-5
RMS_EPS = 1e-6
RWKV_HEAD = 64
HGRN_HEAD = 128
FOX_HEAD = 128
TOP_K = 2
LANES = 128
VMEM_LIMIT_BYTES = 56 * 2**20


def _cparams(*sem):
    return pltpu.CompilerParams(dimension_semantics=sem, vmem_limit_bytes=VMEM_LIMIT_BYTES)


def _divisor_tile(n, cap, mult):
    best = None
    for d in range(mult, min(n, cap) + 1, mult):
        if n % d == 0:
            best = d
    assert best is not None, (n, cap, mult)
    return best


def _layer_norm(y, g, b):
    mu = jnp.mean(y, axis=-1, keepdims=True)
    yc = y - mu
    var = jnp.mean(yc * yc, axis=-1, keepdims=True)
    return yc * lax.rsqrt(var + LN_EPS) * g + b


def _row_halves(tm):
    split = -(-tm // 32) * 16
    return [(0, split), (split, tm)] if split < tm else [(0, tm)]


def _sigmoid(x):
    return 1.0 / (1.0 + jnp.exp(-x))


def _silu(x):
    return x * _sigmoid(x)


def _mm_res_ln_kernel(z_ref, w_ref, h_ref, g_ref, b_ref, o_ref, ob_ref, *, nk, alpha):
    k = pl.program_id(1)

    @pl.when(k == 0)
    def _():
        o_ref[...] = jnp.dot(z_ref[...], w_ref[...], preferred_element_type=F32)

    @pl.when(jnp.logical_and(k > 0, k < nk - 1))
    def _():
        o_ref[...] += jnp.dot(z_ref[...], w_ref[...], preferred_element_type=F32)

    @pl.when(k == nk - 1)
    def _():
        spans = _row_halves(z_ref.shape[0])
        dots = [jnp.dot(z_ref[lo:hi, :], w_ref[...], preferred_element_type=F32) for lo, hi in spans]
        for (lo, hi), part in zip(spans, dots):
            y = _layer_norm(alpha * h_ref[lo:hi, :] + (o_ref[lo:hi, :] + part), g_ref[...], b_ref[...])
            o_ref[lo:hi, :] = y
            ob_ref[lo:hi, :] = y.astype(BF16)


WEIGHT_SLICE_ROWS = 256


def _load_weight_bf16(w_hbm, wb_ref, stage_ref, sem):
    rows = stage_ref.shape[1]
    nslice = w_hbm.shape[0] // rows

    def copy(s):
        return pltpu.make_async_copy(w_hbm.at[pl.ds(s * rows, rows), :], stage_ref.at[s % 2], sem.at[s % 2])

    copy(0).start()
    for s in range(nslice):
        if s + 1 < nslice:
            copy(s + 1).start()
        copy(s).wait()
        wb_ref[s * rows:(s + 1) * rows, :] = stage_ref[s % 2].astype(BF16)


def _mm_res_ln_resident_kernel(z_ref, w_ref, h_ref, g_ref, b_ref, o_ref, ob_ref, wb_ref, stage_ref, sem,
                               *, alpha):
    @pl.when(pl.program_id(0) == 0)
    def _():
        _load_weight_bf16(w_ref, wb_ref, stage_ref, sem)

    spans = _row_halves(z_ref.shape[0])
    dots = [jnp.dot(z_ref[lo:hi, :], wb_ref[...], preferred_element_type=F32) for lo, hi in spans]
    for (lo, hi), part in zip(spans, dots):
        y = _layer_norm(alpha * h_ref[lo:hi, :] + part, g_ref[...], b_ref[...])
        o_ref[lo:hi, :] = y
        ob_ref[lo:hi, :] = y.astype(BF16)


RESIDENT_WEIGHT_BYTES = 16 * 2**20


def _mm_res_ln_resident(z, w, h, g, b, alpha, tm):
    n, kdim = z.shape
    d = w.shape[1]
    ws = _divisor_tile(kdim, WEIGHT_SLICE_ROWS, 8)
    const = lambda m: (0, 0)
    return pl.pallas_call(
        functools.partial(_mm_res_ln_resident_kernel, alpha=alpha),
        grid=(n // tm,),
        in_specs=[
            pl.BlockSpec((tm, kdim), lambda m: (m, 0)),
            pl.BlockSpec(memory_space=pl.ANY),
            pl.BlockSpec((tm, d), lambda m: (m, 0)),
            pl.BlockSpec((1, d), const),
            pl.BlockSpec((1, d), const),
        ],
        out_specs=[
            pl.BlockSpec((tm, d), lambda m: (m, 0)),
            pl.BlockSpec((tm, d), lambda m: (m, 0)),
        ],
        out_shape=[jax.ShapeDtypeStruct((n, d), F32), jax.ShapeDtypeStruct((n, d), BF16)],
        scratch_shapes=[pltpu.VMEM((kdim, d), BF16), pltpu.VMEM((2, ws, d), F32),
                        pltpu.SemaphoreType.DMA((2,))],
        compiler_params=_cparams("arbitrary"),
        name="mm_res_ln_resident",
    )(z, w, h, g.reshape(1, d), b.reshape(1, d))


def _mm_res_ln(z, w, h, g, b, alpha, tm):
    n, kdim = z.shape
    d = w.shape[1]
    if kdim * d * 4 <= RESIDENT_WEIGHT_BYTES:
        return _mm_res_ln_resident(z, w, h, g, b, alpha, tm)
    w = w.astype(BF16)
    tk = _divisor_tile(kdim, 1408, LANES)
    nk = kdim // tk
    assert nk >= 2
    return pl.pallas_call(
        functools.partial(_mm_res_ln_kernel, nk=nk, alpha=alpha),
        grid=(n // tm, nk),
        in_specs=[
            pl.BlockSpec((tm, tk), lambda m, k: (m, k)),
            pl.BlockSpec((tk, d), lambda m, k: (k, 0)),
            pl.BlockSpec((tm, d), lambda m, k: (m, 0)),
            pl.BlockSpec((1, d), lambda m, k: (0, 0)),
            pl.BlockSpec((1, d), lambda m, k: (0, 0)),
        ],
        out_specs=[
            pl.BlockSpec((tm, d), lambda m, k: (m, 0)),
            pl.BlockSpec((tm, d), lambda m, k: (m, 0)),
        ],
        out_shape=[jax.ShapeDtypeStruct((n, d), F32), jax.ShapeDtypeStruct((n, d), BF16)],
        compiler_params=_cparams("parallel", "arbitrary"),
        name="mm_res_ln",
    )(z, w, h, g.reshape(1, d), b.reshape(1, d))


def _ffn_up_kernel(exp_ref, nact_ref, x_ref, w1_ref, w3_ref, o_ref, w1b_ref, w3b_ref):
    c = pl.program_id(1)
    prev = exp_ref[jnp.maximum(c - 1, 0)]
    new_weights = jnp.logical_or(c == 0, exp_ref[c] != prev)

    @pl.when(new_weights)
    def _():
        w1b_ref[...] = w1_ref[...].astype(BF16)
        w3b_ref[...] = w3_ref[...].astype(BF16)

    @pl.when(c < nact_ref[0])
    def _():
        x = x_ref[...]
        a = jnp.dot(x, w1b_ref[...], preferred_element_type=F32)
        bb = jnp.dot(x, w3b_ref[...], preferred_element_type=F32)
        o_ref[...] = (_silu(a) * bb).astype(BF16)

    @pl.when(c >= nact_ref[0])
    def _():
        o_ref[...] = jnp.zeros_like(o_ref)


def _ffn_up(x, w1, w3, chunk_expert, n_active, rows):
    p, d = x.shape
    f = w1.shape[2]
    tf = _divisor_tile(f, 512, LANES)
    grid_spec = pltpu.PrefetchScalarGridSpec(
        num_scalar_prefetch=2,
        grid=(f // tf, p // rows),
        in_specs=[
            pl.BlockSpec((rows, d), lambda j, c, e, na: (c, 0)),
            pl.BlockSpec((None, d, tf), lambda j, c, e, na: (e[c], 0, j)),
            pl.BlockSpec((None, d, tf), lambda j, c, e, na: (e[c], 0, j)),
        ],
        out_specs=pl.BlockSpec((rows, tf), lambda j, c, e, na: (c, j)),
        scratch_shapes=[pltpu.VMEM((d, tf), BF16), pltpu.VMEM((d, tf), BF16)],
    )
    return pl.pallas_call(
        _ffn_up_kernel,
        grid_spec=grid_spec,
        out_shape=jax.ShapeDtypeStruct((p, f), BF16),
        compiler_params=_cparams("arbitrary", "arbitrary"),
        name="ffn_up",
    )(chunk_expert, n_active, x, w1, w3)


def _dense_ffn(hb, h, w1, w3, w2, g, b, alpha, tm):
    n = hb.shape[0]
    nchunks = n // tm
    hmid = _ffn_up(hb, w1[None], w3[None], jnp.zeros((nchunks,), jnp.int32),
                   jnp.full((1,), nchunks, jnp.int32), tm)
    return _mm_res_ln(hmid, w2, h, g, b, alpha, tm)


def _conv_proj_kernel(x_ref, wb_ref, wc_ref, wh_ref, cw_ref, cb_ref, o_ref,
                      wbb_ref, wcb_ref, whb_ref, carry_ref):
    bi = pl.program_id(1)
    ti = pl.program_id(2)

    @pl.when(jnp.logical_and(bi == 0, ti == 0))
    def _():
        wbb_ref[...] = wb_ref[...].astype(BF16)
        wcb_ref[...] = wc_ref[...].astype(BF16)
        whb_ref[...] = wh_ref[...].astype(BF16)

    @pl.when(ti == 0)
    def _():
        carry_ref[...] = jnp.zeros_like(carry_ref)

    x = x_ref[...]
    gate_b = jnp.dot(x, wbb_ref[...], preferred_element_type=F32)
    gate_c = jnp.dot(x, wcb_ref[...], preferred_element_type=F32)
    hh = jnp.dot(x, whb_ref[...], preferred_element_type=F32)
    u = gate_c * hh
    tm = u.shape[0]
    prev1 = carry_ref[7:8, :]
    prev2 = carry_ref[6:7, :]
    row = lax.broadcasted_iota(jnp.int32, (tm, 1), 0)
    r1 = jnp.where(row == 0, prev1, pltpu.roll(u, 1, axis=0))
    r2 = jnp.where(row == 0, prev2, jnp.where(row == 1, prev1, pltpu.roll(u, 2, axis=0)))
    v = cw_ref[0:1, :] * r2 + cw_ref[1:2, :] * r1 + cw_ref[2:3, :] * u + cb_ref[...]
    carry_ref[...] = u[tm - 8:, :]
    o_ref[...] = (gate_b * v).astype(BF16)


def _conv_proj(hb, w_in, conv_w, conv_b, batch, tm):
    n, d = hb.shape
    tn = _divisor_tile(d, 512, LANES)
    nd = d // tn
    tpb = n // batch // tm
    return pl.pallas_call(
        _conv_proj_kernel,
        grid=(nd, batch, tpb),
        in_specs=[
            pl.BlockSpec((tm, d), lambda j, bi, ti: (bi * tpb + ti, 0)),
            pl.BlockSpec((d, tn), lambda j, bi, ti: (0, j)),
            pl.BlockSpec((d, tn), lambda j, bi, ti: (0, nd + j)),
            pl.BlockSpec((d, tn), lambda j, bi, ti: (0, 2 * nd + j)),
            pl.BlockSpec((3, tn), lambda j, bi, ti: (0, j)),
            pl.BlockSpec((1, tn), lambda j, bi, ti: (0, j)),
        ],
        out_specs=pl.BlockSpec((tm, tn), lambda j, bi, ti: (bi * tpb + ti, j)),
        out_shape=jax.ShapeDtypeStruct((n, d), BF16),
        scratch_shapes=[pltpu.VMEM((d, tn), BF16)] * 3 + [pltpu.VMEM((8, tn), F32)],
        compiler_params=_cparams("arbitrary", "arbitrary", "arbitrary"),
        name="conv_proj",
    )(hb, w_in, w_in, w_in, conv_w, conv_b.reshape(1, d))


def _conv_mixer_layer(h, hb, p, g, b, alpha, batch, tm):
    z = _conv_proj(hb, p["conv_w_in"], p["conv_w"], p["conv_b"], batch, tm)
    return _mm_res_ln(z, p["conv_w_out"], h, g, b, alpha, tm)


MOE_ROWS = 128


def _split_bf16(a):
    hi = a.astype(BF16)
    lo = (a - hi.astype(F32)).astype(BF16)
    return hi, lo


def _dot_f32(a, b):
    ah, al = _split_bf16(a)
    bh, bl = _split_bf16(b)
    return (jnp.dot(ah, bh, preferred_element_type=F32)
            + (jnp.dot(ah, bl, preferred_element_type=F32)
               + jnp.dot(al, bh, preferred_element_type=F32)))


def _router_kernel(h_ref, w_ref, b_ref, info_ref, cnt_ref, carry_ref):
    i = pl.program_id(0)

    @pl.when(i == 0)
    def _():
        carry_ref[...] = jnp.zeros_like(carry_ref)

    logits = _dot_f32(h_ref[...], w_ref[...]) + b_ref[...]
    tm, ne = logits.shape
    lane = lax.broadcasted_iota(jnp.int32, (tm, ne), 1)
    m1 = jnp.max(logits, axis=-1, keepdims=True)
    i1 = jnp.min(jnp.where(logits == m1, lane, ne), axis=-1, keepdims=True)
    mask1 = lane == i1
    rest = jnp.where(mask1, -jnp.inf, logits)
    m2 = jnp.max(rest, axis=-1, keepdims=True)
    i2 = jnp.min(jnp.where(rest == m2, lane, ne), axis=-1, keepdims=True)
    mask2 = lane == i2
    dd = jnp.exp(m2 - m1)
    g1 = 1.0 / (1.0 + dd)
    g2 = dd / (1.0 + dd)
    sel = jnp.where(jnp.logical_or(mask1, mask2), 1.0, 0.0)
    r_i = lax.broadcasted_iota(jnp.int32, (tm, tm), 0)
    c_i = lax.broadcasted_iota(jnp.int32, (tm, tm), 1)
    tril = jnp.where(c_i < r_i, 1.0, 0.0).astype(BF16)
    rank = jnp.dot(tril, sel.astype(BF16), preferred_element_type=F32) + carry_ref[...]
    r1 = jnp.sum(jnp.where(mask1, rank, 0.0), axis=-1, keepdims=True)
    r2 = jnp.sum(jnp.where(mask2, rank, 0.0), axis=-1, keepdims=True)
    info = jnp.where(lane == 0, i1.astype(F32),
           jnp.where(lane == 1, i2.astype(F32),
           jnp.where(lane == 2, g1,
           jnp.where(lane == 3, g2,
           jnp.where(lane == 4, r1,
           jnp.where(lane == 5, r2, 0.0))))))
    info_ref[...] = info
    total = carry_ref[...] + jnp.sum(sel, axis=0, keepdims=True)
    carry_ref[...] = total
    cnt_ref[...] = total


def _router(h, w, b, tm):
    n, d = h.shape
    ne = w.shape[1]
    assert ne >= 6
    return pl.pallas_call(
        _router_kernel,
        grid=(n // tm,),
        in_specs=[
            pl.BlockSpec((tm, d), lambda i: (i, 0)),
            pl.BlockSpec((d, ne), lambda i: (0, 0)),
            pl.BlockSpec((1, ne), lambda i: (0, 0)),
        ],
        out_specs=[
            pl.BlockSpec((tm, ne), lambda i: (i, 0)),
            pl.BlockSpec((1, ne), lambda i: (0, 0)),
        ],
        out_shape=[jax.ShapeDtypeStruct((n, ne), F32), jax.ShapeDtypeStruct((1, ne), F32)],
        scratch_shapes=[pltpu.VMEM((1, ne), F32)],
        compiler_params=_cparams("arbitrary"),
        name="moe_router",
    )(h, w, b.reshape(1, ne))


def _row_copy(src_hbm, row, dst_vmem, r, sem):
    return pltpu.make_async_copy(src_hbm.at[pl.ds(row, 1), :], dst_vmem.at[pl.ds(r, 1), :], sem)


def _rows_wait(src_hbm, dst_vmem, sem):
    pltpu.make_async_copy(src_hbm.at[pl.ds(0, dst_vmem.shape[0]), :], dst_vmem, sem).wait()


GATHER_SLOTS = 3
GATHER_UNITS = 4


def _moe_gather_kernel(src_ref, nact_ref, h_ref, o_ref, buf_ref, sem):
    c = pl.program_id(0)
    rows = buf_ref.shape[1]

    nact = nact_ref[0]
    slot = c % GATHER_SLOTS
    slot1 = (c + 1) % GATHER_SLOTS
    slot2 = (c + 2) % GATHER_SLOTS

    def start(chunk, to_slot, r):
        _row_copy(h_ref, src_ref[chunk * rows + r], buf_ref.at[to_slot], r, sem.at[to_slot]).start()

    @pl.when(jnp.logical_and(c == 0, nact > 0))
    def _():
        lax.fori_loop(0, rows, lambda r, carry: (start(0, slot, r), carry)[1], 0, unroll=8)

    @pl.when(jnp.logical_and(c == 0, nact > 1))
    def _():
        lax.fori_loop(0, rows, lambda r, carry: (start(1, slot1, r), carry)[1], 0, unroll=8)

    @pl.when(c < nact)
    def _():
        _rows_wait(h_ref, buf_ref.at[slot], sem.at[slot])
        nxt = jnp.minimum(c + 2, nact - 1)
        for r in range(rows):
            start(nxt, slot2, r)
        o_ref[...] = buf_ref[slot].astype(BF16)

        @pl.when(c == nact - 1)
        def _():
            _rows_wait(h_ref, buf_ref.at[slot2], sem.at[slot2])

            @pl.when(nact > 1)
            def _():
                _rows_wait(h_ref, buf_ref.at[slot1], sem.at[slot1])

    @pl.when(c >= nact)
    def _():
        o_ref[...] = jnp.zeros_like(o_ref)


def _moe_gather(h, src, n_active, p, rows):
    n, d = h.shape
    grid_spec = pltpu.PrefetchScalarGridSpec(
        num_scalar_prefetch=2,
        grid=(p // rows,),
        in_specs=[pl.BlockSpec(memory_space=pl.ANY)],
        out_specs=pl.BlockSpec((rows, d), lambda c, s, na: (c, 0)),
        scratch_shapes=[pltpu.VMEM((GATHER_SLOTS, rows, d), F32), pltpu.SemaphoreType.DMA((GATHER_SLOTS,))],
    )
    return pl.pallas_call(
        _moe_gather_kernel,
        grid_spec=grid_spec,
        out_shape=jax.ShapeDtypeStruct((p, d), BF16),
        compiler_params=_cparams("arbitrary"),
        name="moe_gather",
    )(src, n_active, h)


def _moe_combine_kernel(p1_ref, p2_ref, y_ref, h_ref, info_ref, g_ref, b_ref, o_ref, ob_ref,
                        buf1_ref, buf2_ref, sem, *, alpha):
    i = pl.program_id(0)
    tm = buf1_ref.shape[1]
    ntile = pl.num_programs(0)
    last = ntile - 1
    slot = i % GATHER_SLOTS
    slot1 = (i + 1) % GATHER_SLOTS
    slot2 = (i + 2) % GATHER_SLOTS

    def start(tile, to_slot, r):
        _row_copy(y_ref, p1_ref[tile * tm + r], buf1_ref.at[to_slot], r, sem.at[to_slot]).start()
        _row_copy(y_ref, p2_ref[tile * tm + r], buf2_ref.at[to_slot], r, sem.at[to_slot]).start()

    def wait(from_slot):
        _rows_wait(y_ref, buf1_ref.at[from_slot], sem.at[from_slot])
        _rows_wait(y_ref, buf2_ref.at[from_slot], sem.at[from_slot])

    @pl.when(i == 0)
    def _():
        lax.fori_loop(0, tm, lambda r, c: (start(0, slot, r), c)[1], 0, unroll=8)

    @pl.when(jnp.logical_and(i == 0, ntile > 1))
    def _():
        lax.fori_loop(0, tm, lambda r, c: (start(1, slot1, r), c)[1], 0, unroll=8)

    wait(slot)
    nxt = jnp.minimum(i + 2, last)
    for r in range(tm):
        start(nxt, slot2, r)
    info = info_ref[...]
    y = alpha * h_ref[...] + (info[:, 2:3] * buf1_ref[slot] + info[:, 3:4] * buf2_ref[slot])
    y = _layer_norm(y, g_ref[...], b_ref[...])
    o_ref[...] = y
    ob_ref[...] = y.astype(BF16)

    @pl.when(i == last)
    def _():
        wait(slot2)

        @pl.when(ntile > 1)
        def _():
            wait(slot1)


def _moe_combine(y, h, info, p1, p2, g, b, alpha, tm):
    n, d = h.shape
    ne = info.shape[1]
    grid_spec = pltpu.PrefetchScalarGridSpec(
        num_scalar_prefetch=2,
        grid=(n // tm,),
        in_specs=[
            pl.BlockSpec(memory_space=pl.ANY),
            pl.BlockSpec((tm, d), lambda i, a, c: (i, 0)),
            pl.BlockSpec((tm, ne), lambda i, a, c: (i, 0)),
            pl.BlockSpec((1, d), lambda i, a, c: (0, 0)),
            pl.BlockSpec((1, d), lambda i, a, c: (0, 0)),
        ],
        out_specs=[
            pl.BlockSpec((tm, d), lambda i, a, c: (i, 0)),
            pl.BlockSpec((tm, d), lambda i, a, c: (i, 0)),
        ],
        scratch_shapes=[pltpu.VMEM((GATHER_SLOTS, tm, d), F32), pltpu.VMEM((GATHER_SLOTS, tm, d), F32),
                        pltpu.SemaphoreType.DMA((GATHER_SLOTS,))],
    )
    return pl.pallas_call(
        functools.partial(_moe_combine_kernel, alpha=alpha),
        grid_spec=grid_spec,
        out_shape=[jax.ShapeDtypeStruct((n, d), F32), jax.ShapeDtypeStruct((n, d), BF16)],
        compiler_params=_cparams("arbitrary"),
        name="moe_combine",
    )(p1, p2, y, h, info, g.reshape(1, d), b.reshape(1, d))


MOE_PASS_BLOCKS = 18
MOE_F_TILE = 256
MOE_BLOCK_UNITS = 8


def _expert_ffn_kernel(pe_ref, ps_ref, pn_ref, nu_ref, xs_ref, w1_ref, w3_ref, w2_ref, y_ref,
                       x_buf, acc_ref, w1b_ref, w3b_ref, w2b_ref, sem):
    p = pl.program_id(0)
    f = pl.program_id(1)
    nf = pl.num_programs(1)
    nb = pn_ref[p]
    sb = MOE_ROWS
    start = pl.multiple_of(ps_ref[p], MOE_ROWS)
    nblock = nb // MOE_BLOCK_UNITS

    @pl.when(nb > 0)
    def _():
        @pl.when(f == 0)
        def _():
            cp = pltpu.make_async_copy(xs_ref.at[pl.ds(start, x_buf.shape[0]), :], x_buf, sem.at[0])
            cp.start()
            cp.wait()

        w1b_ref[...] = w1_ref[...].astype(BF16)
        w3b_ref[...] = w3_ref[...].astype(BF16)

        @pl.when(nblock == 0)
        def _():
            w2b_ref[...] = w2_ref[...].astype(BF16)

        def up(unit, nunit):
            x = x_buf[pl.ds(pl.multiple_of(unit * sb, sb), nunit * sb), :]
            a = jnp.dot(x, w1b_ref[...], preferred_element_type=F32)
            bb = jnp.dot(x, w3b_ref[...], preferred_element_type=F32)
            return (_silu(a) * bb).astype(BF16)

        def out_copy(i):
            rows = pl.ds(pl.multiple_of(i * sb, sb), sb)
            dst = pl.ds(pl.multiple_of(start + i * sb, sb), sb)
            return pltpu.make_async_copy(acc_ref.at[rows, :], y_ref.at[dst, :], sem.at[1])

        def down(unit, nunit, hmid, first):
            rows = pl.ds(pl.multiple_of(unit * sb, sb), nunit * sb)
            part = jnp.dot(hmid, w2b_ref[...], preferred_element_type=F32)
            if first:
                acc_ref[rows, :] = part
            else:
                acc_ref[rows, :] += part

            @pl.when(f == nf - 1)
            def _():
                for u in range(nunit):
                    out_copy(unit + u).start()

        def sweep(first):
            bu = MOE_BLOCK_UNITS

            @pl.when(nblock > 0)
            def _():
                def body(i, hprev):
                    down(bu * (i - 1), bu, hprev, first)
                    return up(bu * i, bu)

                h0 = up(0, bu)
                w2b_ref[...] = w2_ref[...].astype(BF16)
                down(bu * (nblock - 1), bu, lax.fori_loop(1, nblock, body, h0), first)

            done = nblock * bu
            size = bu // 2
            while size >= 1:
                @pl.when((nb - done) & size != 0)
                def _(done=done, size=size):
                    down(done, size, up(done, size), first)

                done = done + ((nb - done) & size)
                size //= 2

        @pl.when(f == 0)
        def _():
            sweep(True)

        @pl.when(f > 0)
        def _():
            sweep(False)

        @pl.when(f == nf - 1)
        def _():
            lax.fori_loop(0, nb, lambda i, c: (out_copy(i).wait(), c)[1], 0)

    @pl.when(jnp.logical_and(p == pl.num_programs(0) - 1, f == nf - 1))
    def _():
        acc_ref[0:sb, :] = jnp.zeros((sb, acc_ref.shape[1]), F32)

        def zero_copy(i):
            dst = pl.ds(pl.multiple_of(i * sb, sb), sb)
            return pltpu.make_async_copy(acc_ref.at[0:sb, :], y_ref.at[dst, :], sem.at[1])

        n_blocks = y_ref.shape[0] // sb
        lax.fori_loop(nu_ref[1], n_blocks, lambda i, c: (zero_copy(i).start(), c)[1], 0)
        lax.fori_loop(nu_ref[1], n_blocks, lambda i, c: (zero_copy(i).wait(), c)[1], 0)


def _expert_ffn(xs, w1, w3, w2, pass_expert, pass_start, pass_nb, n_used, p_rows):
    d = xs.shape[1]
    f = w1.shape[2]
    tf = _divisor_tile(f, MOE_F_TILE, LANES)
    nf = f // tf
    npass = pass_expert.shape[0]
    r = MOE_PASS_BLOCKS * MOE_ROWS

    def fidx(p, j, nu):
        return jnp.where(p < nu[0], j, nf - 1)

    grid_spec = pltpu.PrefetchScalarGridSpec(
        num_scalar_prefetch=4,
        grid=(npass, nf),
        in_specs=[
            pl.BlockSpec(memory_space=pl.ANY),
            pl.BlockSpec((None, d, tf), lambda p, j, pe, ps, pn, nu: (pe[p], 0, fidx(p, j, nu))),
            pl.BlockSpec((None, d, tf), lambda p, j, pe, ps, pn, nu: (pe[p], 0, fidx(p, j, nu))),
            pl.BlockSpec((None, tf, d), lambda p, j, pe, ps, pn, nu: (pe[p], fidx(p, j, nu), 0)),
        ],
        out_specs=pl.BlockSpec(memory_space=pl.ANY),
        scratch_shapes=[
            pltpu.VMEM((r, d), BF16), pltpu.VMEM((r, d), F32),
            pltpu.VMEM((d, tf), BF16), pltpu.VMEM((d, tf), BF16), pltpu.VMEM((tf, d), BF16),
            pltpu.SemaphoreType.DMA((2,)),
        ],
    )
    return pl.pallas_call(
        _expert_ffn_kernel,
        grid_spec=grid_spec,
        out_shape=jax.ShapeDtypeStruct((p_rows, d), F32),
        compiler_params=_cparams("arbitrary", "arbitrary"),
        name="expert_ffn",
    )(pass_expert, pass_start, pass_nb, n_used, xs, w1, w3, w2)


def _moe_ffn(h, hb, router_w, router_b, w1, w3, w2, g, b, alpha, tm):
    n, d = h.shape
    ne = router_w.shape[1]
    rows = MOE_ROWS
    info, counts = _router(h, router_w, router_b, tm)
    counts = counts[0].astype(jnp.int32)
    nblk_e = (counts + rows - 1) // rows
    blk_end = jnp.cumsum(nblk_e)
    starts = (blk_end - nblk_e) * rows
    n_blocks = (n * TOP_K + ne * (rows - 1)) // rows
    p = n_blocks * rows
    n_active = blk_end[-1:].astype(jnp.int32)
    npass_e = (nblk_e + MOE_PASS_BLOCKS - 1) // MOE_PASS_BLOCKS
    pass_end = jnp.cumsum(npass_e)
    max_pass = n_blocks // MOE_PASS_BLOCKS + ne
    pidx = jnp.arange(max_pass, dtype=jnp.int32)
    n_used = jnp.stack([pass_end[-1], blk_end[-1]]).astype(jnp.int32)
    last_expert = jnp.sum(pass_end < pass_end[-1]).astype(jnp.int32)
    pass_expert = jnp.minimum(jnp.sum(pidx[:, None] >= pass_end[None, :], axis=1), last_expert).astype(jnp.int32)
    local = pidx - (pass_end - npass_e)[pass_expert]
    pass_start = (starts[pass_expert] + local * (MOE_PASS_BLOCKS * rows)).astype(jnp.int32)
    pass_nb = jnp.where(pidx < n_used[0],
                        jnp.clip(nblk_e[pass_expert] - local * MOE_PASS_BLOCKS, 0, MOE_PASS_BLOCKS),
                        0).astype(jnp.int32)
    pass_start = jnp.where(pass_nb > 0, pass_start, 0).astype(jnp.int32)
    i1 = info[:, 0].astype(jnp.int32)
    i2 = info[:, 1].astype(jnp.int32)
    p1 = starts[i1] + info[:, 4].astype(jnp.int32)
    p2 = starts[i2] + info[:, 5].astype(jnp.int32)
    tok = jnp.arange(n, dtype=jnp.int32)
    grows = GATHER_UNITS * rows
    p_in = -(-(p + MOE_PASS_BLOCKS * rows) // grows) * grows
    src = jnp.zeros((p_in,), jnp.int32).at[jnp.concatenate([p1, p2])].set(jnp.concatenate([tok, tok]))
    xs = _moe_gather(h, src, (n_active + GATHER_UNITS - 1) // GATHER_UNITS, p_in, grows)
    y = _expert_ffn(xs, w1, w3, w2, pass_expert, pass_start, pass_nb, n_used, p)
    tmc = _divisor_tile(n, 384, 16)
    return _moe_combine(y, h, info, p1, p2, g, b, alpha, tmc)


def _proj_kernel(*refs, epilogue, n_extra, n_out):
    x_ref, w_ref = refs[0], refs[1]
    extra = refs[2:2 + n_extra]
    outs = refs[2 + n_extra:2 + n_extra + n_out]
    wb_ref = refs[2 + n_extra + n_out]

    @pl.when(pl.program_id(1) == 0)
    def _():
        wb_ref[...] = w_ref[...].astype(BF16)

    y = jnp.dot(x_ref[...], wb_ref[...], preferred_element_type=F32)
    res = epilogue(y, *[e[...] for e in extra])
    for o_ref, r in zip(outs, res):
        o_ref[...] = r.astype(o_ref.dtype)


def _proj(xb, w, col0, ncols, epilogue, extras, out_dtypes, tm, name, tn_cap=1024):
    n, kdim = xb.shape
    tn = _divisor_tile(ncols, tn_cap, LANES)
    assert col0 % tn == 0
    off = col0 // tn
    outs = pl.pallas_call(
        functools.partial(_proj_kernel, epilogue=epilogue, n_extra=len(extras), n_out=len(out_dtypes)),
        grid=(ncols // tn, n // tm),
        in_specs=[
            pl.BlockSpec((tm, kdim), lambda j, m: (m, 0)),
            pl.BlockSpec((kdim, tn), lambda j, m: (0, off + j)),
        ] + [pl.BlockSpec((1, tn), lambda j, m: (0, j))] * len(extras),
        out_specs=[pl.BlockSpec((tm, tn), lambda j, m: (m, j))] * len(out_dtypes),
        out_shape=[jax.ShapeDtypeStruct((n, ncols), dt) for dt in out_dtypes],
        scratch_shapes=[pltpu.VMEM((kdim, tn), BF16)],
        compiler_params=_cparams("arbitrary", "arbitrary"),
        name=name,
    )(xb, w, *[e.reshape(1, ncols) for e in extras])
    return outs


def _tril_bf16(c, inclusive):
    r_i = lax.broadcasted_iota(jnp.int32, (c, c), 0)
    c_i = lax.broadcasted_iota(jnp.int32, (c, c), 1)
    keep = (c_i <= r_i) if inclusive else (c_i < r_i)
    return jnp.where(keep, 1.0, 0.0).astype(BF16)


def _cumsum_rows(x, tril):
    hi = x.astype(BF16)
    r1 = x - hi.astype(F32)
    mid = r1.astype(BF16)
    lo = (r1 - mid.astype(F32)).astype(BF16)
    return (jnp.dot(tril, hi, preferred_element_type=F32)
            + (jnp.dot(tril, mid, preferred_element_type=F32)
               + jnp.dot(tril, lo, preferred_element_type=F32)))


def _dot_nt(a, b):
    return lax.dot_general(a.astype(BF16), b.astype(BF16), (((1,), (1,)), ((), ())),
                           preferred_element_type=F32)


def _dot_tn(a, b):
    return lax.dot_general(a.astype(BF16), b.astype(BF16), (((0,), (0,)), ((), ())),
                           preferred_element_type=F32)


def _dot_nn(a, b):
    return jnp.dot(a.astype(BF16), b.astype(BF16), preferred_element_type=F32)


def _chunk_len(t):
    return _divisor_tile(t, 64, 16)


def _round_robin(gens):
    done = [None] * len(gens)
    while any(d is None for d in done):
        for j, gen in enumerate(gens):
            if done[j] is None:
                done[j] = next(gen)
    return done


HGRN_SUB = 16


def _hgrn_scan_kernel(q_ref, lf_ref, v_ref, gs_ref, ng_ref, o_ref, st_ref, *, chunk):
    t, w = q_ref.shape
    nhead = w // HGRN_HEAD
    nsub = chunk // HGRN_SUB
    st_ref[...] = jnp.zeros_like(st_ref)
    tril = _tril_bf16(chunk, True)
    row16 = lax.broadcasted_iota(jnp.int32, (HGRN_SUB, 1), 0)

    def head_chunk(q, lf, v, st):
        k = 1.0 - jnp.exp(lf)
        cum = _cumsum_rows(lf, tril)
        yield None
        o_inter = _dot_nt(q * jnp.exp(cum), st)
        vb = v.astype(BF16)
        cl = cum[chunk - 1:chunk]
        kd = k * jnp.exp(cl - cum)
        st_new = st * jnp.exp(cl) + _dot_tn(v, kd)
        yield None
        outs = []
        for i in range(nsub):
            lo, hi = i * HGRN_SUB, (i + 1) * HGRN_SUB
            qi, ki, vi, cumi = q[lo:hi], k[lo:hi], v[lo:hi], cum[lo:hi]
            oi = o_inter[lo:hi]
            if i > 0:
                ci = cum[lo:lo + 1]
                qt = qi * jnp.exp(cumi - ci)
                kt = k[0:lo] * jnp.exp(ci - cum[0:lo])
                oi = oi + jnp.dot(_dot_nt(qt, kt).astype(BF16), vb[0:lo], preferred_element_type=F32)
            for s in range(HGRN_SUB):
                dec = jnp.exp(jnp.minimum(cumi - cumi[s:s + 1], 0.0))
                col = jnp.sum(qi * dec * ki[s:s + 1], axis=-1, keepdims=True)
                col = jnp.where(row16 >= s, col, 0.0)
                oi = oi + col * vi[s:s + 1]
                if s % 4 == 3:
                    yield None
            outs.append(oi)
        o = jnp.concatenate(outs, axis=0)
        o = o * lax.rsqrt(jnp.mean(o * o, axis=-1, keepdims=True) + RMS_EPS)
        yield o, st_new

    def body(c, carry):
        rows = pl.ds(pl.multiple_of(c * chunk, 16), chunk)
        q = q_ref[rows, :].astype(F32)
        lf = lf_ref[rows, :]
        v = v_ref[rows, :].astype(F32)
        heads = []
        for hh in range(nhead):
            cols = slice(hh * HGRN_HEAD, (hh + 1) * HGRN_HEAD)
            heads.append(head_chunk(q[:, cols], lf[:, cols], v[:, cols], st_ref[hh]))
        outs = []
        for hh, (o, st_new) in enumerate(_round_robin(heads)):
            st_ref[hh] = st_new
            outs.append(o)
        o = jnp.concatenate(outs, axis=1)
        o_ref[rows, :] = (o * ng_ref[...] * gs_ref[rows, :].astype(F32)).astype(BF16)
        return carry

    lax.fori_loop(0, t // chunk, body, 0)


HGRN_GROUP = 8


def _hgrn_scan(q, lf, v, gs, norm_g, batch):
    n, d = q.shape
    t = n // batch
    chunk = _chunk_len(t)
    w = min(d, HGRN_GROUP * HGRN_HEAD)
    blk = pl.BlockSpec((t, w), lambda b, j: (b, j))
    return pl.pallas_call(
        functools.partial(_hgrn_scan_kernel, chunk=chunk),
        grid=(batch, d // w),
        in_specs=[blk, blk, blk, blk, pl.BlockSpec((1, w), lambda b, j: (0, j))],
        out_specs=blk,
        out_shape=jax.ShapeDtypeStruct((n, d), BF16),
        scratch_shapes=[pltpu.VMEM((w // HGRN_HEAD, HGRN_HEAD, HGRN_HEAD), F32)],
        compiler_params=_cparams("parallel", "parallel"),
        name="hgrn_scan",
    )(q, lf, v, gs, norm_g.reshape(1, d))


def _hgrn_mixer_layer(h, hb, p, layer_idx, g, b, alpha, batch, tm):
    d = h.shape[1]
    w_in = p["hgrn_w_in"]
    lb = jnp.cumsum(jax.nn.softmax(p["hgrn_lb"].astype(F32), axis=0), axis=0)
    lb = lb[layer_idx] - lb[0]
    (q,) = _proj(hb, w_in, 0, d, lambda y: (_silu(y),), [], [BF16], tm, "hgrn_proj_q")
    (lf,) = _proj(hb, w_in, d, d, lambda y, lbv: (jnp.log(lbv + (1.0 - lbv) * _sigmoid(y)),),
                  [lb], [F32], tm, "hgrn_proj_f")
    (v,) = _proj(hb, w_in, 2 * d, d, lambda y: (y,), [], [BF16], tm, "hgrn_proj_i")
    (gs,) = _proj(hb, w_in, 3 * d, d, lambda y: (_silu(y),), [], [BF16], tm, "hgrn_proj_g")
    z = _hgrn_scan(q, lf, v, gs, p["hgrn_norm_g"], batch)
    return _mm_res_ln(z, p["hgrn_w_out"], h, g, b, alpha, tm)


def _fox_gate_kernel(h_ref, w_ref, bf_ref, c_ref, carry_ref):
    @pl.when(pl.program_id(1) == 0)
    def _():
        carry_ref[...] = jnp.zeros_like(carry_ref)

    x = _dot_f32(h_ref[...], w_ref[...]) + bf_ref[...]
    log_f = jnp.minimum(x, 0.0) - jnp.log(1.0 + jnp.exp(-jnp.abs(x)))
    tm = x.shape[0]
    c = _cumsum_rows(log_f, _tril_bf16(tm, True)) + carry_ref[...]
    c_ref[...] = c
    carry_ref[...] = c[tm - 1:tm, :]


def _fox_gate(h, w_f, b_f, batch, tm):
    n, d = h.shape
    nh = w_f.shape[1]
    tpb = n // batch // tm
    return pl.pallas_call(
        _fox_gate_kernel,
        grid=(batch, tpb),
        in_specs=[
            pl.BlockSpec((tm, d), lambda b, t: (b * tpb + t, 0)),
            pl.BlockSpec((d, nh), lambda b, t: (0, 0)),
            pl.BlockSpec((1, nh), lambda b, t: (0, 0)),
        ],
        out_specs=pl.BlockSpec((tm, nh), lambda b, t: (b * tpb + t, 0)),
        out_shape=jax.ShapeDtypeStruct((n, nh), F32),
        scratch_shapes=[pltpu.VMEM((1, nh), F32)],
        compiler_params=_cparams("arbitrary", "arbitrary"),
        name="fox_gate",
    )(h, w_f, b_f.reshape(1, nh))


def _fox_attn_kernel(q_ref, k_ref, v_ref, sg_ref, c_ref, ct_ref, o_ref, *, tq):
    hd = pl.program_id(1)
    t = q_ref.shape[0]
    nh = c_ref.shape[1]
    lane = lax.broadcasted_iota(jnp.int32, (t, nh), 1)
    c_col = jnp.sum(jnp.where(lane == hd, c_ref[...], 0.0), axis=-1, keepdims=True)
    c_row = ct_ref[pl.ds(hd, 1), :]
    def query_tile(i):
        lo, hi = i * tq, (i + 1) * tq
        s = lax.dot_general(q_ref[lo:hi, :], k_ref[0:hi, :], (((1,), (1,)), ((), ())),
                            preferred_element_type=F32)
        yield None
        s = s + c_col[lo:hi] - c_row[:, 0:hi]
        r_i = lax.broadcasted_iota(jnp.int32, (tq, hi), 0) + lo
        c_i = lax.broadcasted_iota(jnp.int32, (tq, hi), 1)
        s = jnp.where(c_i <= r_i, s, -jnp.inf)
        m = jnp.max(s, axis=-1, keepdims=True)
        p = jnp.exp(s - m)
        l = jnp.sum(p, axis=-1, keepdims=True)
        yield None
        o = jnp.dot(p.astype(BF16), v_ref[0:hi, :], preferred_element_type=F32) / l
        o_ref[lo:hi, :] = (o * sg_ref[lo:hi, :]).astype(BF16)
        yield True

    _round_robin([query_tile(i) for i in range(t // tq)])


def _fox_attn(q, k, v, sg, c, ct, batch):
    n, d = q.shape
    t = n // batch
    nh = d // FOX_HEAD
    tq = _divisor_tile(t, 768, 16)
    blk = pl.BlockSpec((t, FOX_HEAD), lambda b, h: (b, h))
    return pl.pallas_call(
        functools.partial(_fox_attn_kernel, tq=tq),
        grid=(batch, nh),
        in_specs=[blk, blk, blk, blk,
                  pl.BlockSpec((t, nh), lambda b, h: (b, 0)),
                  pl.BlockSpec((None, nh, t), lambda b, h: (b, 0, 0))],
        out_specs=blk,
        out_shape=jax.ShapeDtypeStruct((n, d), BF16),
        compiler_params=_cparams("parallel", "parallel"),
        name="fox_attn",
    )(q, k, v, sg, c, ct)


def _head_rms_epilogue(scale):
    def epi(y, gain):
        outs = []
        for j in range(y.shape[1] // FOX_HEAD):
            yj = y[:, j * FOX_HEAD:(j + 1) * FOX_HEAD]
            yj = yj * lax.rsqrt(jnp.mean(yj * yj, axis=-1, keepdims=True) + RMS_EPS)
            outs.append(yj * gain[:, j * FOX_HEAD:(j + 1) * FOX_HEAD] * scale)
        return (jnp.concatenate(outs, axis=1),)
    return epi


def _fox_mixer_layer(h, hb, p, g, b, alpha, batch, tm):
    n, d = h.shape
    nh = d // FOX_HEAD
    w_in = p["fox_w_in"]
    qg = jnp.tile(p["fox_q_norm_g"], nh)
    kg = jnp.tile(p["fox_k_norm_g"], nh)
    (q,) = _proj(hb, w_in, 0, d, _head_rms_epilogue(FOX_HEAD ** -0.5), [qg], [BF16], tm, "fox_proj_q")
    (k,) = _proj(hb, w_in, d, d, _head_rms_epilogue(1.0), [kg], [BF16], tm, "fox_proj_k")
    (v,) = _proj(hb, w_in, 2 * d, d, lambda y: (y,), [], [BF16], tm, "fox_proj_v")
    (sg,) = _proj(hb, w_in, 3 * d, d, lambda y: (_sigmoid(y),), [], [F32], tm, "fox_proj_g")
    c = _fox_gate(h, w_in[:, 4 * d:], p["fox_b_f"], batch, tm)
    ct = c.reshape(batch, n // batch, nh).transpose(0, 2, 1)
    z = _fox_attn(q, k, v, sg, c, ct, batch)
    return _mm_res_ln(z, p["fox_w_out"], h, g, b, alpha, tm)


def _rwkv_mix_kernel(h_ref, mu_ref, *refs):
    outs, carry_ref = refs[:-1], refs[-1]

    @pl.when(pl.program_id(1) == 0)
    def _():
        carry_ref[...] = jnp.zeros_like(carry_ref)

    x = h_ref[...]
    tm = x.shape[0]
    row = lax.broadcasted_iota(jnp.int32, (tm, 1), 0)
    prev = jnp.where(row == 0, carry_ref[7:8, :], pltpu.roll(x, 1, axis=0))
    xx = prev - x
    carry_ref[...] = x[tm - 8:, :]
    for j, o_ref in enumerate(outs):
        o_ref[...] = (x + xx * mu_ref[j:j + 1, :]).astype(BF16)


def _rwkv_mix(h, mu, batch, tm):
    n, d = h.shape
    nmix = mu.shape[0]
    tpb = n // batch // tm
    blk = pl.BlockSpec((tm, d), lambda b, t: (b * tpb + t, 0))
    return pl.pallas_call(
        _rwkv_mix_kernel,
        grid=(batch, tpb),
        in_specs=[blk, pl.BlockSpec((nmix, d), lambda b, t: (0, 0))],
        out_specs=[blk] * nmix,
        out_shape=[jax.ShapeDtypeStruct((n, d), BF16)] * nmix,
        scratch_shapes=[pltpu.VMEM((8, d), F32)],
        compiler_params=_cparams("arbitrary", "arbitrary"),
        name="rwkv_mix",
    )(h, mu)


def _lora_kernel(x_ref, wa_ref, wb_ref, bias_ref, o_ref, wab_ref, wbb_ref, *, mid_act, out_act):
    @pl.when(pl.program_id(0) == 0)
    def _():
        wab_ref[...] = wa_ref[...].astype(BF16)
        wbb_ref[...] = wb_ref[...].astype(BF16)

    mid = mid_act(jnp.dot(x_ref[...], wab_ref[...], preferred_element_type=F32))
    y = jnp.dot(mid.astype(BF16), wbb_ref[...], preferred_element_type=F32)
    o_ref[...] = out_act(bias_ref[...] + y)


def _lora(xb, wa, wb, bias, mid_act, out_act, tm, name):
    n, d = xb.shape
    r = wa.shape[1]
    dout = wb.shape[1]
    return pl.pallas_call(
        functools.partial(_lora_kernel, mid_act=mid_act, out_act=out_act),
        grid=(n // tm,),
        in_specs=[
            pl.BlockSpec((tm, d), lambda i: (i, 0)),
            pl.BlockSpec((d, r), lambda i: (0, 0)),
            pl.BlockSpec((r, dout), lambda i: (0, 0)),
            pl.BlockSpec((1, dout), lambda i: (0, 0)),
        ],
        out_specs=pl.BlockSpec((tm, dout), lambda i: (i, 0)),
        out_shape=jax.ShapeDtypeStruct((n, dout), F32),
        scratch_shapes=[pltpu.VMEM((d, r), BF16), pltpu.VMEM((r, dout), BF16)],
        compiler_params=_cparams("arbitrary"),
        name=name,
    )(xb, wa, wb, bias.reshape(1, dout))


def _rwkv_log_decay(z):
    w_log = -(jnp.maximum(-z, 0.0) + jnp.log(1.0 + jnp.exp(-jnp.abs(z)))) - 0.5
    return -jnp.exp(w_log)


RWKV_GROUP = 4
RWKV_UNROLL = 8


def _seg_sum(x, bd):
    hi = x.astype(BF16)
    lo = (x - hi.astype(F32)).astype(BF16)
    return jnp.dot(hi, bd, preferred_element_type=F32) + jnp.dot(lo, bd, preferred_element_type=F32)


def _rwkv_scan_kernel(r_ref, kr_ref, v_ref, lw_ref, a_ref, g_ref, kk_p, ka_p, rk_p, gg_p, gb_p,
                      o_ref, kk_s, k_s, bonus_s, y_s, st_ref, *, chunk, ptile):
    t, w = r_ref.shape
    nhead = w // RWKV_HEAD
    sc = nhead * chunk
    lane_r = lax.broadcasted_iota(jnp.int32, (w, w), 0) // RWKV_HEAD
    lane_c = lax.broadcasted_iota(jnp.int32, (w, w), 1) // RWKV_HEAD
    bd = jnp.where(lane_r == lane_c, 1.0, 0.0).astype(BF16)

    def prologue(i, carry):
        rows = pl.ds(pl.multiple_of(i * ptile, 8), ptile)
        kr = kr_ref[rows, :]
        a = a_ref[rows, :]
        kkr = kr * kk_p[...]
        nrm = jnp.maximum(jnp.sqrt(_seg_sum(kkr * kkr, bd)), 1e-12)
        kk_s[rows, :] = kkr / nrm
        k = kr * (1.0 + (a - 1.0) * ka_p[...])
        k_s[rows, :] = k
        bonus_s[rows, :] = _seg_sum(r_ref[rows, :] * k * rk_p[...], bd) * v_ref[rows, :]
        return carry

    lax.fori_loop(0, t // ptile, prologue, 0)

    st_ref[...] = jnp.zeros_like(st_ref)
    tril = _tril_bf16(chunk, True)
    head_of_lane = lax.broadcasted_iota(jnp.int32, (chunk, w), 1) // RWKV_HEAD
    ri = lax.broadcasted_iota(jnp.int32, (2 * sc, sc), 0)
    ci = lax.broadcasted_iota(jnp.int32, (2 * sc, sc), 1)
    low_mask = ci < jnp.where(ri < sc, ri, ri - sc + 1)
    nsteps = max(1, (chunk - 1).bit_length())

    def stack(x):
        return jnp.concatenate([jnp.where(head_of_lane == hh, x, 0.0) for hh in range(nhead)], axis=0)

    eye = jnp.where(lax.broadcasted_iota(jnp.int32, (sc, sc), 0)
                    == lax.broadcasted_iota(jnp.int32, (sc, sc), 1), 1.0, 0.0)

    def prepare(c):
        start = c * chunk
        rows = pl.ds(start if isinstance(start, int) else pl.multiple_of(start, 16), chunk)
        r = r_ref[rows, :]
        v = v_ref[rows, :]
        lw = lw_ref[rows, :]
        a = a_ref[rows, :]
        kk = kk_s[rows, :]
        k = k_s[rows, :]
        cum = _cumsum_rows(lw, tril)
        e_neg = jnp.exp(-cum)
        at2 = stack(-kk * jnp.exp(cum - lw))
        rt2 = stack(r * jnp.exp(cum))
        bvec = kk * a
        bb2 = stack(bvec * e_neg)
        kb2 = stack(k * e_neg)
        v2 = stack(v)
        ar2 = jnp.concatenate([at2, rt2], axis=0).astype(BF16)
        cl = cum[chunk - 1:chunk]
        e_end = jnp.exp(cl - cum)
        khbh = jnp.concatenate([stack(k * e_end), stack(bvec * e_end)], axis=0).astype(BF16)
        yield None
        pb = jnp.where(low_mask, _dot_nt(ar2, bb2), 0.0)
        pk = jnp.where(low_mask, _dot_nt(ar2, kb2), 0.0)
        m_ab, m_rb = pb[:sc], pb[sc:]
        m_ak, m_rk = pk[:sc], pk[sc:]
        yield None
        u0 = _dot_nn(m_ak, v2)
        y0 = _dot_nn(m_rk, v2)
        tinv = eye + m_ab
        lpow = m_ab
        for _ in range(nsteps - 1):
            yield None
            lpow = _dot_nn(lpow, lpow)
            tinv = tinv + _dot_nn(tinv, lpow)
        return dict(rows=rows, ar2=ar2, tinv=tinv.astype(BF16), m_rb=m_rb.astype(BF16),
                    u0=u0, y0=y0, v2=v2.astype(BF16), khbh=khbh, decay=jnp.exp(cl))

    def chunk_steps(c, j, run):
        pc = yield from prepare(c)
        while run["turn"] != j:
            yield None
        st = run["st"]
        ps = _dot_nt(pc["ar2"], st)
        yield None
        u2 = jnp.dot(pc["tinv"], (ps[:sc] + pc["u0"]).astype(BF16), preferred_element_type=F32)
        u2b = u2.astype(BF16)
        yield None
        y2 = ps[sc:] + pc["y0"] + jnp.dot(pc["m_rb"], u2b, preferred_element_type=F32)
        y = y2[0:chunk]
        for hh in range(1, nhead):
            y = y + y2[hh * chunk:(hh + 1) * chunk]
        y_s[pc["rows"], :] = y
        run["st"] = st * pc["decay"] + _dot_tn(jnp.concatenate([pc["v2"], u2b], axis=0), pc["khbh"])
        run["turn"] = j + 1
        yield True

    def run_group(chunk_ids):
        run = dict(st=st_ref[...], turn=0)
        _round_robin([chunk_steps(c, j, run) for j, c in enumerate(chunk_ids)])
        st_ref[...] = run["st"]

    nchunks = t // chunk

    def body(i, carry):
        run_group([i * RWKV_UNROLL + j for j in range(RWKV_UNROLL)])
        return carry

    lax.fori_loop(0, nchunks // RWKV_UNROLL, body, 0)
    tail = list(range(nchunks - nchunks % RWKV_UNROLL, nchunks))
    if tail:
        run_group(tail)

    inv = 1.0 / RWKV_HEAD

    def epilogue(i, carry):
        rows = pl.ds(pl.multiple_of(i * ptile, 8), ptile)
        y = y_s[rows, :]
        mu = _seg_sum(y, bd) * inv
        yc = y - mu
        var = _seg_sum(yc * yc, bd) * inv
        yn = yc * lax.rsqrt(var + 1e-5 * RWKV_HEAD) * gg_p[...] + gb_p[...]
        o_ref[rows, :] = ((yn + bonus_s[rows, :]) * g_ref[rows, :]).astype(BF16)
        return carry

    lax.fori_loop(0, t // ptile, epilogue, 0)


def _rwkv_scan(r, kr, v, lw, a, g, p, batch):
    n, d = r.shape
    t = n // batch
    chunk = _chunk_len(t)
    w = min(d, RWKV_GROUP * RWKV_HEAD)
    ptile = _divisor_tile(t, 768, 16)
    blk = pl.BlockSpec((t, w), lambda b, j: (b, j))
    prm = pl.BlockSpec((1, w), lambda b, j: (0, j))
    params = [p["rwkv_k_k"], p["rwkv_k_a"], p["rwkv_r_k"], p["rwkv_gn_g"], p["rwkv_gn_b"]]
    return pl.pallas_call(
        functools.partial(_rwkv_scan_kernel, chunk=chunk, ptile=ptile),
        grid=(batch, d // w),
        in_specs=[blk] * 6 + [prm] * 5,
        out_specs=blk,
        out_shape=jax.ShapeDtypeStruct((n, d), BF16),
        scratch_shapes=[pltpu.VMEM((t, w), F32)] * 4 + [pltpu.VMEM((w, w), F32)],
        compiler_params=_cparams("parallel", "parallel"),
        name="rwkv_scan",
    )(r, kr, v, lw, a, g, *[x.reshape(1, d) for x in params])


def _rwkv_mixer_layer(h, hb, p, g, b, alpha, batch, tm):
    n, d = h.shape
    ident = lambda y: y
    xr, xw, xk, xv, xa, xg = _rwkv_mix(h, p["rwkv_mu"], batch, tm)
    (r,) = _proj(xr, p["rwkv_w_r"], 0, d, lambda y: (y,), [], [F32], tm, "rwkv_proj_r")
    (kr,) = _proj(xk, p["rwkv_w_k"], 0, d, lambda y: (y,), [], [F32], tm, "rwkv_proj_k")
    (v,) = _proj(xv, p["rwkv_w_v"], 0, d, lambda y: (y,), [], [F32], tm, "rwkv_proj_v")
    lw = _lora(xw, p["rwkv_w1"], p["rwkv_w2"], p["rwkv_w0"], jnp.tanh, _rwkv_log_decay, tm, "rwkv_lora_w")
    a = _lora(xa, p["rwkv_a1"], p["rwkv_a2"], p["rwkv_a0"], ident, _sigmoid, tm, "rwkv_lora_a")
    gate = _lora(xg, p["rwkv_g1"], p["rwkv_g2"], jnp.zeros((d,), F32), _sigmoid, ident, tm, "rwkv_lora_g")
    z = _rwkv_scan(r, kr, v, lw, a, gate, p, batch)
    return _mm_res_ln(z, p["rwkv_w_out"], h, g, b, alpha, tm)


def _embed_kernel(x_ref, meta_ref, h_ref, hb_ref, sem):
    b = pl.program_id(0)
    j = pl.program_id(1)
    tm = h_ref.shape[0]
    nmeta = meta_ref.shape[0]

    @pl.when(j == 0)
    def _():
        h_ref[0:nmeta, :] = meta_ref[...]
        cp = pltpu.make_async_copy(x_ref.at[b, pl.ds(0, tm - nmeta), :], h_ref.at[pl.ds(nmeta, tm - nmeta), :], sem)
        cp.start()
        cp.wait()

    @pl.when(j > 0)
    def _():
        first = pl.multiple_of(j * tm - nmeta, 8)
        cp = pltpu.make_async_copy(x_ref.at[b, pl.ds(first, tm), :], h_ref, sem)
        cp.start()
        cp.wait()

    hb_ref[...] = h_ref[...].astype(BF16)


def _embed(x, meta, tm):
    batch, seq, d = x.shape
    nmeta = meta.shape[0]
    t = nmeta + seq
    tpb = t // tm
    assert nmeta % 8 == 0 and tm % 8 == 0
    blk = pl.BlockSpec((tm, d), lambda b, j: (b * tpb + j, 0))
    return pl.pallas_call(
        _embed_kernel,
        grid=(batch, tpb),
        in_specs=[pl.BlockSpec(memory_space=pl.ANY), pl.BlockSpec((nmeta, d), lambda b, j: (0, 0))],
        out_specs=[blk, blk],
        out_shape=[jax.ShapeDtypeStruct((batch * t, d), F32), jax.ShapeDtypeStruct((batch * t, d), BF16)],
        scratch_shapes=[pltpu.SemaphoreType.DMA],
        compiler_params=_cparams("parallel", "arbitrary"),
        name="embed",
    )(x, meta.astype(x.dtype))


def kernel(x, meta, ln_mix_g, ln_mix_b, ln_ffn_g, ln_ffn_b, conv_w_in, conv_w, conv_b, conv_w_out, rwkv_mu, rwkv_w_r, rwkv_w_k, rwkv_w_v, rwkv_w0, rwkv_w1, rwkv_w2, rwkv_a0, rwkv_a1, rwkv_a2, rwkv_g1, rwkv_g2, rwkv_k_k, rwkv_k_a, rwkv_r_k, rwkv_gn_g, rwkv_gn_b, rwkv_w_out, hgrn_w_in, hgrn_lb, hgrn_norm_g, hgrn_w_out, fox_w_in, fox_b_f, fox_q_norm_g, fox_k_norm_g, fox_w_out, ffn0_w1, ffn0_w3, ffn0_w2, moe1_router, moe1_router_b, moe1_w1, moe1_w3, moe1_w2, ffn2_w1, ffn2_w3, ffn2_w2, moe3_router, moe3_router_b, moe3_w1, moe3_w3, moe3_w2):
    batch, seq, d = x.shape
    depth = ln_mix_g.shape[0]
    assert depth == 4
    alpha = (2.0 * depth) ** 0.25
    t = N_META + seq
    n = batch * t
    tm = _divisor_tile(t, 768, 16)
    p = dict(
        conv_w_in=conv_w_in, conv_w=conv_w, conv_b=conv_b, conv_w_out=conv_w_out,
        rwkv_mu=rwkv_mu, rwkv_w_r=rwkv_w_r, rwkv_w_k=rwkv_w_k, rwkv_w_v=rwkv_w_v, rwkv_w0=rwkv_w0,
        rwkv_w1=rwkv_w1, rwkv_w2=rwkv_w2, rwkv_a0=rwkv_a0, rwkv_a1=rwkv_a1, rwkv_a2=rwkv_a2,
        rwkv_g1=rwkv_g1, rwkv_g2=rwkv_g2, rwkv_k_k=rwkv_k_k, rwkv_k_a=rwkv_k_a, rwkv_r_k=rwkv_r_k,
        rwkv_gn_g=rwkv_gn_g, rwkv_gn_b=rwkv_gn_b, rwkv_w_out=rwkv_w_out,
        hgrn_w_in=hgrn_w_in, hgrn_lb=hgrn_lb, hgrn_norm_g=hgrn_norm_g, hgrn_w_out=hgrn_w_out,
        fox_w_in=fox_w_in, fox_b_f=fox_b_f, fox_q_norm_g=fox_q_norm_g, fox_k_norm_g=fox_k_norm_g,
        fox_w_out=fox_w_out,
    )
    assert meta.shape[0] == N_META
    h, hb = _embed(x, meta, tm)

    h, hb = _conv_mixer_layer(h, hb, p, ln_mix_g[0], ln_mix_b[0], alpha, batch, tm)
    h, hb = _dense_ffn(hb, h, ffn0_w1, ffn0_w3, ffn0_w2, ln_ffn_g[0], ln_ffn_b[0], alpha, tm)
    h, hb = _rwkv_mixer_layer(h, hb, p, ln_mix_g[1], ln_mix_b[1], alpha, batch, tm)
    h, hb = _moe_ffn(h, hb, moe1_router, moe1_router_b, moe1_w1, moe1_w3, moe1_w2,
                     ln_ffn_g[1], ln_ffn_b[1], alpha, tm)
    h, hb = _hgrn_mixer_layer(h, hb, p, 2, ln_mix_g[2], ln_mix_b[2], alpha, batch, tm)
    h, hb = _dense_ffn(hb, h, ffn2_w1, ffn2_w3, ffn2_w2, ln_ffn_g[2], ln_ffn_b[2], alpha, tm)
    h, hb = _fox_mixer_layer(h, hb, p, ln_mix_g[3], ln_mix_b[3], alpha, batch, tm)
    h, hb = _moe_ffn(h, hb, moe3_router, moe3_router_b, moe3_w1, moe3_w3, moe3_w2,
                     ln_ffn_g[3], ln_ffn_b[3], alpha, tm)
    return h.reshape(batch, t, d)[:, N_META:]
```

```python
import functools

import jax
import jax.numpy as jnp
from jax import lax
from jax.experimental import pallas as pl
from jax.experimental.pallas import tpu as pltpu

F32 = jnp.float32
BF16 = jnp.bfloat16

N_META = 16
LN_EPS = 1e-5
RMS_EPS = 1e-6
RWKV_HEAD = 64
HGRN_HEAD = 128
FOX_HEAD = 128
TOP_K = 2
LANES = 128
VMEM_LIMIT_BYTES = 56 * 2**20


def _cparams(*sem):
    return pltpu.CompilerParams(dimension_semantics=sem, vmem_limit_bytes=VMEM_LIMIT_BYTES)


def _divisor_tile(n, cap, mult):
    best = None
    for d in range(mult, min(n, cap) + 1, mult):
        if n % d == 0:
            best = d
    assert best is not None, (n, cap, mult)
    return best


def _layer_norm(y, g, b):
    mu = jnp.mean(y, axis=-1, keepdims=True)
    yc = y - mu
    var = jnp.mean(yc * yc, axis=-1, keepdims=True)
    return yc * lax.rsqrt(var + LN_EPS) * g + b


def _row_halves(tm):
    split = -(-tm // 32) * 16
    return [(0, split), (split, tm)] if split < tm else [(0, tm)]


def _sigmoid(x):
    return 1.0 / (1.0 + jnp.exp(-x))


def _silu(x):
    return x * _sigmoid(x)


def _mm_res_ln_kernel(z_ref, w_ref, h_ref, g_ref, b_ref, o_ref, ob_ref, *, nk, alpha):
    k = pl.program_id(1)

    @pl.when(k == 0)
    def _():
        o_ref[...] = jnp.dot(z_ref[...], w_ref[...], preferred_element_type=F32)

    @pl.when(jnp.logical_and(k > 0, k < nk - 1))
    def _():
        o_ref[...] += jnp.dot(z_ref[...], w_ref[...], preferred_element_type=F32)

    @pl.when(k == nk - 1)
    def _():
        spans = _row_halves(z_ref.shape[0])
        dots = [jnp.dot(z_ref[lo:hi, :], w_ref[...], preferred_element_type=F32) for lo, hi in spans]
        for (lo, hi), part in zip(spans, dots):
            y = _layer_norm(alpha * h_ref[lo:hi, :] + (o_ref[lo:hi, :] + part), g_ref[...], b_ref[...])
            o_ref[lo:hi, :] = y
            ob_ref[lo:hi, :] = y.astype(BF16)


WEIGHT_SLICE_ROWS = 256


def _load_weight_bf16(w_hbm, wb_ref, stage_ref, sem):
    rows = stage_ref.shape[1]
    nslice = w_hbm.shape[0] // rows

    def copy(s):
        return pltpu.make_async_copy(w_hbm.at[pl.ds(s * rows, rows), :], stage_ref.at[s % 2], sem.at[s % 2])

    copy(0).start()
    for s in range(nslice):
        if s + 1 < nslice:
            copy(s + 1).start()
        copy(s).wait()
        wb_ref[s * rows:(s + 1) * rows, :] = stage_ref[s % 2].astype(BF16)


def _mm_res_ln_resident_kernel(z_ref, w_ref, h_ref, g_ref, b_ref, o_ref, ob_ref, wb_ref, stage_ref, sem,
                               *, alpha):
    @pl.when(pl.program_id(0) == 0)
    def _():
        _load_weight_bf16(w_ref, wb_ref, stage_ref, sem)

    spans = _row_halves(z_ref.shape[0])
    dots = [jnp.dot(z_ref[lo:hi, :], wb_ref[...], preferred_element_type=F32) for lo, hi in spans]
    for (lo, hi), part in zip(spans, dots):
        y = _layer_norm(alpha * h_ref[lo:hi, :] + part, g_ref[...], b_ref[...])
        o_ref[lo:hi, :] = y
        ob_ref[lo:hi, :] = y.astype(BF16)


RESIDENT_WEIGHT_BYTES = 16 * 2**20


def _mm_res_ln_resident(z, w, h, g, b, alpha, tm):
    n, kdim = z.shape
    d = w.shape[1]
    ws = _divisor_tile(kdim, WEIGHT_SLICE_ROWS, 8)
    const = lambda m: (0, 0)
    return pl.pallas_call(
        functools.partial(_mm_res_ln_resident_kernel, alpha=alpha),
        grid=(n // tm,),
        in_specs=[
            pl.BlockSpec((tm, kdim), lambda m: (m, 0)),
            pl.BlockSpec(memory_space=pl.ANY),
            pl.BlockSpec((tm, d), lambda m: (m, 0)),
            pl.BlockSpec((1, d), const),
            pl.BlockSpec((1, d), const),
        ],
        out_specs=[
            pl.BlockSpec((tm, d), lambda m: (m, 0)),
            pl.BlockSpec((tm, d), lambda m: (m, 0)),
        ],
        out_shape=[jax.ShapeDtypeStruct((n, d), F32), jax.ShapeDtypeStruct((n, d), BF16)],
        scratch_shapes=[pltpu.VMEM((kdim, d), BF16), pltpu.VMEM((2, ws, d), F32),
                        pltpu.SemaphoreType.DMA((2,))],
        compiler_params=_cparams("arbitrary"),
        name="mm_res_ln_resident",
    )(z, w, h, g.reshape(1, d), b.reshape(1, d))


def _mm_res_ln(z, w, h, g, b, alpha, tm):
    n, kdim = z.shape
    d = w.shape[1]
    if kdim * d * 4 <= RESIDENT_WEIGHT_BYTES:
        return _mm_res_ln_resident(z, w, h, g, b, alpha, tm)
    w = w.astype(BF16)
    tk = _divisor_tile(kdim, 1408, LANES)
    nk = kdim // tk
    assert nk >= 2
    return pl.pallas_call(
        functools.partial(_mm_res_ln_kernel, nk=nk, alpha=alpha),
        grid=(n // tm, nk),
        in_specs=[
            pl.BlockSpec((tm, tk), lambda m, k: (m, k)),
            pl.BlockSpec((tk, d), lambda m, k: (k, 0)),
            pl.BlockSpec((tm, d), lambda m, k: (m, 0)),
            pl.BlockSpec((1, d), lambda m, k: (0, 0)),
            pl.BlockSpec((1, d), lambda m, k: (0, 0)),
        ],
        out_specs=[
            pl.BlockSpec((tm, d), lambda m, k: (m, 0)),
            pl.BlockSpec((tm, d), lambda m, k: (m, 0)),
        ],
        out_shape=[jax.ShapeDtypeStruct((n, d), F32), jax.ShapeDtypeStruct((n, d), BF16)],
        compiler_params=_cparams("parallel", "arbitrary"),
        name="mm_res_ln",
    )(z, w, h, g.reshape(1, d), b.reshape(1, d))


def _ffn_up_kernel(exp_ref, nact_ref, x_ref, w1_ref, w3_ref, o_ref, w1b_ref, w3b_ref):
    c = pl.program_id(1)
    prev = exp_ref[jnp.maximum(c - 1, 0)]
    new_weights = jnp.logical_or(c == 0, exp_ref[c] != prev)

    @pl.when(new_weights)
    def _():
        w1b_ref[...] = w1_ref[...].astype(BF16)
        w3b_ref[...] = w3_ref[...].astype(BF16)

    @pl.when(c < nact_ref[0])
    def _():
        x = x_ref[...]
        a = jnp.dot(x, w1b_ref[...], preferred_element_type=F32)
        bb = jnp.dot(x, w3b_ref[...], preferred_element_type=F32)
        o_ref[...] = (_silu(a) * bb).astype(BF16)

    @pl.when(c >= nact_ref[0])
    def _():
        o_ref[...] = jnp.zeros_like(o_ref)


def _ffn_up(x, w1, w3, chunk_expert, n_active, rows):
    p, d = x.shape
    f = w1.shape[2]
    tf = _divisor_tile(f, 512, LANES)
    grid_spec = pltpu.PrefetchScalarGridSpec(
        num_scalar_prefetch=2,
        grid=(f // tf, p // rows),
        in_specs=[
            pl.BlockSpec((rows, d), lambda j, c, e, na: (c, 0)),
            pl.BlockSpec((None, d, tf), lambda j, c, e, na: (e[c], 0, j)),
            pl.BlockSpec((None, d, tf), lambda j, c, e, na: (e[c], 0, j)),
        ],
        out_specs=pl.BlockSpec((rows, tf), lambda j, c, e, na: (c, j)),
        scratch_shapes=[pltpu.VMEM((d, tf), BF16), pltpu.VMEM((d, tf), BF16)],
    )
    return pl.pallas_call(
        _ffn_up_kernel,
        grid_spec=grid_spec,
        out_shape=jax.ShapeDtypeStruct((p, f), BF16),
        compiler_params=_cparams("arbitrary", "arbitrary"),
        name="ffn_up",
    )(chunk_expert, n_active, x, w1, w3)


def _dense_ffn(hb, h, w1, w3, w2, g, b, alpha, tm):
    n = hb.shape[0]
    nchunks = n // tm
    hmid = _ffn_up(hb, w1[None], w3[None], jnp.zeros((nchunks,), jnp.int32),
                   jnp.full((1,), nchunks, jnp.int32), tm)
    return _mm_res_ln(hmid, w2, h, g, b, alpha, tm)


def _conv_proj_kernel(x_ref, wb_ref, wc_ref, wh_ref, cw_ref, cb_ref, o_ref,
                      wbb_ref, wcb_ref, whb_ref, carry_ref):
    bi = pl.program_id(1)
    ti = pl.program_id(2)

    @pl.when(jnp.logical_and(bi == 0, ti == 0))
    def _():
        wbb_ref[...] = wb_ref[...].astype(BF16)
        wcb_ref[...] = wc_ref[...].astype(BF16)
        whb_ref[...] = wh_ref[...].astype(BF16)

    @pl.when(ti == 0)
    def _():
        carry_ref[...] = jnp.zeros_like(carry_ref)

    x = x_ref[...]
    gate_b = jnp.dot(x, wbb_ref[...], preferred_element_type=F32)
    gate_c = jnp.dot(x, wcb_ref[...], preferred_element_type=F32)
    hh = jnp.dot(x, whb_ref[...], preferred_element_type=F32)
    u = gate_c * hh
    tm = u.shape[0]
    prev1 = carry_ref[7:8, :]
    prev2 = carry_ref[6:7, :]
    row = lax.broadcasted_iota(jnp.int32, (tm, 1), 0)
    r1 = jnp.where(row == 0, prev1, pltpu.roll(u, 1, axis=0))
    r2 = jnp.where(row == 0, prev2, jnp.where(row == 1, prev1, pltpu.roll(u, 2, axis=0)))
    v = cw_ref[0:1, :] * r2 + cw_ref[1:2, :] * r1 + cw_ref[2:3, :] * u + cb_ref[...]
    carry_ref[...] = u[tm - 8:, :]
    o_ref[...] = (gate_b * v).astype(BF16)


def _conv_proj(hb, w_in, conv_w, conv_b, batch, tm):
    n, d = hb.shape
    tn = _divisor_tile(d, 512, LANES)
    nd = d // tn
    tpb = n // batch // tm
    return pl.pallas_call(
        _conv_proj_kernel,
        grid=(nd, batch, tpb),
        in_specs=[
            pl.BlockSpec((tm, d), lambda j, bi, ti: (bi * tpb + ti, 0)),
            pl.BlockSpec((d, tn), lambda j, bi, ti: (0, j)),
            pl.BlockSpec((d, tn), lambda j, bi, ti: (0, nd + j)),
            pl.BlockSpec((d, tn), lambda j, bi, ti: (0, 2 * nd + j)),
            pl.BlockSpec((3, tn), lambda j, bi, ti: (0, j)),
            pl.BlockSpec((1, tn), lambda j, bi, ti: (0, j)),
        ],
        out_specs=pl.BlockSpec((tm, tn), lambda j, bi, ti: (bi * tpb + ti, j)),
        out_shape=jax.ShapeDtypeStruct((n, d), BF16),
        scratch_shapes=[pltpu.VMEM((d, tn), BF16)] * 3 + [pltpu.VMEM((8, tn), F32)],
        compiler_params=_cparams("arbitrary", "arbitrary", "arbitrary"),
        name="conv_proj",
    )(hb, w_in, w_in, w_in, conv_w, conv_b.reshape(1, d))


def _conv_mixer_layer(h, hb, p, g, b, alpha, batch, tm):
    z = _conv_proj(hb, p["conv_w_in"], p["conv_w"], p["conv_b"], batch, tm)
    return _mm_res_ln(z, p["conv_w_out"], h, g, b, alpha, tm)


MOE_ROWS = 128


def _split_bf16(a):
    hi = a.astype(BF16)
    lo = (a - hi.astype(F32)).astype(BF16)
    return hi, lo


def _dot_f32(a, b):
    ah, al = _split_bf16(a)
    bh, bl = _split_bf16(b)
    return (jnp.dot(ah, bh, preferred_element_type=F32)
            + (jnp.dot(ah, bl, preferred_element_type=F32)
               + jnp.dot(al, bh, preferred_element_type=F32)))


def _router_kernel(h_ref, w_ref, b_ref, info_ref, cnt_ref, carry_ref):
    i = pl.program_id(0)

    @pl.when(i == 0)
    def _():
        carry_ref[...] = jnp.zeros_like(carry_ref)

    logits = _dot_f32(h_ref[...], w_ref[...]) + b_ref[...]
    tm, ne = logits.shape
    lane = lax.broadcasted_iota(jnp.int32, (tm, ne), 1)
    m1 = jnp.max(logits, axis=-1, keepdims=True)
    i1 = jnp.min(jnp.where(logits == m1, lane, ne), axis=-1, keepdims=True)
    mask1 = lane == i1
    rest = jnp.where(mask1, -jnp.inf, logits)
    m2 = jnp.max(rest, axis=-1, keepdims=True)
    i2 = jnp.min(jnp.where(rest == m2, lane, ne), axis=-1, keepdims=True)
    mask2 = lane == i2
    dd = jnp.exp(m2 - m1)
    g1 = 1.0 / (1.0 + dd)
    g2 = dd / (1.0 + dd)
    sel = jnp.where(jnp.logical_or(mask1, mask2), 1.0, 0.0)
    r_i = lax.broadcasted_iota(jnp.int32, (tm, tm), 0)
    c_i = lax.broadcasted_iota(jnp.int32, (tm, tm), 1)
    tril = jnp.where(c_i < r_i, 1.0, 0.0).astype(BF16)
    rank = jnp.dot(tril, sel.astype(BF16), preferred_element_type=F32) + carry_ref[...]
    r1 = jnp.sum(jnp.where(mask1, rank, 0.0), axis=-1, keepdims=True)
    r2 = jnp.sum(jnp.where(mask2, rank, 0.0), axis=-1, keepdims=True)
    info = jnp.where(lane == 0, i1.astype(F32),
           jnp.where(lane == 1, i2.astype(F32),
           jnp.where(lane == 2, g1,
           jnp.where(lane == 3, g2,
           jnp.where(lane == 4, r1,
           jnp.where(lane == 5, r2, 0.0))))))
    info_ref[...] = info
    total = carry_ref[...] + jnp.sum(sel, axis=0, keepdims=True)
    carry_ref[...] = total
    cnt_ref[...] = total


def _router(h, w, b, tm):
    n, d = h.shape
    ne = w.shape[1]
    assert ne >= 6
    return pl.pallas_call(
        _router_kernel,
        grid=(n // tm,),
        in_specs=[
            pl.BlockSpec((tm, d), lambda i: (i, 0)),
            pl.BlockSpec((d, ne), lambda i: (0, 0)),
            pl.BlockSpec((1, ne), lambda i: (0, 0)),
        ],
        out_specs=[
            pl.BlockSpec((tm, ne), lambda i: (i, 0)),
            pl.BlockSpec((1, ne), lambda i: (0, 0)),
        ],
        out_shape=[jax.ShapeDtypeStruct((n, ne), F32), jax.ShapeDtypeStruct((1, ne), F32)],
        scratch_shapes=[pltpu.VMEM((1, ne), F32)],
        compiler_params=_cparams("arbitrary"),
        name="moe_router",
    )(h, w, b.reshape(1, ne))


def _row_copy(src_hbm, row, dst_vmem, r, sem):
    return pltpu.make_async_copy(src_hbm.at[pl.ds(row, 1), :], dst_vmem.at[pl.ds(r, 1), :], sem)


def _rows_wait(src_hbm, dst_vmem, sem):
    pltpu.make_async_copy(src_hbm.at[pl.ds(0, dst_vmem.shape[0]), :], dst_vmem, sem).wait()


GATHER_SLOTS = 3
GATHER_UNITS = 2


def _moe_gather_kernel(src_ref, nact_ref, h_ref, o_ref, buf_ref, sem):
    c = pl.program_id(0)
    rows = buf_ref.shape[1]

    nact = nact_ref[0]
    slot = c % GATHER_SLOTS
    slot1 = (c + 1) % GATHER_SLOTS
    slot2 = (c + 2) % GATHER_SLOTS

    def start(chunk, to_slot, r, priority=0):
        _row_copy(h_ref, src_ref[chunk * rows + r], buf_ref.at[to_slot], r, sem.at[to_slot]).start(priority)

    @pl.when(jnp.logical_and(c == 0, nact > 0))
    def _():
        lax.fori_loop(0, rows, lambda r, carry: (start(0, slot, r), carry)[1], 0, unroll=8)

    @pl.when(jnp.logical_and(c == 0, nact > 1))
    def _():
        lax.fori_loop(0, rows, lambda r, carry: (start(1, slot1, r), carry)[1], 0, unroll=8)

    @pl.when(c < nact)
    def _():
        _rows_wait(h_ref, buf_ref.at[slot], sem.at[slot])
        nxt = jnp.minimum(c + 2, nact - 1)
        for r in range(rows):
            start(nxt, slot2, r, priority=r % 2)
        o_ref[...] = buf_ref[slot].astype(BF16)

        @pl.when(c == nact - 1)
        def _():
            _rows_wait(h_ref, buf_ref.at[slot2], sem.at[slot2])

            @pl.when(nact > 1)
            def _():
                _rows_wait(h_ref, buf_ref.at[slot1], sem.at[slot1])

    @pl.when(c >= nact)
    def _():
        o_ref[...] = jnp.zeros_like(o_ref)


def _moe_gather(h, src, n_active, p, rows):
    n, d = h.shape
    grid_spec = pltpu.PrefetchScalarGridSpec(
        num_scalar_prefetch=2,
        grid=(p // rows,),
        in_specs=[pl.BlockSpec(memory_space=pl.ANY)],
        out_specs=pl.BlockSpec((rows, d), lambda c, s, na: (c, 0)),
        scratch_shapes=[pltpu.VMEM((GATHER_SLOTS, rows, d), F32), pltpu.SemaphoreType.DMA((GATHER_SLOTS,))],
    )
    return pl.pallas_call(
        _moe_gather_kernel,
        grid_spec=grid_spec,
        out_shape=jax.ShapeDtypeStruct((p, d), BF16),
        compiler_params=_cparams("arbitrary"),
        name="moe_gather",
    )(src, n_active, h)


def _moe_combine_kernel(p1_ref, p2_ref, y_ref, h_ref, info_ref, g_ref, b_ref, o_ref, ob_ref,
                        buf1_ref, buf2_ref, sem, *, alpha):
    i = pl.program_id(0)
    tm = buf1_ref.shape[1]
    ntile = pl.num_programs(0)
    last = ntile - 1
    slot = i % GATHER_SLOTS
    slot1 = (i + 1) % GATHER_SLOTS
    slot2 = (i + 2) % GATHER_SLOTS

    def start(tile, to_slot, r, split=False):
        _row_copy(y_ref, p1_ref[tile * tm + r], buf1_ref.at[to_slot], r, sem.at[to_slot]).start(0)
        _row_copy(y_ref, p2_ref[tile * tm + r], buf2_ref.at[to_slot], r, sem.at[to_slot]).start(1 if split else 0)

    def wait(from_slot):
        _rows_wait(y_ref, buf1_ref.at[from_slot], sem.at[from_slot])
        _rows_wait(y_ref, buf2_ref.at[from_slot], sem.at[from_slot])

    @pl.when(i == 0)
    def _():
        lax.fori_loop(0, tm, lambda r, c: (start(0, slot, r), c)[1], 0, unroll=8)

    @pl.when(jnp.logical_and(i == 0, ntile > 1))
    def _():
        lax.fori_loop(0, tm, lambda r, c: (start(1, slot1, r), c)[1], 0, unroll=8)

    wait(slot)
    nxt = jnp.minimum(i + 2, last)
    for r in range(tm):
        start(nxt, slot2, r, split=True)
    info = info_ref[...]
    y = alpha * h_ref[...] + (info[:, 2:3] * buf1_ref[slot] + info[:, 3:4] * buf2_ref[slot])
    y = _layer_norm(y, g_ref[...], b_ref[...])
    o_ref[...] = y
    ob_ref[...] = y.astype(BF16)

    @pl.when(i == last)
    def _():
        wait(slot2)

        @pl.when(ntile > 1)
        def _():
            wait(slot1)


def _moe_combine(y, h, info, p1, p2, g, b, alpha, tm):
    n, d = h.shape
    ne = info.shape[1]
    grid_spec = pltpu.PrefetchScalarGridSpec(
        num_scalar_prefetch=2,
        grid=(n // tm,),
        in_specs=[
            pl.BlockSpec(memory_space=pl.ANY),
            pl.BlockSpec((tm, d), lambda i, a, c: (i, 0)),
            pl.BlockSpec((tm, ne), lambda i, a, c: (i, 0)),
            pl.BlockSpec((1, d), lambda i, a, c: (0, 0)),
            pl.BlockSpec((1, d), lambda i, a, c: (0, 0)),
        ],
        out_specs=[
            pl.BlockSpec((tm, d), lambda i, a, c: (i, 0)),
            pl.BlockSpec((tm, d), lambda i, a, c: (i, 0)),
        ],
        scratch_shapes=[pltpu.VMEM((GATHER_SLOTS, tm, d), F32), pltpu.VMEM((GATHER_SLOTS, tm, d), F32),
                        pltpu.SemaphoreType.DMA((GATHER_SLOTS,))],
    )
    return pl.pallas_call(
        functools.partial(_moe_combine_kernel, alpha=alpha),
        grid_spec=grid_spec,
        out_shape=[jax.ShapeDtypeStruct((n, d), F32), jax.ShapeDtypeStruct((n, d), BF16)],
        compiler_params=_cparams("arbitrary"),
        name="moe_combine",
    )(p1, p2, y, h, info, g.reshape(1, d), b.reshape(1, d))


MOE_PASS_BLOCKS = 18
MOE_F_TILE = 256
MOE_BLOCK_UNITS = 8


def _expert_ffn_kernel(pe_ref, ps_ref, pn_ref, nu_ref, xs_ref, w1_ref, w3_ref, w2_ref, y_ref,
                       x_buf, acc_ref, w1b_ref, w3b_ref, w2b_ref, sem):
    p = pl.program_id(0)
    f = pl.program_id(1)
    nf = pl.num_programs(1)
    nb = pn_ref[p]
    sb = MOE_ROWS
    start = pl.multiple_of(ps_ref[p], MOE_ROWS)
    nblock = nb // MOE_BLOCK_UNITS

    @pl.when(nb > 0)
    def _():
        @pl.when(f == 0)
        def _():
            cp = pltpu.make_async_copy(xs_ref.at[pl.ds(start, x_buf.shape[0]), :], x_buf, sem.at[0])
            cp.start()
            cp.wait()

        w1b_ref[...] = w1_ref[...].astype(BF16)
        w3b_ref[...] = w3_ref[...].astype(BF16)

        @pl.when(nblock == 0)
        def _():
            w2b_ref[...] = w2_ref[...].astype(BF16)

        def up(unit, nunit):
            x = x_buf[pl.ds(pl.multiple_of(unit * sb, sb), nunit * sb), :]
            a = jnp.dot(x, w1b_ref[...], preferred_element_type=F32)
            bb = jnp.dot(x, w3b_ref[...], preferred_element_type=F32)
            return (_silu(a) * bb).astype(BF16)

        def out_copy(i):
            rows = pl.ds(pl.multiple_of(i * sb, sb), sb)
            dst = pl.ds(pl.multiple_of(start + i * sb, sb), sb)
            return pltpu.make_async_copy(acc_ref.at[rows, :], y_ref.at[dst, :], sem.at[1])

        def down(unit, nunit, hmid, first):
            rows = pl.ds(pl.multiple_of(unit * sb, sb), nunit * sb)
            part = jnp.dot(hmid, w2b_ref[...], preferred_element_type=F32)
            if first:
                acc_ref[rows, :] = part
            else:
                acc_ref[rows, :] += part

            @pl.when(f == nf - 1)
            def _():
                for u in range(nunit):
                    out_copy(unit + u).start()

        def sweep(first):
            bu = MOE_BLOCK_UNITS

            @pl.when(nblock > 0)
            def _():
                def body(i, hprev):
                    down(bu * (i - 1), bu, hprev, first)
                    return up(bu * i, bu)

                h0 = up(0, bu)
                w2b_ref[...] = w2_ref[...].astype(BF16)
                down(bu * (nblock - 1), bu, lax.fori_loop(1, nblock, body, h0), first)

            done = nblock * bu
            size = bu // 2
            while size >= 1:
                @pl.when((nb - done) & size != 0)
                def _(done=done, size=size):
                    down(done, size, up(done, size), first)

                done = done + ((nb - done) & size)
                size //= 2

        @pl.when(f == 0)
        def _():
            sweep(True)

        @pl.when(f > 0)
        def _():
            sweep(False)

        @pl.when(f == nf - 1)
        def _():
            lax.fori_loop(0, nb, lambda i, c: (out_copy(i).wait(), c)[1], 0)

    @pl.when(jnp.logical_and(p == pl.num_programs(0) - 1, f == nf - 1))
    def _():
        acc_ref[0:sb, :] = jnp.zeros((sb, acc_ref.shape[1]), F32)

        def zero_copy(i):
            dst = pl.ds(pl.multiple_of(i * sb, sb), sb)
            return pltpu.make_async_copy(acc_ref.at[0:sb, :], y_ref.at[dst, :], sem.at[1])

        n_blocks = y_ref.shape[0] // sb
        lax.fori_loop(nu_ref[1], n_blocks, lambda i, c: (zero_copy(i).start(), c)[1], 0)
        lax.fori_loop(nu_ref[1], n_blocks, lambda i, c: (zero_copy(i).wait(), c)[1], 0)


def _expert_ffn(xs, w1, w3, w2, pass_expert, pass_start, pass_nb, n_used, p_rows):
    d = xs.shape[1]
    f = w1.shape[2]
    tf = _divisor_tile(f, MOE_F_TILE, LANES)
    nf = f // tf
    npass = pass_expert.shape[0]
    r = MOE_PASS_BLOCKS * MOE_ROWS

    def fidx(p, j, nu):
        return jnp.where(p < nu[0], j, nf - 1)

    grid_spec = pltpu.PrefetchScalarGridSpec(
        num_scalar_prefetch=4,
        grid=(npass, nf),
        in_specs=[
            pl.BlockSpec(memory_space=pl.ANY),
            pl.BlockSpec((None, d, tf), lambda p, j, pe, ps, pn, nu: (pe[p], 0, fidx(p, j, nu))),
            pl.BlockSpec((None, d, tf), lambda p, j, pe, ps, pn, nu: (pe[p], 0, fidx(p, j, nu))),
            pl.BlockSpec((None, tf, d), lambda p, j, pe, ps, pn, nu: (pe[p], fidx(p, j, nu), 0)),
        ],
        out_specs=pl.BlockSpec(memory_space=pl.ANY),
        scratch_shapes=[
            pltpu.VMEM((r, d), BF16), pltpu.VMEM((r, d), F32),
            pltpu.VMEM((d, tf), BF16), pltpu.VMEM((d, tf), BF16), pltpu.VMEM((tf, d), BF16),
            pltpu.SemaphoreType.DMA((2,)),
        ],
    )
    return pl.pallas_call(
        _expert_ffn_kernel,
        grid_spec=grid_spec,
        out_shape=jax.ShapeDtypeStruct((p_rows, d), F32),
        compiler_params=_cparams("arbitrary", "arbitrary"),
        name="expert_ffn",
    )(pass_expert, pass_start, pass_nb, n_used, xs, w1, w3, w2)


def _moe_ffn(h, hb, router_w, router_b, w1, w3, w2, g, b, alpha, tm):
    n, d = h.shape
    ne = router_w.shape[1]
    rows = MOE_ROWS
    info, counts = _router(h, router_w, router_b, tm)
    counts = counts[0].astype(jnp.int32)
    nblk_e = (counts + rows - 1) // rows
    blk_end = jnp.cumsum(nblk_e)
    starts = (blk_end - nblk_e) * rows
    n_blocks = (n * TOP_K + ne * (rows - 1)) // rows
    p = n_blocks * rows
    n_active = blk_end[-1:].astype(jnp.int32)
    npass_e = (nblk_e + MOE_PASS_BLOCKS - 1) // MOE_PASS_BLOCKS
    pass_end = jnp.cumsum(npass_e)
    max_pass = n_blocks // MOE_PASS_BLOCKS + ne
    pidx = jnp.arange(max_pass, dtype=jnp.int32)
    n_used = jnp.stack([pass_end[-1], blk_end[-1]]).astype(jnp.int32)
    last_expert = jnp.sum(pass_end < pass_end[-1]).astype(jnp.int32)
    pass_expert = jnp.minimum(jnp.sum(pidx[:, None] >= pass_end[None, :], axis=1), last_expert).astype(jnp.int32)
    local = pidx - (pass_end - npass_e)[pass_expert]
    pass_start = (starts[pass_expert] + local * (MOE_PASS_BLOCKS * rows)).astype(jnp.int32)
    pass_nb = jnp.where(pidx < n_used[0],
                        jnp.clip(nblk_e[pass_expert] - local * MOE_PASS_BLOCKS, 0, MOE_PASS_BLOCKS),
                        0).astype(jnp.int32)
    pass_start = jnp.where(pass_nb > 0, pass_start, 0).astype(jnp.int32)
    i1 = info[:, 0].astype(jnp.int32)
    i2 = info[:, 1].astype(jnp.int32)
    p1 = starts[i1] + info[:, 4].astype(jnp.int32)
    p2 = starts[i2] + info[:, 5].astype(jnp.int32)
    tok = jnp.arange(n, dtype=jnp.int32)
    grows = GATHER_UNITS * rows
    p_in = -(-(p + MOE_PASS_BLOCKS * rows) // grows) * grows
    src = jnp.zeros((p_in,), jnp.int32).at[jnp.concatenate([p1, p2])].set(jnp.concatenate([tok, tok]))
    xs = _moe_gather(h, src, (n_active + GATHER_UNITS - 1) // GATHER_UNITS, p_in, grows)
    y = _expert_ffn(xs, w1, w3, w2, pass_expert, pass_start, pass_nb, n_used, p)
    tmc = _divisor_tile(n, 384, 16)
    return _moe_combine(y, h, info, p1, p2, g, b, alpha, tmc)


def _proj_kernel(*refs, epilogue, n_extra, n_out):
    x_ref, w_ref = refs[0], refs[1]
    extra = refs[2:2 + n_extra]
    outs = refs[2 + n_extra:2 + n_extra + n_out]
    wb_ref = refs[2 + n_extra + n_out]

    @pl.when(pl.program_id(1) == 0)
    def _():
        wb_ref[...] = w_ref[...].astype(BF16)

    y = jnp.dot(x_ref[...], wb_ref[...], preferred_element_type=F32)
    res = epilogue(y, *[e[...] for e in extra])
    for o_ref, r in zip(outs, res):
        o_ref[...] = r.astype(o_ref.dtype)


def _proj(xb, w, col0, ncols, epilogue, extras, out_dtypes, tm, name, tn_cap=1024):
    n, kdim = xb.shape
    tn = _divisor_tile(ncols, tn_cap, LANES)
    assert col0 % tn == 0
    off = col0 // tn
    outs = pl.pallas_call(
        functools.partial(_proj_kernel, epilogue=epilogue, n_extra=len(extras), n_out=len(out_dtypes)),
        grid=(ncols // tn, n // tm),
        in_specs=[
            pl.BlockSpec((tm, kdim), lambda j, m: (m, 0)),
            pl.BlockSpec((kdim, tn), lambda j, m: (0, off + j)),
        ] + [pl.BlockSpec((1, tn), lambda j, m: (0, j))] * len(extras),
        out_specs=[pl.BlockSpec((tm, tn), lambda j, m: (m, j))] * len(out_dtypes),
        out_shape=[jax.ShapeDtypeStruct((n, ncols), dt) for dt in out_dtypes],
        scratch_shapes=[pltpu.VMEM((kdim, tn), BF16)],
        compiler_params=_cparams("arbitrary", "arbitrary"),
        name=name,
    )(xb, w, *[e.reshape(1, ncols) for e in extras])
    return outs


def _tril_bf16(c, inclusive):
    r_i = lax.broadcasted_iota(jnp.int32, (c, c), 0)
    c_i = lax.broadcasted_iota(jnp.int32, (c, c), 1)
    keep = (c_i <= r_i) if inclusive else (c_i < r_i)
    return jnp.where(keep, 1.0, 0.0).astype(BF16)


def _cumsum_rows(x, tril):
    hi = x.astype(BF16)
    r1 = x - hi.astype(F32)
    mid = r1.astype(BF16)
    lo = (r1 - mid.astype(F32)).astype(BF16)
    return (jnp.dot(tril, hi, preferred_element_type=F32)
            + (jnp.dot(tril, mid, preferred_element_type=F32)
               + jnp.dot(tril, lo, preferred_element_type=F32)))


def _dot_nt(a, b):
    return lax.dot_general(a.astype(BF16), b.astype(BF16), (((1,), (1,)), ((), ())),
                           preferred_element_type=F32)


def _dot_tn(a, b):
    return lax.dot_general(a.astype(BF16), b.astype(BF16), (((0,), (0,)), ((), ())),
                           preferred_element_type=F32)


def _dot_nn(a, b):
    return jnp.dot(a.astype(BF16), b.astype(BF16), preferred_element_type=F32)


def _chunk_len(t):
    return _divisor_tile(t, 64, 16)


def _round_robin(gens):
    done = [None] * len(gens)
    while any(d is None for d in done):
        for j, gen in enumerate(gens):
            if done[j] is None:
                done[j] = next(gen)
    return done


HGRN_SUB = 16


def _hgrn_scan_kernel(q_ref, lf_ref, v_ref, gs_ref, ng_ref, o_ref, st_ref, *, chunk):
    t, w = q_ref.shape
    nhead = w // HGRN_HEAD
    nsub = chunk // HGRN_SUB
    st_ref[...] = jnp.zeros_like(st_ref)
    tril = _tril_bf16(chunk, True)
    row16 = lax.broadcasted_iota(jnp.int32, (HGRN_SUB, 1), 0)

    def head_chunk(q, lf, v, st):
        k = 1.0 - jnp.exp(lf)
        cum = _cumsum_rows(lf, tril)
        yield None
        o_inter = _dot_nt(q * jnp.exp(cum), st)
        vb = v.astype(BF16)
        cl = cum[chunk - 1:chunk]
        kd = k * jnp.exp(cl - cum)
        st_new = st * jnp.exp(cl) + _dot_tn(v, kd)
        yield None
        outs = []
        for i in range(nsub):
            lo, hi = i * HGRN_SUB, (i + 1) * HGRN_SUB
            qi, ki, vi, cumi = q[lo:hi], k[lo:hi], v[lo:hi], cum[lo:hi]
            oi = o_inter[lo:hi]
            if i > 0:
                ci = cum[lo:lo + 1]
                qt = qi * jnp.exp(cumi - ci)
                kt = k[0:lo] * jnp.exp(ci - cum[0:lo])
                oi = oi + jnp.dot(_dot_nt(qt, kt).astype(BF16), vb[0:lo], preferred_element_type=F32)
            for s in range(HGRN_SUB):
                dec = jnp.exp(jnp.minimum(cumi - cumi[s:s + 1], 0.0))
                col = jnp.sum(qi * dec * ki[s:s + 1], axis=-1, keepdims=True)
                col = jnp.where(row16 >= s, col, 0.0)
                oi = oi + col * vi[s:s + 1]
                if s % 4 == 3:
                    yield None
            outs.append(oi)
        o = jnp.concatenate(outs, axis=0)
        o = o * lax.rsqrt(jnp.mean(o * o, axis=-1, keepdims=True) + RMS_EPS)
        yield o, st_new

    def body(c, carry):
        rows = pl.ds(pl.multiple_of(c * chunk, 16), chunk)
        q = q_ref[rows, :].astype(F32)
        lf = lf_ref[rows, :]
        v = v_ref[rows, :].astype(F32)
        heads = []
        for hh in range(nhead):
            cols = slice(hh * HGRN_HEAD, (hh + 1) * HGRN_HEAD)
            heads.append(head_chunk(q[:, cols], lf[:, cols], v[:, cols], st_ref[hh]))
        outs = []
        for hh, (o, st_new) in enumerate(_round_robin(heads)):
            st_ref[hh] = st_new
            outs.append(o)
        o = jnp.concatenate(outs, axis=1)
        o_ref[rows, :] = (o * ng_ref[...] * gs_ref[rows, :].astype(F32)).astype(BF16)
        return carry

    lax.fori_loop(0, t // chunk, body, 0)


HGRN_GROUP = 8


def _hgrn_scan(q, lf, v, gs, norm_g, batch):
    n, d = q.shape
    t = n // batch
    chunk = _chunk_len(t)
    w = min(d, HGRN_GROUP * HGRN_HEAD)
    blk = pl.BlockSpec((t, w), lambda b, j: (b, j))
    return pl.pallas_call(
        functools.partial(_hgrn_scan_kernel, chunk=chunk),
        grid=(batch, d // w),
        in_specs=[blk, blk, blk, blk, pl.BlockSpec((1, w), lambda b, j: (0, j))],
        out_specs=blk,
        out_shape=jax.ShapeDtypeStruct((n, d), BF16),
        scratch_shapes=[pltpu.VMEM((w // HGRN_HEAD, HGRN_HEAD, HGRN_HEAD), F32)],
        compiler_params=_cparams("parallel", "parallel"),
        name="hgrn_scan",
    )(q, lf, v, gs, norm_g.reshape(1, d))


def _hgrn_mixer_layer(h, hb, p, layer_idx, g, b, alpha, batch, tm):
    d = h.shape[1]
    w_in = p["hgrn_w_in"]
    lb = jnp.cumsum(jax.nn.softmax(p["hgrn_lb"].astype(F32), axis=0), axis=0)
    lb = lb[layer_idx] - lb[0]
    (q,) = _proj(hb, w_in, 0, d, lambda y: (_silu(y),), [], [BF16], tm, "hgrn_proj_q")
    (lf,) = _proj(hb, w_in, d, d, lambda y, lbv: (jnp.log(lbv + (1.0 - lbv) * _sigmoid(y)),),
                  [lb], [F32], tm, "hgrn_proj_f")
    (v,) = _proj(hb, w_in, 2 * d, d, lambda y: (y,), [], [BF16], tm, "hgrn_proj_i")
    (gs,) = _proj(hb, w_in, 3 * d, d, lambda y: (_silu(y),), [], [BF16], tm, "hgrn_proj_g")
    z = _hgrn_scan(q, lf, v, gs, p["hgrn_norm_g"], batch)
    return _mm_res_ln(z, p["hgrn_w_out"], h, g, b, alpha, tm)


def _fox_gate_kernel(h_ref, w_ref, bf_ref, c_ref, carry_ref):
    @pl.when(pl.program_id(1) == 0)
    def _():
        carry_ref[...] = jnp.zeros_like(carry_ref)

    x = _dot_f32(h_ref[...], w_ref[...]) + bf_ref[...]
    log_f = jnp.minimum(x, 0.0) - jnp.log(1.0 + jnp.exp(-jnp.abs(x)))
    tm = x.shape[0]
    c = _cumsum_rows(log_f, _tril_bf16(tm, True)) + carry_ref[...]
    c_ref[...] = c
    carry_ref[...] = c[tm - 1:tm, :]


def _fox_gate(h, w_f, b_f, batch, tm):
    n, d = h.shape
    nh = w_f.shape[1]
    tpb = n // batch // tm
    return pl.pallas_call(
        _fox_gate_kernel,
        grid=(batch, tpb),
        in_specs=[
            pl.BlockSpec((tm, d), lambda b, t: (b * tpb + t, 0)),
            pl.BlockSpec((d, nh), lambda b, t: (0, 0)),
            pl.BlockSpec((1, nh), lambda b, t: (0, 0)),
        ],
        out_specs=pl.BlockSpec((tm, nh), lambda b, t: (b * tpb + t, 0)),
        out_shape=jax.ShapeDtypeStruct((n, nh), F32),
        scratch_shapes=[pltpu.VMEM((1, nh), F32)],
        compiler_params=_cparams("arbitrary", "arbitrary"),
        name="fox_gate",
    )(h, w_f, b_f.reshape(1, nh))


def _fox_attn_kernel(q_ref, k_ref, v_ref, sg_ref, c_ref, ct_ref, o_ref, *, tq):
    hd = pl.program_id(1)
    t = q_ref.shape[0]
    nh = c_ref.shape[1]
    lane = lax.broadcasted_iota(jnp.int32, (t, nh), 1)
    c_col = jnp.sum(jnp.where(lane == hd, c_ref[...], 0.0), axis=-1, keepdims=True)
    c_row = ct_ref[pl.ds(hd, 1), :]
    def query_tile(i):
        lo, hi = i * tq, (i + 1) * tq
        s = lax.dot_general(q_ref[lo:hi, :], k_ref[0:hi, :], (((1,), (1,)), ((), ())),
                            preferred_element_type=F32)
        yield None
        s = s + c_col[lo:hi] - c_row[:, 0:hi]
        r_i = lax.broadcasted_iota(jnp.int32, (tq, hi), 0) + lo
        c_i = lax.broadcasted_iota(jnp.int32, (tq, hi), 1)
        s = jnp.where(c_i <= r_i, s, -jnp.inf)
        m = jnp.max(s, axis=-1, keepdims=True)
        p = jnp.exp(s - m)
        l = jnp.sum(p, axis=-1, keepdims=True)
        yield None
        o = jnp.dot(p.astype(BF16), v_ref[0:hi, :], preferred_element_type=F32) / l
        o_ref[lo:hi, :] = (o * sg_ref[lo:hi, :]).astype(BF16)
        yield True

    _round_robin([query_tile(i) for i in range(t // tq)])


def _fox_attn(q, k, v, sg, c, ct, batch):
    n, d = q.shape
    t = n // batch
    nh = d // FOX_HEAD
    tq = _divisor_tile(t, 768, 16)
    blk = pl.BlockSpec((t, FOX_HEAD), lambda b, h: (b, h))
    return pl.pallas_call(
        functools.partial(_fox_attn_kernel, tq=tq),
        grid=(batch, nh),
        in_specs=[blk, blk, blk, blk,
                  pl.BlockSpec((t, nh), lambda b, h: (b, 0)),
                  pl.BlockSpec((None, nh, t), lambda b, h: (b, 0, 0))],
        out_specs=blk,
        out_shape=jax.ShapeDtypeStruct((n, d), BF16),
        compiler_params=_cparams("parallel", "parallel"),
        name="fox_attn",
    )(q, k, v, sg, c, ct)


def _head_rms_epilogue(scale):
    def epi(y, gain):
        outs = []
        for j in range(y.shape[1] // FOX_HEAD):
            yj = y[:, j * FOX_HEAD:(j + 1) * FOX_HEAD]
            yj = yj * lax.rsqrt(jnp.mean(yj * yj, axis=-1, keepdims=True) + RMS_EPS)
            outs.append(yj * gain[:, j * FOX_HEAD:(j + 1) * FOX_HEAD] * scale)
        return (jnp.concatenate(outs, axis=1),)
    return epi


def _fox_mixer_layer(h, hb, p, g, b, alpha, batch, tm):
    n, d = h.shape
    nh = d // FOX_HEAD
    w_in = p["fox_w_in"]
    qg = jnp.tile(p["fox_q_norm_g"], nh)
    kg = jnp.tile(p["fox_k_norm_g"], nh)
    (q,) = _proj(hb, w_in, 0, d, _head_rms_epilogue(FOX_HEAD ** -0.5), [qg], [BF16], tm, "fox_proj_q")
    (k,) = _proj(hb, w_in, d, d, _head_rms_epilogue(1.0), [kg], [BF16], tm, "fox_proj_k")
    (v,) = _proj(hb, w_in, 2 * d, d, lambda y: (y,), [], [BF16], tm, "fox_proj_v")
    (sg,) = _proj(hb, w_in, 3 * d, d, lambda y: (_sigmoid(y),), [], [F32], tm, "fox_proj_g")
    c = _fox_gate(h, w_in[:, 4 * d:], p["fox_b_f"], batch, tm)
    ct = c.reshape(batch, n // batch, nh).transpose(0, 2, 1)
    z = _fox_attn(q, k, v, sg, c, ct, batch)
    return _mm_res_ln(z, p["fox_w_out"], h, g, b, alpha, tm)


def _rwkv_mix_kernel(h_ref, mu_ref, *refs):
    outs, carry_ref = refs[:-1], refs[-1]

    @pl.when(pl.program_id(1) == 0)
    def _():
        carry_ref[...] = jnp.zeros_like(carry_ref)

    x = h_ref[...]
    tm = x.shape[0]
    row = lax.broadcasted_iota(jnp.int32, (tm, 1), 0)
    prev = jnp.where(row == 0, carry_ref[7:8, :], pltpu.roll(x, 1, axis=0))
    xx = prev - x
    carry_ref[...] = x[tm - 8:, :]
    for j, o_ref in enumerate(outs):
        o_ref[...] = (x + xx * mu_ref[j:j + 1, :]).astype(BF16)


def _rwkv_mix(h, mu, batch, tm):
    n, d = h.shape
    nmix = mu.shape[0]
    tpb = n // batch // tm
    blk = pl.BlockSpec((tm, d), lambda b, t: (b * tpb + t, 0))
    return pl.pallas_call(
        _rwkv_mix_kernel,
        grid=(batch, tpb),
        in_specs=[blk, pl.BlockSpec((nmix, d), lambda b, t: (0, 0))],
        out_specs=[blk] * nmix,
        out_shape=[jax.ShapeDtypeStruct((n, d), BF16)] * nmix,
        scratch_shapes=[pltpu.VMEM((8, d), F32)],
        compiler_params=_cparams("arbitrary", "arbitrary"),
        name="rwkv_mix",
    )(h, mu)


def _lora_kernel(x_ref, wa_ref, wb_ref, bias_ref, o_ref, wab_ref, wbb_ref, *, mid_act, out_act):
    @pl.when(pl.program_id(0) == 0)
    def _():
        wab_ref[...] = wa_ref[...].astype(BF16)
        wbb_ref[...] = wb_ref[...].astype(BF16)

    mid = mid_act(jnp.dot(x_ref[...], wab_ref[...], preferred_element_type=F32))
    y = jnp.dot(mid.astype(BF16), wbb_ref[...], preferred_element_type=F32)
    o_ref[...] = out_act(bias_ref[...] + y)


def _lora(xb, wa, wb, bias, mid_act, out_act, tm, name):
    n, d = xb.shape
    r = wa.shape[1]
    dout = wb.shape[1]
    return pl.pallas_call(
        functools.partial(_lora_kernel, mid_act=mid_act, out_act=out_act),
        grid=(n // tm,),
        in_specs=[
            pl.BlockSpec((tm, d), lambda i: (i, 0)),
            pl.BlockSpec((d, r), lambda i: (0, 0)),
            pl.BlockSpec((r, dout), lambda i: (0, 0)),
            pl.BlockSpec((1, dout), lambda i: (0, 0)),
        ],
        out_specs=pl.BlockSpec((tm, dout), lambda i: (i, 0)),
        out_shape=jax.ShapeDtypeStruct((n, dout), F32),
        scratch_shapes=[pltpu.VMEM((d, r), BF16), pltpu.VMEM((r, dout), BF16)],
        compiler_params=_cparams("arbitrary"),
        name=name,
    )(xb, wa, wb, bias.reshape(1, dout))


def _rwkv_log_decay(z):
    w_log = -(jnp.maximum(-z, 0.0) + jnp.log(1.0 + jnp.exp(-jnp.abs(z)))) - 0.5
    return -jnp.exp(w_log)


RWKV_GROUP = 4
RWKV_UNROLL = 8


def _seg_sum(x, bd):
    hi = x.astype(BF16)
    lo = (x - hi.astype(F32)).astype(BF16)
    return jnp.dot(hi, bd, preferred_element_type=F32) + jnp.dot(lo, bd, preferred_element_type=F32)


def _rwkv_scan_kernel(r_ref, kr_ref, v_ref, lw_ref, a_ref, g_ref, kk_p, ka_p, rk_p, gg_p, gb_p,
                      o_ref, kk_s, k_s, bonus_s, y_s, st_ref, *, chunk, ptile):
    t, w = r_ref.shape
    nhead = w // RWKV_HEAD
    sc = nhead * chunk
    lane_r = lax.broadcasted_iota(jnp.int32, (w, w), 0) // RWKV_HEAD
    lane_c = lax.broadcasted_iota(jnp.int32, (w, w), 1) // RWKV_HEAD
    bd = jnp.where(lane_r == lane_c, 1.0, 0.0).astype(BF16)

    def prologue(i, carry):
        rows = pl.ds(pl.multiple_of(i * ptile, 8), ptile)
        kr = kr_ref[rows, :]
        a = a_ref[rows, :]
        kkr = kr * kk_p[...]
        nrm = jnp.maximum(jnp.sqrt(_seg_sum(kkr * kkr, bd)), 1e-12)
        kk_s[rows, :] = kkr / nrm
        k = kr * (1.0 + (a - 1.0) * ka_p[...])
        k_s[rows, :] = k
        bonus_s[rows, :] = _seg_sum(r_ref[rows, :] * k * rk_p[...], bd) * v_ref[rows, :]
        return carry

    lax.fori_loop(0, t // ptile, prologue, 0)

    st_ref[...] = jnp.zeros_like(st_ref)
    tril = _tril_bf16(chunk, True)
    head_of_lane = lax.broadcasted_iota(jnp.int32, (chunk, w), 1) // RWKV_HEAD
    ri = lax.broadcasted_iota(jnp.int32, (2 * sc, sc), 0)
    ci = lax.broadcasted_iota(jnp.int32, (2 * sc, sc), 1)
    low_mask = ci < jnp.where(ri < sc, ri, ri - sc + 1)
    nsteps = max(1, (chunk - 1).bit_length())

    def stack(x):
        return jnp.concatenate([jnp.where(head_of_lane == hh, x, 0.0) for hh in range(nhead)], axis=0)

    eye = jnp.where(lax.broadcasted_iota(jnp.int32, (sc, sc), 0)
                    == lax.broadcasted_iota(jnp.int32, (sc, sc), 1), 1.0, 0.0)

    def prepare(c):
        start = c * chunk
        rows = pl.ds(start if isinstance(start, int) else pl.multiple_of(start, 16), chunk)
        r = r_ref[rows, :]
        v = v_ref[rows, :]
        lw = lw_ref[rows, :]
        a = a_ref[rows, :]
        kk = kk_s[rows, :]
        k = k_s[rows, :]
        cum = _cumsum_rows(lw, tril)
        e_neg = jnp.exp(-cum)
        at2 = stack(-kk * jnp.exp(cum - lw))
        rt2 = stack(r * jnp.exp(cum))
        bvec = kk * a
        bb2 = stack(bvec * e_neg)
        kb2 = stack(k * e_neg)
        v2 = stack(v)
        ar2 = jnp.concatenate([at2, rt2], axis=0).astype(BF16)
        cl = cum[chunk - 1:chunk]
        e_end = jnp.exp(cl - cum)
        khbh = jnp.concatenate([stack(k * e_end), stack(bvec * e_end)], axis=0).astype(BF16)
        yield None
        pb = jnp.where(low_mask, _dot_nt(ar2, bb2), 0.0)
        pk = jnp.where(low_mask, _dot_nt(ar2, kb2), 0.0)
        m_ab, m_rb = pb[:sc], pb[sc:]
        m_ak, m_rk = pk[:sc], pk[sc:]
        yield None
        u0 = _dot_nn(m_ak, v2)
        y0 = _dot_nn(m_rk, v2)
        tinv = eye + m_ab
        lpow = m_ab
        for _ in range(nsteps - 1):
            yield None
            lpow = _dot_nn(lpow, lpow)
            tinv = tinv + _dot_nn(tinv, lpow)
        return dict(rows=rows, ar2=ar2, tinv=tinv.astype(BF16), m_rb=m_rb.astype(BF16),
                    u0=u0, y0=y0, v2=v2.astype(BF16), khbh=khbh, decay=jnp.exp(cl))

    def chunk_steps(c, j, run):
        pc = yield from prepare(c)
        while run["turn"] != j:
            yield None
        st = run["st"]
        ps = _dot_nt(pc["ar2"], st)
        yield None
        u2 = jnp.dot(pc["tinv"], (ps[:sc] + pc["u0"]).astype(BF16), preferred_element_type=F32)
        u2b = u2.astype(BF16)
        yield None
        y2 = ps[sc:] + pc["y0"] + jnp.dot(pc["m_rb"], u2b, preferred_element_type=F32)
        y = y2[0:chunk]
        for hh in range(1, nhead):
            y = y + y2[hh * chunk:(hh + 1) * chunk]
        y_s[pc["rows"], :] = y
        run["st"] = st * pc["decay"] + _dot_tn(jnp.concatenate([pc["v2"], u2b], axis=0), pc["khbh"])
        run["turn"] = j + 1
        yield True

    def run_group(chunk_ids):
        run = dict(st=st_ref[...], turn=0)
        _round_robin([chunk_steps(c, j, run) for j, c in enumerate(chunk_ids)])
        st_ref[...] = run["st"]

    nchunks = t // chunk

    def body(i, carry):
        run_group([i * RWKV_UNROLL + j for j in range(RWKV_UNROLL)])
        return carry

    lax.fori_loop(0, nchunks // RWKV_UNROLL, body, 0)
    tail = list(range(nchunks - nchunks % RWKV_UNROLL, nchunks))
    if tail:
        run_group(tail)

    inv = 1.0 / RWKV_HEAD

    def epilogue(i, carry):
        rows = pl.ds(pl.multiple_of(i * ptile, 8), ptile)
        y = y_s[rows, :]
        mu = _seg_sum(y, bd) * inv
        yc = y - mu
        var = _seg_sum(yc * yc, bd) * inv
        yn = yc * lax.rsqrt(var + 1e-5 * RWKV_HEAD) * gg_p[...] + gb_p[...]
        o_ref[rows, :] = ((yn + bonus_s[rows, :]) * g_ref[rows, :]).astype(BF16)
        return carry

    lax.fori_loop(0, t // ptile, epilogue, 0)


def _rwkv_scan(r, kr, v, lw, a, g, p, batch):
    n, d = r.shape
    t = n // batch
    chunk = _chunk_len(t)
    w = min(d, RWKV_GROUP * RWKV_HEAD)
    ptile = _divisor_tile(t, 768, 16)
    blk = pl.BlockSpec((t, w), lambda b, j: (b, j))
    prm = pl.BlockSpec((1, w), lambda b, j: (0, j))
    params = [p["rwkv_k_k"], p["rwkv_k_a"], p["rwkv_r_k"], p["rwkv_gn_g"], p["rwkv_gn_b"]]
    return pl.pallas_call(
        functools.partial(_rwkv_scan_kernel, chunk=chunk, ptile=ptile),
        grid=(batch, d // w),
        in_specs=[blk] * 6 + [prm] * 5,
        out_specs=blk,
        out_shape=jax.ShapeDtypeStruct((n, d), BF16),
        scratch_shapes=[pltpu.VMEM((t, w), F32)] * 4 + [pltpu.VMEM((w, w), F32)],
        compiler_params=_cparams("parallel", "parallel"),
        name="rwkv_scan",
    )(r, kr, v, lw, a, g, *[x.reshape(1, d) for x in params])


def _rwkv_mixer_layer(h, hb, p, g, b, alpha, batch, tm):
    n, d = h.shape
    ident = lambda y: y
    xr, xw, xk, xv, xa, xg = _rwkv_mix(h, p["rwkv_mu"], batch, tm)
    (r,) = _proj(xr, p["rwkv_w_r"], 0, d, lambda y: (y,), [], [F32], tm, "rwkv_proj_r")
    (kr,) = _proj(xk, p["rwkv_w_k"], 0, d, lambda y: (y,), [], [F32], tm, "rwkv_proj_k")
    (v,) = _proj(xv, p["rwkv_w_v"], 0, d, lambda y: (y,), [], [F32], tm, "rwkv_proj_v")
    lw = _lora(xw, p["rwkv_w1"], p["rwkv_w2"], p["rwkv_w0"], jnp.tanh, _rwkv_log_decay, tm, "rwkv_lora_w")
    a = _lora(xa, p["rwkv_a1"], p["rwkv_a2"], p["rwkv_a0"], ident, _sigmoid, tm, "rwkv_lora_a")
    gate = _lora(xg, p["rwkv_g1"], p["rwkv_g2"], jnp.zeros((d,), F32), _sigmoid, ident, tm, "rwkv_lora_g")
    z = _rwkv_scan(r, kr, v, lw, a, gate, p, batch)
    return _mm_res_ln(z, p["rwkv_w_out"], h, g, b, alpha, tm)


def _embed_kernel(x_ref, meta_ref, h_ref, hb_ref, sem):
    b = pl.program_id(0)
    j = pl.program_id(1)
    tm = h_ref.shape[0]
    nmeta = meta_ref.shape[0]

    @pl.when(j == 0)
    def _():
        h_ref[0:nmeta, :] = meta_ref[...]
        cp = pltpu.make_async_copy(x_ref.at[b, pl.ds(0, tm - nmeta), :], h_ref.at[pl.ds(nmeta, tm - nmeta), :], sem)
        cp.start()
        cp.wait()

    @pl.when(j > 0)
    def _():
        first = pl.multiple_of(j * tm - nmeta, 8)
        cp = pltpu.make_async_copy(x_ref.at[b, pl.ds(first, tm), :], h_ref, sem)
        cp.start()
        cp.wait()

    hb_ref[...] = h_ref[...].astype(BF16)


def _embed(x, meta, tm):
    batch, seq, d = x.shape
    nmeta = meta.shape[0]
    t = nmeta + seq
    tpb = t // tm
    assert nmeta % 8 == 0 and tm % 8 == 0
    blk = pl.BlockSpec((tm, d), lambda b, j: (b * tpb + j, 0))
    return pl.pallas_call(
        _embed_kernel,
        grid=(batch, tpb),
        in_specs=[pl.BlockSpec(memory_space=pl.ANY), pl.BlockSpec((nmeta, d), lambda b, j: (0, 0))],
        out_specs=[blk, blk],
        out_shape=[jax.ShapeDtypeStruct((batch * t, d), F32), jax.ShapeDtypeStruct((batch * t, d), BF16)],
        scratch_shapes=[pltpu.SemaphoreType.DMA],
        compiler_params=_cparams("parallel", "arbitrary"),
        name="embed",
    )(x, meta.astype(x.dtype))


def kernel(x, meta, ln_mix_g, ln_mix_b, ln_ffn_g, ln_ffn_b, conv_w_in, conv_w, conv_b, conv_w_out, rwkv_mu, rwkv_w_r, rwkv_w_k, rwkv_w_v, rwkv_w0, rwkv_w1, rwkv_w2, rwkv_a0, rwkv_a1, rwkv_a2, rwkv_g1, rwkv_g2, rwkv_k_k, rwkv_k_a, rwkv_r_k, rwkv_gn_g, rwkv_gn_b, rwkv_w_out, hgrn_w_in, hgrn_lb, hgrn_norm_g, hgrn_w_out, fox_w_in, fox_b_f, fox_q_norm_g, fox_k_norm_g, fox_w_out, ffn0_w1, ffn0_w3, ffn0_w2, moe1_router, moe1_router_b, moe1_w1, moe1_w3, moe1_w2, ffn2_w1, ffn2_w3, ffn2_w2, moe3_router, moe3_router_b, moe3_w1, moe3_w3, moe3_w2):
    batch, seq, d = x.shape
    depth = ln_mix_g.shape[0]
    assert depth == 4
    alpha = (2.0 * depth) ** 0.25
    t = N_META + seq
    n = batch * t
    tm = _divisor_tile(t, 768, 16)
    p = dict(
        conv_w_in=conv_w_in, conv_w=conv_w, conv_b=conv_b, conv_w_out=conv_w_out,
        rwkv_mu=rwkv_mu, rwkv_w_r=rwkv_w_r, rwkv_w_k=rwkv_w_k, rwkv_w_v=rwkv_w_v, rwkv_w0=rwkv_w0,
        rwkv_w1=rwkv_w1, rwkv_w2=rwkv_w2, rwkv_a0=rwkv_a0, rwkv_a1=rwkv_a1, rwkv_a2=rwkv_a2,
        rwkv_g1=rwkv_g1, rwkv_g2=rwkv_g2, rwkv_k_k=rwkv_k_k, rwkv_k_a=rwkv_k_a, rwkv_r_k=rwkv_r_k,
        rwkv_gn_g=rwkv_gn_g, rwkv_gn_b=rwkv_gn_b, rwkv_w_out=rwkv_w_out,
        hgrn_w_in=hgrn_w_in, hgrn_lb=hgrn_lb, hgrn_norm_g=hgrn_norm_g, hgrn_w_out=hgrn_w_out,
        fox_w_in=fox_w_in, fox_b_f=fox_b_f, fox_q_norm_g=fox_q_norm_g, fox_k_norm_g=fox_k_norm_g,
        fox_w_out=fox_w_out,
    )
    assert meta.shape[0] == N_META
    h, hb = _embed(x, meta, tm)

    h, hb = _conv_mixer_layer(h, hb, p, ln_mix_g[0], ln_mix_b[0], alpha, batch, tm)
    h, hb = _dense_ffn(hb, h, ffn0_w1, ffn0_w3, ffn0_w2, ln_ffn_g[0], ln_ffn_b[0], alpha, tm)
    h, hb = _rwkv_mixer_layer(h, hb, p, ln_mix_g[1], ln_mix_b[1], alpha, batch, tm)
    h, hb = _moe_ffn(h, hb, moe1_router, moe1_router_b, moe1_w1, moe1_w3, moe1_w2,
                     ln_ffn_g[1], ln_ffn_b[1], alpha, tm)
    h, hb = _hgrn_mixer_layer(h, hb, p, 2, ln_mix_g[2], ln_mix_b[2], alpha, batch, tm)
    h, hb = _dense_ffn(hb, h, ffn2_w1, ffn2_w3, ffn2_w2, ln_ffn_g[2], ln_ffn_b[2], alpha, tm)
    h, hb = _fox_mixer_layer(h, hb, p, ln_mix_g[3], ln_mix_b[3], alpha, batch, tm)
    h, hb = _moe_ffn(h, hb, moe3_router, moe3_router_b, moe3_w1, moe3_w3, moe3_w2,
                     ln_ffn_g[3], ln_ffn_b[3], alpha, tm)
    return h.reshape(batch, t, d)[:, N_META:]
```
